```python
import math
import jax
import jax.numpy as jnp
from jax import lax
import numpy as np

D_MODEL = 2048
BATCH = 2
SEQ = 4096
DEPTH = 2

GRID_W = 64
CTX_LEN = 256
ROPE_BASE = 10000.0
ROPE_DIM = 64
QBLK = 128
ADALN_EPS = 1e-6
POST_LN_EPS = 1e-5
NEG_INF = -1e30

MLA_HEADS = 4
MLA_Q_RANK = 384
MLA_KV_RANK = 256
MLA_NOPE = 128
MLA_ROPE = ROPE_DIM
MLA_V = 128

RWKV_HEADS = 8
RWKV_HEAD = 64
RWKV_W = RWKV_HEADS * RWKV_HEAD
DECAY_LORA = 32
ICLR_LORA = 32
GATE_LORA = 96
RWKV_STREAM = 3 * RWKV_W + 2 * DECAY_LORA + 2 * ICLR_LORA + GATE_LORA
RWKV_GN_EPS = 64e-5

SWA_HEADS = 8
SWA_KV_HEADS = 2
SWA_GROUP = SWA_HEADS // SWA_KV_HEADS
SWA_HEAD = ROPE_DIM
WINDOW = 128

DIFF_HEADS = 4
DIFF_HEAD = ROPE_DIM

N_BRANCH = 4
BRANCH_W = 512

N_EXPERTS = 64
TOP_K = 6
N_GROUPS = 8
TOPK_GROUPS = 4
EXPERT_FF = 512
SHARED_FF = 512
ROUTED_SCALE = 2.5
MOE_BLK = 128

DN_ALPHA = (2 * DEPTH) ** 0.25
DN_BETA = (8 * DEPTH) ** -0.25

SEG_SIZES = [MLA_Q_RANK,
             MLA_KV_RANK + MLA_ROPE,
             RWKV_STREAM,
             (SWA_HEADS + 2 * SWA_KV_HEADS) * SWA_HEAD,
             3 * DIFF_HEADS * 2 * DIFF_HEAD,
             N_BRANCH * D_MODEL]
SEG_OFFSETS = [int(v) for v in np.cumsum(SEG_SIZES)[:-1]]
IN_COLS = int(sum(SEG_SIZES))

F32 = jnp.float32

kernel_name = 'hybrid_mla_rwkv7_swa_diffattn_moe_block'


def layer_norm(x, gain=None, bias=None, eps=ADALN_EPS):
    xf = x.astype(F32)
    mu = jnp.mean(xf, -1, keepdims=True)
    var = jnp.mean(jnp.square(xf - mu), -1, keepdims=True)
    y = (xf - mu) * lax.rsqrt(var + eps)
    if gain is not None:
        y = y * gain.astype(F32) + bias.astype(F32)
    return y.astype(x.dtype)


def rms_norm(x, gain, eps=1e-6):
    xf = x.astype(F32)
    y = xf * lax.rsqrt(jnp.mean(jnp.square(xf), -1, keepdims=True) + eps)
    return (y * gain.astype(F32)).astype(x.dtype)


def modulate(x, shift, scale):
    return layer_norm(x) * (1 + scale) + shift


def centred_shift(z):
    zp = jnp.pad(z, ((0, 0), (1, 1), (0, 0)))
    return 0.5 * (zp[:, :-2] + zp[:, 2:])


def axial_rope_tables(n_tokens, rot_dim):
    rows = n_tokens // GRID_W
    row = jnp.repeat(jnp.arange(rows, dtype=F32), GRID_W)
    col = jnp.tile(jnp.arange(GRID_W, dtype=F32), rows)
    n_freq = rot_dim // 4
    inv_freq = ROPE_BASE ** (-jnp.arange(n_freq, dtype=F32) / n_freq)
    ang = jnp.concatenate([row[:, None] * inv_freq, col[:, None] * inv_freq], -1)
    return jnp.cos(ang), jnp.sin(ang)


def apply_rope(x, cos, sin):
    shape = (1, cos.shape[0]) + (1,) * (x.ndim - 3) + (cos.shape[1],)
    c = cos.reshape(shape).astype(x.dtype)
    s = sin.reshape(shape).astype(x.dtype)
    x1, x2 = jnp.split(x, 2, axis=-1)
    return jnp.concatenate([x1 * c - x2 * s, x1 * s + x2 * c], -1)


def block_attn(q, k, v, scale):
    B, Tq, H, dq = q.shape
    nb = Tq // QBLK
    qb = jnp.moveaxis(q.reshape(B, nb, QBLK, H, dq), 1, 0)

    def one(q_blk):
        s = jnp.einsum('bqhd,bkhd->bhqk', q_blk, k).astype(F32) * scale
        p = jax.nn.softmax(s, axis=-1).astype(v.dtype)
        return jnp.einsum('bhqk,bkhd->bqhd', p, v)

    o = lax.map(one, qb)
    return jnp.moveaxis(o, 0, 1).reshape(B, Tq, H, v.shape[-1])


def mla_project(q_lat, kv_lat, q_norm, w_qup, kv_norm, w_kvup):
    B, T = q_lat.shape[:2]
    q = (rms_norm(q_lat, q_norm) @ w_qup).reshape(B, T, MLA_HEADS, MLA_NOPE + MLA_ROPE)
    c_kv, k_rope = jnp.split(kv_lat, [MLA_KV_RANK], axis=-1)
    kv = (rms_norm(c_kv, kv_norm) @ w_kvup).reshape(B, T, MLA_HEADS, MLA_NOPE + MLA_V)
    q_nope, q_rope = jnp.split(q, [MLA_NOPE], axis=-1)
    k_nope, v = jnp.split(kv, [MLA_NOPE], axis=-1)
    return q_nope, q_rope, k_nope, k_rope, v


def mla_qk(q_nope, q_rope, k_nope, k_rope):
    q = jnp.concatenate([q_nope, q_rope], -1)
    k_rope_h = jnp.broadcast_to(k_rope[:, :, None, :], k_nope.shape[:3] + (MLA_ROPE,))
    return q, jnp.concatenate([k_nope, k_rope_h], -1)


def mla_mixer(q_l, kv_l, q_c, kv_c, q_norm, w_qup, kv_norm, w_kvup, cos, sin, need_ctx):
    scale = (MLA_NOPE + MLA_ROPE) ** -0.5
    qn_c, qr_c, kn_c, kr_c, v_c = mla_project(q_c, kv_c, q_norm, w_qup, kv_norm, w_kvup)
    qq_c, k_c = mla_qk(qn_c, qr_c, kn_c, kr_c)
    qn_l, qr_l, kn_l, kr_l, v_l = mla_project(q_l, kv_l, q_norm, w_qup, kv_norm, w_kvup)
    qq_l, k_l = mla_qk(qn_l, apply_rope(qr_l, cos, sin), kn_l, apply_rope(kr_l, cos, sin))
    k_all = jnp.concatenate([k_c, k_l], axis=1)
    v_all = jnp.concatenate([v_c, v_l], axis=1)
    B, S = q_l.shape[:2]
    o_l = block_attn(qq_l, k_all, v_all, scale).reshape(B, S, -1)
    o_c = block_attn(qq_c, k_c, v_c, scale).reshape(B, q_c.shape[1], -1) if need_ctx else None
    return o_l, o_c


def rwkv_streams(seg, mu, w0, w_lora, a0, a_lora, g_lora, k_k, k_a):
    B, T = seg.shape[:2]
    seg = seg + (centred_shift(seg) - seg) * mu
    r, k, v, wl, al, gl = jnp.split(
        seg, [RWKV_W, 2 * RWKV_W, 3 * RWKV_W, 3 * RWKV_W + 2 * DECAY_LORA,
              3 * RWKV_W + 2 * DECAY_LORA + 2 * ICLR_LORA], axis=-1)
    w = w0 + jnp.einsum('btdr,drc->btdc', jnp.tanh(wl.reshape(B, T, 2, DECAY_LORA)), w_lora)
    w = -jax.nn.softplus(-w) - 0.5
    decay = jnp.exp(-jnp.exp(w.astype(F32)))
    a = jax.nn.sigmoid(a0 + jnp.einsum('btdr,drc->btdc', al.reshape(B, T, 2, ICLR_LORA), a_lora))
    g = jax.nn.sigmoid(gl) @ g_lora
    kk = (k * k_k).astype(F32).reshape(B, T, RWKV_HEADS, RWKV_HEAD)
    kk = (kk / jnp.maximum(jnp.linalg.norm(kk, axis=-1, keepdims=True), 1e-12)).reshape(B, T, RWKV_W)
    k_dir = k[:, :, None, :] * (1 + (a - 1) * k_a)
    return r, k_dir, v, decay, a, kk, g


def rwkv_scan(state0, r, decay, k_dir, v, kk, a):
    def heads(z):
        return z.reshape(z.shape[:-1] + (RWKV_HEADS, RWKV_HEAD)).astype(F32)

    def shared(z):
        z = jnp.moveaxis(heads(z), 1, 0)
        return jnp.stack([z, z[::-1]], axis=1)

    def per_dir(z):
        z = heads(z).transpose(1, 2, 0, 3, 4)
        return jnp.stack([z[:, 0], z[::-1, 1]], axis=1)

    def step(S, inp):
        r_t, w_t, k_t, v_t, kk_t, a_t = inp
        sa = jnp.einsum('dbhij,dbhj->dbhi', S, -kk_t)
        S = (S * w_t[..., None, :] + sa[..., :, None] * (kk_t * a_t)[..., None, :]
             + v_t[..., :, None] * k_t[..., None, :])
        return S, jnp.einsum('dbhij,dbhj->dbhi', S, r_t)

    xs = (shared(r), per_dir(decay), per_dir(k_dir), shared(v), shared(kk), per_dir(a))
    s_final, ys = lax.scan(step, state0, xs)
    y = ys[:, 0] + ys[::-1, 1]
    return s_final, jnp.moveaxis(y, 0, 1)


def rwkv_output(y, r, k_dir, v, g, r_k, ln_g, ln_b):
    B, T, C = r.shape
    mu = jnp.mean(y, -1, keepdims=True)
    var = jnp.mean(jnp.square(y - mu), -1, keepdims=True)
    yn = ((y - mu) * lax.rsqrt(var + RWKV_GN_EPS)).reshape(B, T, C) * ln_g.astype(F32) + ln_b.astype(F32)
    rh = r.reshape(B, T, 1, RWKV_HEADS, RWKV_HEAD)
    kh = k_dir.reshape(B, T, 2, RWKV_HEADS, RWKV_HEAD)
    vh = v.reshape(B, T, RWKV_HEADS, RWKV_HEAD)
    bonus = (jnp.sum(rh * kh * r_k, axis=(2, 4))[..., None] * vh).reshape(B, T, C)
    return (yn.astype(r.dtype) + bonus) * g


def rwkv_mixer(seg_l, seg_c, mu, w0, w_lora, a0, a_lora, g_lora, k_k, k_a, r_k, ln_g, ln_b, need_ctx):
    B = seg_l.shape[0]
    r_c, kd_c, v_c, w_c, a_c, kk_c, g_c = rwkv_streams(seg_c, mu, w0, w_lora, a0, a_lora, g_lora, k_k, k_a)
    r_l, kd_l, v_l, w_l, a_l, kk_l, g_l = rwkv_streams(seg_l, mu, w0, w_lora, a0, a_lora, g_lora, k_k, k_a)
    s0 = jnp.zeros((2, B, RWKV_HEADS, RWKV_HEAD, RWKV_HEAD), F32)
    s_ctx, y_c = rwkv_scan(s0, r_c, w_c, kd_c, v_c, kk_c, a_c)
    _, y_l = rwkv_scan(s_ctx, r_l, w_l, kd_l, v_l, kk_l, a_l)
    o_l = rwkv_output(y_l, r_l, kd_l, v_l, g_l, r_k, ln_g, ln_b)
    o_c = rwkv_output(y_c, r_c, kd_c, v_c, g_c, r_k, ln_g, ln_b) if need_ctx else None
    return o_l, o_c


def swa_heads(seg):
    B, T = seg.shape[:2]
    q, k, v = jnp.split(seg, [SWA_HEADS * SWA_HEAD, (SWA_HEADS + SWA_KV_HEADS) * SWA_HEAD], axis=-1)
    return (q.reshape(B, T, SWA_KV_HEADS, SWA_GROUP, SWA_HEAD),
            k.reshape(B, T, SWA_KV_HEADS, SWA_HEAD),
            v.reshape(B, T, SWA_KV_HEADS, SWA_HEAD))


def swa_latent(q, k, v, k_ctx, v_ctx, sink):
    B, S = q.shape[:2]
    nb = S // WINDOW
    scale = SWA_HEAD ** -0.5
    pad = ((0, 0), (WINDOW, WINDOW), (0, 0), (0, 0))
    kp = jnp.pad(k, pad).reshape(B, nb + 2, WINDOW, SWA_KV_HEADS, SWA_HEAD)
    vp = jnp.pad(v, pad).reshape(B, nb + 2, WINDOW, SWA_KV_HEADS, SWA_HEAD)
    kb = jnp.concatenate([kp[:, :-2], kp[:, 1:-1], kp[:, 2:]], axis=2)
    vb = jnp.concatenate([vp[:, :-2], vp[:, 1:-1], vp[:, 2:]], axis=2)
    qb = q.reshape(B, nb, WINDOW, SWA_KV_HEADS, SWA_GROUP, SWA_HEAD)
    s_loc = jnp.einsum('bnqhgd,bnkhd->bnhgqk', qb, kb).astype(F32) * scale
    qi = jnp.arange(WINDOW)[:, None]
    kj = jnp.arange(3 * WINDOW)[None, :]
    near = jnp.abs(kj - WINDOW - qi) <= WINDOW
    kpos = jnp.arange(nb)[:, None, None] * WINDOW - WINDOW + kj[None]
    valid = near[None] & (kpos >= 0) & (kpos < S)
    s_loc = jnp.where(valid[None, :, None, None], s_loc, NEG_INF)
    s_ctx = jnp.einsum('bnqhgd,bchd->bnhgqc', qb, k_ctx).astype(F32) * scale
    s_sink = jnp.broadcast_to(sink.reshape(SWA_KV_HEADS, SWA_GROUP)[None, None, :, :, None, None].astype(F32),
                              s_loc.shape[:-1] + (1,))
    p = jax.nn.softmax(jnp.concatenate([s_loc, s_ctx, s_sink], -1), axis=-1).astype(v.dtype)
    n_loc = 3 * WINDOW
    p_loc = p[..., :n_loc]
    p_ctx = p[..., n_loc:n_loc + k_ctx.shape[1]]
    o = (jnp.einsum('bnhgqk,bnkhd->bnqhgd', p_loc, vb)
         + jnp.einsum('bnhgqc,bchd->bnqhgd', p_ctx, v_ctx))
    return o.reshape(B, S, SWA_HEADS * SWA_HEAD)


def swa_context(q, k, v, sink):
    B, T = q.shape[:2]
    s = jnp.einsum('bqhgd,bkhd->bhgqk', q, k).astype(F32) * SWA_HEAD ** -0.5
    s_sink = jnp.broadcast_to(sink.reshape(SWA_KV_HEADS, SWA_GROUP)[None, :, :, None, None].astype(F32),
                              s.shape[:-1] + (1,))
    p = jax.nn.softmax(jnp.concatenate([s, s_sink], -1), axis=-1)[..., :-1].astype(v.dtype)
    return jnp.einsum('bhgqk,bkhd->bqhgd', p, v).reshape(B, T, SWA_HEADS * SWA_HEAD)


def swa_mixer(seg_l, seg_c, sink, cos, sin, need_ctx):
    q_c, k_c, v_c = swa_heads(seg_c)
    q_l, k_l, v_l = swa_heads(seg_l)
    o_l = swa_latent(apply_rope(q_l, cos, sin), apply_rope(k_l, cos, sin), v_l, k_c, v_c, sink)
    o_c = swa_context(q_c, k_c, v_c, sink) if need_ctx else None
    return o_l, o_c


def diff_heads(seg):
    B, T = seg.shape[:2]
    q, k, v = jnp.split(seg, 3, axis=-1)
    q = q.reshape(B, T, DIFF_HEADS, 2, DIFF_HEAD)
    k = k.reshape(B, T, DIFF_HEADS, 2, DIFF_HEAD)
    return q[..., 0, :], q[..., 1, :], k[..., 0, :], k[..., 1, :], v.reshape(B, T, DIFF_HEADS, 2 * DIFF_HEAD)


def diff_mixer(seg_l, seg_c, lam, subln, lam_init, cos, sin, need_ctx):
    q1c, q2c, k1c, k2c, vc = diff_heads(seg_c)
    q1l, q2l, k1l, k2l, vl = diff_heads(seg_l)
    q1l, q2l = apply_rope(q1l, cos, sin), apply_rope(q2l, cos, sin)
    k1l, k2l = apply_rope(k1l, cos, sin), apply_rope(k2l, cos, sin)
    lf = lam.astype(F32)
    lam_val = jnp.exp(jnp.sum(lf[0] * lf[1])) - jnp.exp(jnp.sum(lf[2] * lf[3])) + lam_init
    scale = DIFF_HEAD ** -0.5

    def diff(q1, q2, k1, k2, v):
        o = block_attn(q1, k1, v, scale) - lam_val.astype(v.dtype) * block_attn(q2, k2, v, scale)
        o = rms_norm(o, subln, eps=1e-5) * (1.0 - lam_init)
        return o.reshape(o.shape[0], o.shape[1], -1)

    o_l = diff(q1l, q2l, jnp.concatenate([k1c, k1l], 1), jnp.concatenate([k2c, k2l], 1),
               jnp.concatenate([vc, vl], 1))
    o_c = diff(q1c, q2c, k1c, k2c, vc) if need_ctx else None
    return o_l, o_c


def merge_branches(outs, gate_cols, w_branch, w_out):
    B, T = gate_cols.shape[:2]
    gates = jax.nn.sigmoid(gate_cols.reshape(B, T, N_BRANCH, D_MODEL))
    merged = gates[:, :, 0] * (outs[0] @ w_branch[0])
    for i in range(1, N_BRANCH):
        merged = merged + gates[:, :, i] * (outs[i] @ w_branch[i])
    return merged @ w_out


def token_mixing(pl, pc, layer_idx, need_ctx, cos, sin,
                 mla_q_norm, mla_w_qup, mla_kv_norm, mla_w_kvup,
                 rwkv_mu, rwkv_w0, rwkv_w_lora, rwkv_a0, rwkv_a_lora, rwkv_g_lora,
                 rwkv_k_k, rwkv_k_a, rwkv_r_k, rwkv_ln_g, rwkv_ln_b,
                 swa_sink, diff_lambda, diff_subln, w_branch, w_out):
    mq_l, mkv_l, rw_l, sw_l, df_l, gt_l = jnp.split(pl, SEG_OFFSETS, axis=-1)
    mq_c, mkv_c, rw_c, sw_c, df_c, gt_c = jnp.split(pc, SEG_OFFSETS, axis=-1)
    lam_init = 0.8 - 0.6 * math.exp(-0.3 * layer_idx)
    a_l, a_c = mla_mixer(mq_l, mkv_l, mq_c, mkv_c, mla_q_norm, mla_w_qup, mla_kv_norm, mla_w_kvup,
                         cos, sin, need_ctx)
    b_l, b_c = rwkv_mixer(rw_l, rw_c, rwkv_mu, rwkv_w0, rwkv_w_lora, rwkv_a0, rwkv_a_lora, rwkv_g_lora,
                          rwkv_k_k, rwkv_k_a, rwkv_r_k, rwkv_ln_g, rwkv_ln_b, need_ctx)
    s_l, s_c = swa_mixer(sw_l, sw_c, swa_sink, cos, sin, need_ctx)
    d_l, d_c = diff_mixer(df_l, df_c, diff_lambda, diff_subln, lam_init, cos, sin, need_ctx)
    y_l = merge_branches((a_l, b_l, s_l, d_l), gt_l, w_branch, w_out)
    y_c = merge_branches((a_c, b_c, s_c, d_c), gt_c, w_branch, w_out) if need_ctx else None
    return y_l, y_c


def routed_experts(xt, idx, wts, w_gu, w_dn):
    T = xt.shape[0]
    n_assign = T * TOP_K
    flat_e = idx.reshape(-1)
    order = jnp.argsort(flat_e)
    sorted_e = flat_e[order]
    counts = jnp.bincount(flat_e, length=N_EXPERTS)
    padded = (counts + MOE_BLK - 1) // MOE_BLK * MOE_BLK
    start_sorted = jnp.cumsum(counts) - counts
    pad_end = jnp.cumsum(padded)
    start_pad = pad_end - padded
    dest = start_pad[sorted_e] + jnp.arange(n_assign) - start_sorted[sorted_e]
    n_blocks = -(-(n_assign + N_EXPERTS * (MOE_BLK - 1)) // MOE_BLK)
    n_rows = n_blocks * MOE_BLK
    row_tok = jnp.zeros((n_rows,), jnp.int32).at[dest].set((order // TOP_K).astype(jnp.int32))
    row_w = jnp.zeros((n_rows,), xt.dtype).at[dest].set(wts.reshape(-1)[order])
    blk_e = jnp.minimum(jnp.searchsorted(pad_end, jnp.arange(n_blocks) * MOE_BLK, side='right'),
                        N_EXPERTS - 1).astype(jnp.int32)

    def step(y, inp):
        tok, wr, e = inp
        g, u = jnp.split(xt[tok] @ w_gu[e], 2, axis=-1)
        yb = (jax.nn.silu(g) * u) @ w_dn[e]
        return y.at[tok].add(yb * wr[:, None]), None

    y, _ = lax.scan(step, jnp.zeros_like(xt),
                    (row_tok.reshape(n_blocks, MOE_BLK), row_w.reshape(n_blocks, MOE_BLK), blk_e))
    return y


def moe_ffn(h, router_w, router_bias, w_gu, w_dn, sh_gu, sh_dn):
    shp = h.shape
    xt = h.reshape(-1, D_MODEL)
    T = xt.shape[0]
    scores = jax.nn.sigmoid((xt @ router_w).astype(F32))
    biased = scores + router_bias.astype(F32)
    grp_score = lax.top_k(biased.reshape(T, N_GROUPS, N_EXPERTS // N_GROUPS), 2)[0].sum(-1)
    _, top_g = lax.top_k(grp_score, TOPK_GROUPS)
    gmask = jnp.any(top_g[:, :, None] == jnp.arange(N_GROUPS)[None, None, :], axis=1)
    emask = jnp.repeat(gmask, N_EXPERTS // N_GROUPS, axis=1)
    _, idx = lax.top_k(jnp.where(emask, biased, -jnp.inf), TOP_K)
    w = jnp.take_along_axis(scores, idx, axis=1)
    w = w / jnp.sum(w, -1, keepdims=True) * ROUTED_SCALE
    y = routed_experts(xt, idx, w.astype(xt.dtype), w_gu, w_dn)
    g, u = jnp.split(xt @ sh_gu, 2, axis=-1)
    y = y + (jax.nn.silu(g) * u) @ sh_dn
    return y.reshape(shp)


def setup_inputs(seed: int = 0):
    key = jax.random.key(seed)
    ks = iter(jax.random.split(key, 48))
    L, D, E = DEPTH, D_MODEL, N_EXPERTS

    def nrm(shape, std):
        return jax.random.normal(next(ks), shape, F32) * std

    def unif(shape, lo, hi):
        return jax.random.uniform(next(ks), shape, F32, lo, hi)

    return {
        'x': nrm((BATCH, SEQ, D), 1.0),
        'c': nrm((BATCH, D), 1.0),
        'ctx': nrm((BATCH, CTX_LEN, D), 1.0),
        'c_ctx': nrm((D,), 1.0),
        'w_mod': nrm((L, D, 6 * D), 0.5 * D ** -0.5),
        'b_mod': nrm((L, 6 * D), 0.01),
        'w_in': nrm((L, D, IN_COLS), D ** -0.5),
        'mla_q_norm': 1.0 + nrm((L, MLA_Q_RANK), 0.02),
        'mla_w_qup': nrm((L, MLA_Q_RANK, MLA_HEADS * (MLA_NOPE + MLA_ROPE)), MLA_Q_RANK ** -0.5),
        'mla_kv_norm': 1.0 + nrm((L, MLA_KV_RANK), 0.02),
        'mla_w_kvup': nrm((L, MLA_KV_RANK, MLA_HEADS * (MLA_NOPE + MLA_V)), MLA_KV_RANK ** -0.5),
        'rwkv_mu': unif((L, RWKV_STREAM), 0.0, 1.0),
        'rwkv_w0': unif((L, 2, RWKV_W), -6.5, -1.0),
        'rwkv_w_lora': nrm((L, 2, DECAY_LORA, RWKV_W), 0.5 * DECAY_LORA ** -0.5),
        'rwkv_a0': nrm((L, 2, RWKV_W), 0.1),
        'rwkv_a_lora': nrm((L, 2, ICLR_LORA, RWKV_W), 0.5 * ICLR_LORA ** -0.5),
        'rwkv_g_lora': nrm((L, GATE_LORA, RWKV_W), GATE_LORA ** -0.5),
        'rwkv_k_k': 0.85 + nrm((L, RWKV_W), 0.02),
        'rwkv_k_a': 1.0 + nrm((L, RWKV_W), 0.02),
        'rwkv_r_k': nrm((L, RWKV_HEADS, RWKV_HEAD), 0.1),
        'rwkv_ln_g': 1.0 + nrm((L, RWKV_W), 0.02),
        'rwkv_ln_b': nrm((L, RWKV_W), 0.01),
        'swa_sink': nrm((L, SWA_HEADS), 0.5),
        'diff_lambda': nrm((L, 4, DIFF_HEAD), 0.1),
        'diff_subln': 1.0 + nrm((L, 2 * DIFF_HEAD), 0.02),
        'w_branch': nrm((L, N_BRANCH, BRANCH_W, D), DN_BETA * BRANCH_W ** -0.5),
        'w_out': nrm((L, D, D), DN_BETA * D ** -0.5),
        'ln1_g': 1.0 + nrm((L, D), 0.02),
        'ln1_b': nrm((L, D), 0.01),
        'router_w': nrm((L, D, E), D ** -0.5),
        'router_bias': nrm((L, E), 0.01),
        'exp_w_gu': nrm((L, E, D, 2 * EXPERT_FF), D ** -0.5),
        'exp_w_dn': nrm((L, E, EXPERT_FF, D), DN_BETA * EXPERT_FF ** -0.5),
        'sh_w_gu': nrm((L, D, 2 * SHARED_FF), D ** -0.5),
        'sh_w_dn': nrm((L, SHARED_FF, D), DN_BETA * SHARED_FF ** -0.5),
        'ln2_g': 1.0 + nrm((L, D), 0.02),
        'ln2_b': nrm((L, D), 0.01),
    }


def reference(x, c, ctx, c_ctx, w_mod, b_mod, w_in,
              mla_q_norm, mla_w_qup, mla_kv_norm, mla_w_kvup,
              rwkv_mu, rwkv_w0, rwkv_w_lora, rwkv_a0, rwkv_a_lora, rwkv_g_lora,
              rwkv_k_k, rwkv_k_a, rwkv_r_k, rwkv_ln_g, rwkv_ln_b,
              swa_sink, diff_lambda, diff_subln, w_branch, w_out, ln1_g, ln1_b,
              router_w, router_bias, exp_w_gu, exp_w_dn, sh_w_gu, sh_w_dn, ln2_g, ln2_b):
    n_lat = x.shape[1]
    n_ctx = ctx.shape[1]
    cos, sin = axial_rope_tables(n_lat, ROPE_DIM)
    xc = ctx
    for l in range(DEPTH):
        last = l == DEPTH - 1
        mod_l = jax.nn.silu(c) @ w_mod[l] + b_mod[l]
        mod_c = jax.nn.silu(c_ctx) @ w_mod[l] + b_mod[l]
        sh1, sc1, g1, sh2, sc2, g2 = jnp.split(mod_l[:, None, :], 6, axis=-1)
        sh1c, sc1c, g1c, sh2c, sc2c, g2c = jnp.split(mod_c, 6, axis=-1)
        pl = modulate(x, sh1, sc1) @ w_in[l]
        pc = modulate(xc, sh1c, sc1c) @ w_in[l]
        mix_l, mix_c = token_mixing(
            pl, pc, l, not last, cos, sin,
            mla_q_norm[l], mla_w_qup[l], mla_kv_norm[l], mla_w_kvup[l],
            rwkv_mu[l], rwkv_w0[l], rwkv_w_lora[l], rwkv_a0[l], rwkv_a_lora[l], rwkv_g_lora[l],
            rwkv_k_k[l], rwkv_k_a[l], rwkv_r_k[l], rwkv_ln_g[l], rwkv_ln_b[l],
            swa_sink[l], diff_lambda[l], diff_subln[l], w_branch[l], w_out[l])
        x = layer_norm(DN_ALPHA * x + g1 * mix_l, ln1_g[l], ln1_b[l], eps=POST_LN_EPS)
        h2 = modulate(x, sh2, sc2)
        if last:
            f_l = moe_ffn(h2, router_w[l], router_bias[l], exp_w_gu[l], exp_w_dn[l], sh_w_gu[l], sh_w_dn[l])
        else:
            xc = layer_norm(DN_ALPHA * xc + g1c * mix_c, ln1_g[l], ln1_b[l], eps=POST_LN_EPS)
            h2c = modulate(xc, sh2c, sc2c)
            f = moe_ffn(jnp.concatenate([h2c, h2], axis=1), router_w[l], router_bias[l],
                        exp_w_gu[l], exp_w_dn[l], sh_w_gu[l], sh_w_dn[l])
            f_c, f_l = f[:, :n_ctx], f[:, n_ctx:]
            xc = layer_norm(DN_ALPHA * xc + g2c * f_c, ln2_g[l], ln2_b[l], eps=POST_LN_EPS)
        x = layer_norm(DN_ALPHA * x + g2 * f_l, ln2_g[l], ln2_b[l], eps=POST_LN_EPS)
    return x
```

```python
import functools
import math

import jax
import jax.numpy as jnp
import numpy as np
from jax import lax
from jax.experimental import pallas as pl
from jax.experimental.pallas import tpu as pltpu

F32 = jnp.float32
BF16 = jnp.bfloat16
HIGHEST = lax.Precision.HIGHEST

GRID_W = 64
ROPE_BASE = 10000.0
ROPE_DIM = 64
ADALN_EPS = 1e-6
POST_LN_EPS = 1e-5
NEG_INF = -1e30

MLA_HEADS = 4
MLA_Q_RANK = 384
MLA_KV_RANK = 256
MLA_NOPE = 128
MLA_ROPE = 64
MLA_V = 128

RWKV_HEADS = 8
RWKV_HEAD = 64
RWKV_W = RWKV_HEADS * RWKV_HEAD
DECAY_LORA = 32
ICLR_LORA = 32
GATE_LORA = 96
RWKV_STREAM = 3 * RWKV_W + 2 * DECAY_LORA + 2 * ICLR_LORA + GATE_LORA
RWKV_GN_EPS = 64e-5
RWKV_CHUNK = 64

SWA_HEADS = 8
SWA_KV_HEADS = 2
SWA_GROUP = SWA_HEADS // SWA_KV_HEADS
SWA_HEAD = 64
WINDOW = 128

DIFF_HEADS = 4
DIFF_HEAD = 64

N_BRANCH = 4
BRANCH_W = 512

N_EXPERTS = 64
TOP_K = 6
N_GROUPS = 8
TOPK_GROUPS = 4
GROUP_SIZE = N_EXPERTS // N_GROUPS
ROUTED_SCALE = 2.5
MOE_BLK = 128

LANE = 128
SUBLANE = 8

SEG_MQ = 0
SEG_MKV = 384
SEG_RW = 768
SEG_SW = 2560
SEG_DF = 3328
SMALL_COLS = 5120


def _tile(n, target, mult=SUBLANE):
    best = None
    for d in range(mult, min(n, target) + 1, mult):
        if n % d == 0:
            best = d
    assert best is not None, (n, target, mult)
    return best


def _cparams(sem, vmem_mb=48):
    return pltpu.CompilerParams(dimension_semantics=sem, vmem_limit_bytes=vmem_mb * 1024 * 1024)


def _sigmoid(x):
    return 1.0 / (1.0 + jnp.exp(-x))


def _ln(x, eps):
    mu = jnp.mean(x, -1, keepdims=True)
    xc = x - mu
    var = jnp.mean(xc * xc, -1, keepdims=True)
    return xc * lax.rsqrt(var + eps)


def _rms(x, eps):
    return x * lax.rsqrt(jnp.mean(x * x, -1, keepdims=True) + eps)


def _row_is_ctx(n_rows, row0, ctx):
    rows = row0 + lax.broadcasted_iota(jnp.int32, (n_rows, 1), 0)
    return rows < ctx


def _modulated(x, ml, mc, sh_i, sc_i, isc):
    sh = jnp.where(isc, mc[sh_i:sh_i + 1], ml[sh_i:sh_i + 1])
    sc = jnp.where(isc, mc[sc_i:sc_i + 1], ml[sc_i:sc_i + 1])
    return _ln(x, ADALN_EPS) * (1.0 + sc) + sh


def _swap32(x):
    lane = lax.broadcasted_iota(jnp.int32, x.shape, 1)
    up = pltpu.roll(x, LANE - 32, 1)
    dn = pltpu.roll(x, 32, 1)
    return jnp.where((lane % 64) < 32, up, dn)


def _rope(x, cosf, sinf):
    return x * cosf + _swap32(x) * sinf


def _dot(a, b, **kw):
    return jnp.dot(a, b, preferred_element_type=F32, **kw)


def _dot_nt(a, b, **kw):
    return lax.dot_general(a, b, (((1,), (1,)), ((), ())), preferred_element_type=F32, **kw)


def _dot_tn(a, b, **kw):
    return lax.dot_general(a, b, (((0,), (0,)), ((), ())), preferred_element_type=F32, **kw)


def _mm_kernel(x_ref, w_ref, b_ref, o_ref):
    o_ref[...] = _dot(x_ref[...], w_ref[...], precision=HIGHEST) + b_ref[...]


def _matmul_bias(x, w, b):
    m, k = x.shape
    n = w.shape[1]
    tn = _tile(n, 1024, LANE)
    return pl.pallas_call(
        _mm_kernel,
        grid=(n // tn,),
        in_specs=[pl.BlockSpec((m, k), lambda j: (0, 0)),
                  pl.BlockSpec((k, tn), lambda j: (0, j)),
                  pl.BlockSpec((1, tn), lambda j: (0, j))],
        out_specs=pl.BlockSpec((m, tn), lambda j: (0, j)),
        out_shape=jax.ShapeDtypeStruct((m, n), F32),
        compiler_params=_cparams(("arbitrary",)),
        name="mod_matmul",
    )(x, w, b.reshape(1, n))


def _inproj_kernel(x_ref, ml_ref, mc_ref, w_ref, o_ref, xn_ref, *, tm, ctx):
    i = pl.program_id(1)
    j = pl.program_id(2)

    @pl.when(j == 0)
    def _():
        isc = _row_is_ctx(tm, i * tm, ctx)
        xn_ref[...] = _modulated(x_ref[0], ml_ref[0], mc_ref[0], 0, 1, isc).astype(BF16)

    o_ref[0] = _dot(xn_ref[...], w_ref[...])


def _inproj(x_all, modl, modc, w_small, ctx):
    b, ta, d = x_all.shape
    n = w_small.shape[1]
    tm = _tile(ta, 544)
    tn = _tile(n, 1024, LANE)
    return pl.pallas_call(
        functools.partial(_inproj_kernel, tm=tm, ctx=ctx),
        grid=(b, ta // tm, n // tn),
        in_specs=[pl.BlockSpec((1, tm, d), lambda bb, i, j: (bb, i, 0)),
                  pl.BlockSpec((1, 6, d), lambda bb, i, j: (bb, 0, 0)),
                  pl.BlockSpec((1, 6, d), lambda bb, i, j: (0, 0, 0)),
                  pl.BlockSpec((d, tn), lambda bb, i, j: (0, j))],
        out_specs=pl.BlockSpec((1, tm, tn), lambda bb, i, j: (bb, i, j)),
        out_shape=jax.ShapeDtypeStruct((b, ta, n), F32),
        scratch_shapes=[pltpu.VMEM((tm, d), BF16)],
        compiler_params=_cparams(("parallel", "parallel", "arbitrary")),
        name="inproj",
    )(x_all, modl, modc, w_small)


def _mla_prep_kernel(mq_ref, mkv_ref, qg_ref, kg_ref, wq_ref, wkv_ref, cos_ref, sin_ref,
                     q_ref, k_ref, v_ref):
    mq = mq_ref[0]
    mkv = mkv_ref[0]
    cosv = cos_ref[...]
    sinv = sin_ref[...]
    qn = _rms(mq, 1e-6) * qg_ref[...]
    q = _dot(qn.astype(BF16), wq_ref[...])
    kvn = _rms(mkv[:, :MLA_KV_RANK], 1e-6) * kg_ref[...]
    kv = _dot(kvn.astype(BF16), wkv_ref[...])
    kr = _rope(mkv[:, MLA_KV_RANK:MLA_KV_RANK + LANE], cosv, sinv).astype(BF16)
    scale = (MLA_NOPE + MLA_ROPE) ** -0.5
    for h in range(MLA_HEADS):
        c0 = h * 256
        q_ref[0, :, c0:c0 + 128] = (q[:, c0:c0 + 128] * scale).astype(BF16)
        q_ref[0, :, c0 + 128:c0 + 256] = (_rope(q[:, c0 + 128:c0 + 256], cosv, sinv) * scale).astype(BF16)
        k_ref[0, :, c0:c0 + 128] = kv[:, c0:c0 + 128].astype(BF16)
        k_ref[0, :, c0 + 128:c0 + 256] = kr
        v_ref[0, :, h * 128:(h + 1) * 128] = kv[:, c0 + 128:c0 + 256].astype(BF16)


def _mla_prep(p, qg, kg, wq, wkv, cosf, sinf):
    b, ta, _ = p.shape
    tm = _tile(ta, 544)
    return pl.pallas_call(
        _mla_prep_kernel,
        grid=(b, ta // tm),
        in_specs=[pl.BlockSpec((1, tm, 384), lambda bb, i: (bb, i, SEG_MQ // 384)),
                  pl.BlockSpec((1, tm, 384), lambda bb, i: (bb, i, SEG_MKV // 384)),
                  pl.BlockSpec((1, MLA_Q_RANK), lambda bb, i: (0, 0)),
                  pl.BlockSpec((1, MLA_KV_RANK), lambda bb, i: (0, 0)),
                  pl.BlockSpec((MLA_Q_RANK, 1024), lambda bb, i: (0, 0)),
                  pl.BlockSpec((MLA_KV_RANK, 1024), lambda bb, i: (0, 0)),
                  pl.BlockSpec((tm, LANE), lambda bb, i: (i, 0)),
                  pl.BlockSpec((tm, LANE), lambda bb, i: (i, 0))],
        out_specs=[pl.BlockSpec((1, tm, 1024), lambda bb, i: (bb, i, 0)),
                   pl.BlockSpec((1, tm, 1024), lambda bb, i: (bb, i, 0)),
                   pl.BlockSpec((1, tm, 512), lambda bb, i: (bb, i, 0))],
        out_shape=[jax.ShapeDtypeStruct((b, ta, 1024), BF16),
                   jax.ShapeDtypeStruct((b, ta, 1024), BF16),
                   jax.ShapeDtypeStruct((b, ta, 512), BF16)],
        compiler_params=_cparams(("parallel", "parallel")),
        name="mla_prep",
    )(p, p, qg, kg, wq, wkv, cosf, sinf)


def _softmax_pv(s, v):
    m = jnp.max(s, -1, keepdims=True)
    p = jnp.exp(s - m)
    l = jnp.sum(p, -1, keepdims=True)
    return _dot(p.astype(BF16), v) / l


def _mla_attn_kernel(q_ref, k_ref, v_ref, o_ref, *, n_ctx_tiles, ctx):
    qi = pl.program_id(2)

    def run(nk):
        s = _dot_nt(q_ref[0], k_ref[0, :nk, :])
        o_ref[0] = _softmax_pv(s, v_ref[0, :nk, :]).astype(o_ref.dtype)

    @pl.when(qi < n_ctx_tiles)
    def _():
        run(ctx)

    @pl.when(qi >= n_ctx_tiles)
    def _():
        run(k_ref.shape[1])


def _mla_attn(q, k, v, ctx):
    b, ta, _ = q.shape
    tq = _tile(math.gcd(ta, ctx), 256)
    return pl.pallas_call(
        functools.partial(_mla_attn_kernel, n_ctx_tiles=ctx // tq, ctx=ctx),
        grid=(b, MLA_HEADS, ta // tq),
        in_specs=[pl.BlockSpec((1, tq, 256), lambda bb, h, i: (bb, i, h)),
                  pl.BlockSpec((1, ta, 256), lambda bb, h, i: (bb, 0, h)),
                  pl.BlockSpec((1, ta, 128), lambda bb, h, i: (bb, 0, h))],
        out_specs=pl.BlockSpec((1, tq, 128), lambda bb, h, i: (bb, i, h)),
        out_shape=jax.ShapeDtypeStruct((b, ta, 512), BF16),
        compiler_params=_cparams(("parallel", "parallel", "arbitrary")),
        name="mla_attn",
    )(q, k, v)


def _diff_attn_kernel(q_ref, k_ref, v_ref, cq_ref, sq_ref, ck_ref, sk_ref, lam_ref, sub_ref,
                      o_ref, ks_ref, vs_ref, *, n_ctx_tiles, ctx, post_scale):
    qi = pl.program_id(2)

    @pl.when(qi == 0)
    def _():
        ks_ref[...] = _rope(k_ref[0], ck_ref[...], sk_ref[...]).astype(BF16)
        vs_ref[...] = v_ref[0].astype(BF16)

    def run(nk):
        q = _rope(q_ref[0], cq_ref[...], sq_ref[...]) * (DIFF_HEAD ** -0.5)
        lane = lax.broadcasted_iota(jnp.int32, q.shape, 1)
        q1 = jnp.where(lane < 64, q, 0.0).astype(BF16)
        q2 = jnp.where(lane >= 64, q, 0.0).astype(BF16)
        k = ks_ref[:nk, :]
        v = vs_ref[:nk, :]
        o = _softmax_pv(_dot_nt(q1, k), v) - lam_ref[...] * _softmax_pv(_dot_nt(q2, k), v)
        o = _rms(o, 1e-5) * sub_ref[...] * post_scale
        o_ref[0] = o.astype(o_ref.dtype)

    @pl.when(qi < n_ctx_tiles)
    def _():
        run(ctx)

    @pl.when(qi >= n_ctx_tiles)
    def _():
        run(ks_ref.shape[0])


def _diff_attn(p, cosf, sinf, lamv, subln, ctx, lam_init):
    b, ta, _ = p.shape
    tq = _tile(math.gcd(ta, ctx), 256)
    cb = SEG_DF // LANE
    return pl.pallas_call(
        functools.partial(_diff_attn_kernel, n_ctx_tiles=ctx // tq, ctx=ctx, post_scale=1.0 - lam_init),
        grid=(b, DIFF_HEADS, ta // tq),
        in_specs=[pl.BlockSpec((1, tq, LANE), lambda bb, h, i: (bb, i, cb + h)),
                  pl.BlockSpec((1, ta, LANE), lambda bb, h, i: (bb, 0, cb + DIFF_HEADS + h)),
                  pl.BlockSpec((1, ta, LANE), lambda bb, h, i: (bb, 0, cb + 2 * DIFF_HEADS + h)),
                  pl.BlockSpec((tq, LANE), lambda bb, h, i: (i, 0)),
                  pl.BlockSpec((tq, LANE), lambda bb, h, i: (i, 0)),
                  pl.BlockSpec((ta, LANE), lambda bb, h, i: (0, 0)),
                  pl.BlockSpec((ta, LANE), lambda bb, h, i: (0, 0)),
                  pl.BlockSpec((1, LANE), lambda bb, h, i: (0, 0)),
                  pl.BlockSpec((1, LANE), lambda bb, h, i: (0, 0))],
        out_specs=pl.BlockSpec((1, tq, LANE), lambda bb, h, i: (bb, i, h)),
        out_shape=jax.ShapeDtypeStruct((b, ta, 512), BF16),
        scratch_shapes=[pltpu.VMEM((ta, LANE), BF16), pltpu.VMEM((ta, LANE), BF16)],
        compiler_params=_cparams(("parallel", "parallel", "arbitrary")),
        name="diff_attn",
    )(p, p, p, cosf, sinf, cosf, sinf, lamv, subln)


def _swa_kernel(q_ref, k_ref, v_ref, cq_ref, sq_ref, ck_ref, sk_ref, sink_ref,
                o_ref, ks_ref, vs_ref, *, ctx):
    t = pl.program_id(1)
    n_ctx_tiles = ctx // WINDOW
    ta = ks_ref.shape[0]
    span = 3 * WINDOW

    @pl.when(t == 0)
    def _():
        ks_ref[...] = _rope(k_ref[0], ck_ref[...], sk_ref[...]).astype(BF16)
        vs_ref[...] = v_ref[0].astype(BF16)

    lane = lax.broadcasted_iota(jnp.int32, (WINDOW, LANE), 1)
    cq = cq_ref[...]
    sq = sq_ref[...]
    qblk = [_rope(q_ref[0, :, hb * LANE:(hb + 1) * LANE], cq, sq) * (SWA_HEAD ** -0.5)
            for hb in range(SWA_HEADS // 2)]

    def head_q(hk, g):
        h = hk * SWA_GROUP + g
        blk = qblk[h // 2]
        if h % 2 != hk:
            blk = pltpu.roll(blk, 64, 1)
        keep = (lane >= 64) if hk == 1 else (lane < 64)
        return jnp.where(keep, blk, 0.0)

    def run(local):
        kc = ks_ref[:ctx, :]
        vc = vs_ref[:ctx, :]
        if local:
            n = t - n_ctx_tiles
            ws = jnp.clip(ctx + (n - 1) * WINDOW, ctx, ta - span)
            ws = pl.multiple_of(ws, WINDOW)
            kl = ks_ref[pl.ds(ws, span), :]
            vl = vs_ref[pl.ds(ws, span), :]
            kpos = (ws - ctx) + lax.broadcasted_iota(jnp.int32, (1, span), 1)
            qpos = n * WINDOW + lax.broadcasted_iota(jnp.int32, (SWA_GROUP * WINDOW, 1), 0) % WINDOW
            valid = jnp.abs(kpos - qpos) <= WINDOW
        outs = {}
        for hk in range(SWA_KV_HEADS):
            qst = jnp.concatenate([head_q(hk, g) for g in range(SWA_GROUP)], axis=0).astype(BF16)
            sink = jnp.concatenate(
                [jnp.broadcast_to(sink_ref[hk * SWA_GROUP + g:hk * SWA_GROUP + g + 1, 0:1], (WINDOW, 1))
                 for g in range(SWA_GROUP)], axis=0)
            s_c = _dot_nt(qst, kc)
            m = jnp.maximum(jnp.max(s_c, -1, keepdims=True), sink)
            if local:
                s_l = jnp.where(valid, _dot_nt(qst, kl), NEG_INF)
                m = jnp.maximum(m, jnp.max(s_l, -1, keepdims=True))
            p_c = jnp.exp(s_c - m)
            l = jnp.sum(p_c, -1, keepdims=True) + jnp.exp(sink - m)
            acc = _dot(p_c.astype(BF16), vc)
            if local:
                p_l = jnp.exp(s_l - m)
                l = l + jnp.sum(p_l, -1, keepdims=True)
                acc = acc + _dot(p_l.astype(BF16), vl)
            o = acc / l
            for g in range(SWA_GROUP):
                res = o[g * WINDOW:(g + 1) * WINDOW, :]
                if g % 2 != hk:
                    res = pltpu.roll(res, 64, 1)
                outs[hk * SWA_GROUP + g] = res
        for hb in range(SWA_HEADS // 2):
            blk = jnp.where(lane < 64, outs[2 * hb], outs[2 * hb + 1])
            o_ref[0, :, hb * LANE:(hb + 1) * LANE] = blk.astype(o_ref.dtype)

    @pl.when(t < n_ctx_tiles)
    def _():
        run(False)

    @pl.when(t >= n_ctx_tiles)
    def _():
        run(True)


def _swa(p, cosf, sinf, sinkv, ctx):
    b, ta, _ = p.shape
    qb = SEG_SW // 512
    kb = (SEG_SW + 512) // LANE
    return pl.pallas_call(
        functools.partial(_swa_kernel, ctx=ctx),
        grid=(b, ta // WINDOW),
        in_specs=[pl.BlockSpec((1, WINDOW, 512), lambda bb, t: (bb, t, qb)),
                  pl.BlockSpec((1, ta, LANE), lambda bb, t: (bb, 0, kb)),
                  pl.BlockSpec((1, ta, LANE), lambda bb, t: (bb, 0, kb + 1)),
                  pl.BlockSpec((WINDOW, LANE), lambda bb, t: (t, 0)),
                  pl.BlockSpec((WINDOW, LANE), lambda bb, t: (t, 0)),
                  pl.BlockSpec((ta, LANE), lambda bb, t: (0, 0)),
                  pl.BlockSpec((ta, LANE), lambda bb, t: (0, 0)),
                  pl.BlockSpec((SWA_HEADS, LANE), lambda bb, t: (0, 0))],
        out_specs=pl.BlockSpec((1, WINDOW, 512), lambda bb, t: (bb, t, 0)),
        out_shape=jax.ShapeDtypeStruct((b, ta, 512), BF16),
        scratch_shapes=[pltpu.VMEM((ta, LANE), BF16), pltpu.VMEM((ta, LANE), BF16)],
        compiler_params=_cparams(("parallel", "arbitrary")),
        name="swa_attn",
    )(p, p, p, cosf, sinf, cosf, sinf, sinkv)


def _rwkv_scan_kernel(r_ref, ld_ref, k_ref, v_ref, kk_ref, a_ref, y_ref, st_ref):
    c = pl.program_id(1)
    C = RWKV_CHUNK

    @pl.when(c == 0)
    def _():
        st_ref[...] = jnp.zeros_like(st_ref)

    row = lax.broadcasted_iota(jnp.int32, (C, C), 0)
    col = lax.broadcasted_iota(jnp.int32, (C, C), 1)
    incl = row >= col
    strict = row > col
    lower_ones = incl.astype(F32)
    eye = (row == col).astype(F32)
    hp = dict(precision=HIGHEST)

    for h in range(RWKV_HEADS):
        r = r_ref[0, h]
        ld = ld_ref[0, h]
        k = k_ref[0, h]
        v = v_ref[0, h]
        kk = kk_ref[0, h]
        a = a_ref[0, h]
        cs = _dot(lower_ones, ld, **hp)
        e_cs = jnp.exp(cs)
        e_inv = jnp.exp(-cs)
        kkd = kk * jnp.exp(cs - ld)
        rd = r * e_cs
        bi = kk * a * e_inv
        ki = k * e_inv
        lhs = jnp.concatenate([kkd, rd], axis=0)
        gb = _dot_nt(lhs, bi, **hp)
        gk = _dot_nt(lhs, ki, **hp)
        a_b = jnp.where(strict, gb[:C], 0.0)
        q_b = jnp.where(incl, gb[C:], 0.0)
        a_k = jnp.where(strict, gk[:C], 0.0)
        q_k = jnp.where(incl, gk[C:], 0.0)
        pw = -a_b
        tinv = eye + pw
        for _ in range(5):
            pw = _dot(pw, pw, **hp)
            tinv = tinv + _dot(tinv, pw, **hp)
        s0 = st_ref[h]
        x1 = _dot_nt(lhs, s0, **hp)
        av = _dot(jnp.concatenate([a_k, q_k], axis=0), v, **hp)
        sa = -_dot(tinv, x1[:C] + av[:C], **hp)
        y_ref[0, h] = x1[C:] + av[C:] + _dot(q_b, sa, **hp)
        upd = _dot_tn(jnp.concatenate([sa, v], axis=0), jnp.concatenate([bi, ki], axis=0), **hp)
        st_ref[h] = (s0 + upd) * e_cs[C - 1:C, :]


def _rwkv_scan(r, ld, k, v, kk, a):
    g, nh, ta, hd = r.shape
    spec = pl.BlockSpec((1, nh, RWKV_CHUNK, hd), lambda gg, c: (gg, 0, c, 0))
    return pl.pallas_call(
        _rwkv_scan_kernel,
        grid=(g, ta // RWKV_CHUNK),
        in_specs=[spec] * 6,
        out_specs=spec,
        out_shape=jax.ShapeDtypeStruct((g, nh, ta, hd), F32),
        scratch_shapes=[pltpu.VMEM((nh, hd, hd), F32)],
        compiler_params=_cparams(("parallel", "arbitrary")),
        name="rwkv_scan",
    )(r, ld, k, v, kk, a)


def _centred_shift(z):
    zp = jnp.pad(z, ((0, 0), (1, 1), (0, 0)))
    return 0.5 * (zp[:, :-2] + zp[:, 2:])


def _rwkv_mixer(seg, ctx, mu, w0, w_lora, a0, a_lora, g_lora, k_k, k_a, r_k, ln_g, ln_b):
    b, ta, _ = seg.shape
    shifted = jnp.concatenate([_centred_shift(seg[:, :ctx]), _centred_shift(seg[:, ctx:])], axis=1)
    seg = seg + (shifted - seg) * mu
    r, k, v, wl, al, gl = jnp.split(
        seg, [RWKV_W, 2 * RWKV_W, 3 * RWKV_W, 3 * RWKV_W + 2 * DECAY_LORA,
              3 * RWKV_W + 2 * DECAY_LORA + 2 * ICLR_LORA], axis=-1)
    w = w0 + jnp.einsum('btdr,drc->btdc', jnp.tanh(wl.reshape(b, ta, 2, DECAY_LORA)), w_lora,
                        precision=HIGHEST)
    w = -jax.nn.softplus(-w) - 0.5
    ld = -jnp.exp(w)
    a = jax.nn.sigmoid(a0 + jnp.einsum('btdr,drc->btdc', al.reshape(b, ta, 2, ICLR_LORA), a_lora,
                                       precision=HIGHEST))
    g = jnp.dot(jax.nn.sigmoid(gl), g_lora, precision=HIGHEST)
    kk = (k * k_k).reshape(b, ta, RWKV_HEADS, RWKV_HEAD)
    kk = (kk / jnp.maximum(jnp.linalg.norm(kk, axis=-1, keepdims=True), 1e-12)).reshape(b, ta, RWKV_W)
    k_dir = k[:, :, None, :] * (1 + (a - 1) * k_a)

    def seg_flip(z):
        return jnp.concatenate([z[:, :ctx][:, ::-1], z[:, ctx:][:, ::-1]], axis=1)

    def heads(z):
        return z.reshape(b, ta, RWKV_HEADS, RWKV_HEAD).transpose(0, 2, 1, 3)

    def both(shared):
        return jnp.concatenate([heads(shared), heads(seg_flip(shared))], axis=0)

    def per_dir(z):
        return jnp.concatenate([heads(z[:, :, 0]), heads(seg_flip(z[:, :, 1]))], axis=0)

    y = _rwkv_scan(both(r), per_dir(ld), per_dir(k_dir), both(v), both(kk), per_dir(a))
    y = y.transpose(0, 2, 1, 3)
    y = y[:b] + seg_flip(y[b:].reshape(b, ta, RWKV_W)).reshape(b, ta, RWKV_HEADS, RWKV_HEAD)
    mu_y = jnp.mean(y, -1, keepdims=True)
    var = jnp.mean(jnp.square(y - mu_y), -1, keepdims=True)
    yn = ((y - mu_y) * lax.rsqrt(var + RWKV_GN_EPS)).reshape(b, ta, RWKV_W) * ln_g + ln_b
    rh = r.reshape(b, ta, 1, RWKV_HEADS, RWKV_HEAD)
    kh = k_dir.reshape(b, ta, 2, RWKV_HEADS, RWKV_HEAD)
    vh = v.reshape(b, ta, RWKV_HEADS, RWKV_HEAD)
    bonus = (jnp.sum(rh * kh * r_k, axis=(2, 4))[..., None] * vh).reshape(b, ta, RWKV_W)
    return ((yn + bonus) * g).astype(BF16)


def _merge_a_kernel(x_ref, ml_ref, mc_ref, o4_ref, wg_ref, wb_ref, out_ref, xn_ref, acc_ref, *, tm, ctx):
    i = pl.program_id(1)
    j = pl.program_id(2)
    br = pl.program_id(3)

    @pl.when((j == 0) & (br == 0))
    def _():
        isc = _row_is_ctx(tm, i * tm, ctx)
        xn_ref[...] = _modulated(x_ref[0], ml_ref[0], mc_ref[0], 0, 1, isc).astype(BF16)

    val = _sigmoid(_dot(xn_ref[...], wg_ref[0])) * _dot(o4_ref[0, 0], wb_ref[0])

    @pl.when(br == 0)
    def _():
        acc_ref[...] = val

    @pl.when(br > 0)
    def _():
        acc_ref[...] += val

    @pl.when(br == N_BRANCH - 1)
    def _():
        out_ref[0] = acc_ref[...].astype(out_ref.dtype)


def _merge_a(x_all, modl, modc, o4, wg, wb, ctx):
    b, ta, d = x_all.shape
    tm = _tile(ta, 544)
    tn = _tile(d, 512, LANE)
    return pl.pallas_call(
        functools.partial(_merge_a_kernel, tm=tm, ctx=ctx),
        grid=(b, ta // tm, d // tn, N_BRANCH),
        in_specs=[pl.BlockSpec((1, tm, d), lambda bb, i, j, br: (bb, i, 0)),
                  pl.BlockSpec((1, 6, d), lambda bb, i, j, br: (bb, 0, 0)),
                  pl.BlockSpec((1, 6, d), lambda bb, i, j, br: (0, 0, 0)),
                  pl.BlockSpec((1, 1, tm, BRANCH_W), lambda bb, i, j, br: (br, bb, i, 0)),
                  pl.BlockSpec((1, d, tn), lambda bb, i, j, br: (br, 0, j)),
                  pl.BlockSpec((1, BRANCH_W, tn), lambda bb, i, j, br: (br, 0, j))],
        out_specs=pl.BlockSpec((1, tm, tn), lambda bb, i, j, br: (bb, i, j)),
        out_shape=jax.ShapeDtypeStruct((b, ta, d), BF16),
        scratch_shapes=[pltpu.VMEM((tm, d), BF16), pltpu.VMEM((tm, tn), F32)],
        compiler_params=_cparams(("parallel", "parallel", "arbitrary", "arbitrary")),
        name="merge_gate",
    )(x_all, modl, modc, o4, wg, wb)


def _merge_b_kernel(m_ref, w_ref, x_ref, ml_ref, mc_ref, lg_ref, lb_ref, xo_ref, h2_ref, *, tm, ctx, alpha):
    i = pl.program_id(1)
    isc = _row_is_ctx(tm, i * tm, ctx)
    ml = ml_ref[0]
    mc = mc_ref[0]
    y = _dot(m_ref[0], w_ref[...])
    g1 = jnp.where(isc, mc[2:3], ml[2:3])
    xn = _ln(alpha * x_ref[0] + g1 * y, POST_LN_EPS) * lg_ref[...] + lb_ref[...]
    xo_ref[0] = xn
    h2_ref[0] = _modulated(xn, ml, mc, 3, 4, isc).astype(h2_ref.dtype)


def _merge_b(merged, w_out, x_all, modl, modc, ln_g, ln_b, ctx, alpha):
    b, ta, d = x_all.shape
    tm = _tile(ta, 272)
    row = lambda bb, i: (bb, i, 0)
    return pl.pallas_call(
        functools.partial(_merge_b_kernel, tm=tm, ctx=ctx, alpha=alpha),
        grid=(b, ta // tm),
        in_specs=[pl.BlockSpec((1, tm, d), row),
                  pl.BlockSpec((d, d), lambda bb, i: (0, 0)),
                  pl.BlockSpec((1, tm, d), row),
                  pl.BlockSpec((1, 6, d), lambda bb, i: (bb, 0, 0)),
                  pl.BlockSpec((1, 6, d), lambda bb, i: (0, 0, 0)),
                  pl.BlockSpec((1, d), lambda bb, i: (0, 0)),
                  pl.BlockSpec((1, d), lambda bb, i: (0, 0))],
        out_specs=[pl.BlockSpec((1, tm, d), row), pl.BlockSpec((1, tm, d), row)],
        out_shape=[jax.ShapeDtypeStruct((b, ta, d), F32), jax.ShapeDtypeStruct((b, ta, d), BF16)],
        compiler_params=_cparams(("parallel", "parallel")),
        name="merge_out_ln",
    )(merged, w_out, x_all, modl, modc, ln_g.reshape(1, d), ln_b.reshape(1, d))


def _router_kernel(x_ref, ml_ref, mc_ref, rw_ref, rb_ref, idx_ref, wt_ref, *, tm, ctx):
    i = pl.program_id(1)
    isc = _row_is_ctx(tm, i * tm, ctx)
    h2 = _modulated(x_ref[0], ml_ref[0], mc_ref[0], 3, 4, isc)
    scores = _sigmoid(_dot(h2, rw_ref[...], precision=HIGHEST))
    biased = scores + rb_ref[...]
    lane = lax.broadcasted_iota(jnp.int32, scores.shape, 1)
    grp = lane // GROUP_SIZE
    ninf = -jnp.inf

    def first_argmax(z):
        m = jnp.max(z, -1, keepdims=True)
        idx = jnp.min(jnp.where(z == m, lane, N_EXPERTS), -1, keepdims=True)
        return m, idx

    gscore = jnp.zeros_like(biased)
    for g in range(N_GROUPS):
        zg = jnp.where(grp == g, biased, ninf)
        m1, i1 = first_argmax(zg)
        m2 = jnp.max(jnp.where(lane == i1, ninf, zg), -1, keepdims=True)
        gscore = jnp.where(grp == g, m1 + m2, gscore)
    cand = jnp.where(lane % GROUP_SIZE == 0, gscore, ninf)
    gsel = jnp.zeros(scores.shape, jnp.bool_)
    for _ in range(TOPK_GROUPS):
        _, ig = first_argmax(cand)
        gsel = gsel | (grp == ig // GROUP_SIZE)
        cand = jnp.where(lane == ig, ninf, cand)
    masked = jnp.where(gsel, biased, ninf)
    out_lane = lax.broadcasted_iota(jnp.int32, (tm, LANE), 1)
    idx_out = jnp.zeros((tm, LANE), jnp.int32)
    w_out = jnp.zeros((tm, LANE), F32)
    w_sum = jnp.zeros((tm, 1), F32)
    for kq in range(TOP_K):
        _, ie = first_argmax(masked)
        wk = jnp.sum(jnp.where(lane == ie, scores, 0.0), -1, keepdims=True)
        masked = jnp.where(lane == ie, ninf, masked)
        idx_out = jnp.where(out_lane == kq, ie, idx_out)
        w_out = jnp.where(out_lane == kq, wk, w_out)
        w_sum = w_sum + wk
    idx_ref[0] = idx_out
    wt_ref[0] = w_out / w_sum * ROUTED_SCALE


def _router(x_all, modl, modc, rw, rb, ctx):
    b, ta, d = x_all.shape
    tm = _tile(ta, 544)
    row = lambda bb, i: (bb, i, 0)
    return pl.pallas_call(
        functools.partial(_router_kernel, tm=tm, ctx=ctx),
        grid=(b, ta // tm),
        in_specs=[pl.BlockSpec((1, tm, d), row),
                  pl.BlockSpec((1, 6, d), lambda bb, i: (bb, 0, 0)),
                  pl.BlockSpec((1, 6, d), lambda bb, i: (0, 0, 0)),
                  pl.BlockSpec((d, N_EXPERTS), lambda bb, i: (0, 0)),
                  pl.BlockSpec((1, N_EXPERTS), lambda bb, i: (0, 0))],
        out_specs=[pl.BlockSpec((1, tm, LANE), row), pl.BlockSpec((1, tm, LANE), row)],
        out_shape=[jax.ShapeDtypeStruct((b, ta, LANE), jnp.int32), jax.ShapeDtypeStruct((b, ta, LANE), F32)],
        compiler_params=_cparams(("parallel", "parallel")),
        name="router",
    )(x_all, modl, modc, rw, rb.reshape(1, N_EXPERTS))


def _experts_kernel(be_ref, nu_ref, x_ref, wgu_ref, wdn_ref, o_ref, wgu_s, wdn_s, *, ff):
    i = pl.program_id(0)
    changed = (i == 0) | (be_ref[i] != be_ref[jnp.maximum(i - 1, 0)])

    @pl.when(changed)
    def _():
        wgu_s[...] = wgu_ref[0].astype(BF16)
        wdn_s[...] = wdn_ref[0].astype(BF16)

    @pl.when(i < nu_ref[0])
    def _():
        h = _dot(x_ref[...], wgu_s[...])
        g = h[:, :ff]
        act = (g * _sigmoid(g) * h[:, ff:]).astype(BF16)
        o_ref[...] = _dot(act, wdn_s[...])

    @pl.when(i >= nu_ref[0])
    def _():
        o_ref[...] = jnp.zeros_like(o_ref)


def _experts(xg, blk_e, n_used, w_gu, w_dn, blk):
    n_rows, d = xg.shape
    ff = w_dn.shape[1]
    n_blocks = n_rows // blk
    grid_spec = pltpu.PrefetchScalarGridSpec(
        num_scalar_prefetch=2,
        grid=(n_blocks,),
        in_specs=[pl.BlockSpec((blk, d), lambda i, be, nu: (i, 0)),
                  pl.BlockSpec((1, d, 2 * ff), lambda i, be, nu: (be[i], 0, 0)),
                  pl.BlockSpec((1, ff, d), lambda i, be, nu: (be[i], 0, 0))],
        out_specs=pl.BlockSpec((blk, d), lambda i, be, nu: (i, 0)),
        scratch_shapes=[pltpu.VMEM((d, 2 * ff), BF16), pltpu.VMEM((ff, d), BF16)],
    )
    return pl.pallas_call(
        functools.partial(_experts_kernel, ff=ff),
        grid_spec=grid_spec,
        out_shape=jax.ShapeDtypeStruct((n_rows, d), F32),
        compiler_params=_cparams(("arbitrary",)),
        name="moe_experts",
    )(blk_e, n_used, xg, w_gu, w_dn)


def _moe_plan(idx, blk):
    flat_e = idx.reshape(-1)
    n_assign = flat_e.shape[0]
    order = jnp.argsort(flat_e)
    sorted_e = flat_e[order]
    counts = jnp.bincount(flat_e, length=N_EXPERTS)
    padded = (counts + blk - 1) // blk * blk
    start_sorted = jnp.cumsum(counts) - counts
    pad_end = jnp.cumsum(padded)
    start_pad = pad_end - padded
    dest = (start_pad[sorted_e] + jnp.arange(n_assign) - start_sorted[sorted_e]).astype(jnp.int32)
    n_blocks = -(-(n_assign + N_EXPERTS * (blk - 1)) // blk)
    row_tok = jnp.zeros((n_blocks * blk,), jnp.int32).at[dest].set((order // TOP_K).astype(jnp.int32))
    pos = jnp.zeros((n_assign,), jnp.int32).at[order].set(dest)
    blk_e = jnp.minimum(jnp.searchsorted(pad_end, jnp.arange(n_blocks) * blk, side='right'),
                        N_EXPERTS - 1).astype(jnp.int32)
    n_used = (pad_end[-1] // blk).astype(jnp.int32).reshape(1)
    return row_tok, pos, blk_e, n_used


def _combine_kernel(yg_ref, wt_ref, h2_ref, sgu_ref, sdn_ref, x_ref, ml_ref, mc_ref, lg_ref, lb_ref,
                    xo_ref, *, tm, ctx, alpha, d, ff):
    i = pl.program_id(1)
    isc = _row_is_ctx(tm, i * tm, ctx)
    h = _dot(h2_ref[0], sgu_ref[...])
    g = h[:, :ff]
    f = _dot((g * _sigmoid(g) * h[:, ff:]).astype(BF16), sdn_ref[...])
    wt = wt_ref[0]
    for kq in range(TOP_K):
        f = f + wt[:, kq:kq + 1] * yg_ref[0, :, kq * d:(kq + 1) * d]
    g2 = jnp.where(isc, mc_ref[0][5:6], ml_ref[0][5:6])
    xo_ref[0] = _ln(alpha * x_ref[0] + g2 * f, POST_LN_EPS) * lg_ref[...] + lb_ref[...]


def _combine(yg, wts, h2, sh_gu, sh_dn, x_all, modl, modc, ln_g, ln_b, ctx, alpha):
    b, ta, d = x_all.shape
    ff = sh_dn.shape[0]
    tm = _tile(ta, 136)
    row = lambda bb, i: (bb, i, 0)
    return pl.pallas_call(
        functools.partial(_combine_kernel, tm=tm, ctx=ctx, alpha=alpha, d=d, ff=ff),
        grid=(b, ta // tm),
        in_specs=[pl.BlockSpec((1, tm, TOP_K * d), row),
                  pl.BlockSpec((1, tm, LANE), row),
                  pl.BlockSpec((1, tm, d), row),
                  pl.BlockSpec((d, 2 * ff), lambda bb, i: (0, 0)),
                  pl.BlockSpec((ff, d), lambda bb, i: (0, 0)),
                  pl.BlockSpec((1, tm, d), row),
                  pl.BlockSpec((1, 6, d), lambda bb, i: (bb, 0, 0)),
                  pl.BlockSpec((1, 6, d), lambda bb, i: (0, 0, 0)),
                  pl.BlockSpec((1, d), lambda bb, i: (0, 0)),
                  pl.BlockSpec((1, d), lambda bb, i: (0, 0))],
        out_specs=pl.BlockSpec((1, tm, d), row),
        out_shape=jax.ShapeDtypeStruct((b, ta, d), F32),
        compiler_params=_cparams(("parallel", "parallel")),
        name="moe_combine_ln",
    )(yg, wts, h2, sh_gu, sh_dn, x_all, modl, modc, ln_g.reshape(1, d), ln_b.reshape(1, d))


def _rope_tables(n_lat, ctx):
    rows = n_lat // GRID_W
    row = jnp.repeat(jnp.arange(rows, dtype=F32), GRID_W)
    col = jnp.tile(jnp.arange(GRID_W, dtype=F32), rows)
    n_freq = ROPE_DIM // 4
    inv_freq = ROPE_BASE ** (-jnp.arange(n_freq, dtype=F32) / n_freq)
    ang = jnp.concatenate([row[:, None] * inv_freq, col[:, None] * inv_freq], -1)
    cos, sin = jnp.cos(ang), jnp.sin(ang)
    cos = jnp.concatenate([jnp.ones((ctx, ROPE_DIM // 2), F32), cos], 0)
    sin = jnp.concatenate([jnp.zeros((ctx, ROPE_DIM // 2), F32), sin], 0)
    cosf = jnp.tile(jnp.concatenate([cos, cos], -1), (1, LANE // ROPE_DIM))
    sinf = jnp.tile(jnp.concatenate([-sin, sin], -1), (1, LANE // ROPE_DIM))
    return cosf, sinf


def _pack_small_weights(w_in_l):
    d = w_in_l.shape[0]
    offs = np.cumsum([0, MLA_Q_RANK, MLA_KV_RANK + MLA_ROPE, RWKV_STREAM,
                      (SWA_HEADS + 2 * SWA_KV_HEADS) * SWA_HEAD, 3 * DIFF_HEADS * 2 * DIFF_HEAD])
    starts = [SEG_MQ, SEG_MKV, SEG_RW, SEG_SW, SEG_DF, SMALL_COLS]
    parts = []
    for s in range(5):
        seg = w_in_l[:, int(offs[s]):int(offs[s + 1])]
        pad = starts[s + 1] - starts[s] - seg.shape[1]
        parts.append(seg)
        if pad:
            parts.append(jnp.zeros((d, pad), w_in_l.dtype))
    return jnp.concatenate(parts, axis=1).astype(BF16), int(offs[5])


def kernel(x, c, ctx, c_ctx, w_mod, b_mod, w_in, mla_q_norm, mla_w_qup, mla_kv_norm, mla_w_kvup, rwkv_mu, rwkv_w0, rwkv_w_lora, rwkv_a0, rwkv_a_lora, rwkv_g_lora, rwkv_k_k, rwkv_k_a, rwkv_r_k, rwkv_ln_g, rwkv_ln_b, swa_sink, diff_lambda, diff_subln, w_branch, w_out, ln1_g, ln1_b, router_w, router_bias, exp_w_gu, exp_w_dn, sh_w_gu, sh_w_dn, ln2_g, ln2_b):
    depth = w_mod.shape[0]
    b, n_lat, d = x.shape
    n_ctx = ctx.shape[1]
    ta = n_ctx + n_lat
    alpha = (2 * depth) ** 0.25
    cosf, sinf = _rope_tables(n_lat, n_ctx)
    x_all = jnp.concatenate([ctx, x], axis=1)

    cond = jnp.concatenate([c, c_ctx[None]], axis=0)
    cond = jnp.pad(jax.nn.silu(cond), ((0, SUBLANE - (b + 1) % SUBLANE), (0, 0)))

    for l in range(depth):
        mod = _matmul_bias(cond, w_mod[l], b_mod[l]).reshape(cond.shape[0], 6, d)
        modl, modc = mod[:b], mod[b:b + 1]

        w_small, gate_off = _pack_small_weights(w_in[l])
        p = _inproj(x_all, modl, modc, w_small, n_ctx)

        wq = mla_w_qup[l].reshape(MLA_Q_RANK, MLA_HEADS, MLA_NOPE + MLA_ROPE)
        wq = jnp.pad(wq, ((0, 0), (0, 0), (0, 256 - MLA_NOPE - MLA_ROPE))).reshape(MLA_Q_RANK, 1024)
        q_a, k_a, v_a = _mla_prep(p, mla_q_norm[l].reshape(1, -1), mla_kv_norm[l].reshape(1, -1),
                                  wq.astype(BF16), mla_w_kvup[l].astype(BF16), cosf, sinf)
        out_a = _mla_attn(q_a, k_a, v_a, n_ctx)

        out_b = _rwkv_mixer(p[:, :, SEG_RW:SEG_RW + RWKV_STREAM], n_ctx, rwkv_mu[l], rwkv_w0[l],
                            rwkv_w_lora[l], rwkv_a0[l], rwkv_a_lora[l], rwkv_g_lora[l], rwkv_k_k[l],
                            rwkv_k_a[l], rwkv_r_k[l], rwkv_ln_g[l], rwkv_ln_b[l])

        sinkv = jnp.broadcast_to(swa_sink[l][:, None], (SWA_HEADS, LANE))
        out_s = _swa(p, cosf, sinf, sinkv, n_ctx)

        lf = diff_lambda[l]
        lam_init = 0.8 - 0.6 * math.exp(-0.3 * l)
        lam = jnp.exp(jnp.sum(lf[0] * lf[1])) - jnp.exp(jnp.sum(lf[2] * lf[3])) + lam_init
        out_d = _diff_attn(p, cosf, sinf, jnp.full((1, LANE), lam, F32), diff_subln[l].reshape(1, LANE),
                           n_ctx, lam_init)

        o4 = jnp.stack([out_a, out_b, out_s, out_d], axis=0)
        wg = w_in[l][:, gate_off:].reshape(d, N_BRANCH, d).transpose(1, 0, 2).astype(BF16)
        merged = _merge_a(x_all, modl, modc, o4, wg, w_branch[l].astype(BF16), n_ctx)
        x_all, h2 = _merge_b(merged, w_out[l].astype(BF16), x_all, modl, modc, ln1_g[l], ln1_b[l], n_ctx, alpha)

        idx, wts = _router(x_all, modl, modc, router_w[l], router_bias[l], n_ctx)
        row_tok, pos, blk_e, n_used = _moe_plan(idx[..., :TOP_K], MOE_BLK)
        xg = jnp.take(h2.reshape(b * ta, d), row_tok, axis=0)
        y_rows = _experts(xg, blk_e, n_used, exp_w_gu[l], exp_w_dn[l], MOE_BLK)
        yg = jnp.take(y_rows, pos, axis=0).reshape(b, ta, TOP_K * d)
        x_all = _combine(yg, wts, h2, sh_w_gu[l].astype(BF16), sh_w_dn[l].astype(BF16), x_all, modl, modc,
                         ln2_g[l], ln2_b[l], n_ctx, alpha)
    return x_all[:, n_ctx:]
```

```python
import functools
import math

import jax
import jax.numpy as jnp
import numpy as np
from jax import lax
from jax.experimental import pallas as pl
from jax.experimental.pallas import tpu as pltpu

F32 = jnp.float32
BF16 = jnp.bfloat16
HIGHEST = lax.Precision.HIGHEST

GRID_W = 64
ROPE_BASE = 10000.0
ROPE_DIM = 64
ADALN_EPS = 1e-6
POST_LN_EPS = 1e-5
NEG_INF = -1e30

MLA_HEADS = 4
MLA_Q_RANK = 384
MLA_KV_RANK = 256
MLA_NOPE = 128
MLA_ROPE = 64
MLA_V = 128

RWKV_HEADS = 8
RWKV_HEAD = 64
RWKV_W = RWKV_HEADS * RWKV_HEAD
DECAY_LORA = 32
ICLR_LORA = 32
GATE_LORA = 96
RWKV_STREAM = 3 * RWKV_W + 2 * DECAY_LORA + 2 * ICLR_LORA + GATE_LORA
RWKV_GN_EPS = 64e-5
RWKV_CHUNK = 64

SWA_HEADS = 8
SWA_KV_HEADS = 2
SWA_GROUP = SWA_HEADS // SWA_KV_HEADS
SWA_HEAD = 64
WINDOW = 128

DIFF_HEADS = 4
DIFF_HEAD = 64

N_BRANCH = 4
BRANCH_W = 512

N_EXPERTS = 64
TOP_K = 6
N_GROUPS = 8
TOPK_GROUPS = 4
GROUP_SIZE = N_EXPERTS // N_GROUPS
ROUTED_SCALE = 2.5
MOE_BLK = 128

LANE = 128
SUBLANE = 8

SEG_SW = 0
SEG_MQ = 768
SEG_MKV = 1152
SEG_DF = 1536
SEG_RW = 3584
RW_COLS = 1792
SMALL_COLS = SEG_RW + RW_COLS


def _tile(n, target, mult=SUBLANE):
    best = None
    for d in range(mult, min(n, target) + 1, mult):
        if n % d == 0:
            best = d
    assert best is not None, (n, target, mult)
    return best


def _cparams(sem, vmem_mb=48):
    return pltpu.CompilerParams(dimension_semantics=sem, vmem_limit_bytes=vmem_mb * 1024 * 1024)


def _sigmoid(x):
    return 1.0 / (1.0 + jnp.exp(-x))


def _ln(x, eps):
    mu = jnp.mean(x, -1, keepdims=True)
    xc = x - mu
    var = jnp.mean(xc * xc, -1, keepdims=True)
    return xc * lax.rsqrt(var + eps)


def _rms(x, eps):
    return x * lax.rsqrt(jnp.mean(x * x, -1, keepdims=True) + eps)


def _row_is_ctx(n_rows, row0, ctx):
    rows = row0 + lax.broadcasted_iota(jnp.int32, (n_rows, 1), 0)
    return rows < ctx


def _modulated(x, ml, mc, sh_i, sc_i, isc):
    sh = jnp.where(isc, mc[sh_i:sh_i + 1], ml[sh_i:sh_i + 1])
    sc = jnp.where(isc, mc[sc_i:sc_i + 1], ml[sc_i:sc_i + 1])
    return _ln(x, ADALN_EPS) * (1.0 + sc) + sh


def _swap32(x):
    lane = lax.broadcasted_iota(jnp.int32, x.shape, 1)
    up = pltpu.roll(x, LANE - 32, 1)
    dn = pltpu.roll(x, 32, 1)
    return jnp.where((lane % 64) < 32, up, dn)


def _rope(x, cosf, sinf):
    return x * cosf + _swap32(x) * sinf


def _dot(a, b, **kw):
    return jnp.dot(a, b, preferred_element_type=F32, **kw)


def _dot_nt(a, b, **kw):
    return lax.dot_general(a, b, (((1,), (1,)), ((), ())), preferred_element_type=F32, **kw)


def _dot_tn(a, b, **kw):
    return lax.dot_general(a, b, (((0,), (0,)), ((), ())), preferred_element_type=F32, **kw)


def _mm_kernel(x_ref, w_ref, b_ref, o_ref):
    o_ref[...] = _dot(x_ref[...], w_ref[...], precision=HIGHEST) + b_ref[...]


def _matmul_bias(x, w, b):
    m, k = x.shape
    n = w.shape[1]
    tn = _tile(n, 1024, LANE)
    return pl.pallas_call(
        _mm_kernel,
        grid=(n // tn,),
        in_specs=[pl.BlockSpec((m, k), lambda j: (0, 0)),
                  pl.BlockSpec((k, tn), lambda j: (0, j)),
                  pl.BlockSpec((1, tn), lambda j: (0, j))],
        out_specs=pl.BlockSpec((m, tn), lambda j: (0, j)),
        out_shape=jax.ShapeDtypeStruct((m, n), F32),
        compiler_params=_cparams(("arbitrary",)),
        name="mod_matmul",
    )(x, w, b.reshape(1, n))


def _inproj_kernel(x_ref, ml_ref, mc_ref, w_ref, o_ref, xn_ref, *, tm, ctx):
    i = pl.program_id(1)
    j = pl.program_id(2)

    @pl.when(j == 0)
    def _():
        isc = _row_is_ctx(tm, i * tm, ctx)
        xn_ref[...] = _modulated(x_ref[0], ml_ref[0], mc_ref[0], 0, 1, isc).astype(BF16)

    o_ref[0] = _dot(xn_ref[...], w_ref[...])


def _inproj(x_all, modl, modc, w_small, ctx):
    b, ta, d = x_all.shape
    n = w_small.shape[1]
    tm = _tile(ta, 544)
    tn = _tile(n, 1024, LANE)
    return pl.pallas_call(
        functools.partial(_inproj_kernel, tm=tm, ctx=ctx),
        grid=(b, ta // tm, n // tn),
        in_specs=[pl.BlockSpec((1, tm, d), lambda bb, i, j: (bb, i, 0)),
                  pl.BlockSpec((1, 6, d), lambda bb, i, j: (bb, 0, 0)),
                  pl.BlockSpec((1, 6, d), lambda bb, i, j: (0, 0, 0)),
                  pl.BlockSpec((d, tn), lambda bb, i, j: (0, j))],
        out_specs=pl.BlockSpec((1, tm, tn), lambda bb, i, j: (bb, i, j)),
        out_shape=jax.ShapeDtypeStruct((b, ta, n), F32),
        scratch_shapes=[pltpu.VMEM((tm, d), BF16)],
        compiler_params=_cparams(("parallel", "parallel", "arbitrary")),
        name="inproj",
    )(x_all, modl, modc, w_small)


def _mla_prep_kernel(mq_ref, mkv_ref, qg_ref, kg_ref, wq_ref, wkv_ref, cos_ref, sin_ref,
                     q_ref, k_ref, v_ref):
    mq = mq_ref[0]
    mkv = mkv_ref[0]
    cosv = cos_ref[...]
    sinv = sin_ref[...]
    qn = _rms(mq, 1e-6) * qg_ref[...]
    q = _dot(qn.astype(BF16), wq_ref[...])
    kvn = _rms(mkv[:, :MLA_KV_RANK], 1e-6) * kg_ref[...]
    kv = _dot(kvn.astype(BF16), wkv_ref[...])
    kr = _rope(mkv[:, MLA_KV_RANK:MLA_KV_RANK + LANE], cosv, sinv).astype(BF16)
    scale = (MLA_NOPE + MLA_ROPE) ** -0.5
    for h in range(MLA_HEADS):
        c0 = h * 256
        q_ref[0, :, c0:c0 + 128] = (q[:, c0:c0 + 128] * scale).astype(BF16)
        q_ref[0, :, c0 + 128:c0 + 256] = (_rope(q[:, c0 + 128:c0 + 256], cosv, sinv) * scale).astype(BF16)
        k_ref[0, :, c0:c0 + 128] = kv[:, c0:c0 + 128].astype(BF16)
        k_ref[0, :, c0 + 128:c0 + 256] = kr
        v_ref[0, :, h * 128:(h + 1) * 128] = kv[:, c0 + 128:c0 + 256].astype(BF16)


def _mla_prep(p, qg, kg, wq, wkv, cosf, sinf):
    b, ta, _ = p.shape
    tm = _tile(ta, 544)
    return pl.pallas_call(
        _mla_prep_kernel,
        grid=(b, ta // tm),
        in_specs=[pl.BlockSpec((1, tm, 384), lambda bb, i: (bb, i, SEG_MQ // 384)),
                  pl.BlockSpec((1, tm, 384), lambda bb, i: (bb, i, SEG_MKV // 384)),
                  pl.BlockSpec((1, MLA_Q_RANK), lambda bb, i: (0, 0)),
                  pl.BlockSpec((1, MLA_KV_RANK), lambda bb, i: (0, 0)),
                  pl.BlockSpec((MLA_Q_RANK, 1024), lambda bb, i: (0, 0)),
                  pl.BlockSpec((MLA_KV_RANK, 1024), lambda bb, i: (0, 0)),
                  pl.BlockSpec((tm, LANE), lambda bb, i: (i, 0)),
                  pl.BlockSpec((tm, LANE), lambda bb, i: (i, 0))],
        out_specs=[pl.BlockSpec((1, tm, 1024), lambda bb, i: (bb, i, 0)),
                   pl.BlockSpec((1, tm, 1024), lambda bb, i: (bb, i, 0)),
                   pl.BlockSpec((1, tm, 512), lambda bb, i: (bb, i, 0))],
        out_shape=[jax.ShapeDtypeStruct((b, ta, 1024), BF16),
                   jax.ShapeDtypeStruct((b, ta, 1024), BF16),
                   jax.ShapeDtypeStruct((b, ta, 512), BF16)],
        compiler_params=_cparams(("parallel", "parallel")),
        name="mla_prep",
    )(p, p, qg, kg, wq, wkv, cosf, sinf)


def _softmax_pv(s, v):
    m = jnp.max(s, -1, keepdims=True)
    p = jnp.exp(s - m)
    l = jnp.sum(p, -1, keepdims=True)
    return _dot(p.astype(BF16), v) / l


def _mla_attn_kernel(q_ref, k_ref, v_ref, o_ref, *, n_ctx_tiles, ctx):
    qi = pl.program_id(2)

    def run(nk):
        s = _dot_nt(q_ref[0], k_ref[0, :nk, :])
        o_ref[0] = _softmax_pv(s, v_ref[0, :nk, :]).astype(o_ref.dtype)

    @pl.when(qi < n_ctx_tiles)
    def _():
        run(ctx)

    @pl.when(qi >= n_ctx_tiles)
    def _():
        run(k_ref.shape[1])


def _mla_attn(q, k, v, ctx):
    b, ta, _ = q.shape
    tq = _tile(math.gcd(ta, ctx), 256)
    return pl.pallas_call(
        functools.partial(_mla_attn_kernel, n_ctx_tiles=ctx // tq, ctx=ctx),
        grid=(b, MLA_HEADS, ta // tq),
        in_specs=[pl.BlockSpec((1, tq, 256), lambda bb, h, i: (bb, i, h)),
                  pl.BlockSpec((1, ta, 256), lambda bb, h, i: (bb, 0, h)),
                  pl.BlockSpec((1, ta, 128), lambda bb, h, i: (bb, 0, h))],
        out_specs=pl.BlockSpec((1, tq, 128), lambda bb, h, i: (bb, i, h)),
        out_shape=jax.ShapeDtypeStruct((b, ta, 512), BF16),
        compiler_params=_cparams(("parallel", "parallel", "arbitrary")),
        name="mla_attn",
    )(q, k, v)


def _diff_attn_kernel(q_ref, k_ref, v_ref, cq_ref, sq_ref, ck_ref, sk_ref, lam_ref, sub_ref,
                      o_ref, ks_ref, vs_ref, *, n_ctx_tiles, ctx, post_scale):
    qi = pl.program_id(2)

    @pl.when(qi == 0)
    def _():
        ks_ref[...] = _rope(k_ref[0], ck_ref[...], sk_ref[...]).astype(BF16)
        vs_ref[...] = v_ref[0].astype(BF16)

    def run(nk):
        q = _rope(q_ref[0], cq_ref[...], sq_ref[...]) * (DIFF_HEAD ** -0.5)
        lane = lax.broadcasted_iota(jnp.int32, q.shape, 1)
        q1 = jnp.where(lane < 64, q, 0.0).astype(BF16)
        q2 = jnp.where(lane >= 64, q, 0.0).astype(BF16)
        k = ks_ref[:nk, :]
        v = vs_ref[:nk, :]
        o = _softmax_pv(_dot_nt(q1, k), v) - lam_ref[...] * _softmax_pv(_dot_nt(q2, k), v)
        o = _rms(o, 1e-5) * sub_ref[...] * post_scale
        o_ref[0] = o.astype(o_ref.dtype)

    @pl.when(qi < n_ctx_tiles)
    def _():
        run(ctx)

    @pl.when(qi >= n_ctx_tiles)
    def _():
        run(ks_ref.shape[0])


def _diff_attn(p, cosf, sinf, lamv, subln, ctx, lam_init):
    b, ta, _ = p.shape
    tq = _tile(math.gcd(ta, ctx), 256)
    cb = SEG_DF // LANE
    return pl.pallas_call(
        functools.partial(_diff_attn_kernel, n_ctx_tiles=ctx // tq, ctx=ctx, post_scale=1.0 - lam_init),
        grid=(b, DIFF_HEADS, ta // tq),
        in_specs=[pl.BlockSpec((1, tq, LANE), lambda bb, h, i: (bb, i, cb + h)),
                  pl.BlockSpec((1, ta, LANE), lambda bb, h, i: (bb, 0, cb + DIFF_HEADS + h)),
                  pl.BlockSpec((1, ta, LANE), lambda bb, h, i: (bb, 0, cb + 2 * DIFF_HEADS + h)),
                  pl.BlockSpec((tq, LANE), lambda bb, h, i: (i, 0)),
                  pl.BlockSpec((tq, LANE), lambda bb, h, i: (i, 0)),
                  pl.BlockSpec((ta, LANE), lambda bb, h, i: (0, 0)),
                  pl.BlockSpec((ta, LANE), lambda bb, h, i: (0, 0)),
                  pl.BlockSpec((1, LANE), lambda bb, h, i: (0, 0)),
                  pl.BlockSpec((1, LANE), lambda bb, h, i: (0, 0))],
        out_specs=pl.BlockSpec((1, tq, LANE), lambda bb, h, i: (bb, i, h)),
        out_shape=jax.ShapeDtypeStruct((b, ta, 512), BF16),
        scratch_shapes=[pltpu.VMEM((ta, LANE), BF16), pltpu.VMEM((ta, LANE), BF16)],
        compiler_params=_cparams(("parallel", "parallel", "arbitrary")),
        name="diff_attn",
    )(p, p, p, cosf, sinf, cosf, sinf, lamv, subln)


def _swa_kernel(q_ref, k_ref, v_ref, cq_ref, sq_ref, ck_ref, sk_ref, sink_ref,
                o_ref, ks_ref, vs_ref, *, ctx):
    t = pl.program_id(1)
    n_ctx_tiles = ctx // WINDOW
    ta = ks_ref.shape[0]
    span = 3 * WINDOW

    @pl.when(t == 0)
    def _():
        ks_ref[...] = _rope(k_ref[0], ck_ref[...], sk_ref[...]).astype(BF16)
        vs_ref[...] = v_ref[0].astype(BF16)

    lane = lax.broadcasted_iota(jnp.int32, (WINDOW, LANE), 1)
    cq = cq_ref[...]
    sq = sq_ref[...]
    qblk = [_rope(q_ref[0, :, hb * LANE:(hb + 1) * LANE], cq, sq) * (SWA_HEAD ** -0.5)
            for hb in range(SWA_HEADS // 2)]

    def head_q(hk, g):
        h = hk * SWA_GROUP + g
        blk = qblk[h // 2]
        if h % 2 != hk:
            blk = pltpu.roll(blk, 64, 1)
        keep = (lane >= 64) if hk == 1 else (lane < 64)
        return jnp.where(keep, blk, 0.0)

    def run(local):
        kc = ks_ref[:ctx, :]
        vc = vs_ref[:ctx, :]
        if local:
            n = t - n_ctx_tiles
            ws = jnp.clip(ctx + (n - 1) * WINDOW, ctx, ta - span)
            ws = pl.multiple_of(ws, WINDOW)
            kl = ks_ref[pl.ds(ws, span), :]
            vl = vs_ref[pl.ds(ws, span), :]
            kpos = (ws - ctx) + lax.broadcasted_iota(jnp.int32, (1, span), 1)
            qpos = n * WINDOW + lax.broadcasted_iota(jnp.int32, (SWA_GROUP * WINDOW, 1), 0) % WINDOW
            valid = jnp.abs(kpos - qpos) <= WINDOW
        outs = {}
        for hk in range(SWA_KV_HEADS):
            qst = jnp.concatenate([head_q(hk, g) for g in range(SWA_GROUP)], axis=0).astype(BF16)
            sink = jnp.concatenate(
                [jnp.broadcast_to(sink_ref[hk * SWA_GROUP + g:hk * SWA_GROUP + g + 1, 0:1], (WINDOW, 1))
                 for g in range(SWA_GROUP)], axis=0)
            s_c = _dot_nt(qst, kc)
            m = jnp.maximum(jnp.max(s_c, -1, keepdims=True), sink)
            if local:
                s_l = jnp.where(valid, _dot_nt(qst, kl), NEG_INF)
                m = jnp.maximum(m, jnp.max(s_l, -1, keepdims=True))
            p_c = jnp.exp(s_c - m)
            l = jnp.sum(p_c, -1, keepdims=True) + jnp.exp(sink - m)
            acc = _dot(p_c.astype(BF16), vc)
            if local:
                p_l = jnp.exp(s_l - m)
                l = l + jnp.sum(p_l, -1, keepdims=True)
                acc = acc + _dot(p_l.astype(BF16), vl)
            o = acc / l
            for g in range(SWA_GROUP):
                res = o[g * WINDOW:(g + 1) * WINDOW, :]
                if g % 2 != hk:
                    res = pltpu.roll(res, 64, 1)
                outs[hk * SWA_GROUP + g] = res
        for hb in range(SWA_HEADS // 2):
            blk = jnp.where(lane < 64, outs[2 * hb], outs[2 * hb + 1])
            o_ref[0, :, hb * LANE:(hb + 1) * LANE] = blk.astype(o_ref.dtype)

    @pl.when(t < n_ctx_tiles)
    def _():
        run(False)

    @pl.when(t >= n_ctx_tiles)
    def _():
        run(True)


def _swa(p, cosf, sinf, sinkv, ctx):
    b, ta, _ = p.shape
    qb = SEG_SW // 512
    kb = (SEG_SW + 512) // LANE
    return pl.pallas_call(
        functools.partial(_swa_kernel, ctx=ctx),
        grid=(b, ta // WINDOW),
        in_specs=[pl.BlockSpec((1, WINDOW, 512), lambda bb, t: (bb, t, qb)),
                  pl.BlockSpec((1, ta, LANE), lambda bb, t: (bb, 0, kb)),
                  pl.BlockSpec((1, ta, LANE), lambda bb, t: (bb, 0, kb + 1)),
                  pl.BlockSpec((WINDOW, LANE), lambda bb, t: (t, 0)),
                  pl.BlockSpec((WINDOW, LANE), lambda bb, t: (t, 0)),
                  pl.BlockSpec((ta, LANE), lambda bb, t: (0, 0)),
                  pl.BlockSpec((ta, LANE), lambda bb, t: (0, 0)),
                  pl.BlockSpec((SWA_HEADS, LANE), lambda bb, t: (0, 0))],
        out_specs=pl.BlockSpec((1, WINDOW, 512), lambda bb, t: (bb, t, 0)),
        out_shape=jax.ShapeDtypeStruct((b, ta, 512), BF16),
        scratch_shapes=[pltpu.VMEM((ta, LANE), BF16), pltpu.VMEM((ta, LANE), BF16)],
        compiler_params=_cparams(("parallel", "arbitrary")),
        name="swa_attn",
    )(p, p, p, cosf, sinf, cosf, sinf, sinkv)


def _softplus(x):
    return jnp.maximum(x, 0.0) + jnp.log(1.0 + jnp.exp(-jnp.abs(x)))


def _rwkv_prep_kernel(z_ref, hp_ref, hn_ref, mu_ref, w0_ref, a0_ref, wl_ref, al_ref, gl_ref, kk_ref, ka_ref,
                      rk_ref, bd_ref, r_o, v_o, kk_o, ld0_o, ld1_o, kd0_o, kd1_o, a0_o, a1_o, bv_o, g_o, zs_ref):
    tm = z_ref.shape[1]
    w = RWKV_W
    z = z_ref[0]
    zs_ref[SUBLANE:SUBLANE + tm, :] = z
    zs_ref[SUBLANE - 1:SUBLANE, :] = hp_ref[0, 0]
    zs_ref[SUBLANE + tm:SUBLANE + tm + 1, :] = hn_ref[0, 0]
    shifted = 0.5 * (zs_ref[SUBLANE - 1:SUBLANE - 1 + tm, :] + zs_ref[SUBLANE + 1:SUBLANE + 1 + tm, :])
    seg = z + (shifted - z) * mu_ref[...]
    r = seg[:, 0:w]
    k = seg[:, w:2 * w]
    v = seg[:, 2 * w:3 * w]
    lo = seg[:, 3 * w:3 * w + LANE]
    gl = seg[:, 3 * w + LANE:3 * w + 2 * LANE]
    hp = dict(precision=HIGHEST)
    bd = bd_ref[...]
    kk = k * kk_ref[...]
    kk = kk / jnp.maximum(jnp.sqrt(_dot(kk * kk, bd, **hp)), 1e-12)
    th = jnp.tanh(lo)
    ksum = jnp.zeros_like(k)
    for d, (ld_o, kd_o, a_o) in enumerate(((ld0_o, kd0_o, a0_o), (ld1_o, kd1_o, a1_o))):
        wd = w0_ref[d:d + 1, :] + _dot(th, wl_ref[d], **hp)
        wd = -_softplus(-wd) - 0.5
        a = _sigmoid(a0_ref[d:d + 1, :] + _dot(lo, al_ref[d], **hp))
        kd = k * (1.0 + (a - 1.0) * ka_ref[...])
        ld_o[0] = -jnp.exp(wd)
        kd_o[0] = kd
        a_o[0] = a
        ksum = ksum + kd
    r_o[0] = r
    v_o[0] = v
    kk_o[0] = kk
    bv_o[0] = _dot(r * ksum * rk_ref[...], bd, **hp) * v
    g_o[0] = _dot(_sigmoid(gl), gl_ref[...], **hp)


def _rwkv_prep(p, hprev, hnext, mu, w0, a0, wl, al, gl, k_k, k_a, r_k, bd, tm):
    b, ta, _ = p.shape
    w = RWKV_W
    row = lambda bb, i: (bb, i, 0)
    vec = lambda n: pl.BlockSpec((1, n), lambda bb, i: (0, 0))
    out = jax.ShapeDtypeStruct((b, ta, w), F32)
    return pl.pallas_call(
        _rwkv_prep_kernel,
        grid=(b, ta // tm),
        in_specs=[pl.BlockSpec((1, tm, RW_COLS), lambda bb, i: (bb, i, SEG_RW // RW_COLS)),
                  pl.BlockSpec((1, 1, 1, RW_COLS), lambda bb, i: (bb, i, 0, 0)),
                  pl.BlockSpec((1, 1, 1, RW_COLS), lambda bb, i: (bb, i, 0, 0)),
                  vec(RW_COLS),
                  pl.BlockSpec((2, w), lambda bb, i: (0, 0)),
                  pl.BlockSpec((2, w), lambda bb, i: (0, 0)),
                  pl.BlockSpec((2, LANE, w), lambda bb, i: (0, 0, 0)),
                  pl.BlockSpec((2, LANE, w), lambda bb, i: (0, 0, 0)),
                  pl.BlockSpec((LANE, w), lambda bb, i: (0, 0)),
                  vec(w), vec(w), vec(w),
                  pl.BlockSpec((w, w), lambda bb, i: (0, 0))],
        out_specs=[pl.BlockSpec((1, tm, w), row)] * 11,
        out_shape=[out] * 11,
        scratch_shapes=[pltpu.VMEM((tm + 2 * SUBLANE, RW_COLS), F32)],
        compiler_params=_cparams(("parallel", "parallel")),
        name="rwkv_prep",
    )(p, hprev, hnext, mu, w0, a0, wl, al, gl, k_k, k_a, r_k, bd)


def _split3(x):
    hi = x.astype(BF16)
    r1 = x - hi.astype(F32)
    mid = r1.astype(BF16)
    return hi, mid, (r1 - mid.astype(F32)).astype(BF16)


def _split2(x):
    hi = x.astype(BF16)
    return hi, (x - hi.astype(F32)).astype(BF16)


def _mm3(a, b, dot=_dot):
    return dot(a[0], b[0]) + dot(a[0], b[1]) + dot(a[1], b[0])


def _rwkv_chunk_kernel(r_ref, v_ref, kk_ref, ld0_ref, ld1_ref, kd0_ref, kd1_ref, a0_ref, a1_ref, pz_ref, wu_ref):
    C = RWKV_CHUNK
    hd = RWKV_HEAD
    row = lax.broadcasted_iota(jnp.int32, (C, C), 0)
    col = lax.broadcasted_iota(jnp.int32, (C, C), 1)
    eye = (row == col).astype(F32)
    probs = []
    for d, (ld_ref, kd_ref, a_ref) in enumerate(((ld0_ref, kd0_ref, a0_ref), (ld1_ref, kd1_ref, a1_ref))):
        incl = (row >= col) if d == 0 else (row <= col)
        strict = (row > col) if d == 0 else (row < col)
        last = C - 1 if d == 0 else 0
        for h in range(RWKV_HEADS):
            sl = slice(h * hd, (h + 1) * hd)
            probs.append(dict(d=d, h=h, incl=incl, strict=strict, last=last,
                              ones=jnp.where(incl, 1.0, 0.0).astype(BF16),
                              r=r_ref[0, :, sl], v=v_ref[0, :, sl], kk=kk_ref[0, :, sl],
                              ld=ld_ref[0, :, sl], k=kd_ref[0, :, sl], a=a_ref[0, :, sl]))
    for q in probs:
        q["cs"] = sum(_dot(q["ones"], part) for part in _split3(q["ld"]))
    for q in probs:
        cs = q["cs"]
        e_cs = jnp.exp(cs)
        e_inv = jnp.exp(-cs)
        q["g_end"] = e_cs[q["last"]:q["last"] + 1, :]
        q["kkd"] = q["kk"] * jnp.exp(cs - q["ld"])
        q["rd"] = q["r"] * e_cs
        q["bi"] = _split2(q["kk"] * q["a"] * e_inv)
        q["ki"] = _split2(q["k"] * e_inv)
        q["lhs"] = _split2(jnp.concatenate([q["kkd"], q["rd"]], axis=0))
    for q in probs:
        gb = _mm3(q["lhs"], q["bi"], _dot_nt)
        gk = _mm3(q["lhs"], q["ki"], _dot_nt)
        q["pw"] = -jnp.where(q["strict"], gb[:C], 0.0)
        q["q_b"] = _split2(jnp.where(q["incl"], gb[C:], 0.0))
        q["akqk"] = _split2(jnp.concatenate([jnp.where(q["strict"], gk[:C], 0.0),
                                            jnp.where(q["incl"], gk[C:], 0.0)], axis=0))
        q["tinv"] = eye + q["pw"]
    for _ in range(5):
        for q in probs:
            pw = _split2(q["pw"])
            q["pw"] = _mm3(pw, pw)
        for q in probs:
            q["tinv"] = q["tinv"] + _mm3(_split2(q["tinv"]), _split2(q["pw"]))
    for q in probs:
        q["vs"] = _split2(q["v"])
        q["av"] = _mm3(q["akqk"], q["vs"])
    for q in probs:
        tinv = _split2(q["tinv"])
        q["w1"] = _mm3(tinv, _split2(q["kkd"]))
        q["u1"] = _mm3(tinv, _split2(q["av"][:C]))
    for q in probs:
        w1 = _split2(q["w1"])
        u1 = _split2(q["u1"])
        w2 = q["rd"] - _mm3(q["q_b"], w1)
        u2 = q["av"][C:] - _mm3(q["q_b"], u1)
        pt = (eye - _mm3(w1, q["bi"], _dot_tn)) * q["g_end"]
        zt = (_mm3(q["vs"], q["ki"], _dot_tn) - _mm3(u1, q["bi"], _dot_tn)) * q["g_end"]
        pz_ref[q["d"], 0, 0, q["h"]] = jnp.concatenate([pt, zt], axis=1)
        wu_ref[q["d"], 0, 0, q["h"]] = jnp.concatenate([w2, u2], axis=1)


def _rwkv_chunks(r, v, kk, ld0, ld1, kd0, kd1, a0, a1):
    b, ta, w = r.shape
    nc = ta // RWKV_CHUNK
    spec = pl.BlockSpec((1, RWKV_CHUNK, w), lambda bb, c: (bb, c, 0))
    ospec = pl.BlockSpec((2, 1, 1, RWKV_HEADS, RWKV_HEAD, 2 * RWKV_HEAD), lambda bb, c: (0, bb, c, 0, 0, 0))
    oshape = jax.ShapeDtypeStruct((2, b, nc, RWKV_HEADS, RWKV_HEAD, 2 * RWKV_HEAD), F32)
    return pl.pallas_call(
        _rwkv_chunk_kernel,
        grid=(b, nc),
        in_specs=[spec] * 9,
        out_specs=[ospec, ospec],
        out_shape=[oshape, oshape],
        compiler_params=_cparams(("parallel", "parallel")),
        name="rwkv_chunks",
    )(r, v, kk, ld0, ld1, kd0, kd1, a0, a1)


def _rwkv_state_kernel(pz0_ref, pz1_ref, wu0_ref, wu1_ref, y0_ref, y1_ref, st_ref):
    i = pl.program_id(0)
    hd = RWKV_HEAD
    nb = st_ref.shape[1]

    @pl.when(i == 0)
    def _():
        st_ref[...] = jnp.zeros_like(st_ref)

    probs = []
    for d, (pz_ref, wu_ref) in enumerate(((pz0_ref, wu0_ref), (pz1_ref, wu1_ref))):
        for bb in range(nb):
            for h in range(RWKV_HEADS):
                pz = pz_ref[0, bb, 0, h]
                wu = wu_ref[0, bb, 0, h]
                probs.append(dict(d=d, b=bb, h=h, pt=_split2(pz[:, :hd]), zt=pz[:, hd:],
                                  w2=_split2(wu[:, :hd]), u2=wu[:, hd:], s=_split2(st_ref[d, bb, h])))
    for q in probs:
        q["y"] = _mm3(q["w2"], q["s"], _dot_nt) + q["u2"]
    for q in probs:
        st_ref[q["d"], q["b"], q["h"]] = _mm3(q["s"], q["pt"]) + q["zt"]
    for d, y_ref in enumerate((y0_ref, y1_ref)):
        for bb in range(nb):
            ys = [q["y"] for q in probs if q["d"] == d and q["b"] == bb]
            for hp in range(RWKV_HEADS // 2):
                y_ref[bb, :, hp * LANE:(hp + 1) * LANE] = jnp.concatenate([ys[2 * hp], ys[2 * hp + 1]], axis=1)


def _rwkv_state(pz, wu, ctx):
    _, b, nc, nh, hd, _ = pz.shape
    ncc = ctx // RWKV_CHUNK

    def rev(i):
        return jnp.where(i < ncc, ncc - 1 - i, nc - 1 - (i - ncc))

    blk = (1, b, 1, nh, hd, 2 * hd)
    fwd = lambda i: (0, 0, i, 0, 0, 0)
    bwd = lambda i: (1, 0, rev(i), 0, 0, 0)
    yshape = jax.ShapeDtypeStruct((b, nc * RWKV_CHUNK, nh * hd), F32)
    return pl.pallas_call(
        _rwkv_state_kernel,
        grid=(nc,),
        in_specs=[pl.BlockSpec(blk, fwd), pl.BlockSpec(blk, bwd), pl.BlockSpec(blk, fwd), pl.BlockSpec(blk, bwd)],
        out_specs=[pl.BlockSpec((b, RWKV_CHUNK, nh * hd), lambda i: (0, i, 0)),
                   pl.BlockSpec((b, RWKV_CHUNK, nh * hd), lambda i: (0, rev(i), 0))],
        out_shape=[yshape, yshape],
        scratch_shapes=[pltpu.VMEM((2, b, nh, hd, hd), F32)],
        compiler_params=_cparams(("arbitrary",)),
        name="rwkv_state",
    )(pz, pz, wu, wu)


def _rwkv_out_kernel(y0_ref, y1_ref, bv_ref, g_ref, lg_ref, lb_ref, bd_ref, o_ref):
    hp = dict(precision=HIGHEST)
    bdm = bd_ref[...] * (1.0 / RWKV_HEAD)
    y = y0_ref[0] + y1_ref[0]
    yc = y - _dot(y, bdm, **hp)
    var = _dot(yc * yc, bdm, **hp)
    yn = yc * lax.rsqrt(var + RWKV_GN_EPS) * lg_ref[...] + lb_ref[...]
    o_ref[0] = ((yn + bv_ref[0]) * g_ref[0]).astype(o_ref.dtype)


def _rwkv_out(y0, y1, bv, g, ln_g, ln_b, bd, tm):
    b, ta, w = y0.shape
    row = pl.BlockSpec((1, tm, w), lambda bb, i: (bb, i, 0))
    vec = pl.BlockSpec((1, w), lambda bb, i: (0, 0))
    return pl.pallas_call(
        _rwkv_out_kernel,
        grid=(b, ta // tm),
        in_specs=[row, row, row, row, vec, vec, pl.BlockSpec((w, w), lambda bb, i: (0, 0))],
        out_specs=row,
        out_shape=jax.ShapeDtypeStruct((b, ta, w), BF16),
        compiler_params=_cparams(("parallel", "parallel")),
        name="rwkv_out",
    )(y0, y1, bv, g, ln_g, ln_b, bd)


def _rwkv_mixer(p, ctx, mu, w0, w_lora, a0, a_lora, g_lora, k_k, k_a, r_k, ln_g, ln_b):
    b, ta, _ = p.shape
    w = RWKV_W
    tm = _tile(math.gcd(ta, ctx), 256)
    nt = ta // tm
    edge_lo = p[:, 0::tm, SEG_RW:]
    edge_hi = p[:, tm - 1::tm, SEG_RW:]
    tile_id = jnp.arange(nt)[None, :, None]
    hprev = jnp.where((tile_id == 0) | (tile_id == ctx // tm), 0.0, jnp.roll(edge_hi, 1, axis=1))
    hnext = jnp.where((tile_id == nt - 1) | (tile_id == ctx // tm - 1), 0.0, jnp.roll(edge_lo, -1, axis=1))
    wl = jnp.zeros((2, LANE, w), F32)
    al = jnp.zeros((2, LANE, w), F32)
    for d in range(2):
        wl = wl.at[d, d * DECAY_LORA:(d + 1) * DECAY_LORA].set(w_lora[d])
        al = al.at[d, 2 * DECAY_LORA + d * ICLR_LORA:2 * DECAY_LORA + (d + 1) * ICLR_LORA].set(a_lora[d])
    gl = jnp.zeros((LANE, w), F32).at[:GATE_LORA].set(g_lora)
    mu_p = jnp.pad(mu, (0, RW_COLS - RWKV_STREAM)).reshape(1, RW_COLS)
    head_id = jnp.arange(w) // RWKV_HEAD
    bd = (head_id[:, None] == head_id[None, :]).astype(F32)
    r, v, kk, ld0, ld1, kd0, kd1, a0_, a1_, bv, g = _rwkv_prep(
        p, hprev[:, :, None, :], hnext[:, :, None, :], mu_p, w0, a0, wl, al, gl,
        k_k.reshape(1, w), k_a.reshape(1, w), r_k.reshape(1, w), bd, tm)
    pz, wu = _rwkv_chunks(r, v, kk, ld0, ld1, kd0, kd1, a0_, a1_)
    y0, y1 = _rwkv_state(pz, wu, ctx)
    return _rwkv_out(y0, y1, bv, g, ln_g.reshape(1, w), ln_b.reshape(1, w), bd, tm)


def _merge_a_kernel(x_ref, ml_ref, mc_ref, o4_ref, wg_ref, wb_ref, out_ref, xn_ref, acc_ref, *, tm, ctx):
    i = pl.program_id(1)
    j = pl.program_id(2)
    br = pl.program_id(3)

    @pl.when((j == 0) & (br == 0))
    def _():
        isc = _row_is_ctx(tm, i * tm, ctx)
        xn_ref[...] = _modulated(x_ref[0], ml_ref[0], mc_ref[0], 0, 1, isc).astype(BF16)

    val = _sigmoid(_dot(xn_ref[...], wg_ref[0])) * _dot(o4_ref[0, 0], wb_ref[0])

    @pl.when(br == 0)
    def _():
        acc_ref[...] = val

    @pl.when(br > 0)
    def _():
        acc_ref[...] += val

    @pl.when(br == N_BRANCH - 1)
    def _():
        out_ref[0] = acc_ref[...].astype(out_ref.dtype)


def _merge_a(x_all, modl, modc, o4, wg, wb, ctx):
    b, ta, d = x_all.shape
    tm = _tile(ta, 544)
    tn = _tile(d, 512, LANE)
    return pl.pallas_call(
        functools.partial(_merge_a_kernel, tm=tm, ctx=ctx),
        grid=(b, ta // tm, d // tn, N_BRANCH),
        in_specs=[pl.BlockSpec((1, tm, d), lambda bb, i, j, br: (bb, i, 0)),
                  pl.BlockSpec((1, 6, d), lambda bb, i, j, br: (bb, 0, 0)),
                  pl.BlockSpec((1, 6, d), lambda bb, i, j, br: (0, 0, 0)),
                  pl.BlockSpec((1, 1, tm, BRANCH_W), lambda bb, i, j, br: (br, bb, i, 0)),
                  pl.BlockSpec((1, d, tn), lambda bb, i, j, br: (br, 0, j)),
                  pl.BlockSpec((1, BRANCH_W, tn), lambda bb, i, j, br: (br, 0, j))],
        out_specs=pl.BlockSpec((1, tm, tn), lambda bb, i, j, br: (bb, i, j)),
        out_shape=jax.ShapeDtypeStruct((b, ta, d), BF16),
        scratch_shapes=[pltpu.VMEM((tm, d), BF16), pltpu.VMEM((tm, tn), F32)],
        compiler_params=_cparams(("parallel", "parallel", "arbitrary", "arbitrary")),
        name="merge_gate",
    )(x_all, modl, modc, o4, wg, wb)


def _merge_b_kernel(m_ref, w_ref, x_ref, ml_ref, mc_ref, lg_ref, lb_ref, xo_ref, h2_ref, *, tm, ctx, alpha):
    i = pl.program_id(1)
    isc = _row_is_ctx(tm, i * tm, ctx)
    ml = ml_ref[0]
    mc = mc_ref[0]
    y = _dot(m_ref[0], w_ref[...])
    g1 = jnp.where(isc, mc[2:3], ml[2:3])
    xn = _ln(alpha * x_ref[0] + g1 * y, POST_LN_EPS) * lg_ref[...] + lb_ref[...]
    xo_ref[0] = xn
    h2_ref[0] = _modulated(xn, ml, mc, 3, 4, isc).astype(h2_ref.dtype)


def _merge_b(merged, w_out, x_all, modl, modc, ln_g, ln_b, ctx, alpha):
    b, ta, d = x_all.shape
    tm = _tile(ta, 272)
    row = lambda bb, i: (bb, i, 0)
    return pl.pallas_call(
        functools.partial(_merge_b_kernel, tm=tm, ctx=ctx, alpha=alpha),
        grid=(b, ta // tm),
        in_specs=[pl.BlockSpec((1, tm, d), row),
                  pl.BlockSpec((d, d), lambda bb, i: (0, 0)),
                  pl.BlockSpec((1, tm, d), row),
                  pl.BlockSpec((1, 6, d), lambda bb, i: (bb, 0, 0)),
                  pl.BlockSpec((1, 6, d), lambda bb, i: (0, 0, 0)),
                  pl.BlockSpec((1, d), lambda bb, i: (0, 0)),
                  pl.BlockSpec((1, d), lambda bb, i: (0, 0))],
        out_specs=[pl.BlockSpec((1, tm, d), row), pl.BlockSpec((1, tm, d), row)],
        out_shape=[jax.ShapeDtypeStruct((b, ta, d), F32), jax.ShapeDtypeStruct((b, ta, d), BF16)],
        compiler_params=_cparams(("parallel", "parallel")),
        name="merge_out_ln",
    )(merged, w_out, x_all, modl, modc, ln_g.reshape(1, d), ln_b.reshape(1, d))


def _router_kernel(x_ref, ml_ref, mc_ref, rw_ref, rb_ref, idx_ref, wt_ref, *, tm, ctx):
    i = pl.program_id(1)
    isc = _row_is_ctx(tm, i * tm, ctx)
    h2 = _modulated(x_ref[0], ml_ref[0], mc_ref[0], 3, 4, isc)
    scores = _sigmoid(_dot(h2, rw_ref[...], precision=HIGHEST))
    biased = scores + rb_ref[...]
    lane = lax.broadcasted_iota(jnp.int32, scores.shape, 1)
    grp = lane // GROUP_SIZE
    ninf = -jnp.inf

    def first_argmax(z):
        m = jnp.max(z, -1, keepdims=True)
        idx = jnp.min(jnp.where(z == m, lane, N_EXPERTS), -1, keepdims=True)
        return m, idx

    gscore = jnp.zeros_like(biased)
    for g in range(N_GROUPS):
        zg = jnp.where(grp == g, biased, ninf)
        m1, i1 = first_argmax(zg)
        m2 = jnp.max(jnp.where(lane == i1, ninf, zg), -1, keepdims=True)
        gscore = jnp.where(grp == g, m1 + m2, gscore)
    cand = jnp.where(lane % GROUP_SIZE == 0, gscore, ninf)
    gsel = jnp.zeros(scores.shape, jnp.bool_)
    for _ in range(TOPK_GROUPS):
        _, ig = first_argmax(cand)
        gsel = gsel | (grp == ig // GROUP_SIZE)
        cand = jnp.where(lane == ig, ninf, cand)
    masked = jnp.where(gsel, biased, ninf)
    out_lane = lax.broadcasted_iota(jnp.int32, (tm, LANE), 1)
    idx_out = jnp.zeros((tm, LANE), jnp.int32)
    w_out = jnp.zeros((tm, LANE), F32)
    w_sum = jnp.zeros((tm, 1), F32)
    for kq in range(TOP_K):
        _, ie = first_argmax(masked)
        wk = jnp.sum(jnp.where(lane == ie, scores, 0.0), -1, keepdims=True)
        masked = jnp.where(lane == ie, ninf, masked)
        idx_out = jnp.where(out_lane == kq, ie, idx_out)
        w_out = jnp.where(out_lane == kq, wk, w_out)
        w_sum = w_sum + wk
    idx_ref[0] = idx_out
    wt_ref[0] = w_out / w_sum * ROUTED_SCALE


def _router(x_all, modl, modc, rw, rb, ctx):
    b, ta, d = x_all.shape
    tm = _tile(ta, 544)
    row = lambda bb, i: (bb, i, 0)
    return pl.pallas_call(
        functools.partial(_router_kernel, tm=tm, ctx=ctx),
        grid=(b, ta // tm),
        in_specs=[pl.BlockSpec((1, tm, d), row),
                  pl.BlockSpec((1, 6, d), lambda bb, i: (bb, 0, 0)),
                  pl.BlockSpec((1, 6, d), lambda bb, i: (0, 0, 0)),
                  pl.BlockSpec((d, N_EXPERTS), lambda bb, i: (0, 0)),
                  pl.BlockSpec((1, N_EXPERTS), lambda bb, i: (0, 0))],
        out_specs=[pl.BlockSpec((1, tm, LANE), row), pl.BlockSpec((1, tm, LANE), row)],
        out_shape=[jax.ShapeDtypeStruct((b, ta, LANE), jnp.int32), jax.ShapeDtypeStruct((b, ta, LANE), F32)],
        compiler_params=_cparams(("parallel", "parallel")),
        name="router",
    )(x_all, modl, modc, rw, rb.reshape(1, N_EXPERTS))


def _experts_kernel(be_ref, nu_ref, x_ref, wgu_ref, wdn_ref, o_ref, wgu_s, wdn_s, *, ff):
    i = pl.program_id(0)
    changed = (i == 0) | (be_ref[i] != be_ref[jnp.maximum(i - 1, 0)])

    @pl.when(changed)
    def _():
        wgu_s[...] = wgu_ref[0].astype(BF16)
        wdn_s[...] = wdn_ref[0].astype(BF16)

    @pl.when(i < nu_ref[0])
    def _():
        h = _dot(x_ref[...], wgu_s[...])
        g = h[:, :ff]
        act = (g * _sigmoid(g) * h[:, ff:]).astype(BF16)
        o_ref[...] = _dot(act, wdn_s[...])

    @pl.when(i >= nu_ref[0])
    def _():
        o_ref[...] = jnp.zeros_like(o_ref)


def _experts(xg, blk_e, n_used, w_gu, w_dn, blk):
    n_rows, d = xg.shape
    ff = w_dn.shape[1]
    n_blocks = n_rows // blk
    grid_spec = pltpu.PrefetchScalarGridSpec(
        num_scalar_prefetch=2,
        grid=(n_blocks,),
        in_specs=[pl.BlockSpec((blk, d), lambda i, be, nu: (i, 0)),
                  pl.BlockSpec((1, d, 2 * ff), lambda i, be, nu: (be[i], 0, 0)),
                  pl.BlockSpec((1, ff, d), lambda i, be, nu: (be[i], 0, 0))],
        out_specs=pl.BlockSpec((blk, d), lambda i, be, nu: (i, 0)),
        scratch_shapes=[pltpu.VMEM((d, 2 * ff), BF16), pltpu.VMEM((ff, d), BF16)],
    )
    return pl.pallas_call(
        functools.partial(_experts_kernel, ff=ff),
        grid_spec=grid_spec,
        out_shape=jax.ShapeDtypeStruct((n_rows, d), F32),
        compiler_params=_cparams(("arbitrary",)),
        name="moe_experts",
    )(blk_e, n_used, xg, w_gu, w_dn)


def _moe_plan(idx, blk):
    flat_e = idx.reshape(-1)
    n_assign = flat_e.shape[0]
    order = jnp.argsort(flat_e)
    sorted_e = flat_e[order]
    counts = jnp.bincount(flat_e, length=N_EXPERTS)
    padded = (counts + blk - 1) // blk * blk
    start_sorted = jnp.cumsum(counts) - counts
    pad_end = jnp.cumsum(padded)
    start_pad = pad_end - padded
    dest = (start_pad[sorted_e] + jnp.arange(n_assign) - start_sorted[sorted_e]).astype(jnp.int32)
    n_blocks = -(-(n_assign + N_EXPERTS * (blk - 1)) // blk)
    row_tok = jnp.zeros((n_blocks * blk,), jnp.int32).at[dest].set((order // TOP_K).astype(jnp.int32))
    pos = jnp.zeros((n_assign,), jnp.int32).at[order].set(dest)
    blk_e = jnp.minimum(jnp.searchsorted(pad_end, jnp.arange(n_blocks) * blk, side='right'),
                        N_EXPERTS - 1).astype(jnp.int32)
    n_used = (pad_end[-1] // blk).astype(jnp.int32).reshape(1)
    return row_tok, pos, blk_e, n_used


def _combine_kernel(yg_ref, wt_ref, h2_ref, sgu_ref, sdn_ref, x_ref, ml_ref, mc_ref, lg_ref, lb_ref,
                    xo_ref, *, tm, ctx, alpha, d, ff):
    i = pl.program_id(1)
    isc = _row_is_ctx(tm, i * tm, ctx)
    h = _dot(h2_ref[0], sgu_ref[...])
    g = h[:, :ff]
    f = _dot((g * _sigmoid(g) * h[:, ff:]).astype(BF16), sdn_ref[...])
    wt = wt_ref[0]
    for kq in range(TOP_K):
        f = f + wt[:, kq:kq + 1] * yg_ref[0, :, kq * d:(kq + 1) * d]
    g2 = jnp.where(isc, mc_ref[0][5:6], ml_ref[0][5:6])
    xo_ref[0] = _ln(alpha * x_ref[0] + g2 * f, POST_LN_EPS) * lg_ref[...] + lb_ref[...]


def _combine(yg, wts, h2, sh_gu, sh_dn, x_all, modl, modc, ln_g, ln_b, ctx, alpha):
    b, ta, d = x_all.shape
    ff = sh_dn.shape[0]
    tm = _tile(ta, 136)
    row = lambda bb, i: (bb, i, 0)
    return pl.pallas_call(
        functools.partial(_combine_kernel, tm=tm, ctx=ctx, alpha=alpha, d=d, ff=ff),
        grid=(b, ta // tm),
        in_specs=[pl.BlockSpec((1, tm, TOP_K * d), row),
                  pl.BlockSpec((1, tm, LANE), row),
                  pl.BlockSpec((1, tm, d), row),
                  pl.BlockSpec((d, 2 * ff), lambda bb, i: (0, 0)),
                  pl.BlockSpec((ff, d), lambda bb, i: (0, 0)),
                  pl.BlockSpec((1, tm, d), row),
                  pl.BlockSpec((1, 6, d), lambda bb, i: (bb, 0, 0)),
                  pl.BlockSpec((1, 6, d), lambda bb, i: (0, 0, 0)),
                  pl.BlockSpec((1, d), lambda bb, i: (0, 0)),
                  pl.BlockSpec((1, d), lambda bb, i: (0, 0))],
        out_specs=pl.BlockSpec((1, tm, d), row),
        out_shape=jax.ShapeDtypeStruct((b, ta, d), F32),
        compiler_params=_cparams(("parallel", "parallel")),
        name="moe_combine_ln",
    )(yg, wts, h2, sh_gu, sh_dn, x_all, modl, modc, ln_g.reshape(1, d), ln_b.reshape(1, d))


def _rope_tables(n_lat, ctx):
    rows = n_lat // GRID_W
    row = jnp.repeat(jnp.arange(rows, dtype=F32), GRID_W)
    col = jnp.tile(jnp.arange(GRID_W, dtype=F32), rows)
    n_freq = ROPE_DIM // 4
    inv_freq = ROPE_BASE ** (-jnp.arange(n_freq, dtype=F32) / n_freq)
    ang = jnp.concatenate([row[:, None] * inv_freq, col[:, None] * inv_freq], -1)
    cos, sin = jnp.cos(ang), jnp.sin(ang)
    cos = jnp.concatenate([jnp.ones((ctx, ROPE_DIM // 2), F32), cos], 0)
    sin = jnp.concatenate([jnp.zeros((ctx, ROPE_DIM // 2), F32), sin], 0)
    cosf = jnp.tile(jnp.concatenate([cos, cos], -1), (1, LANE // ROPE_DIM))
    sinf = jnp.tile(jnp.concatenate([-sin, sin], -1), (1, LANE // ROPE_DIM))
    return cosf, sinf


def _pack_small_weights(w_in_l):
    d = w_in_l.shape[0]
    offs = np.cumsum([0, MLA_Q_RANK, MLA_KV_RANK + MLA_ROPE, RWKV_STREAM,
                      (SWA_HEADS + 2 * SWA_KV_HEADS) * SWA_HEAD, 3 * DIFF_HEADS * 2 * DIFF_HEAD])
    placed = sorted(zip([SEG_MQ, SEG_MKV, SEG_RW, SEG_SW, SEG_DF], range(5)))
    parts = []
    cur = 0
    for start, s in placed:
        if start > cur:
            parts.append(jnp.zeros((d, start - cur), BF16))
        parts.append(w_in_l[:, int(offs[s]):int(offs[s + 1])].astype(BF16))
        cur = start + int(offs[s + 1] - offs[s])
    parts.append(jnp.zeros((d, SMALL_COLS - cur), BF16))
    return jnp.concatenate(parts, axis=1), int(offs[5])


def kernel(x, c, ctx, c_ctx, w_mod, b_mod, w_in, mla_q_norm, mla_w_qup, mla_kv_norm, mla_w_kvup, rwkv_mu, rwkv_w0, rwkv_w_lora, rwkv_a0, rwkv_a_lora, rwkv_g_lora, rwkv_k_k, rwkv_k_a, rwkv_r_k, rwkv_ln_g, rwkv_ln_b, swa_sink, diff_lambda, diff_subln, w_branch, w_out, ln1_g, ln1_b, router_w, router_bias, exp_w_gu, exp_w_dn, sh_w_gu, sh_w_dn, ln2_g, ln2_b):
    depth = w_mod.shape[0]
    b, n_lat, d = x.shape
    n_ctx = ctx.shape[1]
    ta = n_ctx + n_lat
    alpha = (2 * depth) ** 0.25
    cosf, sinf = _rope_tables(n_lat, n_ctx)
    x_all = jnp.concatenate([ctx, x], axis=1)

    cond = jnp.concatenate([c, c_ctx[None]], axis=0)
    cond = jnp.pad(jax.nn.silu(cond), ((0, SUBLANE - (b + 1) % SUBLANE), (0, 0)))

    for l in range(depth):
        mod = _matmul_bias(cond, w_mod[l], b_mod[l]).reshape(cond.shape[0], 6, d)
        modl, modc = mod[:b], mod[b:b + 1]

        w_small, gate_off = _pack_small_weights(w_in[l])
        p = _inproj(x_all, modl, modc, w_small, n_ctx)

        wq = mla_w_qup[l].reshape(MLA_Q_RANK, MLA_HEADS, MLA_NOPE + MLA_ROPE)
        wq = jnp.pad(wq, ((0, 0), (0, 0), (0, 256 - MLA_NOPE - MLA_ROPE))).reshape(MLA_Q_RANK, 1024)
        q_a, k_a, v_a = _mla_prep(p, mla_q_norm[l].reshape(1, -1), mla_kv_norm[l].reshape(1, -1),
                                  wq.astype(BF16), mla_w_kvup[l].astype(BF16), cosf, sinf)
        out_a = _mla_attn(q_a, k_a, v_a, n_ctx)

        out_b = _rwkv_mixer(p, n_ctx, rwkv_mu[l], rwkv_w0[l],
                            rwkv_w_lora[l], rwkv_a0[l], rwkv_a_lora[l], rwkv_g_lora[l], rwkv_k_k[l],
                            rwkv_k_a[l], rwkv_r_k[l], rwkv_ln_g[l], rwkv_ln_b[l])

        sinkv = jnp.broadcast_to(swa_sink[l][:, None], (SWA_HEADS, LANE))
        out_s = _swa(p, cosf, sinf, sinkv, n_ctx)

        lf = diff_lambda[l]
        lam_init = 0.8 - 0.6 * math.exp(-0.3 * l)
        lam = jnp.exp(jnp.sum(lf[0] * lf[1])) - jnp.exp(jnp.sum(lf[2] * lf[3])) + lam_init
        out_d = _diff_attn(p, cosf, sinf, jnp.full((1, LANE), lam, F32), diff_subln[l].reshape(1, LANE),
                           n_ctx, lam_init)

        o4 = jnp.stack([out_a, out_b, out_s, out_d], axis=0)
        wg = w_in[l][:, gate_off:].reshape(d, N_BRANCH, d).transpose(1, 0, 2).astype(BF16)
        merged = _merge_a(x_all, modl, modc, o4, wg, w_branch[l].astype(BF16), n_ctx)
        x_all, h2 = _merge_b(merged, w_out[l].astype(BF16), x_all, modl, modc, ln1_g[l], ln1_b[l], n_ctx, alpha)

        idx, wts = _router(x_all, modl, modc, router_w[l], router_bias[l], n_ctx)
        row_tok, pos, blk_e, n_used = _moe_plan(idx[..., :TOP_K], MOE_BLK)
        xg = jnp.take(h2.reshape(b * ta, d), row_tok, axis=0)
        y_rows = _experts(xg, blk_e, n_used, exp_w_gu[l], exp_w_dn[l], MOE_BLK)
        yg = jnp.take(y_rows, pos, axis=0).reshape(b, ta, TOP_K * d)
        x_all = _combine(yg, wts, h2, sh_w_gu[l].astype(BF16), sh_w_dn[l].astype(BF16), x_all, modl, modc,
                         ln2_g[l], ln2_b[l], n_ctx, alpha)
    return x_all[:, n_ctx:]
```

```python
import functools
import math

import jax
import jax.numpy as jnp
import numpy as np
from jax import lax
from jax.experimental import pallas as pl
from jax.experimental.pallas import tpu as pltpu

F32 = jnp.float32
BF16 = jnp.bfloat16
HIGHEST = lax.Precision.HIGHEST

GRID_W = 64
ROPE_BASE = 10000.0
ROPE_DIM = 64
ADALN_EPS = 1e-6
POST_LN_EPS = 1e-5
NEG_INF = -1e30

MLA_HEADS = 4
MLA_Q_RANK = 384
MLA_KV_RANK = 256
MLA_NOPE = 128
MLA_ROPE = 64
MLA_V = 128

RWKV_HEADS = 8
RWKV_HEAD = 64
RWKV_W = RWKV_HEADS * RWKV_HEAD
DECAY_LORA = 32
ICLR_LORA = 32
GATE_LORA = 96
RWKV_STREAM = 3 * RWKV_W + 2 * DECAY_LORA + 2 * ICLR_LORA + GATE_LORA
RWKV_GN_EPS = 64e-5
RWKV_CHUNK = 64

SWA_HEADS = 8
SWA_KV_HEADS = 2
SWA_GROUP = SWA_HEADS // SWA_KV_HEADS
SWA_HEAD = 64
WINDOW = 128

DIFF_HEADS = 4
DIFF_HEAD = 64

N_BRANCH = 4
BRANCH_W = 512

N_EXPERTS = 64
TOP_K = 6
N_GROUPS = 8
TOPK_GROUPS = 4
GROUP_SIZE = N_EXPERTS // N_GROUPS
ROUTED_SCALE = 2.5
MOE_BLK = 128

LANE = 128
SUBLANE = 8

SEG_SW = 0
SEG_MQ = 768
SEG_MKV = 1152
SEG_DF = 1536
SEG_RW = 3584
RW_COLS = 1792
SMALL_COLS = SEG_RW + RW_COLS


def _tile(n, target, mult=SUBLANE):
    best = None
    for d in range(mult, min(n, target) + 1, mult):
        if n % d == 0:
            best = d
    assert best is not None, (n, target, mult)
    return best


def _cparams(sem, vmem_mb=48):
    return pltpu.CompilerParams(dimension_semantics=sem, vmem_limit_bytes=vmem_mb * 1024 * 1024)


def _sigmoid(x):
    return 1.0 / (1.0 + jnp.exp(-x))


def _ln(x, eps):
    mu = jnp.mean(x, -1, keepdims=True)
    xc = x - mu
    var = jnp.mean(xc * xc, -1, keepdims=True)
    return xc * lax.rsqrt(var + eps)


def _rms(x, eps):
    return x * lax.rsqrt(jnp.mean(x * x, -1, keepdims=True) + eps)


def _row_is_ctx(n_rows, row0, ctx):
    rows = row0 + lax.broadcasted_iota(jnp.int32, (n_rows, 1), 0)
    return rows < ctx


def _modulated(x, ml, mc, sh_i, sc_i, isc):
    sh = jnp.where(isc, mc[sh_i:sh_i + 1], ml[sh_i:sh_i + 1])
    sc = jnp.where(isc, mc[sc_i:sc_i + 1], ml[sc_i:sc_i + 1])
    return _ln(x, ADALN_EPS) * (1.0 + sc) + sh


def _swap32(x):
    lane = lax.broadcasted_iota(jnp.int32, x.shape, 1)
    up = pltpu.roll(x, LANE - 32, 1)
    dn = pltpu.roll(x, 32, 1)
    return jnp.where((lane % 64) < 32, up, dn)


def _rope(x, cosf, sinf):
    return x * cosf + _swap32(x) * sinf


def _pack_halves(x):
    n = x.shape[1] // 2
    lo = pltpu.bitcast(x[:, :n].astype(BF16).astype(F32), jnp.uint32)
    hi = pltpu.bitcast(x[:, n:].astype(BF16).astype(F32), jnp.uint32)
    return (hi & jnp.uint32(0xFFFF0000)) | (lo >> 16)


def _unpack_halves(u):
    lo = pltpu.bitcast(u << 16, F32)
    hi = pltpu.bitcast(u & jnp.uint32(0xFFFF0000), F32)
    return lo, hi


def _dot(a, b, **kw):
    return jnp.dot(a, b, preferred_element_type=F32, **kw)


def _dot_nt(a, b, **kw):
    return lax.dot_general(a, b, (((1,), (1,)), ((), ())), preferred_element_type=F32, **kw)


def _dot_tn(a, b, **kw):
    return lax.dot_general(a, b, (((0,), (0,)), ((), ())), preferred_element_type=F32, **kw)


def _mm_kernel(x_ref, w_ref, b_ref, o_ref):
    o_ref[...] = _dot(x_ref[...], w_ref[...], precision=HIGHEST) + b_ref[...]


def _matmul_bias(x, w, b):
    m, k = x.shape
    n = w.shape[1]
    tn = _tile(n, 1024, LANE)
    return pl.pallas_call(
        _mm_kernel,
        grid=(n // tn,),
        in_specs=[pl.BlockSpec((m, k), lambda j: (0, 0)),
                  pl.BlockSpec((k, tn), lambda j: (0, j)),
                  pl.BlockSpec((1, tn), lambda j: (0, j))],
        out_specs=pl.BlockSpec((m, tn), lambda j: (0, j)),
        out_shape=jax.ShapeDtypeStruct((m, n), F32),
        compiler_params=_cparams(("arbitrary",)),
        name="mod_matmul",
    )(x, w, b.reshape(1, n))


def _inproj_kernel(x_ref, ml_ref, mc_ref, w_ref, o_ref, xn_ref, *, tm, ctx):
    i = pl.program_id(1)
    j = pl.program_id(2)

    @pl.when(j == 0)
    def _():
        isc = _row_is_ctx(tm, i * tm, ctx)
        xn_ref[...] = _modulated(x_ref[0], ml_ref[0], mc_ref[0], 0, 1, isc).astype(BF16)

    o_ref[0] = _dot(xn_ref[...], w_ref[...])


def _inproj(x_all, modl, modc, w_small, ctx):
    b, ta, d = x_all.shape
    n = w_small.shape[1]
    tm = _tile(ta, 544)
    tn = _tile(n, 1024, LANE)
    return pl.pallas_call(
        functools.partial(_inproj_kernel, tm=tm, ctx=ctx),
        grid=(b, ta // tm, n // tn),
        in_specs=[pl.BlockSpec((1, tm, d), lambda bb, i, j: (bb, i, 0)),
                  pl.BlockSpec((1, 6, d), lambda bb, i, j: (bb, 0, 0)),
                  pl.BlockSpec((1, 6, d), lambda bb, i, j: (0, 0, 0)),
                  pl.BlockSpec((d, tn), lambda bb, i, j: (0, j))],
        out_specs=pl.BlockSpec((1, tm, tn), lambda bb, i, j: (bb, i, j)),
        out_shape=jax.ShapeDtypeStruct((b, ta, n), F32),
        scratch_shapes=[pltpu.VMEM((tm, d), BF16)],
        compiler_params=_cparams(("parallel", "parallel", "arbitrary")),
        name="inproj",
    )(x_all, modl, modc, w_small)


def _mla_prep_kernel(mq_ref, mkv_ref, qg_ref, kg_ref, wq_ref, wkv_ref, cos_ref, sin_ref,
                     q_ref, k_ref, v_ref):
    mq = mq_ref[0]
    mkv = mkv_ref[0]
    cosv = cos_ref[...]
    sinv = sin_ref[...]
    qn = _rms(mq, 1e-6) * qg_ref[...]
    q = _dot(qn.astype(BF16), wq_ref[...])
    kvn = _rms(mkv[:, :MLA_KV_RANK], 1e-6) * kg_ref[...]
    kv = _dot(kvn.astype(BF16), wkv_ref[...])
    kr = _rope(mkv[:, MLA_KV_RANK:MLA_KV_RANK + LANE], cosv, sinv).astype(BF16)
    scale = (MLA_NOPE + MLA_ROPE) ** -0.5
    for h in range(MLA_HEADS):
        c0 = h * 256
        q_ref[0, :, c0:c0 + 128] = (q[:, c0:c0 + 128] * scale).astype(BF16)
        q_ref[0, :, c0 + 128:c0 + 256] = (_rope(q[:, c0 + 128:c0 + 256], cosv, sinv) * scale).astype(BF16)
        k_ref[0, :, c0:c0 + 128] = kv[:, c0:c0 + 128].astype(BF16)
        k_ref[0, :, c0 + 128:c0 + 256] = kr
        v_ref[0, :, h * 128:(h + 1) * 128] = kv[:, c0 + 128:c0 + 256].astype(BF16)


def _mla_prep(p, qg, kg, wq, wkv, cosf, sinf):
    b, ta, _ = p.shape
    tm = _tile(ta, 544)
    return pl.pallas_call(
        _mla_prep_kernel,
        grid=(b, ta // tm),
        in_specs=[pl.BlockSpec((1, tm, 384), lambda bb, i: (bb, i, SEG_MQ // 384)),
                  pl.BlockSpec((1, tm, 384), lambda bb, i: (bb, i, SEG_MKV // 384)),
                  pl.BlockSpec((1, MLA_Q_RANK), lambda bb, i: (0, 0)),
                  pl.BlockSpec((1, MLA_KV_RANK), lambda bb, i: (0, 0)),
                  pl.BlockSpec((MLA_Q_RANK, 1024), lambda bb, i: (0, 0)),
                  pl.BlockSpec((MLA_KV_RANK, 1024), lambda bb, i: (0, 0)),
                  pl.BlockSpec((tm, LANE), lambda bb, i: (i, 0)),
                  pl.BlockSpec((tm, LANE), lambda bb, i: (i, 0))],
        out_specs=[pl.BlockSpec((1, tm, 1024), lambda bb, i: (bb, i, 0)),
                   pl.BlockSpec((1, tm, 1024), lambda bb, i: (bb, i, 0)),
                   pl.BlockSpec((1, tm, 512), lambda bb, i: (bb, i, 0))],
        out_shape=[jax.ShapeDtypeStruct((b, ta, 1024), BF16),
                   jax.ShapeDtypeStruct((b, ta, 1024), BF16),
                   jax.ShapeDtypeStruct((b, ta, 512), BF16)],
        compiler_params=_cparams(("parallel", "parallel")),
        name="mla_prep",
    )(p, p, qg, kg, wq, wkv, cosf, sinf)


def _softmax_pv(s, v):
    m = jnp.max(s, -1, keepdims=True)
    p = jnp.exp(s - m)
    l = jnp.sum(p, -1, keepdims=True)
    return _dot(p.astype(BF16), v) / l


def _mla_attn_kernel(q_ref, k_ref, v_ref, o_ref, *, n_ctx_tiles, ctx):
    qi = pl.program_id(2)

    def run(nk):
        s = _dot_nt(q_ref[0], k_ref[0, :nk, :])
        o_ref[0] = _softmax_pv(s, v_ref[0, :nk, :]).astype(o_ref.dtype)

    @pl.when(qi < n_ctx_tiles)
    def _():
        run(ctx)

    @pl.when(qi >= n_ctx_tiles)
    def _():
        run(k_ref.shape[1])


def _mla_attn(q, k, v, ctx):
    b, ta, _ = q.shape
    tq = _tile(math.gcd(ta, ctx), 256)
    return pl.pallas_call(
        functools.partial(_mla_attn_kernel, n_ctx_tiles=ctx // tq, ctx=ctx),
        grid=(b, MLA_HEADS, ta // tq),
        in_specs=[pl.BlockSpec((1, tq, 256), lambda bb, h, i: (bb, i, h)),
                  pl.BlockSpec((1, ta, 256), lambda bb, h, i: (bb, 0, h)),
                  pl.BlockSpec((1, ta, 128), lambda bb, h, i: (bb, 0, h))],
        out_specs=pl.BlockSpec((1, tq, 128), lambda bb, h, i: (bb, i, h)),
        out_shape=jax.ShapeDtypeStruct((b, ta, 512), BF16),
        compiler_params=_cparams(("parallel", "parallel", "arbitrary")),
        name="mla_attn",
    )(q, k, v)


def _diff_attn_kernel(q_ref, k_ref, v_ref, cq_ref, sq_ref, ck_ref, sk_ref, lam_ref, sub_ref,
                      o_ref, ks_ref, vs_ref, *, n_ctx_tiles, ctx, post_scale):
    qi = pl.program_id(2)

    @pl.when(qi == 0)
    def _():
        ks_ref[...] = _rope(k_ref[0], ck_ref[...], sk_ref[...]).astype(BF16)
        vs_ref[...] = v_ref[0].astype(BF16)

    def run(nk):
        q = _rope(q_ref[0], cq_ref[...], sq_ref[...]) * (DIFF_HEAD ** -0.5)
        lane = lax.broadcasted_iota(jnp.int32, q.shape, 1)
        q1 = jnp.where(lane < 64, q, 0.0).astype(BF16)
        q2 = jnp.where(lane >= 64, q, 0.0).astype(BF16)
        k = ks_ref[:nk, :]
        v = vs_ref[:nk, :]
        o = _softmax_pv(_dot_nt(q1, k), v) - lam_ref[...] * _softmax_pv(_dot_nt(q2, k), v)
        o = _rms(o, 1e-5) * sub_ref[...] * post_scale
        o_ref[0] = o.astype(o_ref.dtype)

    @pl.when(qi < n_ctx_tiles)
    def _():
        run(ctx)

    @pl.when(qi >= n_ctx_tiles)
    def _():
        run(ks_ref.shape[0])


def _diff_attn(p, cosf, sinf, lamv, subln, ctx, lam_init):
    b, ta, _ = p.shape
    tq = _tile(math.gcd(ta, ctx), 256)
    cb = SEG_DF // LANE
    return pl.pallas_call(
        functools.partial(_diff_attn_kernel, n_ctx_tiles=ctx // tq, ctx=ctx, post_scale=1.0 - lam_init),
        grid=(b, DIFF_HEADS, ta // tq),
        in_specs=[pl.BlockSpec((1, tq, LANE), lambda bb, h, i: (bb, i, cb + h)),
                  pl.BlockSpec((1, ta, LANE), lambda bb, h, i: (bb, 0, cb + DIFF_HEADS + h)),
                  pl.BlockSpec((1, ta, LANE), lambda bb, h, i: (bb, 0, cb + 2 * DIFF_HEADS + h)),
                  pl.BlockSpec((tq, LANE), lambda bb, h, i: (i, 0)),
                  pl.BlockSpec((tq, LANE), lambda bb, h, i: (i, 0)),
                  pl.BlockSpec((ta, LANE), lambda bb, h, i: (0, 0)),
                  pl.BlockSpec((ta, LANE), lambda bb, h, i: (0, 0)),
                  pl.BlockSpec((1, LANE), lambda bb, h, i: (0, 0)),
                  pl.BlockSpec((1, LANE), lambda bb, h, i: (0, 0))],
        out_specs=pl.BlockSpec((1, tq, LANE), lambda bb, h, i: (bb, i, h)),
        out_shape=jax.ShapeDtypeStruct((b, ta, 512), BF16),
        scratch_shapes=[pltpu.VMEM((ta, LANE), BF16), pltpu.VMEM((ta, LANE), BF16)],
        compiler_params=_cparams(("parallel", "parallel", "arbitrary")),
        name="diff_attn",
    )(p, p, p, cosf, sinf, cosf, sinf, lamv, subln)


def _swa_kernel(q_ref, k_ref, v_ref, cq_ref, sq_ref, ck_ref, sk_ref, sink_ref,
                o_ref, ks_ref, vs_ref, *, ctx):
    t = pl.program_id(1)
    n_ctx_tiles = ctx // WINDOW
    ta = ks_ref.shape[0]
    span = 3 * WINDOW

    @pl.when(t == 0)
    def _():
        ks_ref[...] = _rope(k_ref[0], ck_ref[...], sk_ref[...]).astype(BF16)
        vs_ref[...] = v_ref[0].astype(BF16)

    lane = lax.broadcasted_iota(jnp.int32, (WINDOW, LANE), 1)
    cq = cq_ref[...]
    sq = sq_ref[...]
    qblk = [_rope(q_ref[0, :, hb * LANE:(hb + 1) * LANE], cq, sq) * (SWA_HEAD ** -0.5)
            for hb in range(SWA_HEADS // 2)]

    def head_q(hk, g):
        h = hk * SWA_GROUP + g
        blk = qblk[h // 2]
        if h % 2 != hk:
            blk = pltpu.roll(blk, 64, 1)
        keep = (lane >= 64) if hk == 1 else (lane < 64)
        return jnp.where(keep, blk, 0.0)

    def run(local):
        kc = ks_ref[:ctx, :]
        vc = vs_ref[:ctx, :]
        if local:
            n = t - n_ctx_tiles
            ws = jnp.clip(ctx + (n - 1) * WINDOW, ctx, ta - span)
            ws = pl.multiple_of(ws, WINDOW)
            kl = ks_ref[pl.ds(ws, span), :]
            vl = vs_ref[pl.ds(ws, span), :]
            kpos = (ws - ctx) + lax.broadcasted_iota(jnp.int32, (1, span), 1)
            qpos = n * WINDOW + lax.broadcasted_iota(jnp.int32, (SWA_GROUP * WINDOW, 1), 0) % WINDOW
            valid = jnp.abs(kpos - qpos) <= WINDOW
        outs = {}
        for hk in range(SWA_KV_HEADS):
            qst = jnp.concatenate([head_q(hk, g) for g in range(SWA_GROUP)], axis=0).astype(BF16)
            sink = jnp.concatenate(
                [jnp.broadcast_to(sink_ref[hk * SWA_GROUP + g:hk * SWA_GROUP + g + 1, 0:1], (WINDOW, 1))
                 for g in range(SWA_GROUP)], axis=0)
            s_c = _dot_nt(qst, kc)
            m = jnp.maximum(jnp.max(s_c, -1, keepdims=True), sink)
            if local:
                s_l = jnp.where(valid, _dot_nt(qst, kl), NEG_INF)
                m = jnp.maximum(m, jnp.max(s_l, -1, keepdims=True))
            p_c = jnp.exp(s_c - m)
            l = jnp.sum(p_c, -1, keepdims=True) + jnp.exp(sink - m)
            acc = _dot(p_c.astype(BF16), vc)
            if local:
                p_l = jnp.exp(s_l - m)
                l = l + jnp.sum(p_l, -1, keepdims=True)
                acc = acc + _dot(p_l.astype(BF16), vl)
            o = acc / l
            for g in range(SWA_GROUP):
                res = o[g * WINDOW:(g + 1) * WINDOW, :]
                if g % 2 != hk:
                    res = pltpu.roll(res, 64, 1)
                outs[hk * SWA_GROUP + g] = res
        for hb in range(SWA_HEADS // 2):
            blk = jnp.where(lane < 64, outs[2 * hb], outs[2 * hb + 1])
            o_ref[0, :, hb * LANE:(hb + 1) * LANE] = blk.astype(o_ref.dtype)

    @pl.when(t < n_ctx_tiles)
    def _():
        run(False)

    @pl.when(t >= n_ctx_tiles)
    def _():
        run(True)


def _swa(p, cosf, sinf, sinkv, ctx):
    b, ta, _ = p.shape
    qb = SEG_SW // 512
    kb = (SEG_SW + 512) // LANE
    return pl.pallas_call(
        functools.partial(_swa_kernel, ctx=ctx),
        grid=(b, ta // WINDOW),
        in_specs=[pl.BlockSpec((1, WINDOW, 512), lambda bb, t: (bb, t, qb)),
                  pl.BlockSpec((1, ta, LANE), lambda bb, t: (bb, 0, kb)),
                  pl.BlockSpec((1, ta, LANE), lambda bb, t: (bb, 0, kb + 1)),
                  pl.BlockSpec((WINDOW, LANE), lambda bb, t: (t, 0)),
                  pl.BlockSpec((WINDOW, LANE), lambda bb, t: (t, 0)),
                  pl.BlockSpec((ta, LANE), lambda bb, t: (0, 0)),
                  pl.BlockSpec((ta, LANE), lambda bb, t: (0, 0)),
                  pl.BlockSpec((SWA_HEADS, LANE), lambda bb, t: (0, 0))],
        out_specs=pl.BlockSpec((1, WINDOW, 512), lambda bb, t: (bb, t, 0)),
        out_shape=jax.ShapeDtypeStruct((b, ta, 512), BF16),
        scratch_shapes=[pltpu.VMEM((ta, LANE), BF16), pltpu.VMEM((ta, LANE), BF16)],
        compiler_params=_cparams(("parallel", "arbitrary")),
        name="swa_attn",
    )(p, p, p, cosf, sinf, cosf, sinf, sinkv)


def _softplus(x):
    return jnp.maximum(x, 0.0) + jnp.log(1.0 + jnp.exp(-jnp.abs(x)))


def _rwkv_prep_kernel(z_ref, hp_ref, hn_ref, mu_ref, w0_ref, a0_ref, wl_ref, al_ref, gl_ref, kk_ref, ka_ref,
                      rk_ref, bd_ref, r_o, v_o, kk_o, ld0_o, ld1_o, kd0_o, kd1_o, a0_o, a1_o, bv_o, g_o, zs_ref):
    tm = z_ref.shape[1]
    w = RWKV_W
    z = z_ref[0]
    zs_ref[SUBLANE:SUBLANE + tm, :] = z
    zs_ref[SUBLANE - 1:SUBLANE, :] = hp_ref[0, 0]
    zs_ref[SUBLANE + tm:SUBLANE + tm + 1, :] = hn_ref[0, 0]
    shifted = 0.5 * (zs_ref[SUBLANE - 1:SUBLANE - 1 + tm, :] + zs_ref[SUBLANE + 1:SUBLANE + 1 + tm, :])
    seg = z + (shifted - z) * mu_ref[...]
    r = seg[:, 0:w]
    k = seg[:, w:2 * w]
    v = seg[:, 2 * w:3 * w]
    lo = seg[:, 3 * w:3 * w + LANE]
    gl = seg[:, 3 * w + LANE:3 * w + 2 * LANE]
    hp = dict(precision=HIGHEST)
    bd = bd_ref[...]
    kk = k * kk_ref[...]
    kk = kk / jnp.maximum(jnp.sqrt(_dot(kk * kk, bd, **hp)), 1e-12)
    th = jnp.tanh(lo)
    ksum = jnp.zeros_like(k)
    for d, (ld_o, kd_o, a_o) in enumerate(((ld0_o, kd0_o, a0_o), (ld1_o, kd1_o, a1_o))):
        wd = w0_ref[d:d + 1, :] + _dot(th, wl_ref[d], **hp)
        wd = -_softplus(-wd) - 0.5
        a = _sigmoid(a0_ref[d:d + 1, :] + _dot(lo, al_ref[d], **hp))
        kd = k * (1.0 + (a - 1.0) * ka_ref[...])
        ld_o[0] = -jnp.exp(wd)
        kd_o[0] = kd
        a_o[0] = a
        ksum = ksum + kd
    r_o[0] = r
    v_o[0] = v
    kk_o[0] = kk
    bv_o[0] = _dot(r * ksum * rk_ref[...], bd, **hp) * v
    g_o[0] = _dot(_sigmoid(gl), gl_ref[...], **hp)


def _rwkv_prep(p, hprev, hnext, mu, w0, a0, wl, al, gl, k_k, k_a, r_k, bd, tm):
    b, ta, _ = p.shape
    w = RWKV_W
    row = lambda bb, i: (bb, i, 0)
    vec = lambda n: pl.BlockSpec((1, n), lambda bb, i: (0, 0))
    out = jax.ShapeDtypeStruct((b, ta, w), F32)
    return pl.pallas_call(
        _rwkv_prep_kernel,
        grid=(b, ta // tm),
        in_specs=[pl.BlockSpec((1, tm, RW_COLS), lambda bb, i: (bb, i, SEG_RW // RW_COLS)),
                  pl.BlockSpec((1, 1, 1, RW_COLS), lambda bb, i: (bb, i, 0, 0)),
                  pl.BlockSpec((1, 1, 1, RW_COLS), lambda bb, i: (bb, i, 0, 0)),
                  vec(RW_COLS),
                  pl.BlockSpec((2, w), lambda bb, i: (0, 0)),
                  pl.BlockSpec((2, w), lambda bb, i: (0, 0)),
                  pl.BlockSpec((2, LANE, w), lambda bb, i: (0, 0, 0)),
                  pl.BlockSpec((2, LANE, w), lambda bb, i: (0, 0, 0)),
                  pl.BlockSpec((LANE, w), lambda bb, i: (0, 0)),
                  vec(w), vec(w), vec(w),
                  pl.BlockSpec((w, w), lambda bb, i: (0, 0))],
        out_specs=[pl.BlockSpec((1, tm, w), row)] * 11,
        out_shape=[out] * 11,
        scratch_shapes=[pltpu.VMEM((tm + 2 * SUBLANE, RW_COLS), F32)],
        compiler_params=_cparams(("parallel", "parallel")),
        name="rwkv_prep",
    )(p, hprev, hnext, mu, w0, a0, wl, al, gl, k_k, k_a, r_k, bd)


def _split3(x):
    hi = x.astype(BF16)
    r1 = x - hi.astype(F32)
    mid = r1.astype(BF16)
    return hi, mid, (r1 - mid.astype(F32)).astype(BF16)


def _split2(x):
    hi = x.astype(BF16)
    return hi, (x - hi.astype(F32)).astype(BF16)


def _mm3(a, b, dot=_dot):
    return dot(a[0], b[0]) + dot(a[0], b[1]) + dot(a[1], b[0])


def _rwkv_chunk_kernel(r_ref, v_ref, kk_ref, ld0_ref, ld1_ref, kd0_ref, kd1_ref, a0_ref, a1_ref, pz_ref, wu_ref):
    C = RWKV_CHUNK
    hd = RWKV_HEAD
    row = lax.broadcasted_iota(jnp.int32, (C, C), 0)
    col = lax.broadcasted_iota(jnp.int32, (C, C), 1)
    eye = (row == col).astype(F32)
    probs = []
    for d, (ld_ref, kd_ref, a_ref) in enumerate(((ld0_ref, kd0_ref, a0_ref), (ld1_ref, kd1_ref, a1_ref))):
        incl = (row >= col) if d == 0 else (row <= col)
        strict = (row > col) if d == 0 else (row < col)
        last = C - 1 if d == 0 else 0
        for h in range(RWKV_HEADS):
            sl = slice(h * hd, (h + 1) * hd)
            probs.append(dict(d=d, h=h, incl=incl, strict=strict, last=last,
                              ones=jnp.where(incl, 1.0, 0.0).astype(BF16),
                              r=r_ref[0, :, sl], v=v_ref[0, :, sl], kk=kk_ref[0, :, sl],
                              ld=ld_ref[0, :, sl], k=kd_ref[0, :, sl], a=a_ref[0, :, sl]))
    for q in probs:
        q["cs"] = sum(_dot(q["ones"], part) for part in _split3(q["ld"]))
    for q in probs:
        cs = q["cs"]
        e_cs = jnp.exp(cs)
        e_inv = jnp.exp(-cs)
        q["g_end"] = e_cs[q["last"]:q["last"] + 1, :]
        q["kkd"] = q["kk"] * jnp.exp(cs - q["ld"])
        q["rd"] = q["r"] * e_cs
        q["bi"] = _split2(q["kk"] * q["a"] * e_inv)
        q["ki"] = _split2(q["k"] * e_inv)
        q["lhs"] = _split2(jnp.concatenate([q["kkd"], q["rd"]], axis=0))
    for q in probs:
        gb = _mm3(q["lhs"], q["bi"], _dot_nt)
        gk = _mm3(q["lhs"], q["ki"], _dot_nt)
        q["pw"] = -jnp.where(q["strict"], gb[:C], 0.0)
        q["q_b"] = _split2(jnp.where(q["incl"], gb[C:], 0.0))
        q["akqk"] = _split2(jnp.concatenate([jnp.where(q["strict"], gk[:C], 0.0),
                                            jnp.where(q["incl"], gk[C:], 0.0)], axis=0))
        q["tinv"] = eye + q["pw"]
    for _ in range(5):
        for q in probs:
            pw = _split2(q["pw"])
            q["pw"] = _mm3(pw, pw)
        for q in probs:
            q["tinv"] = q["tinv"] + _mm3(_split2(q["tinv"]), _split2(q["pw"]))
    for q in probs:
        q["vs"] = _split2(q["v"])
        q["av"] = _mm3(q["akqk"], q["vs"])
    for q in probs:
        tinv = _split2(q["tinv"])
        q["w1"] = _mm3(tinv, _split2(q["kkd"]))
        q["u1"] = _mm3(tinv, _split2(q["av"][:C]))
    for q in probs:
        w1 = _split2(q["w1"])
        u1 = _split2(q["u1"])
        w2 = q["rd"] - _mm3(q["q_b"], w1)
        u2 = q["av"][C:] - _mm3(q["q_b"], u1)
        pt = (eye - _mm3(w1, q["bi"], _dot_tn)) * q["g_end"]
        zt = (_mm3(q["vs"], q["ki"], _dot_tn) - _mm3(u1, q["bi"], _dot_tn)) * q["g_end"]
        pz_ref[q["d"], 0, 0, q["h"]] = jnp.concatenate([pt, zt], axis=1)
        wu_ref[q["d"], 0, 0, q["h"]] = jnp.concatenate([w2, u2], axis=1)


def _rwkv_chunks(r, v, kk, ld0, ld1, kd0, kd1, a0, a1):
    b, ta, w = r.shape
    nc = ta // RWKV_CHUNK
    spec = pl.BlockSpec((1, RWKV_CHUNK, w), lambda bb, c: (bb, c, 0))
    ospec = pl.BlockSpec((2, 1, 1, RWKV_HEADS, RWKV_HEAD, 2 * RWKV_HEAD), lambda bb, c: (0, bb, c, 0, 0, 0))
    oshape = jax.ShapeDtypeStruct((2, b, nc, RWKV_HEADS, RWKV_HEAD, 2 * RWKV_HEAD), F32)
    return pl.pallas_call(
        _rwkv_chunk_kernel,
        grid=(b, nc),
        in_specs=[spec] * 9,
        out_specs=[ospec, ospec],
        out_shape=[oshape, oshape],
        compiler_params=_cparams(("parallel", "parallel")),
        name="rwkv_chunks",
    )(r, v, kk, ld0, ld1, kd0, kd1, a0, a1)


def _rwkv_state_kernel(pz0_ref, pz1_ref, wu0_ref, wu1_ref, y0_ref, y1_ref, st_ref):
    i = pl.program_id(0)
    hd = RWKV_HEAD
    nb = st_ref.shape[1]

    @pl.when(i == 0)
    def _():
        st_ref[...] = jnp.zeros_like(st_ref)

    probs = []
    for d, (pz_ref, wu_ref) in enumerate(((pz0_ref, wu0_ref), (pz1_ref, wu1_ref))):
        for bb in range(nb):
            for h in range(RWKV_HEADS):
                pz = pz_ref[0, bb, 0, h]
                wu = wu_ref[0, bb, 0, h]
                probs.append(dict(d=d, b=bb, h=h, pt=_split2(pz[:, :hd]), zt=pz[:, hd:],
                                  w2=_split2(wu[:, :hd]), u2=wu[:, hd:], s=_split2(st_ref[d, bb, h])))
    for q in probs:
        q["y"] = _mm3(q["w2"], q["s"], _dot_nt) + q["u2"]
    for q in probs:
        st_ref[q["d"], q["b"], q["h"]] = _mm3(q["s"], q["pt"]) + q["zt"]
    for d, y_ref in enumerate((y0_ref, y1_ref)):
        for bb in range(nb):
            ys = [q["y"] for q in probs if q["d"] == d and q["b"] == bb]
            for hp in range(RWKV_HEADS // 2):
                y_ref[bb, :, hp * LANE:(hp + 1) * LANE] = jnp.concatenate([ys[2 * hp], ys[2 * hp + 1]], axis=1)


def _rwkv_state(pz, wu, ctx):
    _, b, nc, nh, hd, _ = pz.shape
    ncc = ctx // RWKV_CHUNK

    def rev(i):
        return jnp.where(i < ncc, ncc - 1 - i, nc - 1 - (i - ncc))

    blk = (1, b, 1, nh, hd, 2 * hd)
    fwd = lambda i: (0, 0, i, 0, 0, 0)
    bwd = lambda i: (1, 0, rev(i), 0, 0, 0)
    yshape = jax.ShapeDtypeStruct((b, nc * RWKV_CHUNK, nh * hd), F32)
    return pl.pallas_call(
        _rwkv_state_kernel,
        grid=(nc,),
        in_specs=[pl.BlockSpec(blk, fwd), pl.BlockSpec(blk, bwd), pl.BlockSpec(blk, fwd), pl.BlockSpec(blk, bwd)],
        out_specs=[pl.BlockSpec((b, RWKV_CHUNK, nh * hd), lambda i: (0, i, 0)),
                   pl.BlockSpec((b, RWKV_CHUNK, nh * hd), lambda i: (0, rev(i), 0))],
        out_shape=[yshape, yshape],
        scratch_shapes=[pltpu.VMEM((2, b, nh, hd, hd), F32)],
        compiler_params=_cparams(("arbitrary",)),
        name="rwkv_state",
    )(pz, pz, wu, wu)


def _rwkv_out_kernel(y0_ref, y1_ref, bv_ref, g_ref, lg_ref, lb_ref, bd_ref, o_ref):
    hp = dict(precision=HIGHEST)
    bdm = bd_ref[...] * (1.0 / RWKV_HEAD)
    y = y0_ref[0] + y1_ref[0]
    yc = y - _dot(y, bdm, **hp)
    var = _dot(yc * yc, bdm, **hp)
    yn = yc * lax.rsqrt(var + RWKV_GN_EPS) * lg_ref[...] + lb_ref[...]
    o_ref[0] = ((yn + bv_ref[0]) * g_ref[0]).astype(o_ref.dtype)


def _rwkv_out(y0, y1, bv, g, ln_g, ln_b, bd, tm):
    b, ta, w = y0.shape
    row = pl.BlockSpec((1, tm, w), lambda bb, i: (bb, i, 0))
    vec = pl.BlockSpec((1, w), lambda bb, i: (0, 0))
    return pl.pallas_call(
        _rwkv_out_kernel,
        grid=(b, ta // tm),
        in_specs=[row, row, row, row, vec, vec, pl.BlockSpec((w, w), lambda bb, i: (0, 0))],
        out_specs=row,
        out_shape=jax.ShapeDtypeStruct((b, ta, w), BF16),
        compiler_params=_cparams(("parallel", "parallel")),
        name="rwkv_out",
    )(y0, y1, bv, g, ln_g, ln_b, bd)


def _rwkv_mixer(p, ctx, mu, w0, w_lora, a0, a_lora, g_lora, k_k, k_a, r_k, ln_g, ln_b):
    b, ta, _ = p.shape
    w = RWKV_W
    tm = _tile(math.gcd(ta, ctx), 256)
    nt = ta // tm
    edge_lo = p[:, 0::tm, SEG_RW:]
    edge_hi = p[:, tm - 1::tm, SEG_RW:]
    tile_id = jnp.arange(nt)[None, :, None]
    hprev = jnp.where((tile_id == 0) | (tile_id == ctx // tm), 0.0, jnp.roll(edge_hi, 1, axis=1))
    hnext = jnp.where((tile_id == nt - 1) | (tile_id == ctx // tm - 1), 0.0, jnp.roll(edge_lo, -1, axis=1))
    wl = jnp.zeros((2, LANE, w), F32)
    al = jnp.zeros((2, LANE, w), F32)
    for d in range(2):
        wl = wl.at[d, d * DECAY_LORA:(d + 1) * DECAY_LORA].set(w_lora[d])
        al = al.at[d, 2 * DECAY_LORA + d * ICLR_LORA:2 * DECAY_LORA + (d + 1) * ICLR_LORA].set(a_lora[d])
    gl = jnp.zeros((LANE, w), F32).at[:GATE_LORA].set(g_lora)
    mu_p = jnp.pad(mu, (0, RW_COLS - RWKV_STREAM)).reshape(1, RW_COLS)
    head_id = jnp.arange(w) // RWKV_HEAD
    bd = (head_id[:, None] == head_id[None, :]).astype(F32)
    r, v, kk, ld0, ld1, kd0, kd1, a0_, a1_, bv, g = _rwkv_prep(
        p, hprev[:, :, None, :], hnext[:, :, None, :], mu_p, w0, a0, wl, al, gl,
        k_k.reshape(1, w), k_a.reshape(1, w), r_k.reshape(1, w), bd, tm)
    pz, wu = _rwkv_chunks(r, v, kk, ld0, ld1, kd0, kd1, a0_, a1_)
    y0, y1 = _rwkv_state(pz, wu, ctx)
    return _rwkv_out(y0, y1, bv, g, ln_g.reshape(1, w), ln_b.reshape(1, w), bd, tm)


def _merge_a_kernel(x_ref, ml_ref, mc_ref, o4_ref, wg_ref, wb_ref, out_ref, xn_ref, acc_ref, *, tm, ctx):
    i = pl.program_id(1)
    j = pl.program_id(2)
    br = pl.program_id(3)

    @pl.when((j == 0) & (br == 0))
    def _():
        isc = _row_is_ctx(tm, i * tm, ctx)
        xn_ref[...] = _modulated(x_ref[0], ml_ref[0], mc_ref[0], 0, 1, isc).astype(BF16)

    val = _sigmoid(_dot(xn_ref[...], wg_ref[...])) * _dot(o4_ref[0, 0], wb_ref[0])

    @pl.when(br == 0)
    def _():
        acc_ref[...] = val

    @pl.when(br > 0)
    def _():
        acc_ref[...] += val

    @pl.when(br == N_BRANCH - 1)
    def _():
        out_ref[0] = acc_ref[...].astype(out_ref.dtype)


def _merge_a(x_all, modl, modc, o4, wg, wb, ctx):
    b, ta, d = x_all.shape
    tm = _tile(ta, 544)
    tn = _tile(d, 512, LANE)
    return pl.pallas_call(
        functools.partial(_merge_a_kernel, tm=tm, ctx=ctx),
        grid=(b, ta // tm, d // tn, N_BRANCH),
        in_specs=[pl.BlockSpec((1, tm, d), lambda bb, i, j, br: (bb, i, 0)),
                  pl.BlockSpec((1, 6, d), lambda bb, i, j, br: (bb, 0, 0)),
                  pl.BlockSpec((1, 6, d), lambda bb, i, j, br: (0, 0, 0)),
                  pl.BlockSpec((1, 1, tm, BRANCH_W), lambda bb, i, j, br: (br, bb, i, 0)),
                  pl.BlockSpec((d, tn), lambda bb, i, j, br: (0, br * (d // tn) + j)),
                  pl.BlockSpec((1, BRANCH_W, tn), lambda bb, i, j, br: (br, 0, j))],
        out_specs=pl.BlockSpec((1, tm, tn), lambda bb, i, j, br: (bb, i, j)),
        out_shape=jax.ShapeDtypeStruct((b, ta, d), BF16),
        scratch_shapes=[pltpu.VMEM((tm, d), BF16), pltpu.VMEM((tm, tn), F32)],
        compiler_params=_cparams(("parallel", "parallel", "arbitrary", "arbitrary")),
        name="merge_gate",
    )(x_all, modl, modc, o4, wg, wb)


def _merge_b_kernel(m_ref, w_ref, x_ref, ml_ref, mc_ref, lg_ref, lb_ref, xo_ref, h2_ref, *, tm, ctx, alpha):
    i = pl.program_id(1)
    isc = _row_is_ctx(tm, i * tm, ctx)
    ml = ml_ref[0]
    mc = mc_ref[0]
    y = _dot(m_ref[0], w_ref[...])
    g1 = jnp.where(isc, mc[2:3], ml[2:3])
    xn = _ln(alpha * x_ref[0] + g1 * y, POST_LN_EPS) * lg_ref[...] + lb_ref[...]
    xo_ref[0] = xn
    h2_ref[0] = _pack_halves(_modulated(xn, ml, mc, 3, 4, isc))


def _merge_b(merged, w_out, x_all, modl, modc, ln_g, ln_b, ctx, alpha):
    b, ta, d = x_all.shape
    tm = _tile(ta, 272)
    row = lambda bb, i: (bb, i, 0)
    return pl.pallas_call(
        functools.partial(_merge_b_kernel, tm=tm, ctx=ctx, alpha=alpha),
        grid=(b, ta // tm),
        in_specs=[pl.BlockSpec((1, tm, d), row),
                  pl.BlockSpec((d, d), lambda bb, i: (0, 0)),
                  pl.BlockSpec((1, tm, d), row),
                  pl.BlockSpec((1, 6, d), lambda bb, i: (bb, 0, 0)),
                  pl.BlockSpec((1, 6, d), lambda bb, i: (0, 0, 0)),
                  pl.BlockSpec((1, d), lambda bb, i: (0, 0)),
                  pl.BlockSpec((1, d), lambda bb, i: (0, 0))],
        out_specs=[pl.BlockSpec((1, tm, d), row), pl.BlockSpec((1, tm, d // 2), row)],
        out_shape=[jax.ShapeDtypeStruct((b, ta, d), F32), jax.ShapeDtypeStruct((b, ta, d // 2), jnp.uint32)],
        compiler_params=_cparams(("parallel", "parallel")),
        name="merge_out_ln",
    )(merged, w_out, x_all, modl, modc, ln_g.reshape(1, d), ln_b.reshape(1, d))


def _router_kernel(x_ref, ml_ref, mc_ref, rw_ref, rb_ref, idx_ref, wt_ref, rank_ref, cnt_ref, carry_ref,
                   *, tm, ctx):
    i = pl.program_id(1)

    @pl.when((pl.program_id(0) == 0) & (i == 0))
    def _():
        carry_ref[...] = jnp.zeros_like(carry_ref)

    isc = _row_is_ctx(tm, i * tm, ctx)
    h2 = _modulated(x_ref[0], ml_ref[0], mc_ref[0], 3, 4, isc)
    scores = _sigmoid(_dot(h2, rw_ref[...], precision=HIGHEST))
    biased = scores + rb_ref[...]
    lane = lax.broadcasted_iota(jnp.int32, scores.shape, 1)
    grp = lane // GROUP_SIZE
    ninf = -jnp.inf

    def first_argmax(z):
        m = jnp.max(z, -1, keepdims=True)
        idx = jnp.min(jnp.where(z == m, lane, N_EXPERTS), -1, keepdims=True)
        return m, idx

    gscore = jnp.zeros_like(biased)
    for g in range(N_GROUPS):
        zg = jnp.where(grp == g, biased, ninf)
        m1, i1 = first_argmax(zg)
        m2 = jnp.max(jnp.where(lane == i1, ninf, zg), -1, keepdims=True)
        gscore = jnp.where(grp == g, m1 + m2, gscore)
    cand = jnp.where(lane % GROUP_SIZE == 0, gscore, ninf)
    gsel = jnp.zeros(scores.shape, jnp.bool_)
    for _ in range(TOPK_GROUPS):
        _, ig = first_argmax(cand)
        gsel = gsel | (grp == ig // GROUP_SIZE)
        cand = jnp.where(lane == ig, ninf, cand)
    masked = jnp.where(gsel, biased, ninf)
    out_lane = lax.broadcasted_iota(jnp.int32, (tm, LANE), 1)
    idx_out = jnp.zeros((tm, LANE), jnp.int32)
    w_out = jnp.zeros((tm, LANE), F32)
    w_sum = jnp.zeros((tm, 1), F32)
    chosen = []
    picked = jnp.zeros(scores.shape, F32)
    for kq in range(TOP_K):
        _, ie = first_argmax(masked)
        sel = lane == ie
        wk = jnp.sum(jnp.where(sel, scores, 0.0), -1, keepdims=True)
        masked = jnp.where(sel, ninf, masked)
        idx_out = jnp.where(out_lane == kq, ie, idx_out)
        w_out = jnp.where(out_lane == kq, wk, w_out)
        w_sum = w_sum + wk
        chosen.append(sel)
        picked = jnp.where(sel, 1.0, picked)
    idx_ref[0] = idx_out
    wt_ref[0] = w_out / w_sum * ROUTED_SCALE
    r_i = lax.broadcasted_iota(jnp.int32, (tm, tm), 0)
    c_i = lax.broadcasted_iota(jnp.int32, (tm, tm), 1)
    before = _dot(jnp.where(r_i > c_i, 1.0, 0.0).astype(BF16), picked.astype(BF16)) + carry_ref[...]
    rank_out = jnp.zeros((tm, LANE), jnp.int32)
    for kq in range(TOP_K):
        rk = jnp.sum(jnp.where(chosen[kq], before, 0.0), -1, keepdims=True)
        rank_out = jnp.where(out_lane == kq, rk.astype(jnp.int32), rank_out)
    rank_ref[0] = rank_out
    carry_ref[...] += jnp.sum(picked, axis=0, keepdims=True)
    cnt_ref[...] = carry_ref[...].astype(jnp.int32)


def _router(x_all, modl, modc, rw, rb, ctx):
    b, ta, d = x_all.shape
    tm = _tile(ta, 544)
    row = lambda bb, i: (bb, i, 0)
    tok = jax.ShapeDtypeStruct((b, ta, LANE), jnp.int32)
    return pl.pallas_call(
        functools.partial(_router_kernel, tm=tm, ctx=ctx),
        grid=(b, ta // tm),
        in_specs=[pl.BlockSpec((1, tm, d), row),
                  pl.BlockSpec((1, 6, d), lambda bb, i: (bb, 0, 0)),
                  pl.BlockSpec((1, 6, d), lambda bb, i: (0, 0, 0)),
                  pl.BlockSpec((d, N_EXPERTS), lambda bb, i: (0, 0)),
                  pl.BlockSpec((1, N_EXPERTS), lambda bb, i: (0, 0))],
        out_specs=[pl.BlockSpec((1, tm, LANE), row), pl.BlockSpec((1, tm, LANE), row),
                   pl.BlockSpec((1, tm, LANE), row), pl.BlockSpec((1, N_EXPERTS), lambda bb, i: (0, 0))],
        out_shape=[tok, jax.ShapeDtypeStruct((b, ta, LANE), F32), tok,
                   jax.ShapeDtypeStruct((1, N_EXPERTS), jnp.int32)],
        scratch_shapes=[pltpu.VMEM((1, N_EXPERTS), F32)],
        compiler_params=_cparams(("arbitrary", "arbitrary")),
        name="router",
    )(x_all, modl, modc, rw, rb.reshape(1, N_EXPERTS))


def _row_copy(src_ref, src_row, dst_ref, dst_row, sem):
    return pltpu.make_async_copy(src_ref.at[pl.ds(src_row, 1)], dst_ref.at[pl.ds(dst_row, 1)], sem)


def _dispatch_kernel(pos_ref, h2_ref, xg_in_ref, xg_ref, sem, *, tm):
    del xg_in_ref
    i = pl.program_id(0)

    def issue(t, carry):
        for kq in range(TOP_K):
            _row_copy(h2_ref, i * tm + t, xg_ref, pos_ref[0, 0, t * TOP_K + kq], sem).start()
        return carry

    lax.fori_loop(0, tm, issue, 0)

    def drain(t, carry):
        _row_copy(h2_ref, 0, xg_ref, 0, sem).wait()
        return carry

    lax.fori_loop(0, tm * TOP_K, drain, 0)


def _dispatch(pos, h2p, n_rows, tm):
    n, half = h2p.shape
    xg0 = jnp.zeros((n_rows, half), jnp.uint32)
    return pl.pallas_call(
        functools.partial(_dispatch_kernel, tm=tm),
        grid=(n // tm,),
        in_specs=[pl.BlockSpec((1, 1, tm * TOP_K), lambda i: (i, 0, 0), memory_space=pltpu.SMEM),
                  pl.BlockSpec(memory_space=pl.ANY),
                  pl.BlockSpec(memory_space=pl.ANY)],
        out_specs=pl.BlockSpec(memory_space=pl.ANY),
        out_shape=jax.ShapeDtypeStruct((n_rows, half), jnp.uint32),
        scratch_shapes=[pltpu.SemaphoreType.DMA(())],
        input_output_aliases={2: 0},
        compiler_params=_cparams(("arbitrary",)),
        name="moe_dispatch",
    )(pos, h2p, xg0)


def _swiglu(x_lo, x_hi, wgu, wdn, ff):
    half = x_lo.shape[1]
    h = _dot(x_lo, wgu[:half, :]) + _dot(x_hi, wgu[half:, :])
    g = h[:, :ff]
    return _dot((g * _sigmoid(g) * h[:, ff:]).astype(BF16), wdn)


def _experts_kernel(be_ref, nu_ref, x_ref, wgu_ref, wdn_ref, o_ref, wgu_s, wdn_s, *, ff):
    i = pl.program_id(0)
    changed = (i == 0) | (be_ref[i] != be_ref[jnp.maximum(i - 1, 0)])

    @pl.when(changed)
    def _():
        wgu_s[...] = wgu_ref[0].astype(BF16)
        wdn_s[...] = wdn_ref[0].astype(BF16)

    @pl.when(i < nu_ref[0])
    def _():
        lo, hi = _unpack_halves(x_ref[...])
        o_ref[...] = _pack_halves(_swiglu(lo.astype(BF16), hi.astype(BF16), wgu_s[...], wdn_s[...], ff))

    @pl.when(i >= nu_ref[0])
    def _():
        o_ref[...] = jnp.zeros_like(o_ref)


def _experts(xg, blk_e, n_used, w_gu, w_dn, blk):
    n_rows, half = xg.shape
    d = 2 * half
    ff = w_dn.shape[1]
    n_blocks = n_rows // blk
    grid_spec = pltpu.PrefetchScalarGridSpec(
        num_scalar_prefetch=2,
        grid=(n_blocks,),
        in_specs=[pl.BlockSpec((blk, half), lambda i, be, nu: (i, 0)),
                  pl.BlockSpec((1, d, 2 * ff), lambda i, be, nu: (be[i], 0, 0)),
                  pl.BlockSpec((1, ff, d), lambda i, be, nu: (be[i], 0, 0))],
        out_specs=pl.BlockSpec((blk, half), lambda i, be, nu: (i, 0)),
        scratch_shapes=[pltpu.VMEM((d, 2 * ff), BF16), pltpu.VMEM((ff, d), BF16)],
    )
    return pl.pallas_call(
        functools.partial(_experts_kernel, ff=ff),
        grid_spec=grid_spec,
        out_shape=jax.ShapeDtypeStruct((n_rows, half), jnp.uint32),
        compiler_params=_cparams(("arbitrary",)),
        name="moe_experts",
    )(blk_e, n_used, xg, w_gu, w_dn)


def _moe_plan(idx, rank, counts, blk):
    n_assign = idx.size
    padded = (counts + blk - 1) // blk * blk
    pad_end = jnp.cumsum(padded)
    start_pad = pad_end - padded
    experts = jnp.arange(N_EXPERTS, dtype=jnp.int32)
    start_of = jnp.sum(jnp.where(idx[..., None] == experts, start_pad, 0), axis=-1)
    pos = (start_of + rank).astype(jnp.int32)
    n_blocks = -(-(n_assign + N_EXPERTS * (blk - 1)) // blk)
    blk_start = jnp.arange(n_blocks, dtype=jnp.int32) * blk
    blk_e = jnp.minimum(jnp.sum(blk_start[:, None] >= pad_end[None, :], axis=-1), N_EXPERTS - 1).astype(jnp.int32)
    n_used = (pad_end[-1] // blk).astype(jnp.int32).reshape(1)
    return pos, blk_e, n_used, n_blocks * blk


def _combine_kernel(pos_ref, y_ref, wt_ref, h2_ref, sgu_ref, sdn_ref, x_ref, ml_ref, mc_ref, lg_ref, lb_ref,
                    xo_ref, buf, sem, *, tm, ctx, alpha, ff):
    i = pl.program_id(1)

    def issue(t, carry):
        for kq in range(TOP_K):
            _row_copy(y_ref, pos_ref[0, 0, t * TOP_K + kq], buf.at[kq], t, sem).start()
        return carry

    lax.fori_loop(0, tm, issue, 0)
    lo, hi = _unpack_halves(h2_ref[0])
    f = _swiglu(lo.astype(BF16), hi.astype(BF16), sgu_ref[...], sdn_ref[...], ff)

    def drain(t, carry):
        _row_copy(y_ref, 0, buf.at[0], 0, sem).wait()
        return carry

    lax.fori_loop(0, tm * TOP_K, drain, 0)
    wt = wt_ref[0]
    half = f.shape[1] // 2
    f_lo = f[:, :half]
    f_hi = f[:, half:]
    for kq in range(TOP_K):
        lo, hi = _unpack_halves(buf[kq])
        f_lo = f_lo + wt[:, kq:kq + 1] * lo
        f_hi = f_hi + wt[:, kq:kq + 1] * hi
    f = jnp.concatenate([f_lo, f_hi], axis=1)
    isc = _row_is_ctx(tm, i * tm, ctx)
    g2 = jnp.where(isc, mc_ref[0][5:6], ml_ref[0][5:6])
    xo_ref[0] = _ln(alpha * x_ref[0] + g2 * f, POST_LN_EPS) * lg_ref[...] + lb_ref[...]


def _combine(pos, y_rows, wts, h2p, sh_gu, sh_dn, x_all, modl, modc, ln_g, ln_b, ctx, alpha, tm):
    b, ta, d = x_all.shape
    ff = sh_dn.shape[0]
    nt = ta // tm
    row = lambda bb, i: (bb, i, 0)
    return pl.pallas_call(
        functools.partial(_combine_kernel, tm=tm, ctx=ctx, alpha=alpha, ff=ff),
        grid=(b, nt),
        in_specs=[pl.BlockSpec((1, 1, tm * TOP_K), lambda bb, i: (bb * nt + i, 0, 0), memory_space=pltpu.SMEM),
                  pl.BlockSpec(memory_space=pl.ANY),
                  pl.BlockSpec((1, tm, LANE), row),
                  pl.BlockSpec((1, tm, d // 2), row),
                  pl.BlockSpec((d, 2 * ff), lambda bb, i: (0, 0)),
                  pl.BlockSpec((ff, d), lambda bb, i: (0, 0)),
                  pl.BlockSpec((1, tm, d), row),
                  pl.BlockSpec((1, 6, d), lambda bb, i: (bb, 0, 0)),
                  pl.BlockSpec((1, 6, d), lambda bb, i: (0, 0, 0)),
                  pl.BlockSpec((1, d), lambda bb, i: (0, 0)),
                  pl.BlockSpec((1, d), lambda bb, i: (0, 0))],
        out_specs=pl.BlockSpec((1, tm, d), row),
        out_shape=jax.ShapeDtypeStruct((b, ta, d), F32),
        scratch_shapes=[pltpu.VMEM((TOP_K, tm, d // 2), jnp.uint32), pltpu.SemaphoreType.DMA(())],
        compiler_params=_cparams(("arbitrary", "arbitrary")),
        name="moe_combine_ln",
    )(pos, y_rows, wts, h2p, sh_gu, sh_dn, x_all, modl, modc, ln_g.reshape(1, d), ln_b.reshape(1, d))


def _rope_tables(n_lat, ctx):
    rows = n_lat // GRID_W
    row = jnp.repeat(jnp.arange(rows, dtype=F32), GRID_W)
    col = jnp.tile(jnp.arange(GRID_W, dtype=F32), rows)
    n_freq = ROPE_DIM // 4
    inv_freq = ROPE_BASE ** (-jnp.arange(n_freq, dtype=F32) / n_freq)
    ang = jnp.concatenate([row[:, None] * inv_freq, col[:, None] * inv_freq], -1)
    cos, sin = jnp.cos(ang), jnp.sin(ang)
    cos = jnp.concatenate([jnp.ones((ctx, ROPE_DIM // 2), F32), cos], 0)
    sin = jnp.concatenate([jnp.zeros((ctx, ROPE_DIM // 2), F32), sin], 0)
    cosf = jnp.tile(jnp.concatenate([cos, cos], -1), (1, LANE // ROPE_DIM))
    sinf = jnp.tile(jnp.concatenate([-sin, sin], -1), (1, LANE // ROPE_DIM))
    return cosf, sinf


def _pack_small_weights(w_in_l):
    d = w_in_l.shape[0]
    offs = np.cumsum([0, MLA_Q_RANK, MLA_KV_RANK + MLA_ROPE, RWKV_STREAM,
                      (SWA_HEADS + 2 * SWA_KV_HEADS) * SWA_HEAD, 3 * DIFF_HEADS * 2 * DIFF_HEAD])
    placed = sorted(zip([SEG_MQ, SEG_MKV, SEG_RW, SEG_SW, SEG_DF], range(5)))
    parts = []
    cur = 0
    for start, s in placed:
        if start > cur:
            parts.append(jnp.zeros((d, start - cur), BF16))
        parts.append(w_in_l[:, int(offs[s]):int(offs[s + 1])].astype(BF16))
        cur = start + int(offs[s + 1] - offs[s])
    parts.append(jnp.zeros((d, SMALL_COLS - cur), BF16))
    return jnp.concatenate(parts, axis=1), int(offs[5])


def kernel(x, c, ctx, c_ctx, w_mod, b_mod, w_in, mla_q_norm, mla_w_qup, mla_kv_norm, mla_w_kvup, rwkv_mu, rwkv_w0, rwkv_w_lora, rwkv_a0, rwkv_a_lora, rwkv_g_lora, rwkv_k_k, rwkv_k_a, rwkv_r_k, rwkv_ln_g, rwkv_ln_b, swa_sink, diff_lambda, diff_subln, w_branch, w_out, ln1_g, ln1_b, router_w, router_bias, exp_w_gu, exp_w_dn, sh_w_gu, sh_w_dn, ln2_g, ln2_b):
    depth = w_mod.shape[0]
    b, n_lat, d = x.shape
    n_ctx = ctx.shape[1]
    ta = n_ctx + n_lat
    alpha = (2 * depth) ** 0.25
    cosf, sinf = _rope_tables(n_lat, n_ctx)
    x_all = jnp.concatenate([ctx, x], axis=1)

    cond = jnp.concatenate([c, c_ctx[None]], axis=0)
    cond = jnp.pad(jax.nn.silu(cond), ((0, SUBLANE - (b + 1) % SUBLANE), (0, 0)))

    for l in range(depth):
        mod = _matmul_bias(cond, w_mod[l], b_mod[l]).reshape(cond.shape[0], 6, d)
        modl, modc = mod[:b], mod[b:b + 1]

        w_small, gate_off = _pack_small_weights(w_in[l])
        p = _inproj(x_all, modl, modc, w_small, n_ctx)

        wq = mla_w_qup[l].reshape(MLA_Q_RANK, MLA_HEADS, MLA_NOPE + MLA_ROPE)
        wq = jnp.pad(wq, ((0, 0), (0, 0), (0, 256 - MLA_NOPE - MLA_ROPE))).reshape(MLA_Q_RANK, 1024)
        q_a, k_a, v_a = _mla_prep(p, mla_q_norm[l].reshape(1, -1), mla_kv_norm[l].reshape(1, -1),
                                  wq.astype(BF16), mla_w_kvup[l].astype(BF16), cosf, sinf)
        out_a = _mla_attn(q_a, k_a, v_a, n_ctx)

        out_b = _rwkv_mixer(p, n_ctx, rwkv_mu[l], rwkv_w0[l],
                            rwkv_w_lora[l], rwkv_a0[l], rwkv_a_lora[l], rwkv_g_lora[l], rwkv_k_k[l],
                            rwkv_k_a[l], rwkv_r_k[l], rwkv_ln_g[l], rwkv_ln_b[l])

        sinkv = jnp.broadcast_to(swa_sink[l][:, None], (SWA_HEADS, LANE))
        out_s = _swa(p, cosf, sinf, sinkv, n_ctx)

        lf = diff_lambda[l]
        lam_init = 0.8 - 0.6 * math.exp(-0.3 * l)
        lam = jnp.exp(jnp.sum(lf[0] * lf[1])) - jnp.exp(jnp.sum(lf[2] * lf[3])) + lam_init
        out_d = _diff_attn(p, cosf, sinf, jnp.full((1, LANE), lam, F32), diff_subln[l].reshape(1, LANE),
                           n_ctx, lam_init)

        o4 = jnp.stack([out_a, out_b, out_s, out_d], axis=0)
        wg = w_in[l][:, gate_off:].astype(BF16)
        merged = _merge_a(x_all, modl, modc, o4, wg, w_branch[l].astype(BF16), n_ctx)
        x_all, h2p = _merge_b(merged, w_out[l].astype(BF16), x_all, modl, modc, ln1_g[l], ln1_b[l], n_ctx, alpha)

        idx, wts, rank, counts = _router(x_all, modl, modc, router_w[l], router_bias[l], n_ctx)
        pos, blk_e, n_used, n_rows = _moe_plan(idx[..., :TOP_K], rank[..., :TOP_K], counts[0], MOE_BLK)
        tmd = _tile(ta, 128)
        pos = pos.reshape(b * ta // tmd, 1, tmd * TOP_K)
        xg = _dispatch(pos, h2p.reshape(b * ta, d // 2), n_rows, tmd)
        y_rows = _experts(xg, blk_e, n_used, exp_w_gu[l], exp_w_dn[l], MOE_BLK)
        x_all = _combine(pos, y_rows, wts, h2p, sh_w_gu[l].astype(BF16), sh_w_dn[l].astype(BF16), x_all,
                         modl, modc, ln2_g[l], ln2_b[l], n_ctx, alpha, tmd)
    return x_all[:, n_ctx:]
```

```python
import functools
import math

import jax
import jax.numpy as jnp
import numpy as np
from jax import lax
from jax.experimental import pallas as pl
from jax.experimental.pallas import tpu as pltpu

F32 = jnp.float32
BF16 = jnp.bfloat16
HIGHEST = lax.Precision.HIGHEST

GRID_W = 64
ROPE_BASE = 10000.0
ROPE_DIM = 64
ADALN_EPS = 1e-6
POST_LN_EPS = 1e-5
NEG_INF = -1e30

MLA_HEADS = 4
MLA_Q_RANK = 384
MLA_KV_RANK = 256
MLA_NOPE = 128
MLA_ROPE = 64
MLA_V = 128

RWKV_HEADS = 8
RWKV_HEAD = 64
RWKV_W = RWKV_HEADS * RWKV_HEAD
DECAY_LORA = 32
ICLR_LORA = 32
GATE_LORA = 96
RWKV_STREAM = 3 * RWKV_W + 2 * DECAY_LORA + 2 * ICLR_LORA + GATE_LORA
RWKV_GN_EPS = 64e-5
RWKV_CHUNK = 64

SWA_HEADS = 8
SWA_KV_HEADS = 2
SWA_GROUP = SWA_HEADS // SWA_KV_HEADS
SWA_HEAD = 64
WINDOW = 128

DIFF_HEADS = 4
DIFF_HEAD = 64

N_BRANCH = 4
BRANCH_W = 512

N_EXPERTS = 64
TOP_K = 6
N_GROUPS = 8
TOPK_GROUPS = 4
GROUP_SIZE = N_EXPERTS // N_GROUPS
ROUTED_SCALE = 2.5
MOE_BLK = 256

LANE = 128
SUBLANE = 8

SEG_SW = 0
SEG_MQ = 768
SEG_MKV = 1152
SEG_DF = 1536
SEG_RW = 3584
RW_COLS = 1792
SMALL_COLS = SEG_RW + RW_COLS


def _tile(n, target, mult=SUBLANE):
    best = None
    for d in range(mult, min(n, target) + 1, mult):
        if n % d == 0:
            best = d
    assert best is not None, (n, target, mult)
    return best


def _cparams(sem, vmem_mb=48):
    return pltpu.CompilerParams(dimension_semantics=sem, vmem_limit_bytes=vmem_mb * 1024 * 1024)


def _sigmoid(x):
    return 1.0 / (1.0 + jnp.exp(-x))


def _ln(x, eps):
    mu = jnp.mean(x, -1, keepdims=True)
    xc = x - mu
    var = jnp.mean(xc * xc, -1, keepdims=True)
    return xc * lax.rsqrt(var + eps)


def _rms(x, eps):
    return x * lax.rsqrt(jnp.mean(x * x, -1, keepdims=True) + eps)


def _row_is_ctx(n_rows, row0, ctx):
    rows = row0 + lax.broadcasted_iota(jnp.int32, (n_rows, 1), 0)
    return rows < ctx


def _modulated(x, ml, mc, sh_i, sc_i, isc):
    sh = jnp.where(isc, mc[sh_i:sh_i + 1], ml[sh_i:sh_i + 1])
    sc = jnp.where(isc, mc[sc_i:sc_i + 1], ml[sc_i:sc_i + 1])
    return _ln(x, ADALN_EPS) * (1.0 + sc) + sh


def _swap32(x):
    lane = lax.broadcasted_iota(jnp.int32, x.shape, 1)
    up = pltpu.roll(x, LANE - 32, 1)
    dn = pltpu.roll(x, 32, 1)
    return jnp.where((lane % 64) < 32, up, dn)


def _rope(x, cosf, sinf):
    return x * cosf + _swap32(x) * sinf


def _pack_halves(x):
    n = x.shape[1] // 2
    lo = pltpu.bitcast(x[:, :n].astype(BF16).astype(F32), jnp.uint32)
    hi = pltpu.bitcast(x[:, n:].astype(BF16).astype(F32), jnp.uint32)
    return (hi & jnp.uint32(0xFFFF0000)) | (lo >> 16)


def _unpack_halves(u):
    lo = pltpu.bitcast(u << 16, F32)
    hi = pltpu.bitcast(u & jnp.uint32(0xFFFF0000), F32)
    return lo, hi


def _dot(a, b, **kw):
    return jnp.dot(a, b, preferred_element_type=F32, **kw)


def _dot_nt(a, b, **kw):
    return lax.dot_general(a, b, (((1,), (1,)), ((), ())), preferred_element_type=F32, **kw)


def _dot_tn(a, b, **kw):
    return lax.dot_general(a, b, (((0,), (0,)), ((), ())), preferred_element_type=F32, **kw)


def _mm_kernel(x_ref, w_ref, b_ref, o_ref):
    o_ref[...] = _dot(x_ref[...], w_ref[...], precision=HIGHEST) + b_ref[...]


def _matmul_bias(x, w, b):
    m, k = x.shape
    n = w.shape[1]
    tn = _tile(n, 1024, LANE)
    return pl.pallas_call(
        _mm_kernel,
        grid=(n // tn,),
        in_specs=[pl.BlockSpec((m, k), lambda j: (0, 0)),
                  pl.BlockSpec((k, tn), lambda j: (0, j)),
                  pl.BlockSpec((1, tn), lambda j: (0, j))],
        out_specs=pl.BlockSpec((m, tn), lambda j: (0, j)),
        out_shape=jax.ShapeDtypeStruct((m, n), F32),
        compiler_params=_cparams(("arbitrary",)),
        name="mod_matmul",
    )(x, w, b.reshape(1, n))


def _inproj_kernel(x_ref, ml_ref, mc_ref, w_ref, o_ref, xn_ref, *, tm, ctx):
    i = pl.program_id(1)
    j = pl.program_id(2)

    @pl.when(j == 0)
    def _():
        isc = _row_is_ctx(tm, i * tm, ctx)
        xn_ref[...] = _modulated(x_ref[0], ml_ref[0], mc_ref[0], 0, 1, isc).astype(BF16)

    o_ref[0] = _dot(xn_ref[...], w_ref[...])


def _inproj(x_all, modl, modc, w_small, ctx):
    b, ta, d = x_all.shape
    n = w_small.shape[1]
    tm = _tile(ta, 544)
    tn = _tile(n, 1024, LANE)
    return pl.pallas_call(
        functools.partial(_inproj_kernel, tm=tm, ctx=ctx),
        grid=(b, ta // tm, n // tn),
        in_specs=[pl.BlockSpec((1, tm, d), lambda bb, i, j: (bb, i, 0)),
                  pl.BlockSpec((1, 6, d), lambda bb, i, j: (bb, 0, 0)),
                  pl.BlockSpec((1, 6, d), lambda bb, i, j: (0, 0, 0)),
                  pl.BlockSpec((d, tn), lambda bb, i, j: (0, j))],
        out_specs=pl.BlockSpec((1, tm, tn), lambda bb, i, j: (bb, i, j)),
        out_shape=jax.ShapeDtypeStruct((b, ta, n), F32),
        scratch_shapes=[pltpu.VMEM((tm, d), BF16)],
        compiler_params=_cparams(("parallel", "parallel", "arbitrary")),
        name="inproj",
    )(x_all, modl, modc, w_small)


def _mla_prep_kernel(mq_ref, mkv_ref, qg_ref, kg_ref, wq_ref, wkv_ref, cos_ref, sin_ref,
                     q_ref, k_ref, v_ref):
    mq = mq_ref[0]
    mkv = mkv_ref[0]
    cosv = cos_ref[...]
    sinv = sin_ref[...]
    qn = _rms(mq, 1e-6) * qg_ref[...]
    q = _dot(qn.astype(BF16), wq_ref[...])
    kvn = _rms(mkv[:, :MLA_KV_RANK], 1e-6) * kg_ref[...]
    kv = _dot(kvn.astype(BF16), wkv_ref[...])
    kr = _rope(mkv[:, MLA_KV_RANK:MLA_KV_RANK + LANE], cosv, sinv).astype(BF16)
    scale = (MLA_NOPE + MLA_ROPE) ** -0.5
    for h in range(MLA_HEADS):
        c0 = h * 256
        q_ref[0, :, c0:c0 + 128] = (q[:, c0:c0 + 128] * scale).astype(BF16)
        q_ref[0, :, c0 + 128:c0 + 256] = (_rope(q[:, c0 + 128:c0 + 256], cosv, sinv) * scale).astype(BF16)
        k_ref[0, :, c0:c0 + 128] = kv[:, c0:c0 + 128].astype(BF16)
        k_ref[0, :, c0 + 128:c0 + 256] = kr
        v_ref[0, :, h * 128:(h + 1) * 128] = kv[:, c0 + 128:c0 + 256].astype(BF16)


def _mla_prep(p, qg, kg, wq, wkv, cosf, sinf):
    b, ta, _ = p.shape
    tm = _tile(ta, 544)
    return pl.pallas_call(
        _mla_prep_kernel,
        grid=(b, ta // tm),
        in_specs=[pl.BlockSpec((1, tm, 384), lambda bb, i: (bb, i, SEG_MQ // 384)),
                  pl.BlockSpec((1, tm, 384), lambda bb, i: (bb, i, SEG_MKV // 384)),
                  pl.BlockSpec((1, MLA_Q_RANK), lambda bb, i: (0, 0)),
                  pl.BlockSpec((1, MLA_KV_RANK), lambda bb, i: (0, 0)),
                  pl.BlockSpec((MLA_Q_RANK, 1024), lambda bb, i: (0, 0)),
                  pl.BlockSpec((MLA_KV_RANK, 1024), lambda bb, i: (0, 0)),
                  pl.BlockSpec((tm, LANE), lambda bb, i: (i, 0)),
                  pl.BlockSpec((tm, LANE), lambda bb, i: (i, 0))],
        out_specs=[pl.BlockSpec((1, tm, 1024), lambda bb, i: (bb, i, 0)),
                   pl.BlockSpec((1, tm, 1024), lambda bb, i: (bb, i, 0)),
                   pl.BlockSpec((1, tm, 512), lambda bb, i: (bb, i, 0))],
        out_shape=[jax.ShapeDtypeStruct((b, ta, 1024), BF16),
                   jax.ShapeDtypeStruct((b, ta, 1024), BF16),
                   jax.ShapeDtypeStruct((b, ta, 512), BF16)],
        compiler_params=_cparams(("parallel", "parallel")),
        name="mla_prep",
    )(p, p, qg, kg, wq, wkv, cosf, sinf)


def _softmax_pv(s, v):
    m = jnp.max(s, -1, keepdims=True)
    p = jnp.exp(s - m)
    l = jnp.sum(p, -1, keepdims=True)
    return _dot(p.astype(BF16), v) / l


def _mla_attn_kernel(q_ref, k_ref, v_ref, o_ref, *, n_ctx_tiles, ctx):
    qi = pl.program_id(2)

    def run(nk):
        s = _dot_nt(q_ref[0], k_ref[0, :nk, :])
        o_ref[0] = _softmax_pv(s, v_ref[0, :nk, :]).astype(o_ref.dtype)

    @pl.when(qi < n_ctx_tiles)
    def _():
        run(ctx)

    @pl.when(qi >= n_ctx_tiles)
    def _():
        run(k_ref.shape[1])


def _mla_attn(q, k, v, ctx):
    b, ta, _ = q.shape
    tq = _tile(math.gcd(ta, ctx), 256)
    return pl.pallas_call(
        functools.partial(_mla_attn_kernel, n_ctx_tiles=ctx // tq, ctx=ctx),
        grid=(b, MLA_HEADS, ta // tq),
        in_specs=[pl.BlockSpec((1, tq, 256), lambda bb, h, i: (bb, i, h)),
                  pl.BlockSpec((1, ta, 256), lambda bb, h, i: (bb, 0, h)),
                  pl.BlockSpec((1, ta, 128), lambda bb, h, i: (bb, 0, h))],
        out_specs=pl.BlockSpec((1, tq, 128), lambda bb, h, i: (bb, i, h)),
        out_shape=jax.ShapeDtypeStruct((b, ta, 512), BF16),
        compiler_params=_cparams(("parallel", "parallel", "arbitrary")),
        name="mla_attn",
    )(q, k, v)


def _diff_attn_kernel(q_ref, k_ref, v_ref, cq_ref, sq_ref, ck_ref, sk_ref, lam_ref, sub_ref,
                      o_ref, ks_ref, vs_ref, *, n_ctx_tiles, ctx, post_scale):
    qi = pl.program_id(2)

    @pl.when(qi == 0)
    def _():
        ks_ref[...] = _rope(k_ref[0], ck_ref[...], sk_ref[...]).astype(BF16)
        vs_ref[...] = v_ref[0].astype(BF16)

    def run(nk):
        q = _rope(q_ref[0], cq_ref[...], sq_ref[...]) * (DIFF_HEAD ** -0.5)
        lane = lax.broadcasted_iota(jnp.int32, q.shape, 1)
        q1 = jnp.where(lane < 64, q, 0.0).astype(BF16)
        q2 = jnp.where(lane >= 64, q, 0.0).astype(BF16)
        k = ks_ref[:nk, :]
        v = vs_ref[:nk, :]
        o = _softmax_pv(_dot_nt(q1, k), v) - lam_ref[...] * _softmax_pv(_dot_nt(q2, k), v)
        o = _rms(o, 1e-5) * sub_ref[...] * post_scale
        o_ref[0] = o.astype(o_ref.dtype)

    @pl.when(qi < n_ctx_tiles)
    def _():
        run(ctx)

    @pl.when(qi >= n_ctx_tiles)
    def _():
        run(ks_ref.shape[0])


def _diff_attn(p, cosf, sinf, lamv, subln, ctx, lam_init):
    b, ta, _ = p.shape
    tq = _tile(math.gcd(ta, ctx), 256)
    cb = SEG_DF // LANE
    return pl.pallas_call(
        functools.partial(_diff_attn_kernel, n_ctx_tiles=ctx // tq, ctx=ctx, post_scale=1.0 - lam_init),
        grid=(b, DIFF_HEADS, ta // tq),
        in_specs=[pl.BlockSpec((1, tq, LANE), lambda bb, h, i: (bb, i, cb + h)),
                  pl.BlockSpec((1, ta, LANE), lambda bb, h, i: (bb, 0, cb + DIFF_HEADS + h)),
                  pl.BlockSpec((1, ta, LANE), lambda bb, h, i: (bb, 0, cb + 2 * DIFF_HEADS + h)),
                  pl.BlockSpec((tq, LANE), lambda bb, h, i: (i, 0)),
                  pl.BlockSpec((tq, LANE), lambda bb, h, i: (i, 0)),
                  pl.BlockSpec((ta, LANE), lambda bb, h, i: (0, 0)),
                  pl.BlockSpec((ta, LANE), lambda bb, h, i: (0, 0)),
                  pl.BlockSpec((1, LANE), lambda bb, h, i: (0, 0)),
                  pl.BlockSpec((1, LANE), lambda bb, h, i: (0, 0))],
        out_specs=pl.BlockSpec((1, tq, LANE), lambda bb, h, i: (bb, i, h)),
        out_shape=jax.ShapeDtypeStruct((b, ta, 512), BF16),
        scratch_shapes=[pltpu.VMEM((ta, LANE), BF16), pltpu.VMEM((ta, LANE), BF16)],
        compiler_params=_cparams(("parallel", "parallel", "arbitrary")),
        name="diff_attn",
    )(p, p, p, cosf, sinf, cosf, sinf, lamv, subln)


def _swa_kernel(q_ref, k_ref, v_ref, cq_ref, sq_ref, ck_ref, sk_ref, sink_ref,
                o_ref, ks_ref, vs_ref, *, ctx):
    t = pl.program_id(1)
    n_ctx_tiles = ctx // WINDOW
    ta = ks_ref.shape[0]
    span = 3 * WINDOW

    @pl.when(t == 0)
    def _():
        ks_ref[...] = _rope(k_ref[0], ck_ref[...], sk_ref[...]).astype(BF16)
        vs_ref[...] = v_ref[0].astype(BF16)

    lane = lax.broadcasted_iota(jnp.int32, (WINDOW, LANE), 1)
    cq = cq_ref[...]
    sq = sq_ref[...]
    qblk = [_rope(q_ref[0, :, hb * LANE:(hb + 1) * LANE], cq, sq) * (SWA_HEAD ** -0.5)
            for hb in range(SWA_HEADS // 2)]

    def head_q(hk, g):
        h = hk * SWA_GROUP + g
        blk = qblk[h // 2]
        if h % 2 != hk:
            blk = pltpu.roll(blk, 64, 1)
        keep = (lane >= 64) if hk == 1 else (lane < 64)
        return jnp.where(keep, blk, 0.0)

    def run(local):
        kc = ks_ref[:ctx, :]
        vc = vs_ref[:ctx, :]
        if local:
            n = t - n_ctx_tiles
            ws = jnp.clip(ctx + (n - 1) * WINDOW, ctx, ta - span)
            ws = pl.multiple_of(ws, WINDOW)
            kl = ks_ref[pl.ds(ws, span), :]
            vl = vs_ref[pl.ds(ws, span), :]
            kpos = (ws - ctx) + lax.broadcasted_iota(jnp.int32, (1, span), 1)
            qpos = n * WINDOW + lax.broadcasted_iota(jnp.int32, (SWA_GROUP * WINDOW, 1), 0) % WINDOW
            valid = jnp.abs(kpos - qpos) <= WINDOW
        outs = {}
        for hk in range(SWA_KV_HEADS):
            qst = jnp.concatenate([head_q(hk, g) for g in range(SWA_GROUP)], axis=0).astype(BF16)
            sink = jnp.concatenate(
                [jnp.broadcast_to(sink_ref[hk * SWA_GROUP + g:hk * SWA_GROUP + g + 1, 0:1], (WINDOW, 1))
                 for g in range(SWA_GROUP)], axis=0)
            s_c = _dot_nt(qst, kc)
            m = jnp.maximum(jnp.max(s_c, -1, keepdims=True), sink)
            if local:
                s_l = jnp.where(valid, _dot_nt(qst, kl), NEG_INF)
                m = jnp.maximum(m, jnp.max(s_l, -1, keepdims=True))
            p_c = jnp.exp(s_c - m)
            l = jnp.sum(p_c, -1, keepdims=True) + jnp.exp(sink - m)
            acc = _dot(p_c.astype(BF16), vc)
            if local:
                p_l = jnp.exp(s_l - m)
                l = l + jnp.sum(p_l, -1, keepdims=True)
                acc = acc + _dot(p_l.astype(BF16), vl)
            o = acc / l
            for g in range(SWA_GROUP):
                res = o[g * WINDOW:(g + 1) * WINDOW, :]
                if g % 2 != hk:
                    res = pltpu.roll(res, 64, 1)
                outs[hk * SWA_GROUP + g] = res
        for hb in range(SWA_HEADS // 2):
            blk = jnp.where(lane < 64, outs[2 * hb], outs[2 * hb + 1])
            o_ref[0, :, hb * LANE:(hb + 1) * LANE] = blk.astype(o_ref.dtype)

    @pl.when(t < n_ctx_tiles)
    def _():
        run(False)

    @pl.when(t >= n_ctx_tiles)
    def _():
        run(True)


def _swa(p, cosf, sinf, sinkv, ctx):
    b, ta, _ = p.shape
    qb = SEG_SW // 512
    kb = (SEG_SW + 512) // LANE
    return pl.pallas_call(
        functools.partial(_swa_kernel, ctx=ctx),
        grid=(b, ta // WINDOW),
        in_specs=[pl.BlockSpec((1, WINDOW, 512), lambda bb, t: (bb, t, qb)),
                  pl.BlockSpec((1, ta, LANE), lambda bb, t: (bb, 0, kb)),
                  pl.BlockSpec((1, ta, LANE), lambda bb, t: (bb, 0, kb + 1)),
                  pl.BlockSpec((WINDOW, LANE), lambda bb, t: (t, 0)),
                  pl.BlockSpec((WINDOW, LANE), lambda bb, t: (t, 0)),
                  pl.BlockSpec((ta, LANE), lambda bb, t: (0, 0)),
                  pl.BlockSpec((ta, LANE), lambda bb, t: (0, 0)),
                  pl.BlockSpec((SWA_HEADS, LANE), lambda bb, t: (0, 0))],
        out_specs=pl.BlockSpec((1, WINDOW, 512), lambda bb, t: (bb, t, 0)),
        out_shape=jax.ShapeDtypeStruct((b, ta, 512), BF16),
        scratch_shapes=[pltpu.VMEM((ta, LANE), BF16), pltpu.VMEM((ta, LANE), BF16)],
        compiler_params=_cparams(("parallel", "arbitrary")),
        name="swa_attn",
    )(p, p, p, cosf, sinf, cosf, sinf, sinkv)


def _softplus(x):
    return jnp.maximum(x, 0.0) + jnp.log(1.0 + jnp.exp(-jnp.abs(x)))


def _rwkv_prep_kernel(z_ref, hp_ref, hn_ref, mu_ref, w0_ref, a0_ref, wl_ref, al_ref, gl_ref, kk_ref, ka_ref,
                      rk_ref, bd_ref, r_o, v_o, kk_o, ld0_o, ld1_o, kd0_o, kd1_o, a0_o, a1_o, bv_o, g_o, zs_ref):
    tm = z_ref.shape[1]
    w = RWKV_W
    z = z_ref[0]
    zs_ref[SUBLANE:SUBLANE + tm, :] = z
    zs_ref[SUBLANE - 1:SUBLANE, :] = hp_ref[0, 0]
    zs_ref[SUBLANE + tm:SUBLANE + tm + 1, :] = hn_ref[0, 0]
    shifted = 0.5 * (zs_ref[SUBLANE - 1:SUBLANE - 1 + tm, :] + zs_ref[SUBLANE + 1:SUBLANE + 1 + tm, :])
    seg = z + (shifted - z) * mu_ref[...]
    r = seg[:, 0:w]
    k = seg[:, w:2 * w]
    v = seg[:, 2 * w:3 * w]
    lo = seg[:, 3 * w:3 * w + LANE]
    gl = seg[:, 3 * w + LANE:3 * w + 2 * LANE]
    hp = dict(precision=HIGHEST)
    bd = bd_ref[...]
    kk = k * kk_ref[...]
    kk = kk / jnp.maximum(jnp.sqrt(_dot(kk * kk, bd, **hp)), 1e-12)
    th = jnp.tanh(lo)
    ksum = jnp.zeros_like(k)
    for d, (ld_o, kd_o, a_o) in enumerate(((ld0_o, kd0_o, a0_o), (ld1_o, kd1_o, a1_o))):
        wd = w0_ref[d:d + 1, :] + _dot(th, wl_ref[d], **hp)
        wd = -_softplus(-wd) - 0.5
        a = _sigmoid(a0_ref[d:d + 1, :] + _dot(lo, al_ref[d], **hp))
        kd = k * (1.0 + (a - 1.0) * ka_ref[...])
        ld_o[0] = -jnp.exp(wd)
        kd_o[0] = kd
        a_o[0] = a
        ksum = ksum + kd
    r_o[0] = r
    v_o[0] = v
    kk_o[0] = kk
    bv_o[0] = _dot(r * ksum * rk_ref[...], bd, **hp) * v
    g_o[0] = _dot(_sigmoid(gl), gl_ref[...], **hp)


def _rwkv_prep(p, hprev, hnext, mu, w0, a0, wl, al, gl, k_k, k_a, r_k, bd, tm):
    b, ta, _ = p.shape
    w = RWKV_W
    row = lambda bb, i: (bb, i, 0)
    vec = lambda n: pl.BlockSpec((1, n), lambda bb, i: (0, 0))
    out = jax.ShapeDtypeStruct((b, ta, w), F32)
    return pl.pallas_call(
        _rwkv_prep_kernel,
        grid=(b, ta // tm),
        in_specs=[pl.BlockSpec((1, tm, RW_COLS), lambda bb, i: (bb, i, SEG_RW // RW_COLS)),
                  pl.BlockSpec((1, 1, 1, RW_COLS), lambda bb, i: (bb, i, 0, 0)),
                  pl.BlockSpec((1, 1, 1, RW_COLS), lambda bb, i: (bb, i, 0, 0)),
                  vec(RW_COLS),
                  pl.BlockSpec((2, w), lambda bb, i: (0, 0)),
                  pl.BlockSpec((2, w), lambda bb, i: (0, 0)),
                  pl.BlockSpec((2, LANE, w), lambda bb, i: (0, 0, 0)),
                  pl.BlockSpec((2, LANE, w), lambda bb, i: (0, 0, 0)),
                  pl.BlockSpec((LANE, w), lambda bb, i: (0, 0)),
                  vec(w), vec(w), vec(w),
                  pl.BlockSpec((w, w), lambda bb, i: (0, 0))],
        out_specs=[pl.BlockSpec((1, tm, w), row)] * 11,
        out_shape=[out] * 11,
        scratch_shapes=[pltpu.VMEM((tm + 2 * SUBLANE, RW_COLS), F32)],
        compiler_params=_cparams(("parallel", "parallel")),
        name="rwkv_prep",
    )(p, hprev, hnext, mu, w0, a0, wl, al, gl, k_k, k_a, r_k, bd)


def _split3(x):
    hi = x.astype(BF16)
    r1 = x - hi.astype(F32)
    mid = r1.astype(BF16)
    return hi, mid, (r1 - mid.astype(F32)).astype(BF16)


def _split2(x):
    hi = x.astype(BF16)
    return hi, (x - hi.astype(F32)).astype(BF16)


def _mm3(a, b, dot=_dot):
    return dot(a[0], b[0]) + dot(a[0], b[1]) + dot(a[1], b[0])


def _rwkv_chunk_kernel(r_ref, v_ref, kk_ref, ld0_ref, ld1_ref, kd0_ref, kd1_ref, a0_ref, a1_ref, pz_ref, wu_ref):
    C = RWKV_CHUNK
    hd = RWKV_HEAD
    row = lax.broadcasted_iota(jnp.int32, (C, C), 0)
    col = lax.broadcasted_iota(jnp.int32, (C, C), 1)
    eye = (row == col).astype(F32)
    probs = []
    for d, (ld_ref, kd_ref, a_ref) in enumerate(((ld0_ref, kd0_ref, a0_ref), (ld1_ref, kd1_ref, a1_ref))):
        incl = (row >= col) if d == 0 else (row <= col)
        strict = (row > col) if d == 0 else (row < col)
        last = C - 1 if d == 0 else 0
        for h in range(RWKV_HEADS):
            sl = slice(h * hd, (h + 1) * hd)
            probs.append(dict(d=d, h=h, incl=incl, strict=strict, last=last,
                              ones=jnp.where(incl, 1.0, 0.0).astype(BF16),
                              r=r_ref[0, :, sl], v=v_ref[0, :, sl], kk=kk_ref[0, :, sl],
                              ld=ld_ref[0, :, sl], k=kd_ref[0, :, sl], a=a_ref[0, :, sl]))
    for q in probs:
        q["cs"] = sum(_dot(q["ones"], part) for part in _split3(q["ld"]))
    for q in probs:
        cs = q["cs"]
        e_cs = jnp.exp(cs)
        e_inv = jnp.exp(-cs)
        q["g_end"] = e_cs[q["last"]:q["last"] + 1, :]
        q["kkd"] = q["kk"] * jnp.exp(cs - q["ld"])
        q["rd"] = q["r"] * e_cs
        q["bi"] = _split2(q["kk"] * q["a"] * e_inv)
        q["ki"] = _split2(q["k"] * e_inv)
        q["lhs"] = _split2(jnp.concatenate([q["kkd"], q["rd"]], axis=0))
    for q in probs:
        gb = _mm3(q["lhs"], q["bi"], _dot_nt)
        gk = _mm3(q["lhs"], q["ki"], _dot_nt)
        q["pw"] = -jnp.where(q["strict"], gb[:C], 0.0)
        q["q_b"] = _split2(jnp.where(q["incl"], gb[C:], 0.0))
        q["akqk"] = _split2(jnp.concatenate([jnp.where(q["strict"], gk[:C], 0.0),
                                            jnp.where(q["incl"], gk[C:], 0.0)], axis=0))
        q["tinv"] = eye + q["pw"]
    for _ in range(5):
        for q in probs:
            pw = _split2(q["pw"])
            q["pw"] = _mm3(pw, pw)
        for q in probs:
            q["tinv"] = q["tinv"] + _mm3(_split2(q["tinv"]), _split2(q["pw"]))
    for q in probs:
        q["vs"] = _split2(q["v"])
        q["av"] = _mm3(q["akqk"], q["vs"])
    for q in probs:
        tinv = _split2(q["tinv"])
        q["w1"] = _mm3(tinv, _split2(q["kkd"]))
        q["u1"] = _mm3(tinv, _split2(q["av"][:C]))
    for q in probs:
        w1 = _split2(q["w1"])
        u1 = _split2(q["u1"])
        w2 = q["rd"] - _mm3(q["q_b"], w1)
        u2 = q["av"][C:] - _mm3(q["q_b"], u1)
        pt = (eye - _mm3(w1, q["bi"], _dot_tn)) * q["g_end"]
        zt = (_mm3(q["vs"], q["ki"], _dot_tn) - _mm3(u1, q["bi"], _dot_tn)) * q["g_end"]
        pz_ref[q["d"], 0, 0, q["h"]] = jnp.concatenate([pt, zt], axis=1)
        wu_ref[q["d"], 0, 0, q["h"]] = jnp.concatenate([w2, u2], axis=1)


def _rwkv_chunks(r, v, kk, ld0, ld1, kd0, kd1, a0, a1):
    b, ta, w = r.shape
    nc = ta // RWKV_CHUNK
    spec = pl.BlockSpec((1, RWKV_CHUNK, w), lambda bb, c: (bb, c, 0))
    ospec = pl.BlockSpec((2, 1, 1, RWKV_HEADS, RWKV_HEAD, 2 * RWKV_HEAD), lambda bb, c: (0, bb, c, 0, 0, 0))
    oshape = jax.ShapeDtypeStruct((2, b, nc, RWKV_HEADS, RWKV_HEAD, 2 * RWKV_HEAD), F32)
    return pl.pallas_call(
        _rwkv_chunk_kernel,
        grid=(b, nc),
        in_specs=[spec] * 9,
        out_specs=[ospec, ospec],
        out_shape=[oshape, oshape],
        compiler_params=_cparams(("parallel", "parallel")),
        name="rwkv_chunks",
    )(r, v, kk, ld0, ld1, kd0, kd1, a0, a1)


def _rwkv_state_kernel(pz0_ref, pz1_ref, wu0_ref, wu1_ref, y0_ref, y1_ref, st_ref):
    i = pl.program_id(0)
    hd = RWKV_HEAD
    nb = st_ref.shape[1]

    @pl.when(i == 0)
    def _():
        st_ref[...] = jnp.zeros_like(st_ref)

    probs = []
    for d, (pz_ref, wu_ref) in enumerate(((pz0_ref, wu0_ref), (pz1_ref, wu1_ref))):
        for bb in range(nb):
            for h in range(RWKV_HEADS):
                pz = pz_ref[0, bb, 0, h]
                wu = wu_ref[0, bb, 0, h]
                probs.append(dict(d=d, b=bb, h=h, pt=_split2(pz[:, :hd]), zt=pz[:, hd:],
                                  w2=_split2(wu[:, :hd]), u2=wu[:, hd:], s=_split2(st_ref[d, bb, h])))
    for q in probs:
        q["y"] = _mm3(q["w2"], q["s"], _dot_nt) + q["u2"]
    for q in probs:
        st_ref[q["d"], q["b"], q["h"]] = _mm3(q["s"], q["pt"]) + q["zt"]
    for d, y_ref in enumerate((y0_ref, y1_ref)):
        for bb in range(nb):
            ys = [q["y"] for q in probs if q["d"] == d and q["b"] == bb]
            for hp in range(RWKV_HEADS // 2):
                y_ref[bb, :, hp * LANE:(hp + 1) * LANE] = jnp.concatenate([ys[2 * hp], ys[2 * hp + 1]], axis=1)


def _rwkv_state(pz, wu, ctx):
    _, b, nc, nh, hd, _ = pz.shape
    ncc = ctx // RWKV_CHUNK

    def rev(i):
        return jnp.where(i < ncc, ncc - 1 - i, nc - 1 - (i - ncc))

    blk = (1, b, 1, nh, hd, 2 * hd)
    fwd = lambda i: (0, 0, i, 0, 0, 0)
    bwd = lambda i: (1, 0, rev(i), 0, 0, 0)
    yshape = jax.ShapeDtypeStruct((b, nc * RWKV_CHUNK, nh * hd), F32)
    return pl.pallas_call(
        _rwkv_state_kernel,
        grid=(nc,),
        in_specs=[pl.BlockSpec(blk, fwd), pl.BlockSpec(blk, bwd), pl.BlockSpec(blk, fwd), pl.BlockSpec(blk, bwd)],
        out_specs=[pl.BlockSpec((b, RWKV_CHUNK, nh * hd), lambda i: (0, i, 0)),
                   pl.BlockSpec((b, RWKV_CHUNK, nh * hd), lambda i: (0, rev(i), 0))],
        out_shape=[yshape, yshape],
        scratch_shapes=[pltpu.VMEM((2, b, nh, hd, hd), F32)],
        compiler_params=_cparams(("arbitrary",)),
        name="rwkv_state",
    )(pz, pz, wu, wu)


def _rwkv_out_kernel(y0_ref, y1_ref, bv_ref, g_ref, lg_ref, lb_ref, bd_ref, o_ref):
    hp = dict(precision=HIGHEST)
    bdm = bd_ref[...] * (1.0 / RWKV_HEAD)
    y = y0_ref[0] + y1_ref[0]
    yc = y - _dot(y, bdm, **hp)
    var = _dot(yc * yc, bdm, **hp)
    yn = yc * lax.rsqrt(var + RWKV_GN_EPS) * lg_ref[...] + lb_ref[...]
    o_ref[0] = ((yn + bv_ref[0]) * g_ref[0]).astype(o_ref.dtype)


def _rwkv_out(y0, y1, bv, g, ln_g, ln_b, bd, tm):
    b, ta, w = y0.shape
    row = pl.BlockSpec((1, tm, w), lambda bb, i: (bb, i, 0))
    vec = pl.BlockSpec((1, w), lambda bb, i: (0, 0))
    return pl.pallas_call(
        _rwkv_out_kernel,
        grid=(b, ta // tm),
        in_specs=[row, row, row, row, vec, vec, pl.BlockSpec((w, w), lambda bb, i: (0, 0))],
        out_specs=row,
        out_shape=jax.ShapeDtypeStruct((b, ta, w), BF16),
        compiler_params=_cparams(("parallel", "parallel")),
        name="rwkv_out",
    )(y0, y1, bv, g, ln_g, ln_b, bd)


def _rwkv_mixer(p, ctx, mu, w0, w_lora, a0, a_lora, g_lora, k_k, k_a, r_k, ln_g, ln_b):
    b, ta, _ = p.shape
    w = RWKV_W
    tm = _tile(math.gcd(ta, ctx), 256)
    nt = ta // tm
    edge_lo = p[:, 0::tm, SEG_RW:]
    edge_hi = p[:, tm - 1::tm, SEG_RW:]
    tile_id = jnp.arange(nt)[None, :, None]
    hprev = jnp.where((tile_id == 0) | (tile_id == ctx // tm), 0.0, jnp.roll(edge_hi, 1, axis=1))
    hnext = jnp.where((tile_id == nt - 1) | (tile_id == ctx // tm - 1), 0.0, jnp.roll(edge_lo, -1, axis=1))
    wl = jnp.zeros((2, LANE, w), F32)
    al = jnp.zeros((2, LANE, w), F32)
    for d in range(2):
        wl = wl.at[d, d * DECAY_LORA:(d + 1) * DECAY_LORA].set(w_lora[d])
        al = al.at[d, 2 * DECAY_LORA + d * ICLR_LORA:2 * DECAY_LORA + (d + 1) * ICLR_LORA].set(a_lora[d])
    gl = jnp.zeros((LANE, w), F32).at[:GATE_LORA].set(g_lora)
    mu_p = jnp.pad(mu, (0, RW_COLS - RWKV_STREAM)).reshape(1, RW_COLS)
    head_id = jnp.arange(w) // RWKV_HEAD
    bd = (head_id[:, None] == head_id[None, :]).astype(F32)
    r, v, kk, ld0, ld1, kd0, kd1, a0_, a1_, bv, g = _rwkv_prep(
        p, hprev[:, :, None, :], hnext[:, :, None, :], mu_p, w0, a0, wl, al, gl,
        k_k.reshape(1, w), k_a.reshape(1, w), r_k.reshape(1, w), bd, tm)
    pz, wu = _rwkv_chunks(r, v, kk, ld0, ld1, kd0, kd1, a0_, a1_)
    y0, y1 = _rwkv_state(pz, wu, ctx)
    return _rwkv_out(y0, y1, bv, g, ln_g.reshape(1, w), ln_b.reshape(1, w), bd, tm)


def _merge_a_kernel(x_ref, ml_ref, mc_ref, o4_ref, wg_ref, wb_ref, out_ref, xn_ref, acc_ref, *, tm, ctx):
    i = pl.program_id(1)
    j = pl.program_id(2)
    br = pl.program_id(3)

    @pl.when((j == 0) & (br == 0))
    def _():
        isc = _row_is_ctx(tm, i * tm, ctx)
        xn_ref[...] = _modulated(x_ref[0], ml_ref[0], mc_ref[0], 0, 1, isc).astype(BF16)

    val = _sigmoid(_dot(xn_ref[...], wg_ref[...])) * _dot(o4_ref[0, 0], wb_ref[0])

    @pl.when(br == 0)
    def _():
        acc_ref[...] = val

    @pl.when(br > 0)
    def _():
        acc_ref[...] += val

    @pl.when(br == N_BRANCH - 1)
    def _():
        out_ref[0] = acc_ref[...].astype(out_ref.dtype)


def _gate_window_weights(w_in_l, gate_off, w_branch_l, w_out_l):
    d = w_in_l.shape[0]
    shift = gate_off % LANE
    start = gate_off - shift
    gw = d + 2 * LANE
    tail = gw - d - shift
    wsrc = w_in_l[:, start:].astype(BF16)
    wsrc = jnp.pad(wsrc, ((0, 0), (0, (N_BRANCH - 1) * d + gw - wsrc.shape[1])))
    wg = jnp.concatenate([wsrc[:, br * d:br * d + gw] for br in range(N_BRANCH)], axis=1)
    wb = jnp.pad(w_branch_l.astype(BF16), ((0, 0), (0, 0), (shift, tail)))
    wo = jnp.pad(w_out_l.astype(BF16), ((shift, tail), (0, 0)))
    return wg, wb, wo


def _merge_a(x_all, modl, modc, o4, wg, wb, ctx):
    b, ta, d = x_all.shape
    gw = wb.shape[2]
    tm = _tile(ta, 544)
    tn = _tile(gw, 768, LANE)
    return pl.pallas_call(
        functools.partial(_merge_a_kernel, tm=tm, ctx=ctx),
        grid=(b, ta // tm, gw // tn, N_BRANCH),
        in_specs=[pl.BlockSpec((1, tm, d), lambda bb, i, j, br: (bb, i, 0)),
                  pl.BlockSpec((1, 6, d), lambda bb, i, j, br: (bb, 0, 0)),
                  pl.BlockSpec((1, 6, d), lambda bb, i, j, br: (0, 0, 0)),
                  pl.BlockSpec((1, 1, tm, BRANCH_W), lambda bb, i, j, br: (br, bb, i, 0)),
                  pl.BlockSpec((d, tn), lambda bb, i, j, br: (0, br * (gw // tn) + j)),
                  pl.BlockSpec((1, BRANCH_W, tn), lambda bb, i, j, br: (br, 0, j))],
        out_specs=pl.BlockSpec((1, tm, tn), lambda bb, i, j, br: (bb, i, j)),
        out_shape=jax.ShapeDtypeStruct((b, ta, gw), BF16),
        scratch_shapes=[pltpu.VMEM((tm, d), BF16), pltpu.VMEM((tm, tn), F32)],
        compiler_params=_cparams(("parallel", "parallel", "arbitrary", "arbitrary")),
        name="merge_gate",
    )(x_all, modl, modc, o4, wg, wb)


def _merge_b_kernel(m_ref, w_ref, x_ref, ml_ref, mc_ref, lg_ref, lb_ref, xo_ref, h2_ref, *, tm, ctx, alpha):
    i = pl.program_id(1)
    isc = _row_is_ctx(tm, i * tm, ctx)
    ml = ml_ref[0]
    mc = mc_ref[0]
    y = _dot(m_ref[0], w_ref[...])
    g1 = jnp.where(isc, mc[2:3], ml[2:3])
    xn = _ln(alpha * x_ref[0] + g1 * y, POST_LN_EPS) * lg_ref[...] + lb_ref[...]
    xo_ref[0] = xn
    h2_ref[0] = _pack_halves(_modulated(xn, ml, mc, 3, 4, isc))


def _merge_b(merged, w_out, x_all, modl, modc, ln_g, ln_b, ctx, alpha):
    b, ta, d = x_all.shape
    gw = w_out.shape[0]
    tm = _tile(ta, 272)
    row = lambda bb, i: (bb, i, 0)
    return pl.pallas_call(
        functools.partial(_merge_b_kernel, tm=tm, ctx=ctx, alpha=alpha),
        grid=(b, ta // tm),
        in_specs=[pl.BlockSpec((1, tm, gw), row),
                  pl.BlockSpec((gw, d), lambda bb, i: (0, 0)),
                  pl.BlockSpec((1, tm, d), row),
                  pl.BlockSpec((1, 6, d), lambda bb, i: (bb, 0, 0)),
                  pl.BlockSpec((1, 6, d), lambda bb, i: (0, 0, 0)),
                  pl.BlockSpec((1, d), lambda bb, i: (0, 0)),
                  pl.BlockSpec((1, d), lambda bb, i: (0, 0))],
        out_specs=[pl.BlockSpec((1, tm, d), row), pl.BlockSpec((1, tm, d // 2), row)],
        out_shape=[jax.ShapeDtypeStruct((b, ta, d), F32), jax.ShapeDtypeStruct((b, ta, d // 2), jnp.uint32)],
        compiler_params=_cparams(("parallel", "parallel")),
        name="merge_out_ln",
    )(merged, w_out, x_all, modl, modc, ln_g.reshape(1, d), ln_b.reshape(1, d))


def _router_kernel(x_ref, ml_ref, mc_ref, rw_ref, rb_ref, idx_ref, wt_ref, rank_ref, cnt_ref, carry_ref,
                   *, tm, ctx):
    i = pl.program_id(1)

    @pl.when((pl.program_id(0) == 0) & (i == 0))
    def _():
        carry_ref[...] = jnp.zeros_like(carry_ref)

    isc = _row_is_ctx(tm, i * tm, ctx)
    h2 = _modulated(x_ref[0], ml_ref[0], mc_ref[0], 3, 4, isc)
    scores = _sigmoid(_dot(h2, rw_ref[...], precision=HIGHEST))
    biased = scores + rb_ref[...]
    lane = lax.broadcasted_iota(jnp.int32, scores.shape, 1)
    grp = lane // GROUP_SIZE
    ninf = -jnp.inf

    def first_argmax(z):
        m = jnp.max(z, -1, keepdims=True)
        idx = jnp.min(jnp.where(z == m, lane, N_EXPERTS), -1, keepdims=True)
        return m, idx

    gscore = jnp.zeros_like(biased)
    for g in range(N_GROUPS):
        zg = jnp.where(grp == g, biased, ninf)
        m1, i1 = first_argmax(zg)
        m2 = jnp.max(jnp.where(lane == i1, ninf, zg), -1, keepdims=True)
        gscore = jnp.where(grp == g, m1 + m2, gscore)
    cand = jnp.where(lane % GROUP_SIZE == 0, gscore, ninf)
    gsel = jnp.zeros(scores.shape, jnp.bool_)
    for _ in range(TOPK_GROUPS):
        _, ig = first_argmax(cand)
        gsel = gsel | (grp == ig // GROUP_SIZE)
        cand = jnp.where(lane == ig, ninf, cand)
    masked = jnp.where(gsel, biased, ninf)
    out_lane = lax.broadcasted_iota(jnp.int32, (tm, LANE), 1)
    idx_out = jnp.zeros((tm, LANE), jnp.int32)
    w_out = jnp.zeros((tm, LANE), F32)
    w_sum = jnp.zeros((tm, 1), F32)
    chosen = []
    picked = jnp.zeros(scores.shape, F32)
    for kq in range(TOP_K):
        _, ie = first_argmax(masked)
        sel = lane == ie
        wk = jnp.sum(jnp.where(sel, scores, 0.0), -1, keepdims=True)
        masked = jnp.where(sel, ninf, masked)
        idx_out = jnp.where(out_lane == kq, ie, idx_out)
        w_out = jnp.where(out_lane == kq, wk, w_out)
        w_sum = w_sum + wk
        chosen.append(sel)
        picked = jnp.where(sel, 1.0, picked)
    idx_ref[0] = idx_out
    wt_ref[0] = w_out / w_sum * ROUTED_SCALE
    r_i = lax.broadcasted_iota(jnp.int32, (tm, tm), 0)
    c_i = lax.broadcasted_iota(jnp.int32, (tm, tm), 1)
    before = _dot(jnp.where(r_i > c_i, 1.0, 0.0).astype(BF16), picked.astype(BF16)) + carry_ref[...]
    rank_out = jnp.zeros((tm, LANE), jnp.int32)
    for kq in range(TOP_K):
        rk = jnp.sum(jnp.where(chosen[kq], before, 0.0), -1, keepdims=True)
        rank_out = jnp.where(out_lane == kq, rk.astype(jnp.int32), rank_out)
    rank_ref[0] = rank_out
    carry_ref[...] += jnp.sum(picked, axis=0, keepdims=True)
    cnt_ref[...] = carry_ref[...].astype(jnp.int32)


def _router(x_all, modl, modc, rw, rb, ctx):
    b, ta, d = x_all.shape
    tm = _tile(ta, 544)
    row = lambda bb, i: (bb, i, 0)
    tok = jax.ShapeDtypeStruct((b, ta, LANE), jnp.int32)
    return pl.pallas_call(
        functools.partial(_router_kernel, tm=tm, ctx=ctx),
        grid=(b, ta // tm),
        in_specs=[pl.BlockSpec((1, tm, d), row),
                  pl.BlockSpec((1, 6, d), lambda bb, i: (bb, 0, 0)),
                  pl.BlockSpec((1, 6, d), lambda bb, i: (0, 0, 0)),
                  pl.BlockSpec((d, N_EXPERTS), lambda bb, i: (0, 0)),
                  pl.BlockSpec((1, N_EXPERTS), lambda bb, i: (0, 0))],
        out_specs=[pl.BlockSpec((1, tm, LANE), row), pl.BlockSpec((1, tm, LANE), row),
                   pl.BlockSpec((1, tm, LANE), row), pl.BlockSpec((1, N_EXPERTS), lambda bb, i: (0, 0))],
        out_shape=[tok, jax.ShapeDtypeStruct((b, ta, LANE), F32), tok,
                   jax.ShapeDtypeStruct((1, N_EXPERTS), jnp.int32)],
        scratch_shapes=[pltpu.VMEM((1, N_EXPERTS), F32)],
        compiler_params=_cparams(("arbitrary", "arbitrary")),
        name="router",
    )(x_all, modl, modc, rw, rb.reshape(1, N_EXPERTS))


def _row_copy(src_ref, src_row, dst_ref, dst_row, sem):
    return pltpu.make_async_copy(src_ref.at[pl.ds(src_row, 1)], dst_ref.at[pl.ds(dst_row, 1)], sem)


def _dispatch_kernel(pos_ref, h2_ref, xg_in_ref, xg_ref, sem, *, tm):
    del xg_in_ref

    def issue(t, carry):
        for kq in range(TOP_K):
            _row_copy(h2_ref, t, xg_ref, pos_ref[0, 0, t * TOP_K + kq], sem).start()
        return carry

    lax.fori_loop(0, tm, issue, 0)

    def drain(t, carry):
        _row_copy(h2_ref, 0, xg_ref, 0, sem).wait()
        return carry

    lax.fori_loop(0, tm * TOP_K, drain, 0)


def _dispatch(pos, h2p, n_rows, tm):
    n, half = h2p.shape
    xg0 = jnp.zeros((n_rows, half), jnp.uint32)
    return pl.pallas_call(
        functools.partial(_dispatch_kernel, tm=tm),
        grid=(n // tm,),
        in_specs=[pl.BlockSpec((1, 1, tm * TOP_K), lambda i: (i, 0, 0), memory_space=pltpu.SMEM),
                  pl.BlockSpec((tm, half), lambda i: (i, 0)),
                  pl.BlockSpec(memory_space=pl.ANY)],
        out_specs=pl.BlockSpec(memory_space=pl.ANY),
        out_shape=jax.ShapeDtypeStruct((n_rows, half), jnp.uint32),
        scratch_shapes=[pltpu.SemaphoreType.DMA(())],
        input_output_aliases={2: 0},
        compiler_params=_cparams(("arbitrary",)),
        name="moe_dispatch",
    )(pos, h2p, xg0)


def _swiglu(x_lo, x_hi, wgu, wdn, ff):
    half = x_lo.shape[1]
    h = _dot(x_lo, wgu[:half, :]) + _dot(x_hi, wgu[half:, :])
    g = h[:, :ff]
    return _dot((g * _sigmoid(g) * h[:, ff:]).astype(BF16), wdn)


def _experts_kernel(be_ref, nu_ref, x_ref, wgu_ref, wdn_ref, o_ref, wgu_s, wdn_s, *, ff):
    i = pl.program_id(0)
    changed = (i == 0) | (be_ref[i] != be_ref[jnp.maximum(i - 1, 0)])

    @pl.when(changed)
    def _():
        wgu_s[...] = wgu_ref[0].astype(BF16)
        wdn_s[...] = wdn_ref[0].astype(BF16)

    @pl.when(i < nu_ref[0])
    def _():
        lo, hi = _unpack_halves(x_ref[...])
        o_ref[...] = _pack_halves(_swiglu(lo.astype(BF16), hi.astype(BF16), wgu_s[...], wdn_s[...], ff))

    @pl.when(i >= nu_ref[0])
    def _():
        o_ref[...] = jnp.zeros_like(o_ref)


def _experts(xg, blk_e, n_used, w_gu, w_dn, blk):
    n_rows, half = xg.shape
    d = 2 * half
    ff = w_dn.shape[1]
    n_blocks = n_rows // blk
    grid_spec = pltpu.PrefetchScalarGridSpec(
        num_scalar_prefetch=2,
        grid=(n_blocks,),
        in_specs=[pl.BlockSpec((blk, half), lambda i, be, nu: (i, 0)),
                  pl.BlockSpec((1, d, 2 * ff), lambda i, be, nu: (be[i], 0, 0)),
                  pl.BlockSpec((1, ff, d), lambda i, be, nu: (be[i], 0, 0))],
        out_specs=pl.BlockSpec((blk, half), lambda i, be, nu: (i, 0)),
        scratch_shapes=[pltpu.VMEM((d, 2 * ff), BF16), pltpu.VMEM((ff, d), BF16)],
    )
    return pl.pallas_call(
        functools.partial(_experts_kernel, ff=ff),
        grid_spec=grid_spec,
        out_shape=jax.ShapeDtypeStruct((n_rows, half), jnp.uint32),
        compiler_params=_cparams(("arbitrary",)),
        name="moe_experts",
    )(blk_e, n_used, xg, w_gu, w_dn)


def _moe_plan(idx, rank, counts, blk):
    n_assign = idx.size
    padded = (counts + blk - 1) // blk * blk
    pad_end = jnp.cumsum(padded)
    start_pad = pad_end - padded
    experts = jnp.arange(N_EXPERTS, dtype=jnp.int32)
    start_of = jnp.sum(jnp.where(idx[..., None] == experts, start_pad, 0), axis=-1)
    pos = (start_of + rank).astype(jnp.int32)
    n_blocks = -(-(n_assign + N_EXPERTS * (blk - 1)) // blk)
    blk_start = jnp.arange(n_blocks, dtype=jnp.int32) * blk
    blk_e = jnp.minimum(jnp.sum(blk_start[:, None] >= pad_end[None, :], axis=-1), N_EXPERTS - 1).astype(jnp.int32)
    n_used = (pad_end[-1] // blk).astype(jnp.int32).reshape(1)
    return pos, blk_e, n_used, n_blocks * blk


def _combine_kernel(pos_ref, y_ref, wt_ref, h2_ref, sgu_ref, sdn_ref, x_ref, ml_ref, mc_ref, lg_ref, lb_ref,
                    xo_ref, buf, sem, *, tm, ctx, alpha, ff):
    i = pl.program_id(1)

    def issue(t, carry):
        for kq in range(TOP_K):
            _row_copy(y_ref, pos_ref[0, 0, t * TOP_K + kq], buf.at[kq], t, sem).start()
        return carry

    lax.fori_loop(0, tm, issue, 0)
    lo, hi = _unpack_halves(h2_ref[0])
    f = _swiglu(lo.astype(BF16), hi.astype(BF16), sgu_ref[...], sdn_ref[...], ff)

    def drain(t, carry):
        _row_copy(y_ref, 0, buf.at[0], 0, sem).wait()
        return carry

    lax.fori_loop(0, tm * TOP_K, drain, 0)
    wt = wt_ref[0]
    half = f.shape[1] // 2
    f_lo = f[:, :half]
    f_hi = f[:, half:]
    for kq in range(TOP_K):
        lo, hi = _unpack_halves(buf[kq])
        f_lo = f_lo + wt[:, kq:kq + 1] * lo
        f_hi = f_hi + wt[:, kq:kq + 1] * hi
    f = jnp.concatenate([f_lo, f_hi], axis=1)
    isc = _row_is_ctx(tm, i * tm, ctx)
    g2 = jnp.where(isc, mc_ref[0][5:6], ml_ref[0][5:6])
    xo_ref[0] = _ln(alpha * x_ref[0] + g2 * f, POST_LN_EPS) * lg_ref[...] + lb_ref[...]


def _combine(pos, y_rows, wts, h2p, sh_gu, sh_dn, x_all, modl, modc, ln_g, ln_b, ctx, alpha, tm):
    b, ta, d = x_all.shape
    ff = sh_dn.shape[0]
    nt = ta // tm
    row = lambda bb, i: (bb, i, 0)
    return pl.pallas_call(
        functools.partial(_combine_kernel, tm=tm, ctx=ctx, alpha=alpha, ff=ff),
        grid=(b, nt),
        in_specs=[pl.BlockSpec((1, 1, tm * TOP_K), lambda bb, i: (bb * nt + i, 0, 0), memory_space=pltpu.SMEM),
                  pl.BlockSpec(memory_space=pl.ANY),
                  pl.BlockSpec((1, tm, LANE), row),
                  pl.BlockSpec((1, tm, d // 2), row),
                  pl.BlockSpec((d, 2 * ff), lambda bb, i: (0, 0)),
                  pl.BlockSpec((ff, d), lambda bb, i: (0, 0)),
                  pl.BlockSpec((1, tm, d), row),
                  pl.BlockSpec((1, 6, d), lambda bb, i: (bb, 0, 0)),
                  pl.BlockSpec((1, 6, d), lambda bb, i: (0, 0, 0)),
                  pl.BlockSpec((1, d), lambda bb, i: (0, 0)),
                  pl.BlockSpec((1, d), lambda bb, i: (0, 0))],
        out_specs=pl.BlockSpec((1, tm, d), row),
        out_shape=jax.ShapeDtypeStruct((b, ta, d), F32),
        scratch_shapes=[pltpu.VMEM((TOP_K, tm, d // 2), jnp.uint32), pltpu.SemaphoreType.DMA(())],
        compiler_params=_cparams(("arbitrary", "arbitrary")),
        name="moe_combine_ln",
    )(pos, y_rows, wts, h2p, sh_gu, sh_dn, x_all, modl, modc, ln_g.reshape(1, d), ln_b.reshape(1, d))


def _rope_tables(n_lat, ctx):
    rows = n_lat // GRID_W
    row = jnp.repeat(jnp.arange(rows, dtype=F32), GRID_W)
    col = jnp.tile(jnp.arange(GRID_W, dtype=F32), rows)
    n_freq = ROPE_DIM // 4
    inv_freq = ROPE_BASE ** (-jnp.arange(n_freq, dtype=F32) / n_freq)
    ang = jnp.concatenate([row[:, None] * inv_freq, col[:, None] * inv_freq], -1)
    cos, sin = jnp.cos(ang), jnp.sin(ang)
    cos = jnp.concatenate([jnp.ones((ctx, ROPE_DIM // 2), F32), cos], 0)
    sin = jnp.concatenate([jnp.zeros((ctx, ROPE_DIM // 2), F32), sin], 0)
    cosf = jnp.tile(jnp.concatenate([cos, cos], -1), (1, LANE // ROPE_DIM))
    sinf = jnp.tile(jnp.concatenate([-sin, sin], -1), (1, LANE // ROPE_DIM))
    return cosf, sinf


def _pack_small_weights(w_in_l):
    d = w_in_l.shape[0]
    offs = np.cumsum([0, MLA_Q_RANK, MLA_KV_RANK + MLA_ROPE, RWKV_STREAM,
                      (SWA_HEADS + 2 * SWA_KV_HEADS) * SWA_HEAD, 3 * DIFF_HEADS * 2 * DIFF_HEAD])
    placed = sorted(zip([SEG_MQ, SEG_MKV, SEG_RW, SEG_SW, SEG_DF], range(5)))
    parts = []
    cur = 0
    for start, s in placed:
        if start > cur:
            parts.append(jnp.zeros((d, start - cur), BF16))
        parts.append(w_in_l[:, int(offs[s]):int(offs[s + 1])].astype(BF16))
        cur = start + int(offs[s + 1] - offs[s])
    parts.append(jnp.zeros((d, SMALL_COLS - cur), BF16))
    return jnp.concatenate(parts, axis=1), int(offs[5])


def kernel(x, c, ctx, c_ctx, w_mod, b_mod, w_in, mla_q_norm, mla_w_qup, mla_kv_norm, mla_w_kvup, rwkv_mu, rwkv_w0, rwkv_w_lora, rwkv_a0, rwkv_a_lora, rwkv_g_lora, rwkv_k_k, rwkv_k_a, rwkv_r_k, rwkv_ln_g, rwkv_ln_b, swa_sink, diff_lambda, diff_subln, w_branch, w_out, ln1_g, ln1_b, router_w, router_bias, exp_w_gu, exp_w_dn, sh_w_gu, sh_w_dn, ln2_g, ln2_b):
    depth = w_mod.shape[0]
    b, n_lat, d = x.shape
    n_ctx = ctx.shape[1]
    ta = n_ctx + n_lat
    alpha = (2 * depth) ** 0.25
    cosf, sinf = _rope_tables(n_lat, n_ctx)
    x_all = jnp.concatenate([ctx, x], axis=1)

    cond = jnp.concatenate([c, c_ctx[None]], axis=0)
    cond = jnp.pad(jax.nn.silu(cond), ((0, SUBLANE - (b + 1) % SUBLANE), (0, 0)))

    for l in range(depth):
        mod = _matmul_bias(cond, w_mod[l], b_mod[l]).reshape(cond.shape[0], 6, d)
        modl, modc = mod[:b], mod[b:b + 1]

        w_small, gate_off = _pack_small_weights(w_in[l])
        p = _inproj(x_all, modl, modc, w_small, n_ctx)

        wq = mla_w_qup[l].reshape(MLA_Q_RANK, MLA_HEADS, MLA_NOPE + MLA_ROPE)
        wq = jnp.pad(wq, ((0, 0), (0, 0), (0, 256 - MLA_NOPE - MLA_ROPE))).reshape(MLA_Q_RANK, 1024)
        q_a, k_a, v_a = _mla_prep(p, mla_q_norm[l].reshape(1, -1), mla_kv_norm[l].reshape(1, -1),
                                  wq.astype(BF16), mla_w_kvup[l].astype(BF16), cosf, sinf)
        out_a = _mla_attn(q_a, k_a, v_a, n_ctx)

        out_b = _rwkv_mixer(p, n_ctx, rwkv_mu[l], rwkv_w0[l],
                            rwkv_w_lora[l], rwkv_a0[l], rwkv_a_lora[l], rwkv_g_lora[l], rwkv_k_k[l],
                            rwkv_k_a[l], rwkv_r_k[l], rwkv_ln_g[l], rwkv_ln_b[l])

        sinkv = jnp.broadcast_to(swa_sink[l][:, None], (SWA_HEADS, LANE))
        out_s = _swa(p, cosf, sinf, sinkv, n_ctx)

        lf = diff_lambda[l]
        lam_init = 0.8 - 0.6 * math.exp(-0.3 * l)
        lam = jnp.exp(jnp.sum(lf[0] * lf[1])) - jnp.exp(jnp.sum(lf[2] * lf[3])) + lam_init
        out_d = _diff_attn(p, cosf, sinf, jnp.full((1, LANE), lam, F32), diff_subln[l].reshape(1, LANE),
                           n_ctx, lam_init)

        o4 = jnp.stack([out_a, out_b, out_s, out_d], axis=0)
        wg, wb, wo = _gate_window_weights(w_in[l], gate_off, w_branch[l], w_out[l])
        merged = _merge_a(x_all, modl, modc, o4, wg, wb, n_ctx)
        x_all, h2p = _merge_b(merged, wo, x_all, modl, modc, ln1_g[l], ln1_b[l], n_ctx, alpha)

        idx, wts, rank, counts = _router(x_all, modl, modc, router_w[l], router_bias[l], n_ctx)
        pos, blk_e, n_used, n_rows = _moe_plan(idx[..., :TOP_K], rank[..., :TOP_K], counts[0], MOE_BLK)
        tmd = _tile(ta, 128)
        pos = pos.reshape(b * ta // tmd, 1, tmd * TOP_K)
        xg = _dispatch(pos, h2p.reshape(b * ta, d // 2), n_rows, tmd)
        y_rows = _experts(xg, blk_e, n_used, exp_w_gu[l], exp_w_dn[l], MOE_BLK)
        x_all = _combine(pos, y_rows, wts, h2p, sh_w_gu[l].astype(BF16), sh_w_dn[l].astype(BF16), x_all,
                         modl, modc, ln2_g[l], ln2_b[l], n_ctx, alpha, tmd)
    return x_all[:, n_ctx:]
```

```python
import functools
import math

import jax
import jax.numpy as jnp
import numpy as np
from jax import lax
from jax.experimental import pallas as pl
from jax.experimental.pallas import tpu as pltpu

F32 = jnp.float32
BF16 = jnp.bfloat16
HIGHEST = lax.Precision.HIGHEST

GRID_W = 64
ROPE_BASE = 10000.0
ROPE_DIM = 64
ADALN_EPS = 1e-6
POST_LN_EPS = 1e-5
NEG_INF = -1e30

MLA_HEADS = 4
MLA_Q_RANK = 384
MLA_KV_RANK = 256
MLA_NOPE = 128
MLA_ROPE = 64
MLA_V = 128

RWKV_HEADS = 8
RWKV_HEAD = 64
RWKV_W = RWKV_HEADS * RWKV_HEAD
DECAY_LORA = 32
ICLR_LORA = 32
GATE_LORA = 96
RWKV_STREAM = 3 * RWKV_W + 2 * DECAY_LORA + 2 * ICLR_LORA + GATE_LORA
RWKV_GN_EPS = 64e-5
RWKV_CHUNK = 64

SWA_HEADS = 8
SWA_KV_HEADS = 2
SWA_GROUP = SWA_HEADS // SWA_KV_HEADS
SWA_HEAD = 64
WINDOW = 128

DIFF_HEADS = 4
DIFF_HEAD = 64

N_BRANCH = 4
BRANCH_W = 512

N_EXPERTS = 64
TOP_K = 6
N_GROUPS = 8
TOPK_GROUPS = 4
GROUP_SIZE = N_EXPERTS // N_GROUPS
ROUTED_SCALE = 2.5
MOE_BLK = 256

LANE = 128
SUBLANE = 8

SEG_SW = 0
SEG_MQ = 768
SEG_MKV = 1152
SEG_DF = 1536
SEG_RW = 3584
RW_COLS = 1792
SMALL_COLS = SEG_RW + RW_COLS


def _tile(n, target, mult=SUBLANE):
    best = None
    for d in range(mult, min(n, target) + 1, mult):
        if n % d == 0:
            best = d
    assert best is not None, (n, target, mult)
    return best


def _cparams(sem, vmem_mb=48):
    return pltpu.CompilerParams(dimension_semantics=sem, vmem_limit_bytes=vmem_mb * 1024 * 1024)


def _sigmoid(x):
    return 1.0 / (1.0 + jnp.exp(-x))


def _ln(x, eps):
    mu = jnp.mean(x, -1, keepdims=True)
    xc = x - mu
    var = jnp.mean(xc * xc, -1, keepdims=True)
    return xc * lax.rsqrt(var + eps)


def _rms(x, eps):
    return x * lax.rsqrt(jnp.mean(x * x, -1, keepdims=True) + eps)


def _row_is_ctx(n_rows, row0, ctx):
    rows = row0 + lax.broadcasted_iota(jnp.int32, (n_rows, 1), 0)
    return rows < ctx


def _modulated(x, ml, mc, sh_i, sc_i, isc):
    sh = jnp.where(isc, mc[sh_i:sh_i + 1], ml[sh_i:sh_i + 1])
    sc = jnp.where(isc, mc[sc_i:sc_i + 1], ml[sc_i:sc_i + 1])
    return _ln(x, ADALN_EPS) * (1.0 + sc) + sh


def _swap32(x):
    lane = lax.broadcasted_iota(jnp.int32, x.shape, 1)
    up = pltpu.roll(x, LANE - 32, 1)
    dn = pltpu.roll(x, 32, 1)
    return jnp.where((lane % 64) < 32, up, dn)


def _rope(x, cosf, sinf):
    return x * cosf + _swap32(x) * sinf


def _pack_halves(x):
    n = x.shape[1] // 2
    lo = pltpu.bitcast(x[:, :n].astype(BF16).astype(F32), jnp.uint32)
    hi = pltpu.bitcast(x[:, n:].astype(BF16).astype(F32), jnp.uint32)
    return (hi & jnp.uint32(0xFFFF0000)) | (lo >> 16)


def _unpack_halves(u):
    lo = pltpu.bitcast(u << 16, F32)
    hi = pltpu.bitcast(u & jnp.uint32(0xFFFF0000), F32)
    return lo, hi


def _dot(a, b, **kw):
    return jnp.dot(a, b, preferred_element_type=F32, **kw)


def _dot_nt(a, b, **kw):
    return lax.dot_general(a, b, (((1,), (1,)), ((), ())), preferred_element_type=F32, **kw)


def _dot_tn(a, b, **kw):
    return lax.dot_general(a, b, (((0,), (0,)), ((), ())), preferred_element_type=F32, **kw)


def _mm_kernel(x_ref, w_ref, b_ref, o_ref):
    o_ref[...] = _dot(x_ref[...], w_ref[0], precision=HIGHEST) + b_ref[0]


def _matmul_bias(x, w, b, l):
    m, k = x.shape
    n = w.shape[2]
    tn = _tile(n, 1024, LANE)
    return pl.pallas_call(
        _mm_kernel,
        grid=(n // tn,),
        in_specs=[pl.BlockSpec((m, k), lambda j: (0, 0)),
                  pl.BlockSpec((1, k, tn), lambda j: (l, 0, j)),
                  pl.BlockSpec((1, 1, tn), lambda j: (l, 0, j))],
        out_specs=pl.BlockSpec((m, tn), lambda j: (0, j)),
        out_shape=jax.ShapeDtypeStruct((m, n), F32),
        compiler_params=_cparams(("arbitrary",)),
        name="mod_matmul",
    )(x, w, b.reshape(b.shape[0], 1, n))


def _inproj_kernel(x_ref, ml_ref, mc_ref, w_ref, o_ref, xn_ref, *, tm, ctx):
    i = pl.program_id(1)
    j = pl.program_id(2)

    @pl.when(j == 0)
    def _():
        isc = _row_is_ctx(tm, i * tm, ctx)
        xn_ref[...] = _modulated(x_ref[0], ml_ref[0], mc_ref[0], 0, 1, isc).astype(BF16)

    o_ref[0] = _dot(xn_ref[...], w_ref[...])


def _inproj(x_all, modl, modc, w_small, ctx):
    b, ta, d = x_all.shape
    n = w_small.shape[1]
    tm = _tile(ta, 544)
    tn = _tile(n, 1024, LANE)
    return pl.pallas_call(
        functools.partial(_inproj_kernel, tm=tm, ctx=ctx),
        grid=(b, ta // tm, n // tn),
        in_specs=[pl.BlockSpec((1, tm, d), lambda bb, i, j: (bb, i, 0)),
                  pl.BlockSpec((1, 6, d), lambda bb, i, j: (bb, 0, 0)),
                  pl.BlockSpec((1, 6, d), lambda bb, i, j: (0, 0, 0)),
                  pl.BlockSpec((d, tn), lambda bb, i, j: (0, j))],
        out_specs=pl.BlockSpec((1, tm, tn), lambda bb, i, j: (bb, i, j)),
        out_shape=jax.ShapeDtypeStruct((b, ta, n), F32),
        scratch_shapes=[pltpu.VMEM((tm, d), BF16)],
        compiler_params=_cparams(("parallel", "parallel", "arbitrary")),
        name="inproj",
    )(x_all, modl, modc, w_small)


def _mla_prep_kernel(mq_ref, mkv_ref, qg_ref, kg_ref, wq_ref, wkv_ref, cos_ref, sin_ref,
                     q_ref, k_ref, v_ref):
    mq = mq_ref[0]
    mkv = mkv_ref[0]
    cosv = cos_ref[...]
    sinv = sin_ref[...]
    qn = _rms(mq, 1e-6) * qg_ref[...]
    q = _dot(qn.astype(BF16), wq_ref[...])
    kvn = _rms(mkv[:, :MLA_KV_RANK], 1e-6) * kg_ref[...]
    kv = _dot(kvn.astype(BF16), wkv_ref[...])
    kr = _rope(mkv[:, MLA_KV_RANK:MLA_KV_RANK + LANE], cosv, sinv).astype(BF16)
    scale = (MLA_NOPE + MLA_ROPE) ** -0.5 * LOG2E
    for h in range(MLA_HEADS):
        c0 = h * 256
        q_ref[0, :, c0:c0 + 128] = (q[:, c0:c0 + 128] * scale).astype(BF16)
        q_ref[0, :, c0 + 128:c0 + 256] = (_rope(q[:, c0 + 128:c0 + 256], cosv, sinv) * scale).astype(BF16)
        k_ref[0, :, c0:c0 + 128] = kv[:, c0:c0 + 128].astype(BF16)
        k_ref[0, :, c0 + 128:c0 + 256] = kr
        v_ref[0, :, h * 128:(h + 1) * 128] = kv[:, c0 + 128:c0 + 256].astype(BF16)


def _mla_prep(p, qg, kg, wq, wkv, cosf, sinf):
    b, ta, _ = p.shape
    tm = _tile(ta, 544)
    return pl.pallas_call(
        _mla_prep_kernel,
        grid=(b, ta // tm),
        in_specs=[pl.BlockSpec((1, tm, 384), lambda bb, i: (bb, i, SEG_MQ // 384)),
                  pl.BlockSpec((1, tm, 384), lambda bb, i: (bb, i, SEG_MKV // 384)),
                  pl.BlockSpec((1, MLA_Q_RANK), lambda bb, i: (0, 0)),
                  pl.BlockSpec((1, MLA_KV_RANK), lambda bb, i: (0, 0)),
                  pl.BlockSpec((MLA_Q_RANK, 1024), lambda bb, i: (0, 0)),
                  pl.BlockSpec((MLA_KV_RANK, 1024), lambda bb, i: (0, 0)),
                  pl.BlockSpec((tm, LANE), lambda bb, i: (i, 0)),
                  pl.BlockSpec((tm, LANE), lambda bb, i: (i, 0))],
        out_specs=[pl.BlockSpec((1, tm, 1024), lambda bb, i: (bb, i, 0)),
                   pl.BlockSpec((1, tm, 1024), lambda bb, i: (bb, i, 0)),
                   pl.BlockSpec((1, tm, 512), lambda bb, i: (bb, i, 0))],
        out_shape=[jax.ShapeDtypeStruct((b, ta, 1024), BF16),
                   jax.ShapeDtypeStruct((b, ta, 1024), BF16),
                   jax.ShapeDtypeStruct((b, ta, 512), BF16)],
        compiler_params=_cparams(("parallel", "parallel")),
        name="mla_prep",
    )(p, p, qg, kg, wq, wkv, cosf, sinf)


LOG2E = 1.4426950408889634


def _softmax_pv(s, v):
    m = jnp.max(s, -1, keepdims=True)
    p = jnp.exp2(s - m)
    l = jnp.sum(p, -1, keepdims=True)
    return _dot(p.astype(BF16), v) / l


def _mla_attn_kernel(q_ref, k_ref, v_ref, o_ref, *, n_ctx_tiles, ctx):
    qi = pl.program_id(2)

    def run(nk):
        s = _dot_nt(q_ref[0], k_ref[0, :nk, :])
        o_ref[0] = _softmax_pv(s, v_ref[0, :nk, :]).astype(o_ref.dtype)

    @pl.when(qi < n_ctx_tiles)
    def _():
        run(ctx)

    @pl.when(qi >= n_ctx_tiles)
    def _():
        run(k_ref.shape[1])


def _mla_attn(q, k, v, ctx):
    b, ta, _ = q.shape
    tq = _tile(math.gcd(ta, ctx), 256)
    return pl.pallas_call(
        functools.partial(_mla_attn_kernel, n_ctx_tiles=ctx // tq, ctx=ctx),
        grid=(b, MLA_HEADS, ta // tq),
        in_specs=[pl.BlockSpec((1, tq, 256), lambda bb, h, i: (bb, i, h)),
                  pl.BlockSpec((1, ta, 256), lambda bb, h, i: (bb, 0, h)),
                  pl.BlockSpec((1, ta, 128), lambda bb, h, i: (bb, 0, h))],
        out_specs=pl.BlockSpec((1, tq, 128), lambda bb, h, i: (bb, i, h)),
        out_shape=jax.ShapeDtypeStruct((b, ta, 512), BF16),
        compiler_params=_cparams(("parallel", "parallel", "arbitrary")),
        name="mla_attn",
    )(q, k, v)


def _diff_attn_kernel(q_ref, k_ref, v_ref, cq_ref, sq_ref, ck_ref, sk_ref, lam_ref, sub_ref,
                      o_ref, ks_ref, vs_ref, *, n_ctx_tiles, ctx, post_scale):
    qi = pl.program_id(2)

    @pl.when(qi == 0)
    def _():
        ks_ref[...] = _rope(k_ref[0], ck_ref[...], sk_ref[...]).astype(BF16)
        vs_ref[...] = v_ref[0].astype(BF16)

    def run(nk):
        q = _rope(q_ref[0], cq_ref[...], sq_ref[...]) * (DIFF_HEAD ** -0.5 * LOG2E)
        lane = lax.broadcasted_iota(jnp.int32, q.shape, 1)
        q1 = jnp.where(lane < 64, q, 0.0).astype(BF16)
        q2 = jnp.where(lane >= 64, q, 0.0).astype(BF16)
        k = ks_ref[:nk, :]
        v = vs_ref[:nk, :]
        o = _softmax_pv(_dot_nt(q1, k), v) - lam_ref[...] * _softmax_pv(_dot_nt(q2, k), v)
        o = _rms(o, 1e-5) * sub_ref[...] * post_scale
        o_ref[0] = o.astype(o_ref.dtype)

    @pl.when(qi < n_ctx_tiles)
    def _():
        run(ctx)

    @pl.when(qi >= n_ctx_tiles)
    def _():
        run(ks_ref.shape[0])


def _diff_attn(p, cosf, sinf, lamv, subln, ctx, lam_init):
    b, ta, _ = p.shape
    tq = _tile(math.gcd(ta, ctx), 256)
    cb = SEG_DF // LANE
    return pl.pallas_call(
        functools.partial(_diff_attn_kernel, n_ctx_tiles=ctx // tq, ctx=ctx, post_scale=1.0 - lam_init),
        grid=(b, DIFF_HEADS, ta // tq),
        in_specs=[pl.BlockSpec((1, tq, LANE), lambda bb, h, i: (bb, i, cb + h)),
                  pl.BlockSpec((1, ta, LANE), lambda bb, h, i: (bb, 0, cb + DIFF_HEADS + h)),
                  pl.BlockSpec((1, ta, LANE), lambda bb, h, i: (bb, 0, cb + 2 * DIFF_HEADS + h)),
                  pl.BlockSpec((tq, LANE), lambda bb, h, i: (i, 0)),
                  pl.BlockSpec((tq, LANE), lambda bb, h, i: (i, 0)),
                  pl.BlockSpec((ta, LANE), lambda bb, h, i: (0, 0)),
                  pl.BlockSpec((ta, LANE), lambda bb, h, i: (0, 0)),
                  pl.BlockSpec((1, LANE), lambda bb, h, i: (0, 0)),
                  pl.BlockSpec((1, LANE), lambda bb, h, i: (0, 0))],
        out_specs=pl.BlockSpec((1, tq, LANE), lambda bb, h, i: (bb, i, h)),
        out_shape=jax.ShapeDtypeStruct((b, ta, 512), BF16),
        scratch_shapes=[pltpu.VMEM((ta, LANE), BF16), pltpu.VMEM((ta, LANE), BF16)],
        compiler_params=_cparams(("parallel", "parallel", "arbitrary")),
        name="diff_attn",
    )(p, p, p, cosf, sinf, cosf, sinf, lamv, subln)


def _swa_kernel(q_ref, k_ref, v_ref, cq_ref, sq_ref, ck_ref, sk_ref, sink_ref,
                o_ref, ks_ref, vs_ref, *, ctx):
    t = pl.program_id(1)
    n_ctx_tiles = ctx // WINDOW
    ta = ks_ref.shape[0]
    span = 3 * WINDOW

    @pl.when(t == 0)
    def _():
        ks_ref[...] = _rope(k_ref[0], ck_ref[...], sk_ref[...]).astype(BF16)
        vs_ref[...] = v_ref[0].astype(BF16)

    lane = lax.broadcasted_iota(jnp.int32, (WINDOW, LANE), 1)
    cq = cq_ref[...]
    sq = sq_ref[...]
    qblk = [_rope(q_ref[0, :, hb * LANE:(hb + 1) * LANE], cq, sq) * (SWA_HEAD ** -0.5)
            for hb in range(SWA_HEADS // 2)]

    def head_q(hk, g):
        h = hk * SWA_GROUP + g
        blk = qblk[h // 2]
        if h % 2 != hk:
            blk = pltpu.roll(blk, 64, 1)
        keep = (lane >= 64) if hk == 1 else (lane < 64)
        return jnp.where(keep, blk, 0.0)

    def run(local):
        kc = ks_ref[:ctx, :]
        vc = vs_ref[:ctx, :]
        if local:
            n = t - n_ctx_tiles
            ws = jnp.clip(ctx + (n - 1) * WINDOW, ctx, ta - span)
            ws = pl.multiple_of(ws, WINDOW)
            kl = ks_ref[pl.ds(ws, span), :]
            vl = vs_ref[pl.ds(ws, span), :]
            kpos = (ws - ctx) + lax.broadcasted_iota(jnp.int32, (1, span), 1)
            qpos = n * WINDOW + lax.broadcasted_iota(jnp.int32, (SWA_GROUP * WINDOW, 1), 0) % WINDOW
            valid = jnp.abs(kpos - qpos) <= WINDOW
        outs = {}
        for hk in range(SWA_KV_HEADS):
            qst = jnp.concatenate([head_q(hk, g) for g in range(SWA_GROUP)], axis=0).astype(BF16)
            sink = jnp.concatenate(
                [jnp.broadcast_to(sink_ref[hk * SWA_GROUP + g:hk * SWA_GROUP + g + 1, 0:1], (WINDOW, 1))
                 for g in range(SWA_GROUP)], axis=0)
            s_c = _dot_nt(qst, kc)
            m = jnp.maximum(jnp.max(s_c, -1, keepdims=True), sink)
            if local:
                s_l = jnp.where(valid, _dot_nt(qst, kl), NEG_INF)
                m = jnp.maximum(m, jnp.max(s_l, -1, keepdims=True))
            p_c = jnp.exp(s_c - m)
            l = jnp.sum(p_c, -1, keepdims=True) + jnp.exp(sink - m)
            acc = _dot(p_c.astype(BF16), vc)
            if local:
                p_l = jnp.exp(s_l - m)
                l = l + jnp.sum(p_l, -1, keepdims=True)
                acc = acc + _dot(p_l.astype(BF16), vl)
            o = acc / l
            for g in range(SWA_GROUP):
                res = o[g * WINDOW:(g + 1) * WINDOW, :]
                if g % 2 != hk:
                    res = pltpu.roll(res, 64, 1)
                outs[hk * SWA_GROUP + g] = res
        for hb in range(SWA_HEADS // 2):
            blk = jnp.where(lane < 64, outs[2 * hb], outs[2 * hb + 1])
            o_ref[0, :, hb * LANE:(hb + 1) * LANE] = blk.astype(o_ref.dtype)

    @pl.when(t < n_ctx_tiles)
    def _():
        run(False)

    @pl.when(t >= n_ctx_tiles)
    def _():
        run(True)


def _swa(p, cosf, sinf, sinkv, ctx):
    b, ta, _ = p.shape
    qb = SEG_SW // 512
    kb = (SEG_SW + 512) // LANE
    return pl.pallas_call(
        functools.partial(_swa_kernel, ctx=ctx),
        grid=(b, ta // WINDOW),
        in_specs=[pl.BlockSpec((1, WINDOW, 512), lambda bb, t: (bb, t, qb)),
                  pl.BlockSpec((1, ta, LANE), lambda bb, t: (bb, 0, kb)),
                  pl.BlockSpec((1, ta, LANE), lambda bb, t: (bb, 0, kb + 1)),
                  pl.BlockSpec((WINDOW, LANE), lambda bb, t: (t, 0)),
                  pl.BlockSpec((WINDOW, LANE), lambda bb, t: (t, 0)),
                  pl.BlockSpec((ta, LANE), lambda bb, t: (0, 0)),
                  pl.BlockSpec((ta, LANE), lambda bb, t: (0, 0)),
                  pl.BlockSpec((SWA_HEADS, LANE), lambda bb, t: (0, 0))],
        out_specs=pl.BlockSpec((1, WINDOW, 512), lambda bb, t: (bb, t, 0)),
        out_shape=jax.ShapeDtypeStruct((b, ta, 512), BF16),
        scratch_shapes=[pltpu.VMEM((ta, LANE), BF16), pltpu.VMEM((ta, LANE), BF16)],
        compiler_params=_cparams(("parallel", "arbitrary")),
        name="swa_attn",
    )(p, p, p, cosf, sinf, cosf, sinf, sinkv)


def _softplus(x):
    return jnp.maximum(x, 0.0) + jnp.log(1.0 + jnp.exp(-jnp.abs(x)))


def _rwkv_prep_kernel(z_ref, hp_ref, hn_ref, mu_ref, w0_ref, a0_ref, wl_ref, al_ref, gl_ref, kk_ref, ka_ref,
                      rk_ref, bd_ref, r_o, v_o, kk_o, ld0_o, ld1_o, kd0_o, kd1_o, a0_o, a1_o, bv_o, g_o, zs_ref):
    tm = z_ref.shape[1]
    w = RWKV_W
    z = z_ref[0]
    zs_ref[SUBLANE:SUBLANE + tm, :] = z
    zs_ref[SUBLANE - 1:SUBLANE, :] = hp_ref[0, 0]
    zs_ref[SUBLANE + tm:SUBLANE + tm + 1, :] = hn_ref[0, 0]
    shifted = 0.5 * (zs_ref[SUBLANE - 1:SUBLANE - 1 + tm, :] + zs_ref[SUBLANE + 1:SUBLANE + 1 + tm, :])
    seg = z + (shifted - z) * mu_ref[...]
    r = seg[:, 0:w]
    k = seg[:, w:2 * w]
    v = seg[:, 2 * w:3 * w]
    lo = seg[:, 3 * w:3 * w + LANE]
    gl = seg[:, 3 * w + LANE:3 * w + 2 * LANE]
    hp = dict(precision=HIGHEST)
    bd = bd_ref[...]
    kk = k * kk_ref[...]
    kk = kk / jnp.maximum(jnp.sqrt(_dot(kk * kk, bd, **hp)), 1e-12)
    th = jnp.tanh(lo)
    ksum = jnp.zeros_like(k)
    for d, (ld_o, kd_o, a_o) in enumerate(((ld0_o, kd0_o, a0_o), (ld1_o, kd1_o, a1_o))):
        wd = w0_ref[d:d + 1, :] + _dot(th, wl_ref[d], **hp)
        wd = -_softplus(-wd) - 0.5
        a = _sigmoid(a0_ref[d:d + 1, :] + _dot(lo, al_ref[d], **hp))
        kd = k * (1.0 + (a - 1.0) * ka_ref[...])
        ld_o[0] = -jnp.exp(wd)
        kd_o[0] = kd
        a_o[0] = a
        ksum = ksum + kd
    r_o[0] = r
    v_o[0] = v
    kk_o[0] = kk
    bv_o[0] = _dot(r * ksum * rk_ref[...], bd, **hp) * v
    g_o[0] = _dot(_sigmoid(gl), gl_ref[...], **hp)


def _rwkv_prep(p, hprev, hnext, mu, w0, a0, wl, al, gl, k_k, k_a, r_k, bd, tm):
    b, ta, _ = p.shape
    w = RWKV_W
    row = lambda bb, i: (bb, i, 0)
    vec = lambda n: pl.BlockSpec((1, n), lambda bb, i: (0, 0))
    out = jax.ShapeDtypeStruct((b, ta, w), F32)
    return pl.pallas_call(
        _rwkv_prep_kernel,
        grid=(b, ta // tm),
        in_specs=[pl.BlockSpec((1, tm, RW_COLS), lambda bb, i: (bb, i, SEG_RW // RW_COLS)),
                  pl.BlockSpec((1, 1, 1, RW_COLS), lambda bb, i: (bb, i, 0, 0)),
                  pl.BlockSpec((1, 1, 1, RW_COLS), lambda bb, i: (bb, i, 0, 0)),
                  vec(RW_COLS),
                  pl.BlockSpec((2, w), lambda bb, i: (0, 0)),
                  pl.BlockSpec((2, w), lambda bb, i: (0, 0)),
                  pl.BlockSpec((2, LANE, w), lambda bb, i: (0, 0, 0)),
                  pl.BlockSpec((2, LANE, w), lambda bb, i: (0, 0, 0)),
                  pl.BlockSpec((LANE, w), lambda bb, i: (0, 0)),
                  vec(w), vec(w), vec(w),
                  pl.BlockSpec((w, w), lambda bb, i: (0, 0))],
        out_specs=[pl.BlockSpec((1, tm, w), row)] * 11,
        out_shape=[out] * 11,
        scratch_shapes=[pltpu.VMEM((tm + 2 * SUBLANE, RW_COLS), F32)],
        compiler_params=_cparams(("parallel", "parallel")),
        name="rwkv_prep",
    )(p, hprev, hnext, mu, w0, a0, wl, al, gl, k_k, k_a, r_k, bd)


def _split3(x):
    hi = x.astype(BF16)
    r1 = x - hi.astype(F32)
    mid = r1.astype(BF16)
    return hi, mid, (r1 - mid.astype(F32)).astype(BF16)


def _split2(x):
    hi = x.astype(BF16)
    return hi, (x - hi.astype(F32)).astype(BF16)


def _mm3(a, b, dot=_dot):
    return dot(a[0], b[0]) + dot(a[0], b[1]) + dot(a[1], b[0])


def _rwkv_chunk_kernel(r_ref, v_ref, kk_ref, ld0_ref, ld1_ref, kd0_ref, kd1_ref, a0_ref, a1_ref, pz_ref, wu_ref):
    C = RWKV_CHUNK
    hd = RWKV_HEAD
    row = lax.broadcasted_iota(jnp.int32, (C, C), 0)
    col = lax.broadcasted_iota(jnp.int32, (C, C), 1)
    eye = (row == col).astype(F32)
    probs = []
    for d, (ld_ref, kd_ref, a_ref) in enumerate(((ld0_ref, kd0_ref, a0_ref), (ld1_ref, kd1_ref, a1_ref))):
        incl = (row >= col) if d == 0 else (row <= col)
        strict = (row > col) if d == 0 else (row < col)
        last = C - 1 if d == 0 else 0
        for h in range(RWKV_HEADS):
            sl = slice(h * hd, (h + 1) * hd)
            probs.append(dict(d=d, h=h, incl=incl, strict=strict, last=last,
                              ones=jnp.where(incl, 1.0, 0.0).astype(BF16),
                              r=r_ref[0, :, sl], v=v_ref[0, :, sl], kk=kk_ref[0, :, sl],
                              ld=ld_ref[0, :, sl], k=kd_ref[0, :, sl], a=a_ref[0, :, sl]))
    bf = lambda z: z.astype(BF16)
    for q in probs:
        q["cs"] = sum(_dot(q["ones"], part) for part in _split3(q["ld"]))
    for q in probs:
        cs = q["cs"]
        e_cs = jnp.exp(cs)
        e_inv = jnp.exp(-cs)
        q["g_end"] = e_cs[q["last"]:q["last"] + 1, :]
        q["kkd"] = q["kk"] * jnp.exp(cs - q["ld"])
        q["rd"] = q["r"] * e_cs
        q["bi"] = bf(q["kk"] * q["a"] * e_inv)
        q["ki"] = bf(q["k"] * e_inv)
        q["lhs"] = bf(jnp.concatenate([q["kkd"], q["rd"]], axis=0))
    for q in probs:
        gb = _dot_nt(q["lhs"], q["bi"])
        gk = _dot_nt(q["lhs"], q["ki"])
        q["pw"] = -jnp.where(q["strict"], gb[:C], 0.0)
        q["q_b"] = bf(jnp.where(q["incl"], gb[C:], 0.0))
        q["akqk"] = bf(jnp.concatenate([jnp.where(q["strict"], gk[:C], 0.0),
                                        jnp.where(q["incl"], gk[C:], 0.0)], axis=0))
        q["tinv"] = eye + q["pw"]
    for _ in range(5):
        for q in probs:
            pw = bf(q["pw"])
            q["pw"] = _dot(pw, pw)
        for q in probs:
            q["tinv"] = q["tinv"] + _dot(bf(q["tinv"]), bf(q["pw"]))
    for q in probs:
        q["vs"] = bf(q["v"])
        q["av"] = _dot(q["akqk"], q["vs"])
    for q in probs:
        tinv = bf(q["tinv"])
        q["w1"] = _dot(tinv, bf(q["kkd"]))
        q["u1"] = _dot(tinv, bf(q["av"][:C]))
    for q in probs:
        w1 = bf(q["w1"])
        u1 = bf(q["u1"])
        w2 = q["rd"] - _dot(q["q_b"], w1)
        u2 = q["av"][C:] - _dot(q["q_b"], u1)
        pt = (eye - _dot_tn(w1, q["bi"])) * q["g_end"]
        zt = (_dot_tn(q["vs"], q["ki"]) - _dot_tn(u1, q["bi"])) * q["g_end"]
        pz_ref[q["d"], 0, 0, q["h"]] = jnp.concatenate([pt, zt], axis=1)
        wu_ref[q["d"], 0, 0, q["h"]] = jnp.concatenate([w2, u2], axis=1)


def _rwkv_chunks(r, v, kk, ld0, ld1, kd0, kd1, a0, a1):
    b, ta, w = r.shape
    nc = ta // RWKV_CHUNK
    spec = pl.BlockSpec((1, RWKV_CHUNK, w), lambda bb, c: (bb, c, 0))
    ospec = pl.BlockSpec((2, 1, 1, RWKV_HEADS, RWKV_HEAD, 2 * RWKV_HEAD), lambda bb, c: (0, bb, c, 0, 0, 0))
    oshape = jax.ShapeDtypeStruct((2, b, nc, RWKV_HEADS, RWKV_HEAD, 2 * RWKV_HEAD), F32)
    return pl.pallas_call(
        _rwkv_chunk_kernel,
        grid=(b, nc),
        in_specs=[spec] * 9,
        out_specs=[ospec, ospec],
        out_shape=[oshape, oshape],
        compiler_params=_cparams(("parallel", "parallel")),
        name="rwkv_chunks",
    )(r, v, kk, ld0, ld1, kd0, kd1, a0, a1)


def _rwkv_state_kernel(pz0_ref, pz1_ref, wu0_ref, wu1_ref, y0_ref, y1_ref, st_ref):
    i = pl.program_id(0)
    hd = RWKV_HEAD
    nb = st_ref.shape[1]

    @pl.when(i == 0)
    def _():
        st_ref[...] = jnp.zeros_like(st_ref)

    probs = []
    for d, (pz_ref, wu_ref) in enumerate(((pz0_ref, wu0_ref), (pz1_ref, wu1_ref))):
        for bb in range(nb):
            for h in range(RWKV_HEADS):
                pz = pz_ref[0, bb, 0, h]
                wu = wu_ref[0, bb, 0, h]
                probs.append(dict(d=d, b=bb, h=h, pt=_split2(pz[:, :hd]), zt=pz[:, hd:],
                                  w2=_split2(wu[:, :hd]), u2=wu[:, hd:], s=_split2(st_ref[d, bb, h])))
    for q in probs:
        q["y"] = _mm3(q["w2"], q["s"], _dot_nt) + q["u2"]
    for q in probs:
        st_ref[q["d"], q["b"], q["h"]] = _mm3(q["s"], q["pt"]) + q["zt"]
    for d, y_ref in enumerate((y0_ref, y1_ref)):
        for bb in range(nb):
            ys = [q["y"] for q in probs if q["d"] == d and q["b"] == bb]
            for hp in range(RWKV_HEADS // 2):
                y_ref[bb, :, hp * LANE:(hp + 1) * LANE] = jnp.concatenate([ys[2 * hp], ys[2 * hp + 1]], axis=1)


def _rwkv_state(pz, wu, ctx):
    _, b, nc, nh, hd, _ = pz.shape
    ncc = ctx // RWKV_CHUNK

    def rev(i):
        return jnp.where(i < ncc, ncc - 1 - i, nc - 1 - (i - ncc))

    blk = (1, b, 1, nh, hd, 2 * hd)
    fwd = lambda i: (0, 0, i, 0, 0, 0)
    bwd = lambda i: (1, 0, rev(i), 0, 0, 0)
    yshape = jax.ShapeDtypeStruct((b, nc * RWKV_CHUNK, nh * hd), F32)
    return pl.pallas_call(
        _rwkv_state_kernel,
        grid=(nc,),
        in_specs=[pl.BlockSpec(blk, fwd), pl.BlockSpec(blk, bwd), pl.BlockSpec(blk, fwd), pl.BlockSpec(blk, bwd)],
        out_specs=[pl.BlockSpec((b, RWKV_CHUNK, nh * hd), lambda i: (0, i, 0)),
                   pl.BlockSpec((b, RWKV_CHUNK, nh * hd), lambda i: (0, rev(i), 0))],
        out_shape=[yshape, yshape],
        scratch_shapes=[pltpu.VMEM((2, b, nh, hd, hd), F32)],
        compiler_params=_cparams(("arbitrary",)),
        name="rwkv_state",
    )(pz, pz, wu, wu)


def _rwkv_out_kernel(y0_ref, y1_ref, bv_ref, g_ref, lg_ref, lb_ref, bd_ref, o_ref):
    hp = dict(precision=HIGHEST)
    bdm = bd_ref[...] * (1.0 / RWKV_HEAD)
    y = y0_ref[0] + y1_ref[0]
    yc = y - _dot(y, bdm, **hp)
    var = _dot(yc * yc, bdm, **hp)
    yn = yc * lax.rsqrt(var + RWKV_GN_EPS) * lg_ref[...] + lb_ref[...]
    o_ref[0] = ((yn + bv_ref[0]) * g_ref[0]).astype(o_ref.dtype)


def _rwkv_out(y0, y1, bv, g, ln_g, ln_b, bd, tm):
    b, ta, w = y0.shape
    row = pl.BlockSpec((1, tm, w), lambda bb, i: (bb, i, 0))
    vec = pl.BlockSpec((1, w), lambda bb, i: (0, 0))
    return pl.pallas_call(
        _rwkv_out_kernel,
        grid=(b, ta // tm),
        in_specs=[row, row, row, row, vec, vec, pl.BlockSpec((w, w), lambda bb, i: (0, 0))],
        out_specs=row,
        out_shape=jax.ShapeDtypeStruct((b, ta, w), BF16),
        compiler_params=_cparams(("parallel", "parallel")),
        name="rwkv_out",
    )(y0, y1, bv, g, ln_g, ln_b, bd)


def _rwkv_mixer(p, ctx, mu, w0, w_lora, a0, a_lora, g_lora, k_k, k_a, r_k, ln_g, ln_b):
    b, ta, _ = p.shape
    w = RWKV_W
    tm = _tile(math.gcd(ta, ctx), 256)
    nt = ta // tm
    edge_lo = p[:, 0::tm, SEG_RW:]
    edge_hi = p[:, tm - 1::tm, SEG_RW:]
    tile_id = jnp.arange(nt)[None, :, None]
    hprev = jnp.where((tile_id == 0) | (tile_id == ctx // tm), 0.0, jnp.roll(edge_hi, 1, axis=1))
    hnext = jnp.where((tile_id == nt - 1) | (tile_id == ctx // tm - 1), 0.0, jnp.roll(edge_lo, -1, axis=1))
    wl = jnp.zeros((2, LANE, w), F32)
    al = jnp.zeros((2, LANE, w), F32)
    for d in range(2):
        wl = wl.at[d, d * DECAY_LORA:(d + 1) * DECAY_LORA].set(w_lora[d])
        al = al.at[d, 2 * DECAY_LORA + d * ICLR_LORA:2 * DECAY_LORA + (d + 1) * ICLR_LORA].set(a_lora[d])
    gl = jnp.zeros((LANE, w), F32).at[:GATE_LORA].set(g_lora)
    mu_p = jnp.pad(mu, (0, RW_COLS - RWKV_STREAM)).reshape(1, RW_COLS)
    head_id = jnp.arange(w) // RWKV_HEAD
    bd = (head_id[:, None] == head_id[None, :]).astype(F32)
    r, v, kk, ld0, ld1, kd0, kd1, a0_, a1_, bv, g = _rwkv_prep(
        p, hprev[:, :, None, :], hnext[:, :, None, :], mu_p, w0, a0, wl, al, gl,
        k_k.reshape(1, w), k_a.reshape(1, w), r_k.reshape(1, w), bd, tm)
    pz, wu = _rwkv_chunks(r, v, kk, ld0, ld1, kd0, kd1, a0_, a1_)
    y0, y1 = _rwkv_state(pz, wu, ctx)
    return _rwkv_out(y0, y1, bv, g, ln_g.reshape(1, w), ln_b.reshape(1, w), bd, tm)


def _merge_a_kernel(x_ref, ml_ref, mc_ref, o4_ref, wg_ref, wb_ref, out_ref, xn_ref, acc_ref, *, tm, ctx):
    i = pl.program_id(1)
    j = pl.program_id(2)
    br = pl.program_id(3)

    @pl.when((j == 0) & (br == 0))
    def _():
        isc = _row_is_ctx(tm, i * tm, ctx)
        xn_ref[...] = _modulated(x_ref[0], ml_ref[0], mc_ref[0], 0, 1, isc).astype(BF16)

    val = _sigmoid(_dot(xn_ref[...], wg_ref[...])) * _dot(o4_ref[0, 0], wb_ref[0])

    @pl.when(br == 0)
    def _():
        acc_ref[...] = val

    @pl.when(br > 0)
    def _():
        acc_ref[...] += val

    @pl.when(br == N_BRANCH - 1)
    def _():
        out_ref[0] = acc_ref[...].astype(out_ref.dtype)


def _gate_window_weights(w_in_l, gate_off, w_branch_l, w_out_l):
    d = w_in_l.shape[0]
    shift = gate_off % LANE
    start = gate_off - shift
    gw = d + 2 * LANE
    tail = gw - d - shift
    wsrc = w_in_l[:, start:].astype(BF16)
    wsrc = jnp.pad(wsrc, ((0, 0), (0, (N_BRANCH - 1) * d + gw - wsrc.shape[1])))
    wg = jnp.concatenate([wsrc[:, br * d:br * d + gw] for br in range(N_BRANCH)], axis=1)
    wb = jnp.pad(w_branch_l.astype(BF16), ((0, 0), (0, 0), (shift, tail)))
    wo = jnp.pad(w_out_l.astype(BF16), ((shift, tail), (0, 0)))
    return wg, wb, wo


def _merge_a(x_all, modl, modc, o4, wg, wb, ctx):
    b, ta, d = x_all.shape
    gw = wb.shape[2]
    tm = _tile(ta, 544)
    tn = _tile(gw, 768, LANE)
    return pl.pallas_call(
        functools.partial(_merge_a_kernel, tm=tm, ctx=ctx),
        grid=(b, ta // tm, gw // tn, N_BRANCH),
        in_specs=[pl.BlockSpec((1, tm, d), lambda bb, i, j, br: (bb, i, 0)),
                  pl.BlockSpec((1, 6, d), lambda bb, i, j, br: (bb, 0, 0)),
                  pl.BlockSpec((1, 6, d), lambda bb, i, j, br: (0, 0, 0)),
                  pl.BlockSpec((1, 1, tm, BRANCH_W), lambda bb, i, j, br: (br, bb, i, 0)),
                  pl.BlockSpec((d, tn), lambda bb, i, j, br: (0, br * (gw // tn) + j)),
                  pl.BlockSpec((1, BRANCH_W, tn), lambda bb, i, j, br: (br, 0, j))],
        out_specs=pl.BlockSpec((1, tm, tn), lambda bb, i, j, br: (bb, i, j)),
        out_shape=jax.ShapeDtypeStruct((b, ta, gw), BF16),
        scratch_shapes=[pltpu.VMEM((tm, d), BF16), pltpu.VMEM((tm, tn), F32)],
        compiler_params=_cparams(("parallel", "parallel", "arbitrary", "arbitrary")),
        name="merge_gate",
    )(x_all, modl, modc, o4, wg, wb)


def _merge_b_kernel(m_ref, w_ref, x_ref, ml_ref, mc_ref, lg_ref, lb_ref, xo_ref, h2_ref, *, tm, ctx, alpha):
    i = pl.program_id(1)
    isc = _row_is_ctx(tm, i * tm, ctx)
    ml = ml_ref[0]
    mc = mc_ref[0]
    y = _dot(m_ref[0], w_ref[...])
    g1 = jnp.where(isc, mc[2:3], ml[2:3])
    xn = _ln(alpha * x_ref[0] + g1 * y, POST_LN_EPS) * lg_ref[...] + lb_ref[...]
    xo_ref[0] = xn
    h2_ref[0] = _pack_halves(_modulated(xn, ml, mc, 3, 4, isc))


def _merge_b(merged, w_out, x_all, modl, modc, ln_g, ln_b, ctx, alpha):
    b, ta, d = x_all.shape
    gw = w_out.shape[0]
    tm = _tile(ta, 272)
    row = lambda bb, i: (bb, i, 0)
    return pl.pallas_call(
        functools.partial(_merge_b_kernel, tm=tm, ctx=ctx, alpha=alpha),
        grid=(b, ta // tm),
        in_specs=[pl.BlockSpec((1, tm, gw), row),
                  pl.BlockSpec((gw, d), lambda bb, i: (0, 0)),
                  pl.BlockSpec((1, tm, d), row),
                  pl.BlockSpec((1, 6, d), lambda bb, i: (bb, 0, 0)),
                  pl.BlockSpec((1, 6, d), lambda bb, i: (0, 0, 0)),
                  pl.BlockSpec((1, d), lambda bb, i: (0, 0)),
                  pl.BlockSpec((1, d), lambda bb, i: (0, 0))],
        out_specs=[pl.BlockSpec((1, tm, d), row), pl.BlockSpec((1, tm, d // 2), row)],
        out_shape=[jax.ShapeDtypeStruct((b, ta, d), F32), jax.ShapeDtypeStruct((b, ta, d // 2), jnp.uint32)],
        compiler_params=_cparams(("parallel", "parallel")),
        name="merge_out_ln",
    )(merged, w_out, x_all, modl, modc, ln_g.reshape(1, d), ln_b.reshape(1, d))


def _router_kernel(x_ref, ml_ref, mc_ref, rw_ref, rb_ref, idx_ref, wt_ref, rank_ref, cnt_ref, carry_ref,
                   *, tm, ctx):
    i = pl.program_id(1)

    @pl.when((pl.program_id(0) == 0) & (i == 0))
    def _():
        carry_ref[...] = jnp.zeros_like(carry_ref)

    isc = _row_is_ctx(tm, i * tm, ctx)
    h2 = _modulated(x_ref[0], ml_ref[0], mc_ref[0], 3, 4, isc)
    scores = _sigmoid(_dot(h2, rw_ref[...], precision=HIGHEST))
    biased = scores + rb_ref[...]
    lane = lax.broadcasted_iota(jnp.int32, scores.shape, 1)
    grp = lane // GROUP_SIZE
    ninf = -jnp.inf

    def first_argmax(z):
        m = jnp.max(z, -1, keepdims=True)
        idx = jnp.min(jnp.where(z == m, lane, N_EXPERTS), -1, keepdims=True)
        return m, idx

    gscore = jnp.zeros_like(biased)
    for g in range(N_GROUPS):
        zg = jnp.where(grp == g, biased, ninf)
        m1, i1 = first_argmax(zg)
        m2 = jnp.max(jnp.where(lane == i1, ninf, zg), -1, keepdims=True)
        gscore = jnp.where(grp == g, m1 + m2, gscore)
    cand = jnp.where(lane % GROUP_SIZE == 0, gscore, ninf)
    gsel = jnp.zeros(scores.shape, jnp.bool_)
    for _ in range(TOPK_GROUPS):
        _, ig = first_argmax(cand)
        gsel = gsel | (grp == ig // GROUP_SIZE)
        cand = jnp.where(lane == ig, ninf, cand)
    masked = jnp.where(gsel, biased, ninf)
    out_lane = lax.broadcasted_iota(jnp.int32, (tm, LANE), 1)
    idx_out = jnp.zeros((tm, LANE), jnp.int32)
    w_out = jnp.zeros((tm, LANE), F32)
    w_sum = jnp.zeros((tm, 1), F32)
    chosen = []
    picked = jnp.zeros(scores.shape, F32)
    for kq in range(TOP_K):
        _, ie = first_argmax(masked)
        sel = lane == ie
        wk = jnp.sum(jnp.where(sel, scores, 0.0), -1, keepdims=True)
        masked = jnp.where(sel, ninf, masked)
        idx_out = jnp.where(out_lane == kq, ie, idx_out)
        w_out = jnp.where(out_lane == kq, wk, w_out)
        w_sum = w_sum + wk
        chosen.append(sel)
        picked = jnp.where(sel, 1.0, picked)
    idx_ref[0] = idx_out
    wt_ref[0] = w_out / w_sum * ROUTED_SCALE
    r_i = lax.broadcasted_iota(jnp.int32, (tm, tm), 0)
    c_i = lax.broadcasted_iota(jnp.int32, (tm, tm), 1)
    before = _dot(jnp.where(r_i > c_i, 1.0, 0.0).astype(BF16), picked.astype(BF16)) + carry_ref[...]
    rank_out = jnp.zeros((tm, LANE), jnp.int32)
    for kq in range(TOP_K):
        rk = jnp.sum(jnp.where(chosen[kq], before, 0.0), -1, keepdims=True)
        rank_out = jnp.where(out_lane == kq, rk.astype(jnp.int32), rank_out)
    rank_ref[0] = rank_out
    carry_ref[...] += jnp.sum(picked, axis=0, keepdims=True)
    cnt_ref[...] = carry_ref[...].astype(jnp.int32)


def _router(x_all, modl, modc, rw, rb, ctx):
    b, ta, d = x_all.shape
    tm = _tile(ta, 544)
    row = lambda bb, i: (bb, i, 0)
    tok = jax.ShapeDtypeStruct((b, ta, LANE), jnp.int32)
    return pl.pallas_call(
        functools.partial(_router_kernel, tm=tm, ctx=ctx),
        grid=(b, ta // tm),
        in_specs=[pl.BlockSpec((1, tm, d), row),
                  pl.BlockSpec((1, 6, d), lambda bb, i: (bb, 0, 0)),
                  pl.BlockSpec((1, 6, d), lambda bb, i: (0, 0, 0)),
                  pl.BlockSpec((d, N_EXPERTS), lambda bb, i: (0, 0)),
                  pl.BlockSpec((1, N_EXPERTS), lambda bb, i: (0, 0))],
        out_specs=[pl.BlockSpec((1, tm, LANE), row), pl.BlockSpec((1, tm, LANE), row),
                   pl.BlockSpec((1, tm, LANE), row), pl.BlockSpec((1, N_EXPERTS), lambda bb, i: (0, 0))],
        out_shape=[tok, jax.ShapeDtypeStruct((b, ta, LANE), F32), tok,
                   jax.ShapeDtypeStruct((1, N_EXPERTS), jnp.int32)],
        scratch_shapes=[pltpu.VMEM((1, N_EXPERTS), F32)],
        compiler_params=_cparams(("arbitrary", "arbitrary")),
        name="router",
    )(x_all, modl, modc, rw, rb.reshape(1, N_EXPERTS))


def _row_copy(src_ref, src_row, dst_ref, dst_row, sem):
    return pltpu.make_async_copy(src_ref.at[pl.ds(src_row, 1)], dst_ref.at[pl.ds(dst_row, 1)], sem)


def _dispatch_kernel(pos_ref, h2_ref, xg_in_ref, xg_ref, sem, *, tm):
    del xg_in_ref

    def issue(t, carry):
        for kq in range(TOP_K):
            _row_copy(h2_ref, t, xg_ref, pos_ref[0, 0, t * TOP_K + kq], sem).start()
        return carry

    lax.fori_loop(0, tm, issue, 0, unroll=4)
    for _ in range(TOP_K):
        pltpu.make_async_copy(h2_ref, xg_ref.at[pl.ds(0, tm)], sem).wait()


def _dispatch(pos, h2p, n_rows, tm):
    n, half = h2p.shape
    xg0 = jnp.zeros((n_rows, half), jnp.uint32)
    return pl.pallas_call(
        functools.partial(_dispatch_kernel, tm=tm),
        grid=(n // tm,),
        in_specs=[pl.BlockSpec((1, 1, tm * TOP_K), lambda i: (i, 0, 0), memory_space=pltpu.SMEM),
                  pl.BlockSpec((tm, half), lambda i: (i, 0)),
                  pl.BlockSpec(memory_space=pl.ANY)],
        out_specs=pl.BlockSpec(memory_space=pl.ANY),
        out_shape=jax.ShapeDtypeStruct((n_rows, half), jnp.uint32),
        scratch_shapes=[pltpu.SemaphoreType.DMA(())],
        input_output_aliases={2: 0},
        compiler_params=_cparams(("arbitrary",)),
        name="moe_dispatch",
    )(pos, h2p, xg0)


def _swiglu(x_lo, x_hi, wgu, wdn, ff):
    half = x_lo.shape[1]
    h = _dot(x_lo, wgu[:half, :]) + _dot(x_hi, wgu[half:, :])
    g = h[:, :ff]
    return _dot((g * _sigmoid(g) * h[:, ff:]).astype(BF16), wdn)


def _experts_kernel(be_ref, nu_ref, x_ref, wgu_ref, wdn_ref, o_ref, wgu_s, wdn_s, *, ff):
    i = pl.program_id(0)
    changed = (i == 0) | (be_ref[i] != be_ref[jnp.maximum(i - 1, 0)])

    @pl.when(changed)
    def _():
        wgu_s[...] = wgu_ref[0, 0].astype(BF16)
        wdn_s[...] = wdn_ref[0, 0].astype(BF16)

    @pl.when(i < nu_ref[0])
    def _():
        lo, hi = _unpack_halves(x_ref[...])
        o_ref[...] = _pack_halves(_swiglu(lo.astype(BF16), hi.astype(BF16), wgu_s[...], wdn_s[...], ff))

    @pl.when(i >= nu_ref[0])
    def _():
        o_ref[...] = jnp.zeros_like(o_ref)


def _experts(xg, blk_e, n_used, w_gu, w_dn, l, blk):
    n_rows, half = xg.shape
    d = 2 * half
    ff = w_dn.shape[2]
    n_blocks = n_rows // blk
    grid_spec = pltpu.PrefetchScalarGridSpec(
        num_scalar_prefetch=2,
        grid=(n_blocks,),
        in_specs=[pl.BlockSpec((blk, half), lambda i, be, nu: (i, 0)),
                  pl.BlockSpec((1, 1, d, 2 * ff), lambda i, be, nu: (l, be[i], 0, 0)),
                  pl.BlockSpec((1, 1, ff, d), lambda i, be, nu: (l, be[i], 0, 0))],
        out_specs=pl.BlockSpec((blk, half), lambda i, be, nu: (i, 0)),
        scratch_shapes=[pltpu.VMEM((d, 2 * ff), BF16), pltpu.VMEM((ff, d), BF16)],
    )
    return pl.pallas_call(
        functools.partial(_experts_kernel, ff=ff),
        grid_spec=grid_spec,
        out_shape=jax.ShapeDtypeStruct((n_rows, half), jnp.uint32),
        compiler_params=_cparams(("arbitrary",)),
        name="moe_experts",
    )(blk_e, n_used, xg, w_gu, w_dn)


def _moe_plan(idx, rank, counts, blk):
    n_assign = idx.size
    padded = (counts + blk - 1) // blk * blk
    pad_end = jnp.cumsum(padded)
    start_pad = pad_end - padded
    experts = jnp.arange(N_EXPERTS, dtype=jnp.int32)
    start_of = jnp.sum(jnp.where(idx[..., None] == experts, start_pad, 0), axis=-1)
    pos = (start_of + rank).astype(jnp.int32)
    n_blocks = -(-(n_assign + N_EXPERTS * (blk - 1)) // blk)
    blk_start = jnp.arange(n_blocks, dtype=jnp.int32) * blk
    blk_e = jnp.minimum(jnp.sum(blk_start[:, None] >= pad_end[None, :], axis=-1), N_EXPERTS - 1).astype(jnp.int32)
    n_used = (pad_end[-1] // blk).astype(jnp.int32).reshape(1)
    return pos, blk_e, n_used, n_blocks * blk


def _combine_kernel(pos_ref, y_ref, wt_ref, h2_ref, sgu_ref, sdn_ref, x_ref, ml_ref, mc_ref, lg_ref, lb_ref,
                    xo_ref, buf, sem, *, tm, ctx, alpha, ff):
    i = pl.program_id(1)

    def issue(t, carry):
        for kq in range(TOP_K):
            _row_copy(y_ref, pos_ref[0, 0, t * TOP_K + kq], buf.at[kq], t, sem).start()
        return carry

    lax.fori_loop(0, tm, issue, 0, unroll=4)
    lo, hi = _unpack_halves(h2_ref[0])
    f = _swiglu(lo.astype(BF16), hi.astype(BF16), sgu_ref[...], sdn_ref[...], ff)
    for kq in range(TOP_K):
        pltpu.make_async_copy(y_ref.at[pl.ds(0, tm)], buf.at[kq], sem).wait()
    wt = wt_ref[0]
    half = f.shape[1] // 2
    f_lo = f[:, :half]
    f_hi = f[:, half:]
    for kq in range(TOP_K):
        lo, hi = _unpack_halves(buf[kq])
        f_lo = f_lo + wt[:, kq:kq + 1] * lo
        f_hi = f_hi + wt[:, kq:kq + 1] * hi
    f = jnp.concatenate([f_lo, f_hi], axis=1)
    isc = _row_is_ctx(tm, i * tm, ctx)
    g2 = jnp.where(isc, mc_ref[0][5:6], ml_ref[0][5:6])
    xo_ref[0] = _ln(alpha * x_ref[0] + g2 * f, POST_LN_EPS) * lg_ref[...] + lb_ref[...]


def _combine(pos, y_rows, wts, h2p, sh_gu, sh_dn, x_all, modl, modc, ln_g, ln_b, ctx, alpha, tm):
    b, ta, d = x_all.shape
    ff = sh_dn.shape[0]
    nt = ta // tm
    row = lambda bb, i: (bb, i, 0)
    return pl.pallas_call(
        functools.partial(_combine_kernel, tm=tm, ctx=ctx, alpha=alpha, ff=ff),
        grid=(b, nt),
        in_specs=[pl.BlockSpec((1, 1, tm * TOP_K), lambda bb, i: (bb * nt + i, 0, 0), memory_space=pltpu.SMEM),
                  pl.BlockSpec(memory_space=pl.ANY),
                  pl.BlockSpec((1, tm, LANE), row),
                  pl.BlockSpec((1, tm, d // 2), row),
                  pl.BlockSpec((d, 2 * ff), lambda bb, i: (0, 0)),
                  pl.BlockSpec((ff, d), lambda bb, i: (0, 0)),
                  pl.BlockSpec((1, tm, d), row),
                  pl.BlockSpec((1, 6, d), lambda bb, i: (bb, 0, 0)),
                  pl.BlockSpec((1, 6, d), lambda bb, i: (0, 0, 0)),
                  pl.BlockSpec((1, d), lambda bb, i: (0, 0)),
                  pl.BlockSpec((1, d), lambda bb, i: (0, 0))],
        out_specs=pl.BlockSpec((1, tm, d), row),
        out_shape=jax.ShapeDtypeStruct((b, ta, d), F32),
        scratch_shapes=[pltpu.VMEM((TOP_K, tm, d // 2), jnp.uint32), pltpu.SemaphoreType.DMA(())],
        compiler_params=_cparams(("arbitrary", "arbitrary")),
        name="moe_combine_ln",
    )(pos, y_rows, wts, h2p, sh_gu, sh_dn, x_all, modl, modc, ln_g.reshape(1, d), ln_b.reshape(1, d))


def _rope_tables(n_lat, ctx):
    rows = n_lat // GRID_W
    row = jnp.repeat(jnp.arange(rows, dtype=F32), GRID_W)
    col = jnp.tile(jnp.arange(GRID_W, dtype=F32), rows)
    n_freq = ROPE_DIM // 4
    inv_freq = ROPE_BASE ** (-jnp.arange(n_freq, dtype=F32) / n_freq)
    ang = jnp.concatenate([row[:, None] * inv_freq, col[:, None] * inv_freq], -1)
    cos, sin = jnp.cos(ang), jnp.sin(ang)
    cos = jnp.concatenate([jnp.ones((ctx, ROPE_DIM // 2), F32), cos], 0)
    sin = jnp.concatenate([jnp.zeros((ctx, ROPE_DIM // 2), F32), sin], 0)
    cosf = jnp.tile(jnp.concatenate([cos, cos], -1), (1, LANE // ROPE_DIM))
    sinf = jnp.tile(jnp.concatenate([-sin, sin], -1), (1, LANE // ROPE_DIM))
    return cosf, sinf


def _pack_small_weights(w_in_l):
    d = w_in_l.shape[0]
    offs = np.cumsum([0, MLA_Q_RANK, MLA_KV_RANK + MLA_ROPE, RWKV_STREAM,
                      (SWA_HEADS + 2 * SWA_KV_HEADS) * SWA_HEAD, 3 * DIFF_HEADS * 2 * DIFF_HEAD])
    placed = sorted(zip([SEG_MQ, SEG_MKV, SEG_RW, SEG_SW, SEG_DF], range(5)))
    parts = []
    cur = 0
    for start, s in placed:
        if start > cur:
            parts.append(jnp.zeros((d, start - cur), BF16))
        parts.append(w_in_l[:, int(offs[s]):int(offs[s + 1])].astype(BF16))
        cur = start + int(offs[s + 1] - offs[s])
    parts.append(jnp.zeros((d, SMALL_COLS - cur), BF16))
    return jnp.concatenate(parts, axis=1), int(offs[5])


def kernel(x, c, ctx, c_ctx, w_mod, b_mod, w_in, mla_q_norm, mla_w_qup, mla_kv_norm, mla_w_kvup, rwkv_mu, rwkv_w0, rwkv_w_lora, rwkv_a0, rwkv_a_lora, rwkv_g_lora, rwkv_k_k, rwkv_k_a, rwkv_r_k, rwkv_ln_g, rwkv_ln_b, swa_sink, diff_lambda, diff_subln, w_branch, w_out, ln1_g, ln1_b, router_w, router_bias, exp_w_gu, exp_w_dn, sh_w_gu, sh_w_dn, ln2_g, ln2_b):
    depth = w_mod.shape[0]
    b, n_lat, d = x.shape
    n_ctx = ctx.shape[1]
    ta = n_ctx + n_lat
    alpha = (2 * depth) ** 0.25
    cosf, sinf = _rope_tables(n_lat, n_ctx)
    x_all = jnp.concatenate([ctx, x], axis=1)

    cond = jnp.concatenate([c, c_ctx[None]], axis=0)
    cond = jnp.pad(jax.nn.silu(cond), ((0, SUBLANE - (b + 1) % SUBLANE), (0, 0)))

    for l in range(depth):
        mod = _matmul_bias(cond, w_mod, b_mod, l).reshape(cond.shape[0], 6, d)
        modl, modc = mod[:b], mod[b:b + 1]

        w_small, gate_off = _pack_small_weights(w_in[l])
        p = _inproj(x_all, modl, modc, w_small, n_ctx)

        wq = mla_w_qup[l].reshape(MLA_Q_RANK, MLA_HEADS, MLA_NOPE + MLA_ROPE)
        wq = jnp.pad(wq, ((0, 0), (0, 0), (0, 256 - MLA_NOPE - MLA_ROPE))).reshape(MLA_Q_RANK, 1024)
        q_a, k_a, v_a = _mla_prep(p, mla_q_norm[l].reshape(1, -1), mla_kv_norm[l].reshape(1, -1),
                                  wq.astype(BF16), mla_w_kvup[l].astype(BF16), cosf, sinf)
        out_a = _mla_attn(q_a, k_a, v_a, n_ctx)

        out_b = _rwkv_mixer(p, n_ctx, rwkv_mu[l], rwkv_w0[l],
                            rwkv_w_lora[l], rwkv_a0[l], rwkv_a_lora[l], rwkv_g_lora[l], rwkv_k_k[l],
                            rwkv_k_a[l], rwkv_r_k[l], rwkv_ln_g[l], rwkv_ln_b[l])

        sinkv = jnp.broadcast_to(swa_sink[l][:, None], (SWA_HEADS, LANE))
        out_s = _swa(p, cosf, sinf, sinkv, n_ctx)

        lf = diff_lambda[l]
        lam_init = 0.8 - 0.6 * math.exp(-0.3 * l)
        lam = jnp.exp(jnp.sum(lf[0] * lf[1])) - jnp.exp(jnp.sum(lf[2] * lf[3])) + lam_init
        out_d = _diff_attn(p, cosf, sinf, jnp.full((1, LANE), lam, F32), diff_subln[l].reshape(1, LANE),
                           n_ctx, lam_init)

        o4 = jnp.stack([out_a, out_b, out_s, out_d], axis=0)
        wg, wb, wo = _gate_window_weights(w_in[l], gate_off, w_branch[l], w_out[l])
        merged = _merge_a(x_all, modl, modc, o4, wg, wb, n_ctx)
        x_all, h2p = _merge_b(merged, wo, x_all, modl, modc, ln1_g[l], ln1_b[l], n_ctx, alpha)

        idx, wts, rank, counts = _router(x_all, modl, modc, router_w[l], router_bias[l], n_ctx)
        pos, blk_e, n_used, n_rows = _moe_plan(idx[..., :TOP_K], rank[..., :TOP_K], counts[0], MOE_BLK)
        tmd = _tile(ta, 128)
        pos = pos.reshape(b * ta // tmd, 1, tmd * TOP_K)
        xg = _dispatch(pos, h2p.reshape(b * ta, d // 2), n_rows, tmd)
        y_rows = _experts(xg, blk_e, n_used, exp_w_gu, exp_w_dn, l, MOE_BLK)
        x_all = _combine(pos, y_rows, wts, h2p, sh_w_gu[l].astype(BF16), sh_w_dn[l].astype(BF16), x_all,
                         modl, modc, ln2_g[l], ln2_b[l], n_ctx, alpha, tmd)
    return x_all[:, n_ctx:]
```

```python
import functools
import math

import jax
import jax.numpy as jnp
import numpy as np
from jax import lax
from jax.experimental import pallas as pl
from jax.experimental.pallas import tpu as pltpu

F32 = jnp.float32
BF16 = jnp.bfloat16
HIGHEST = lax.Precision.HIGHEST

GRID_W = 64
ROPE_BASE = 10000.0
ROPE_DIM = 64
ADALN_EPS = 1e-6
POST_LN_EPS = 1e-5
NEG_INF = -1e30

MLA_HEADS = 4
MLA_Q_RANK = 384
MLA_KV_RANK = 256
MLA_NOPE = 128
MLA_ROPE = 64
MLA_V = 128

RWKV_HEADS = 8
RWKV_HEAD = 64
RWKV_W = RWKV_HEADS * RWKV_HEAD
DECAY_LORA = 32
ICLR_LORA = 32
GATE_LORA = 96
RWKV_STREAM = 3 * RWKV_W + 2 * DECAY_LORA + 2 * ICLR_LORA + GATE_LORA
RWKV_GN_EPS = 64e-5
RWKV_CHUNK = 64

SWA_HEADS = 8
SWA_KV_HEADS = 2
SWA_GROUP = SWA_HEADS // SWA_KV_HEADS
SWA_HEAD = 64
WINDOW = 128

DIFF_HEADS = 4
DIFF_HEAD = 64

N_BRANCH = 4
BRANCH_W = 512

N_EXPERTS = 64
TOP_K = 6
N_GROUPS = 8
TOPK_GROUPS = 4
GROUP_SIZE = N_EXPERTS // N_GROUPS
ROUTED_SCALE = 2.5
MOE_BLK = 256

LANE = 128
SUBLANE = 8

SEG_SW = 0
SEG_MQ = 768
SEG_MKV = 1152
SEG_DF = 1536
SEG_RW = 3584
RW_COLS = 1792
SMALL_COLS = SEG_RW + RW_COLS


def _tile(n, target, mult=SUBLANE):
    best = None
    for d in range(mult, min(n, target) + 1, mult):
        if n % d == 0:
            best = d
    assert best is not None, (n, target, mult)
    return best


def _cparams(sem, vmem_mb=48):
    return pltpu.CompilerParams(dimension_semantics=sem, vmem_limit_bytes=vmem_mb * 1024 * 1024)


def _sigmoid(x):
    return 1.0 / (1.0 + jnp.exp(-x))


def _ln(x, eps):
    mu = jnp.mean(x, -1, keepdims=True)
    xc = x - mu
    var = jnp.mean(xc * xc, -1, keepdims=True)
    return xc * lax.rsqrt(var + eps)


def _rms(x, eps):
    return x * lax.rsqrt(jnp.mean(x * x, -1, keepdims=True) + eps)


def _row_is_ctx(n_rows, row0, ctx):
    rows = row0 + lax.broadcasted_iota(jnp.int32, (n_rows, 1), 0)
    return rows < ctx


def _modulated(x, ml, mc, sh_i, sc_i, isc):
    sh = jnp.where(isc, mc[sh_i:sh_i + 1], ml[sh_i:sh_i + 1])
    sc = jnp.where(isc, mc[sc_i:sc_i + 1], ml[sc_i:sc_i + 1])
    return _ln(x, ADALN_EPS) * (1.0 + sc) + sh


def _fill_modulated(xn_ref, x_ref, ml, mc, row0, ctx):
    tm = xn_ref.shape[0]
    ch = _tile(tm, 272)
    for r in range(0, tm, ch):
        isc = _row_is_ctx(ch, row0 + r, ctx)
        xn_ref[r:r + ch, :] = _modulated(x_ref[0, r:r + ch, :], ml, mc, 0, 1, isc).astype(BF16)


def _swap32(x):
    lane = lax.broadcasted_iota(jnp.int32, x.shape, 1)
    up = pltpu.roll(x, LANE - 32, 1)
    dn = pltpu.roll(x, 32, 1)
    return jnp.where((lane % 64) < 32, up, dn)


def _rope(x, cosf, sinf):
    return x * cosf + _swap32(x) * sinf


def _pack_halves(x):
    n = x.shape[1] // 2
    lo = pltpu.bitcast(x[:, :n].astype(BF16).astype(F32), jnp.uint32)
    hi = pltpu.bitcast(x[:, n:].astype(BF16).astype(F32), jnp.uint32)
    return (hi & jnp.uint32(0xFFFF0000)) | (lo >> 16)


def _unpack_halves(u):
    lo = pltpu.bitcast(u << 16, F32)
    hi = pltpu.bitcast(u & jnp.uint32(0xFFFF0000), F32)
    return lo, hi


def _dot(a, b, **kw):
    return jnp.dot(a, b, preferred_element_type=F32, **kw)


def _dot_nt(a, b, **kw):
    return lax.dot_general(a, b, (((1,), (1,)), ((), ())), preferred_element_type=F32, **kw)


def _dot_tn(a, b, **kw):
    return lax.dot_general(a, b, (((0,), (0,)), ((), ())), preferred_element_type=F32, **kw)


def _mm_kernel(x_ref, w_ref, b_ref, o_ref):
    o_ref[...] = _dot(x_ref[...], w_ref[0], precision=HIGHEST) + b_ref[0]


def _matmul_bias(x, w, b, l):
    m, k = x.shape
    n = w.shape[2]
    tn = _tile(n, 1024, LANE)
    return pl.pallas_call(
        _mm_kernel,
        grid=(n // tn,),
        in_specs=[pl.BlockSpec((m, k), lambda j: (0, 0)),
                  pl.BlockSpec((1, k, tn), lambda j: (l, 0, j)),
                  pl.BlockSpec((1, 1, tn), lambda j: (l, 0, j))],
        out_specs=pl.BlockSpec((m, tn), lambda j: (0, j)),
        out_shape=jax.ShapeDtypeStruct((m, n), F32),
        compiler_params=_cparams(("arbitrary",)),
        name="mod_matmul",
    )(x, w, b.reshape(b.shape[0], 1, n))


def _inproj_kernel(x_ref, ml_ref, mc_ref, w_ref, o_ref, xn_ref, *, tm, ctx):
    i = pl.program_id(1)
    j = pl.program_id(2)

    @pl.when(j == 0)
    def _():
        _fill_modulated(xn_ref, x_ref, ml_ref[0], mc_ref[0], i * tm, ctx)

    o_ref[0] = _dot(xn_ref[...], w_ref[...])


def _inproj(x_all, modl, modc, w_small, ctx):
    b, ta, d = x_all.shape
    n = w_small.shape[1]
    tm = _tile(ta, 1088)
    tn = _tile(n, 1024, LANE)
    return pl.pallas_call(
        functools.partial(_inproj_kernel, tm=tm, ctx=ctx),
        grid=(b, ta // tm, n // tn),
        in_specs=[pl.BlockSpec((1, tm, d), lambda bb, i, j: (bb, i, 0)),
                  pl.BlockSpec((1, 6, d), lambda bb, i, j: (bb, 0, 0)),
                  pl.BlockSpec((1, 6, d), lambda bb, i, j: (0, 0, 0)),
                  pl.BlockSpec((d, tn), lambda bb, i, j: (0, j))],
        out_specs=pl.BlockSpec((1, tm, tn), lambda bb, i, j: (bb, i, j)),
        out_shape=jax.ShapeDtypeStruct((b, ta, n), F32),
        scratch_shapes=[pltpu.VMEM((tm, d), BF16)],
        compiler_params=_cparams(("parallel", "parallel", "arbitrary"), 56),
        name="inproj",
    )(x_all, modl, modc, w_small)


def _mla_prep_kernel(mq_ref, mkv_ref, qg_ref, kg_ref, wq_ref, wkv_ref, cos_ref, sin_ref,
                     q_ref, k_ref, v_ref):
    mq = mq_ref[0]
    mkv = mkv_ref[0]
    cosv = cos_ref[...]
    sinv = sin_ref[...]
    qn = _rms(mq, 1e-6) * qg_ref[...]
    q = _dot(qn.astype(BF16), wq_ref[...])
    kvn = _rms(mkv[:, :MLA_KV_RANK], 1e-6) * kg_ref[...]
    kv = _dot(kvn.astype(BF16), wkv_ref[...])
    kr = _rope(mkv[:, MLA_KV_RANK:MLA_KV_RANK + LANE], cosv, sinv).astype(BF16)
    scale = (MLA_NOPE + MLA_ROPE) ** -0.5 * LOG2E
    for h in range(MLA_HEADS):
        c0 = h * 256
        q_ref[0, :, c0:c0 + 128] = (q[:, c0:c0 + 128] * scale).astype(BF16)
        q_ref[0, :, c0 + 128:c0 + 256] = (_rope(q[:, c0 + 128:c0 + 256], cosv, sinv) * scale).astype(BF16)
        k_ref[0, :, c0:c0 + 128] = kv[:, c0:c0 + 128].astype(BF16)
        k_ref[0, :, c0 + 128:c0 + 256] = kr
        v_ref[0, :, h * 128:(h + 1) * 128] = kv[:, c0 + 128:c0 + 256].astype(BF16)


def _mla_prep(p, qg, kg, wq, wkv, cosf, sinf):
    b, ta, _ = p.shape
    tm = _tile(ta, 544)
    return pl.pallas_call(
        _mla_prep_kernel,
        grid=(b, ta // tm),
        in_specs=[pl.BlockSpec((1, tm, 384), lambda bb, i: (bb, i, SEG_MQ // 384)),
                  pl.BlockSpec((1, tm, 384), lambda bb, i: (bb, i, SEG_MKV // 384)),
                  pl.BlockSpec((1, MLA_Q_RANK), lambda bb, i: (0, 0)),
                  pl.BlockSpec((1, MLA_KV_RANK), lambda bb, i: (0, 0)),
                  pl.BlockSpec((MLA_Q_RANK, 1024), lambda bb, i: (0, 0)),
                  pl.BlockSpec((MLA_KV_RANK, 1024), lambda bb, i: (0, 0)),
                  pl.BlockSpec((tm, LANE), lambda bb, i: (i, 0)),
                  pl.BlockSpec((tm, LANE), lambda bb, i: (i, 0))],
        out_specs=[pl.BlockSpec((1, tm, 1024), lambda bb, i: (bb, i, 0)),
                   pl.BlockSpec((1, tm, 1024), lambda bb, i: (bb, i, 0)),
                   pl.BlockSpec((1, tm, 512), lambda bb, i: (bb, i, 0))],
        out_shape=[jax.ShapeDtypeStruct((b, ta, 1024), BF16),
                   jax.ShapeDtypeStruct((b, ta, 1024), BF16),
                   jax.ShapeDtypeStruct((b, ta, 512), BF16)],
        compiler_params=_cparams(("parallel", "parallel")),
        name="mla_prep",
    )(p, p, qg, kg, wq, wkv, cosf, sinf)


LOG2E = 1.4426950408889634


def _softmax_pv(s, v):
    m = jnp.max(s, -1, keepdims=True)
    p = jnp.exp2(s - m)
    l = jnp.sum(p, -1, keepdims=True)
    return _dot(p.astype(BF16), v) / l


def _mla_attn_kernel(q_ref, k_ref, v_ref, o_ref, *, n_ctx_tiles, ctx):
    qi = pl.program_id(2)

    def run(nk):
        s = _dot_nt(q_ref[0], k_ref[0, :nk, :])
        o_ref[0] = _softmax_pv(s, v_ref[0, :nk, :]).astype(o_ref.dtype)

    @pl.when(qi < n_ctx_tiles)
    def _():
        run(ctx)

    @pl.when(qi >= n_ctx_tiles)
    def _():
        run(k_ref.shape[1])


def _mla_attn(q, k, v, ctx):
    b, ta, _ = q.shape
    tq = _tile(math.gcd(ta, ctx), 256)
    return pl.pallas_call(
        functools.partial(_mla_attn_kernel, n_ctx_tiles=ctx // tq, ctx=ctx),
        grid=(b, MLA_HEADS, ta // tq),
        in_specs=[pl.BlockSpec((1, tq, 256), lambda bb, h, i: (bb, i, h)),
                  pl.BlockSpec((1, ta, 256), lambda bb, h, i: (bb, 0, h)),
                  pl.BlockSpec((1, ta, 128), lambda bb, h, i: (bb, 0, h))],
        out_specs=pl.BlockSpec((1, tq, 128), lambda bb, h, i: (bb, i, h)),
        out_shape=jax.ShapeDtypeStruct((b, ta, 512), BF16),
        compiler_params=_cparams(("parallel", "parallel", "arbitrary")),
        name="mla_attn",
    )(q, k, v)


def _diff_attn_kernel(q_ref, k_ref, v_ref, cq_ref, sq_ref, ck_ref, sk_ref, lam_ref, sub_ref,
                      o_ref, ks_ref, vs_ref, *, n_ctx_tiles, ctx, post_scale):
    qi = pl.program_id(2)

    @pl.when(qi == 0)
    def _():
        ks_ref[...] = _rope(k_ref[0], ck_ref[...], sk_ref[...]).astype(BF16)
        vs_ref[...] = v_ref[0].astype(BF16)

    def run(nk):
        q = _rope(q_ref[0], cq_ref[...], sq_ref[...]) * (DIFF_HEAD ** -0.5 * LOG2E)
        lane = lax.broadcasted_iota(jnp.int32, q.shape, 1)
        q1 = jnp.where(lane < 64, q, 0.0).astype(BF16)
        q2 = jnp.where(lane >= 64, q, 0.0).astype(BF16)
        k = ks_ref[:nk, :]
        v = vs_ref[:nk, :]
        o = _softmax_pv(_dot_nt(q1, k), v) - lam_ref[...] * _softmax_pv(_dot_nt(q2, k), v)
        o = _rms(o, 1e-5) * sub_ref[...] * post_scale
        o_ref[0] = o.astype(o_ref.dtype)

    @pl.when(qi < n_ctx_tiles)
    def _():
        run(ctx)

    @pl.when(qi >= n_ctx_tiles)
    def _():
        run(ks_ref.shape[0])


def _diff_attn(p, cosf, sinf, lamv, subln, ctx, lam_init):
    b, ta, _ = p.shape
    tq = _tile(math.gcd(ta, ctx), 256)
    cb = SEG_DF // LANE
    return pl.pallas_call(
        functools.partial(_diff_attn_kernel, n_ctx_tiles=ctx // tq, ctx=ctx, post_scale=1.0 - lam_init),
        grid=(b, DIFF_HEADS, ta // tq),
        in_specs=[pl.BlockSpec((1, tq, LANE), lambda bb, h, i: (bb, i, cb + h)),
                  pl.BlockSpec((1, ta, LANE), lambda bb, h, i: (bb, 0, cb + DIFF_HEADS + h)),
                  pl.BlockSpec((1, ta, LANE), lambda bb, h, i: (bb, 0, cb + 2 * DIFF_HEADS + h)),
                  pl.BlockSpec((tq, LANE), lambda bb, h, i: (i, 0)),
                  pl.BlockSpec((tq, LANE), lambda bb, h, i: (i, 0)),
                  pl.BlockSpec((ta, LANE), lambda bb, h, i: (0, 0)),
                  pl.BlockSpec((ta, LANE), lambda bb, h, i: (0, 0)),
                  pl.BlockSpec((1, LANE), lambda bb, h, i: (0, 0)),
                  pl.BlockSpec((1, LANE), lambda bb, h, i: (0, 0))],
        out_specs=pl.BlockSpec((1, tq, LANE), lambda bb, h, i: (bb, i, h)),
        out_shape=jax.ShapeDtypeStruct((b, ta, 512), BF16),
        scratch_shapes=[pltpu.VMEM((ta, LANE), BF16), pltpu.VMEM((ta, LANE), BF16)],
        compiler_params=_cparams(("parallel", "parallel", "arbitrary")),
        name="diff_attn",
    )(p, p, p, cosf, sinf, cosf, sinf, lamv, subln)


def _swa_kernel(q_ref, k_ref, v_ref, cq_ref, sq_ref, ck_ref, sk_ref, sink_ref,
                o_ref, ks_ref, vs_ref, *, ctx):
    t = pl.program_id(1)
    n_ctx_tiles = ctx // WINDOW
    ta = ks_ref.shape[0]
    span = 3 * WINDOW

    @pl.when(t == 0)
    def _():
        ks_ref[...] = _rope(k_ref[0], ck_ref[...], sk_ref[...]).astype(BF16)
        vs_ref[...] = v_ref[0].astype(BF16)

    lane = lax.broadcasted_iota(jnp.int32, (WINDOW, LANE), 1)
    cq = cq_ref[...]
    sq = sq_ref[...]
    qblk = [_rope(q_ref[0, :, hb * LANE:(hb + 1) * LANE], cq, sq) * (SWA_HEAD ** -0.5)
            for hb in range(SWA_HEADS // 2)]

    def head_q(hk, g):
        h = hk * SWA_GROUP + g
        blk = qblk[h // 2]
        if h % 2 != hk:
            blk = pltpu.roll(blk, 64, 1)
        keep = (lane >= 64) if hk == 1 else (lane < 64)
        return jnp.where(keep, blk, 0.0)

    def run(local):
        kc = ks_ref[:ctx, :]
        vc = vs_ref[:ctx, :]
        if local:
            n = t - n_ctx_tiles
            ws = jnp.clip(ctx + (n - 1) * WINDOW, ctx, ta - span)
            ws = pl.multiple_of(ws, WINDOW)
            kl = ks_ref[pl.ds(ws, span), :]
            vl = vs_ref[pl.ds(ws, span), :]
            kpos = (ws - ctx) + lax.broadcasted_iota(jnp.int32, (1, span), 1)
            qpos = n * WINDOW + lax.broadcasted_iota(jnp.int32, (SWA_GROUP * WINDOW, 1), 0) % WINDOW
            valid = jnp.abs(kpos - qpos) <= WINDOW
        outs = {}
        for hk in range(SWA_KV_HEADS):
            qst = jnp.concatenate([head_q(hk, g) for g in range(SWA_GROUP)], axis=0).astype(BF16)
            sink = jnp.concatenate(
                [jnp.broadcast_to(sink_ref[hk * SWA_GROUP + g:hk * SWA_GROUP + g + 1, 0:1], (WINDOW, 1))
                 for g in range(SWA_GROUP)], axis=0)
            s_c = _dot_nt(qst, kc)
            m = jnp.maximum(jnp.max(s_c, -1, keepdims=True), sink)
            if local:
                s_l = jnp.where(valid, _dot_nt(qst, kl), NEG_INF)
                m = jnp.maximum(m, jnp.max(s_l, -1, keepdims=True))
            p_c = jnp.exp(s_c - m)
            l = jnp.sum(p_c, -1, keepdims=True) + jnp.exp(sink - m)
            acc = _dot(p_c.astype(BF16), vc)
            if local:
                p_l = jnp.exp(s_l - m)
                l = l + jnp.sum(p_l, -1, keepdims=True)
                acc = acc + _dot(p_l.astype(BF16), vl)
            o = acc / l
            for g in range(SWA_GROUP):
                res = o[g * WINDOW:(g + 1) * WINDOW, :]
                if g % 2 != hk:
                    res = pltpu.roll(res, 64, 1)
                outs[hk * SWA_GROUP + g] = res
        for hb in range(SWA_HEADS // 2):
            blk = jnp.where(lane < 64, outs[2 * hb], outs[2 * hb + 1])
            o_ref[0, :, hb * LANE:(hb + 1) * LANE] = blk.astype(o_ref.dtype)

    @pl.when(t < n_ctx_tiles)
    def _():
        run(False)

    @pl.when(t >= n_ctx_tiles)
    def _():
        run(True)


def _swa(p, cosf, sinf, sinkv, ctx):
    b, ta, _ = p.shape
    qb = SEG_SW // 512
    kb = (SEG_SW + 512) // LANE
    return pl.pallas_call(
        functools.partial(_swa_kernel, ctx=ctx),
        grid=(b, ta // WINDOW),
        in_specs=[pl.BlockSpec((1, WINDOW, 512), lambda bb, t: (bb, t, qb)),
                  pl.BlockSpec((1, ta, LANE), lambda bb, t: (bb, 0, kb)),
                  pl.BlockSpec((1, ta, LANE), lambda bb, t: (bb, 0, kb + 1)),
                  pl.BlockSpec((WINDOW, LANE), lambda bb, t: (t, 0)),
                  pl.BlockSpec((WINDOW, LANE), lambda bb, t: (t, 0)),
                  pl.BlockSpec((ta, LANE), lambda bb, t: (0, 0)),
                  pl.BlockSpec((ta, LANE), lambda bb, t: (0, 0)),
                  pl.BlockSpec((SWA_HEADS, LANE), lambda bb, t: (0, 0))],
        out_specs=pl.BlockSpec((1, WINDOW, 512), lambda bb, t: (bb, t, 0)),
        out_shape=jax.ShapeDtypeStruct((b, ta, 512), BF16),
        scratch_shapes=[pltpu.VMEM((ta, LANE), BF16), pltpu.VMEM((ta, LANE), BF16)],
        compiler_params=_cparams(("parallel", "arbitrary")),
        name="swa_attn",
    )(p, p, p, cosf, sinf, cosf, sinf, sinkv)


def _softplus(x):
    return jnp.maximum(x, 0.0) + jnp.log(1.0 + jnp.exp(-jnp.abs(x)))


def _rwkv_prep_kernel(z_ref, hp_ref, hn_ref, mu_ref, w0_ref, a0_ref, wl_ref, al_ref, gl_ref, kk_ref, ka_ref,
                      rk_ref, bd_ref, r_o, v_o, kk_o, ld0_o, ld1_o, kd0_o, kd1_o, a0_o, a1_o, bv_o, g_o, zs_ref):
    tm = z_ref.shape[1]
    w = RWKV_W
    z = z_ref[0]
    zs_ref[SUBLANE:SUBLANE + tm, :] = z
    zs_ref[SUBLANE - 1:SUBLANE, :] = hp_ref[0, 0]
    zs_ref[SUBLANE + tm:SUBLANE + tm + 1, :] = hn_ref[0, 0]
    shifted = 0.5 * (zs_ref[SUBLANE - 1:SUBLANE - 1 + tm, :] + zs_ref[SUBLANE + 1:SUBLANE + 1 + tm, :])
    seg = z + (shifted - z) * mu_ref[...]
    r = seg[:, 0:w]
    k = seg[:, w:2 * w]
    v = seg[:, 2 * w:3 * w]
    lo = seg[:, 3 * w:3 * w + LANE]
    gl = seg[:, 3 * w + LANE:3 * w + 2 * LANE]
    hp = dict(precision=HIGHEST)
    bd = bd_ref[...]
    kk = k * kk_ref[...]
    kk = kk / jnp.maximum(jnp.sqrt(_dot(kk * kk, bd, **hp)), 1e-12)
    th = jnp.tanh(lo).astype(BF16)
    lo_b = lo.astype(BF16)
    ksum = jnp.zeros_like(k)
    for d, (ld_o, kd_o, a_o) in enumerate(((ld0_o, kd0_o, a0_o), (ld1_o, kd1_o, a1_o))):
        wd = w0_ref[d:d + 1, :] + _dot(th, wl_ref[d])
        wd = -_softplus(-wd) - 0.5
        a = _sigmoid(a0_ref[d:d + 1, :] + _dot(lo_b, al_ref[d]))
        kd = k * (1.0 + (a - 1.0) * ka_ref[...])
        ld_o[0] = -jnp.exp(wd)
        kd_o[0] = kd
        a_o[0] = a
        ksum = ksum + kd
    r_o[0] = r
    v_o[0] = v
    kk_o[0] = kk
    bv_o[0] = _dot(r * ksum * rk_ref[...], bd, **hp) * v
    g_o[0] = _dot(_sigmoid(gl).astype(BF16), gl_ref[...])


def _rwkv_prep(p, hprev, hnext, mu, w0, a0, wl, al, gl, k_k, k_a, r_k, bd, tm):
    b, ta, _ = p.shape
    w = RWKV_W
    row = lambda bb, i: (bb, i, 0)
    vec = lambda n: pl.BlockSpec((1, n), lambda bb, i: (0, 0))
    out = jax.ShapeDtypeStruct((b, ta, w), F32)
    return pl.pallas_call(
        _rwkv_prep_kernel,
        grid=(b, ta // tm),
        in_specs=[pl.BlockSpec((1, tm, RW_COLS), lambda bb, i: (bb, i, SEG_RW // RW_COLS)),
                  pl.BlockSpec((1, 1, 1, RW_COLS), lambda bb, i: (bb, i, 0, 0)),
                  pl.BlockSpec((1, 1, 1, RW_COLS), lambda bb, i: (bb, i, 0, 0)),
                  vec(RW_COLS),
                  pl.BlockSpec((2, w), lambda bb, i: (0, 0)),
                  pl.BlockSpec((2, w), lambda bb, i: (0, 0)),
                  pl.BlockSpec((2, LANE, w), lambda bb, i: (0, 0, 0)),
                  pl.BlockSpec((2, LANE, w), lambda bb, i: (0, 0, 0)),
                  pl.BlockSpec((LANE, w), lambda bb, i: (0, 0)),
                  vec(w), vec(w), vec(w),
                  pl.BlockSpec((w, w), lambda bb, i: (0, 0))],
        out_specs=[pl.BlockSpec((1, tm, w), row)] * 11,
        out_shape=[out] * 11,
        scratch_shapes=[pltpu.VMEM((tm + 2 * SUBLANE, RW_COLS), F32)],
        compiler_params=_cparams(("parallel", "parallel")),
        name="rwkv_prep",
    )(p, hprev, hnext, mu, w0, a0, wl, al, gl, k_k, k_a, r_k, bd)


def _split3(x):
    hi = x.astype(BF16)
    r1 = x - hi.astype(F32)
    mid = r1.astype(BF16)
    return hi, mid, (r1 - mid.astype(F32)).astype(BF16)


def _split2(x):
    hi = x.astype(BF16)
    return hi, (x - hi.astype(F32)).astype(BF16)


def _mm3(a, b, dot=_dot):
    return dot(a[0], b[0]) + dot(a[0], b[1]) + dot(a[1], b[0])


def _rwkv_chunk_kernel(r_ref, v_ref, kk_ref, ld0_ref, ld1_ref, kd0_ref, kd1_ref, a0_ref, a1_ref, pz_ref, wu_ref):
    C = RWKV_CHUNK
    hd = RWKV_HEAD
    row = lax.broadcasted_iota(jnp.int32, (C, C), 0)
    col = lax.broadcasted_iota(jnp.int32, (C, C), 1)
    eye = (row == col).astype(F32)
    probs = []
    for d, (ld_ref, kd_ref, a_ref) in enumerate(((ld0_ref, kd0_ref, a0_ref), (ld1_ref, kd1_ref, a1_ref))):
        incl = (row >= col) if d == 0 else (row <= col)
        strict = (row > col) if d == 0 else (row < col)
        last = C - 1 if d == 0 else 0
        for h in range(RWKV_HEADS):
            sl = slice(h * hd, (h + 1) * hd)
            probs.append(dict(d=d, h=h, incl=incl, strict=strict, last=last,
                              ones=jnp.where(incl, 1.0, 0.0).astype(BF16),
                              r=r_ref[0, :, sl], v=v_ref[0, :, sl], kk=kk_ref[0, :, sl],
                              ld=ld_ref[0, :, sl], k=kd_ref[0, :, sl], a=a_ref[0, :, sl]))
    bf = lambda z: z.astype(BF16)
    for q in probs:
        q["cs"] = sum(_dot(q["ones"], part) for part in _split3(q["ld"]))
    for q in probs:
        cs = q["cs"]
        e_cs = jnp.exp(cs)
        e_inv = jnp.exp(-cs)
        q["g_end"] = e_cs[q["last"]:q["last"] + 1, :]
        q["kkd"] = q["kk"] * jnp.exp(cs - q["ld"])
        q["rd"] = q["r"] * e_cs
        q["bi"] = bf(q["kk"] * q["a"] * e_inv)
        q["ki"] = bf(q["k"] * e_inv)
        q["lhs"] = bf(jnp.concatenate([q["kkd"], q["rd"]], axis=0))
    for q in probs:
        gb = _dot_nt(q["lhs"], q["bi"])
        gk = _dot_nt(q["lhs"], q["ki"])
        q["pw"] = -jnp.where(q["strict"], gb[:C], 0.0)
        q["q_b"] = bf(jnp.where(q["incl"], gb[C:], 0.0))
        q["akqk"] = bf(jnp.concatenate([jnp.where(q["strict"], gk[:C], 0.0),
                                        jnp.where(q["incl"], gk[C:], 0.0)], axis=0))
        q["tinv"] = eye + q["pw"]
    for _ in range(5):
        for q in probs:
            pw = bf(q["pw"])
            q["pw"] = _dot(pw, pw)
        for q in probs:
            q["tinv"] = q["tinv"] + _dot(bf(q["tinv"]), bf(q["pw"]))
    for q in probs:
        q["vs"] = bf(q["v"])
        q["av"] = _dot(q["akqk"], q["vs"])
    for q in probs:
        tinv = bf(q["tinv"])
        q["w1"] = _dot(tinv, bf(q["kkd"]))
        q["u1"] = _dot(tinv, bf(q["av"][:C]))
    for q in probs:
        w1 = bf(q["w1"])
        u1 = bf(q["u1"])
        w2 = q["rd"] - _dot(q["q_b"], w1)
        u2 = q["av"][C:] - _dot(q["q_b"], u1)
        pt = (eye - _dot_tn(w1, q["bi"])) * q["g_end"]
        zt = (_dot_tn(q["vs"], q["ki"]) - _dot_tn(u1, q["bi"])) * q["g_end"]
        pz_ref[q["d"], 0, 0, q["h"]] = jnp.concatenate([pt, zt], axis=1)
        wu_ref[q["d"], 0, 0, q["h"]] = jnp.concatenate([w2, u2], axis=1)


def _rwkv_chunks(r, v, kk, ld0, ld1, kd0, kd1, a0, a1):
    b, ta, w = r.shape
    nc = ta // RWKV_CHUNK
    spec = pl.BlockSpec((1, RWKV_CHUNK, w), lambda bb, c: (bb, c, 0))
    ospec = pl.BlockSpec((2, 1, 1, RWKV_HEADS, RWKV_HEAD, 2 * RWKV_HEAD), lambda bb, c: (0, bb, c, 0, 0, 0))
    oshape = jax.ShapeDtypeStruct((2, b, nc, RWKV_HEADS, RWKV_HEAD, 2 * RWKV_HEAD), F32)
    return pl.pallas_call(
        _rwkv_chunk_kernel,
        grid=(b, nc),
        in_specs=[spec] * 9,
        out_specs=[ospec, ospec],
        out_shape=[oshape, oshape],
        compiler_params=_cparams(("parallel", "parallel")),
        name="rwkv_chunks",
    )(r, v, kk, ld0, ld1, kd0, kd1, a0, a1)


def _rwkv_state_kernel(pz0_ref, pz1_ref, wu0_ref, wu1_ref, y0_ref, y1_ref, st_ref):
    i = pl.program_id(0)
    hd = RWKV_HEAD
    nb = st_ref.shape[1]

    @pl.when(i == 0)
    def _():
        st_ref[...] = jnp.zeros_like(st_ref)

    probs = []
    for d, (pz_ref, wu_ref) in enumerate(((pz0_ref, wu0_ref), (pz1_ref, wu1_ref))):
        for bb in range(nb):
            for h in range(RWKV_HEADS):
                pz = pz_ref[0, bb, 0, h]
                wu = wu_ref[0, bb, 0, h]
                probs.append(dict(d=d, b=bb, h=h, pt=_split2(pz[:, :hd]), zt=pz[:, hd:],
                                  w2=_split2(wu[:, :hd]), u2=wu[:, hd:], s=_split2(st_ref[d, bb, h])))
    for q in probs:
        q["y"] = _mm3(q["w2"], q["s"], _dot_nt) + q["u2"]
    for q in probs:
        st_ref[q["d"], q["b"], q["h"]] = _mm3(q["s"], q["pt"]) + q["zt"]
    for d, y_ref in enumerate((y0_ref, y1_ref)):
        for bb in range(nb):
            ys = [q["y"] for q in probs if q["d"] == d and q["b"] == bb]
            for hp in range(RWKV_HEADS // 2):
                y_ref[bb, :, hp * LANE:(hp + 1) * LANE] = jnp.concatenate([ys[2 * hp], ys[2 * hp + 1]], axis=1)


def _rwkv_state(pz, wu, ctx):
    _, b, nc, nh, hd, _ = pz.shape
    ncc = ctx // RWKV_CHUNK

    def rev(i):
        return jnp.where(i < ncc, ncc - 1 - i, nc - 1 - (i - ncc))

    blk = (1, b, 1, nh, hd, 2 * hd)
    fwd = lambda i: (0, 0, i, 0, 0, 0)
    bwd = lambda i: (1, 0, rev(i), 0, 0, 0)
    yshape = jax.ShapeDtypeStruct((b, nc * RWKV_CHUNK, nh * hd), F32)
    return pl.pallas_call(
        _rwkv_state_kernel,
        grid=(nc,),
        in_specs=[pl.BlockSpec(blk, fwd), pl.BlockSpec(blk, bwd), pl.BlockSpec(blk, fwd), pl.BlockSpec(blk, bwd)],
        out_specs=[pl.BlockSpec((b, RWKV_CHUNK, nh * hd), lambda i: (0, i, 0)),
                   pl.BlockSpec((b, RWKV_CHUNK, nh * hd), lambda i: (0, rev(i), 0))],
        out_shape=[yshape, yshape],
        scratch_shapes=[pltpu.VMEM((2, b, nh, hd, hd), F32)],
        compiler_params=_cparams(("arbitrary",)),
        name="rwkv_state",
    )(pz, pz, wu, wu)


def _rwkv_out_kernel(y0_ref, y1_ref, bv_ref, g_ref, lg_ref, lb_ref, bd_ref, o_ref):
    hp = dict(precision=HIGHEST)
    bdm = bd_ref[...] * (1.0 / RWKV_HEAD)
    y = y0_ref[0] + y1_ref[0]
    yc = y - _dot(y, bdm, **hp)
    var = _dot(yc * yc, bdm, **hp)
    yn = yc * lax.rsqrt(var + RWKV_GN_EPS) * lg_ref[...] + lb_ref[...]
    o_ref[0] = ((yn + bv_ref[0]) * g_ref[0]).astype(o_ref.dtype)


def _rwkv_out(y0, y1, bv, g, ln_g, ln_b, bd, tm):
    b, ta, w = y0.shape
    row = pl.BlockSpec((1, tm, w), lambda bb, i: (bb, i, 0))
    vec = pl.BlockSpec((1, w), lambda bb, i: (0, 0))
    return pl.pallas_call(
        _rwkv_out_kernel,
        grid=(b, ta // tm),
        in_specs=[row, row, row, row, vec, vec, pl.BlockSpec((w, w), lambda bb, i: (0, 0))],
        out_specs=row,
        out_shape=jax.ShapeDtypeStruct((b, ta, w), BF16),
        compiler_params=_cparams(("parallel", "parallel")),
        name="rwkv_out",
    )(y0, y1, bv, g, ln_g, ln_b, bd)


def _rwkv_mixer(p, ctx, mu, w0, w_lora, a0, a_lora, g_lora, k_k, k_a, r_k, ln_g, ln_b):
    b, ta, _ = p.shape
    w = RWKV_W
    tm = _tile(math.gcd(ta, ctx), 256)
    nt = ta // tm
    edge_lo = p[:, 0::tm, SEG_RW:]
    edge_hi = p[:, tm - 1::tm, SEG_RW:]
    tile_id = jnp.arange(nt)[None, :, None]
    hprev = jnp.where((tile_id == 0) | (tile_id == ctx // tm), 0.0, jnp.roll(edge_hi, 1, axis=1))
    hnext = jnp.where((tile_id == nt - 1) | (tile_id == ctx // tm - 1), 0.0, jnp.roll(edge_lo, -1, axis=1))
    def rows_at(mat, start):
        return jnp.pad(mat.astype(BF16), ((start, LANE - start - mat.shape[0]), (0, 0)))

    wl = jnp.stack([rows_at(w_lora[d], d * DECAY_LORA) for d in range(2)])
    al = jnp.stack([rows_at(a_lora[d], 2 * DECAY_LORA + d * ICLR_LORA) for d in range(2)])
    gl = rows_at(g_lora, 0)
    mu_p = jnp.pad(mu, (0, RW_COLS - RWKV_STREAM)).reshape(1, RW_COLS)
    head_id = jnp.arange(w) // RWKV_HEAD
    bd = (head_id[:, None] == head_id[None, :]).astype(F32)
    r, v, kk, ld0, ld1, kd0, kd1, a0_, a1_, bv, g = _rwkv_prep(
        p, hprev[:, :, None, :], hnext[:, :, None, :], mu_p, w0, a0, wl, al, gl,
        k_k.reshape(1, w), k_a.reshape(1, w), r_k.reshape(1, w), bd, tm)
    pz, wu = _rwkv_chunks(r, v, kk, ld0, ld1, kd0, kd1, a0_, a1_)
    y0, y1 = _rwkv_state(pz, wu, ctx)
    return _rwkv_out(y0, y1, bv, g, ln_g.reshape(1, w), ln_b.reshape(1, w), bd, tm)


def _merge_a_kernel(x_ref, ml_ref, mc_ref, o4_ref, wg_ref, wb_ref, out_ref, xn_ref, acc_ref, *, tm, ctx):
    i = pl.program_id(1)
    j = pl.program_id(2)
    br = pl.program_id(3)

    @pl.when((j == 0) & (br == 0))
    def _():
        _fill_modulated(xn_ref, x_ref, ml_ref[0], mc_ref[0], i * tm, ctx)

    val = _sigmoid(_dot(xn_ref[...], wg_ref[...])) * _dot(o4_ref[0, 0], wb_ref[0])

    @pl.when(br == 0)
    def _():
        acc_ref[...] = val

    @pl.when(br > 0)
    def _():
        acc_ref[...] += val

    @pl.when(br == N_BRANCH - 1)
    def _():
        out_ref[0] = acc_ref[...].astype(out_ref.dtype)


def _gate_window_weights(w_in_l, gate_off, w_branch_l, w_out_l):
    d = w_in_l.shape[0]
    shift = gate_off % LANE
    start = gate_off - shift
    gw = d + 2 * LANE
    tail = gw - d - shift
    wsrc = w_in_l[:, start:].astype(BF16)
    wsrc = jnp.pad(wsrc, ((0, 0), (0, (N_BRANCH - 1) * d + gw - wsrc.shape[1])))
    wg = jnp.concatenate([wsrc[:, br * d:br * d + gw] for br in range(N_BRANCH)], axis=1)
    wb = jnp.pad(w_branch_l.astype(BF16), ((0, 0), (0, 0), (shift, tail)))
    wo = jnp.pad(w_out_l.astype(BF16), ((shift, tail), (0, 0)))
    return wg, wb, wo


def _merge_a(x_all, modl, modc, o4, wg, wb, ctx):
    b, ta, d = x_all.shape
    gw = wb.shape[2]
    tm = _tile(ta, 1088)
    tn = _tile(gw, 768, LANE)
    return pl.pallas_call(
        functools.partial(_merge_a_kernel, tm=tm, ctx=ctx),
        grid=(b, ta // tm, gw // tn, N_BRANCH),
        in_specs=[pl.BlockSpec((1, tm, d), lambda bb, i, j, br: (bb, i, 0)),
                  pl.BlockSpec((1, 6, d), lambda bb, i, j, br: (bb, 0, 0)),
                  pl.BlockSpec((1, 6, d), lambda bb, i, j, br: (0, 0, 0)),
                  pl.BlockSpec((1, 1, tm, BRANCH_W), lambda bb, i, j, br: (br, bb, i, 0)),
                  pl.BlockSpec((d, tn), lambda bb, i, j, br: (0, br * (gw // tn) + j)),
                  pl.BlockSpec((1, BRANCH_W, tn), lambda bb, i, j, br: (br, 0, j))],
        out_specs=pl.BlockSpec((1, tm, tn), lambda bb, i, j, br: (bb, i, j)),
        out_shape=jax.ShapeDtypeStruct((b, ta, gw), BF16),
        scratch_shapes=[pltpu.VMEM((tm, d), BF16), pltpu.VMEM((tm, tn), F32)],
        compiler_params=_cparams(("parallel", "parallel", "arbitrary", "arbitrary"), 56),
        name="merge_gate",
    )(x_all, modl, modc, o4, wg, wb)


def _merge_b_kernel(m_ref, w_ref, x_ref, ml_ref, mc_ref, lg_ref, lb_ref, xo_ref, h2_ref, *, tm, ctx, alpha):
    i = pl.program_id(1)
    isc = _row_is_ctx(tm, i * tm, ctx)
    ml = ml_ref[0]
    mc = mc_ref[0]
    y = _dot(m_ref[0], w_ref[...])
    g1 = jnp.where(isc, mc[2:3], ml[2:3])
    xn = _ln(alpha * x_ref[0] + g1 * y, POST_LN_EPS) * lg_ref[...] + lb_ref[...]
    xo_ref[0] = xn
    h2_ref[0] = _pack_halves(_modulated(xn, ml, mc, 3, 4, isc))


def _merge_b(merged, w_out, x_all, modl, modc, ln_g, ln_b, ctx, alpha):
    b, ta, d = x_all.shape
    gw = w_out.shape[0]
    tm = _tile(ta, 272)
    row = lambda bb, i: (bb, i, 0)
    return pl.pallas_call(
        functools.partial(_merge_b_kernel, tm=tm, ctx=ctx, alpha=alpha),
        grid=(b, ta // tm),
        in_specs=[pl.BlockSpec((1, tm, gw), row),
                  pl.BlockSpec((gw, d), lambda bb, i: (0, 0)),
                  pl.BlockSpec((1, tm, d), row),
                  pl.BlockSpec((1, 6, d), lambda bb, i: (bb, 0, 0)),
                  pl.BlockSpec((1, 6, d), lambda bb, i: (0, 0, 0)),
                  pl.BlockSpec((1, d), lambda bb, i: (0, 0)),
                  pl.BlockSpec((1, d), lambda bb, i: (0, 0))],
        out_specs=[pl.BlockSpec((1, tm, d), row), pl.BlockSpec((1, tm, d // 2), row)],
        out_shape=[jax.ShapeDtypeStruct((b, ta, d), F32), jax.ShapeDtypeStruct((b, ta, d // 2), jnp.uint32)],
        compiler_params=_cparams(("parallel", "parallel")),
        name="merge_out_ln",
    )(merged, w_out, x_all, modl, modc, ln_g.reshape(1, d), ln_b.reshape(1, d))


def _router_kernel(x_ref, ml_ref, mc_ref, rw_ref, rb_ref, idx_ref, wt_ref, rank_ref, cnt_ref, carry_ref,
                   *, tm, ctx):
    i = pl.program_id(1)

    @pl.when((pl.program_id(0) == 0) & (i == 0))
    def _():
        carry_ref[...] = jnp.zeros_like(carry_ref)

    isc = _row_is_ctx(tm, i * tm, ctx)
    h2 = _modulated(x_ref[0], ml_ref[0], mc_ref[0], 3, 4, isc)
    scores = _sigmoid(_dot(h2, rw_ref[...], precision=HIGHEST))
    biased = scores + rb_ref[...]
    lane = lax.broadcasted_iota(jnp.int32, scores.shape, 1)
    grp = lane // GROUP_SIZE
    ninf = -jnp.inf

    def first_argmax(z):
        m = jnp.max(z, -1, keepdims=True)
        idx = jnp.min(jnp.where(z == m, lane, N_EXPERTS), -1, keepdims=True)
        return m, idx

    gscore = jnp.zeros_like(biased)
    for g in range(N_GROUPS):
        zg = jnp.where(grp == g, biased, ninf)
        m1, i1 = first_argmax(zg)
        m2 = jnp.max(jnp.where(lane == i1, ninf, zg), -1, keepdims=True)
        gscore = jnp.where(grp == g, m1 + m2, gscore)
    cand = jnp.where(lane % GROUP_SIZE == 0, gscore, ninf)
    gsel = jnp.zeros(scores.shape, jnp.bool_)
    for _ in range(TOPK_GROUPS):
        _, ig = first_argmax(cand)
        gsel = gsel | (grp == ig // GROUP_SIZE)
        cand = jnp.where(lane == ig, ninf, cand)
    masked = jnp.where(gsel, biased, ninf)
    out_lane = lax.broadcasted_iota(jnp.int32, (tm, LANE), 1)
    idx_out = jnp.zeros((tm, LANE), jnp.int32)
    w_out = jnp.zeros((tm, LANE), F32)
    w_sum = jnp.zeros((tm, 1), F32)
    chosen = []
    picked = jnp.zeros(scores.shape, F32)
    for kq in range(TOP_K):
        _, ie = first_argmax(masked)
        sel = lane == ie
        wk = jnp.sum(jnp.where(sel, scores, 0.0), -1, keepdims=True)
        masked = jnp.where(sel, ninf, masked)
        idx_out = jnp.where(out_lane == kq, ie, idx_out)
        w_out = jnp.where(out_lane == kq, wk, w_out)
        w_sum = w_sum + wk
        chosen.append(sel)
        picked = jnp.where(sel, 1.0, picked)
    idx_ref[0] = idx_out
    wt_ref[0] = w_out / w_sum * ROUTED_SCALE
    r_i = lax.broadcasted_iota(jnp.int32, (tm, tm), 0)
    c_i = lax.broadcasted_iota(jnp.int32, (tm, tm), 1)
    before = _dot(jnp.where(r_i > c_i, 1.0, 0.0).astype(BF16), picked.astype(BF16)) + carry_ref[...]
    rank_out = jnp.zeros((tm, LANE), jnp.int32)
    for kq in range(TOP_K):
        rk = jnp.sum(jnp.where(chosen[kq], before, 0.0), -1, keepdims=True)
        rank_out = jnp.where(out_lane == kq, rk.astype(jnp.int32), rank_out)
    rank_ref[0] = rank_out
    carry_ref[...] += jnp.sum(picked, axis=0, keepdims=True)
    cnt_ref[...] = carry_ref[...].astype(jnp.int32)


def _router(x_all, modl, modc, rw, rb, ctx):
    b, ta, d = x_all.shape
    tm = _tile(ta, 544)
    row = lambda bb, i: (bb, i, 0)
    tok = jax.ShapeDtypeStruct((b, ta, LANE), jnp.int32)
    return pl.pallas_call(
        functools.partial(_router_kernel, tm=tm, ctx=ctx),
        grid=(b, ta // tm),
        in_specs=[pl.BlockSpec((1, tm, d), row),
                  pl.BlockSpec((1, 6, d), lambda bb, i: (bb, 0, 0)),
                  pl.BlockSpec((1, 6, d), lambda bb, i: (0, 0, 0)),
                  pl.BlockSpec((d, N_EXPERTS), lambda bb, i: (0, 0)),
                  pl.BlockSpec((1, N_EXPERTS), lambda bb, i: (0, 0))],
        out_specs=[pl.BlockSpec((1, tm, LANE), row), pl.BlockSpec((1, tm, LANE), row),
                   pl.BlockSpec((1, tm, LANE), row), pl.BlockSpec((1, N_EXPERTS), lambda bb, i: (0, 0))],
        out_shape=[tok, jax.ShapeDtypeStruct((b, ta, LANE), F32), tok,
                   jax.ShapeDtypeStruct((1, N_EXPERTS), jnp.int32)],
        scratch_shapes=[pltpu.VMEM((1, N_EXPERTS), F32)],
        compiler_params=_cparams(("arbitrary", "arbitrary")),
        name="router",
    )(x_all, modl, modc, rw, rb.reshape(1, N_EXPERTS))


def _row_copy(src_ref, src_row, dst_ref, dst_row, sem):
    return pltpu.make_async_copy(src_ref.at[pl.ds(src_row, 1)], dst_ref.at[pl.ds(dst_row, 1)], sem)


def _dispatch_kernel(pos_ref, h2_ref, xg_in_ref, xg_ref, sem, *, tm):
    del xg_in_ref

    def issue(t, carry):
        for kq in range(TOP_K):
            _row_copy(h2_ref, t, xg_ref, pos_ref[0, 0, t * TOP_K + kq], sem).start()
        return carry

    lax.fori_loop(0, tm, issue, 0, unroll=4)
    for _ in range(TOP_K):
        pltpu.make_async_copy(h2_ref, xg_ref.at[pl.ds(0, tm)], sem).wait()


def _dispatch(pos, h2p, n_rows, tm):
    n, half = h2p.shape
    xg0 = jnp.zeros((n_rows, half), jnp.uint32)
    return pl.pallas_call(
        functools.partial(_dispatch_kernel, tm=tm),
        grid=(n // tm,),
        in_specs=[pl.BlockSpec((1, 1, tm * TOP_K), lambda i: (i, 0, 0), memory_space=pltpu.SMEM),
                  pl.BlockSpec((tm, half), lambda i: (i, 0)),
                  pl.BlockSpec(memory_space=pl.ANY)],
        out_specs=pl.BlockSpec(memory_space=pl.ANY),
        out_shape=jax.ShapeDtypeStruct((n_rows, half), jnp.uint32),
        scratch_shapes=[pltpu.SemaphoreType.DMA(())],
        input_output_aliases={2: 0},
        compiler_params=_cparams(("arbitrary",)),
        name="moe_dispatch",
    )(pos, h2p, xg0)


def _swiglu(x_lo, x_hi, wgu, wdn, ff):
    half = x_lo.shape[1]
    h = _dot(x_lo, wgu[:half, :]) + _dot(x_hi, wgu[half:, :])
    g = h[:, :ff]
    return _dot((g * _sigmoid(g) * h[:, ff:]).astype(BF16), wdn)


def _experts_kernel(be_ref, nu_ref, x_ref, wgu_ref, wdn_ref, o_ref, wgu_s, wdn_s, *, ff):
    i = pl.program_id(0)
    changed = (i == 0) | (be_ref[i] != be_ref[jnp.maximum(i - 1, 0)])

    @pl.when(changed)
    def _():
        wgu_s[...] = wgu_ref[0, 0].astype(BF16)
        wdn_s[...] = wdn_ref[0, 0].astype(BF16)

    @pl.when(i < nu_ref[0])
    def _():
        lo, hi = _unpack_halves(x_ref[...])
        o_ref[...] = _pack_halves(_swiglu(lo.astype(BF16), hi.astype(BF16), wgu_s[...], wdn_s[...], ff))

    @pl.when(i >= nu_ref[0])
    def _():
        o_ref[...] = jnp.zeros_like(o_ref)


def _experts(xg, blk_e, n_used, w_gu, w_dn, l, blk):
    n_rows, half = xg.shape
    d = 2 * half
    ff = w_dn.shape[2]
    n_blocks = n_rows // blk
    grid_spec = pltpu.PrefetchScalarGridSpec(
        num_scalar_prefetch=2,
        grid=(n_blocks,),
        in_specs=[pl.BlockSpec((blk, half), lambda i, be, nu: (i, 0)),
                  pl.BlockSpec((1, 1, d, 2 * ff), lambda i, be, nu: (l, be[i], 0, 0)),
                  pl.BlockSpec((1, 1, ff, d), lambda i, be, nu: (l, be[i], 0, 0))],
        out_specs=pl.BlockSpec((blk, half), lambda i, be, nu: (i, 0)),
        scratch_shapes=[pltpu.VMEM((d, 2 * ff), BF16), pltpu.VMEM((ff, d), BF16)],
    )
    return pl.pallas_call(
        functools.partial(_experts_kernel, ff=ff),
        grid_spec=grid_spec,
        out_shape=jax.ShapeDtypeStruct((n_rows, half), jnp.uint32),
        compiler_params=_cparams(("arbitrary",)),
        name="moe_experts",
    )(blk_e, n_used, xg, w_gu, w_dn)


def _moe_plan(idx, rank, counts, blk):
    n_assign = idx.size
    padded = (counts + blk - 1) // blk * blk
    pad_end = jnp.cumsum(padded)
    start_pad = pad_end - padded
    experts = jnp.arange(N_EXPERTS, dtype=jnp.int32)
    start_of = jnp.sum(jnp.where(idx[..., None] == experts, start_pad, 0), axis=-1)
    pos = (start_of + rank).astype(jnp.int32)
    n_blocks = -(-(n_assign + N_EXPERTS * (blk - 1)) // blk)
    blk_start = jnp.arange(n_blocks, dtype=jnp.int32) * blk
    blk_e = jnp.minimum(jnp.sum(blk_start[:, None] >= pad_end[None, :], axis=-1), N_EXPERTS - 1).astype(jnp.int32)
    n_used = (pad_end[-1] // blk).astype(jnp.int32).reshape(1)
    return pos, blk_e, n_used, n_blocks * blk


def _combine_kernel(pos_ref, y_ref, wt_ref, h2_ref, sgu_ref, sdn_ref, x_ref, ml_ref, mc_ref, lg_ref, lb_ref,
                    xo_ref, buf, sem, *, tm, ctx, alpha, ff, t0):
    i = pl.program_id(1) + t0

    def issue(t, carry):
        for kq in range(TOP_K):
            _row_copy(y_ref, pos_ref[0, 0, t * TOP_K + kq], buf.at[kq], t, sem).start()
        return carry

    lax.fori_loop(0, tm, issue, 0, unroll=4)
    lo, hi = _unpack_halves(h2_ref[0])
    f = _swiglu(lo.astype(BF16), hi.astype(BF16), sgu_ref[...], sdn_ref[...], ff)
    for kq in range(TOP_K):
        pltpu.make_async_copy(y_ref.at[pl.ds(0, tm)], buf.at[kq], sem).wait()
    wt = wt_ref[0]
    half = f.shape[1] // 2
    f_lo = f[:, :half]
    f_hi = f[:, half:]
    for kq in range(TOP_K):
        lo, hi = _unpack_halves(buf[kq])
        f_lo = f_lo + wt[:, kq:kq + 1] * lo
        f_hi = f_hi + wt[:, kq:kq + 1] * hi
    f = jnp.concatenate([f_lo, f_hi], axis=1)
    isc = _row_is_ctx(tm, i * tm, ctx)
    g2 = jnp.where(isc, mc_ref[0][5:6], ml_ref[0][5:6])
    xo_ref[0] = _ln(alpha * x_ref[0] + g2 * f, POST_LN_EPS) * lg_ref[...] + lb_ref[...]


def _combine(pos, y_rows, wts, h2p, sh_gu, sh_dn, x_all, modl, modc, ln_g, ln_b, ctx, alpha, tm, skip_ctx):
    b, ta, d = x_all.shape
    ff = sh_dn.shape[0]
    nt = ta // tm
    t0 = ctx // tm if skip_ctx else 0
    row = lambda bb, i: (bb, i + t0, 0)
    return pl.pallas_call(
        functools.partial(_combine_kernel, tm=tm, ctx=ctx, alpha=alpha, ff=ff, t0=t0),
        grid=(b, nt - t0),
        in_specs=[pl.BlockSpec((1, 1, tm * TOP_K), lambda bb, i: (bb * nt + i + t0, 0, 0),
                               memory_space=pltpu.SMEM),
                  pl.BlockSpec(memory_space=pl.ANY),
                  pl.BlockSpec((1, tm, LANE), row),
                  pl.BlockSpec((1, tm, d // 2), row),
                  pl.BlockSpec((d, 2 * ff), lambda bb, i: (0, 0)),
                  pl.BlockSpec((ff, d), lambda bb, i: (0, 0)),
                  pl.BlockSpec((1, tm, d), row),
                  pl.BlockSpec((1, 6, d), lambda bb, i: (bb, 0, 0)),
                  pl.BlockSpec((1, 6, d), lambda bb, i: (0, 0, 0)),
                  pl.BlockSpec((1, d), lambda bb, i: (0, 0)),
                  pl.BlockSpec((1, d), lambda bb, i: (0, 0))],
        out_specs=pl.BlockSpec((1, tm, d), lambda bb, i: (bb, i, 0)),
        out_shape=jax.ShapeDtypeStruct((b, ta - t0 * tm, d), F32),
        scratch_shapes=[pltpu.VMEM((TOP_K, tm, d // 2), jnp.uint32), pltpu.SemaphoreType.DMA(())],
        compiler_params=_cparams(("arbitrary", "arbitrary")),
        name="moe_combine_ln",
    )(pos, y_rows, wts, h2p, sh_gu, sh_dn, x_all, modl, modc, ln_g.reshape(1, d), ln_b.reshape(1, d))


def _rope_tables(n_lat, ctx):
    rows = n_lat // GRID_W
    row = jnp.repeat(jnp.arange(rows, dtype=F32), GRID_W)
    col = jnp.tile(jnp.arange(GRID_W, dtype=F32), rows)
    n_freq = ROPE_DIM // 4
    inv_freq = ROPE_BASE ** (-jnp.arange(n_freq, dtype=F32) / n_freq)
    ang = jnp.concatenate([row[:, None] * inv_freq, col[:, None] * inv_freq], -1)
    cos, sin = jnp.cos(ang), jnp.sin(ang)
    cos = jnp.concatenate([jnp.ones((ctx, ROPE_DIM // 2), F32), cos], 0)
    sin = jnp.concatenate([jnp.zeros((ctx, ROPE_DIM // 2), F32), sin], 0)
    cosf = jnp.tile(jnp.concatenate([cos, cos], -1), (1, LANE // ROPE_DIM))
    sinf = jnp.tile(jnp.concatenate([-sin, sin], -1), (1, LANE // ROPE_DIM))
    return cosf, sinf


def _pack_small_weights(w_in_l):
    d = w_in_l.shape[0]
    offs = np.cumsum([0, MLA_Q_RANK, MLA_KV_RANK + MLA_ROPE, RWKV_STREAM,
                      (SWA_HEADS + 2 * SWA_KV_HEADS) * SWA_HEAD, 3 * DIFF_HEADS * 2 * DIFF_HEAD])
    placed = sorted(zip([SEG_MQ, SEG_MKV, SEG_RW, SEG_SW, SEG_DF], range(5)))
    parts = []
    cur = 0
    for start, s in placed:
        if start > cur:
            parts.append(jnp.zeros((d, start - cur), BF16))
        parts.append(w_in_l[:, int(offs[s]):int(offs[s + 1])].astype(BF16))
        cur = start + int(offs[s + 1] - offs[s])
    parts.append(jnp.zeros((d, SMALL_COLS - cur), BF16))
    return jnp.concatenate(parts, axis=1), int(offs[5])


def kernel(x, c, ctx, c_ctx, w_mod, b_mod, w_in, mla_q_norm, mla_w_qup, mla_kv_norm, mla_w_kvup, rwkv_mu, rwkv_w0, rwkv_w_lora, rwkv_a0, rwkv_a_lora, rwkv_g_lora, rwkv_k_k, rwkv_k_a, rwkv_r_k, rwkv_ln_g, rwkv_ln_b, swa_sink, diff_lambda, diff_subln, w_branch, w_out, ln1_g, ln1_b, router_w, router_bias, exp_w_gu, exp_w_dn, sh_w_gu, sh_w_dn, ln2_g, ln2_b):
    depth = w_mod.shape[0]
    b, n_lat, d = x.shape
    n_ctx = ctx.shape[1]
    ta = n_ctx + n_lat
    alpha = (2 * depth) ** 0.25
    cosf, sinf = _rope_tables(n_lat, n_ctx)
    x_all = jnp.concatenate([ctx, x], axis=1)

    cond = jnp.concatenate([c, c_ctx[None]], axis=0)
    cond = jnp.pad(jax.nn.silu(cond), ((0, SUBLANE - (b + 1) % SUBLANE), (0, 0)))

    for l in range(depth):
        mod = _matmul_bias(cond, w_mod, b_mod, l).reshape(cond.shape[0], 6, d)
        modl, modc = mod[:b], mod[b:b + 1]

        w_small, gate_off = _pack_small_weights(w_in[l])
        p = _inproj(x_all, modl, modc, w_small, n_ctx)

        wq = mla_w_qup[l].reshape(MLA_Q_RANK, MLA_HEADS, MLA_NOPE + MLA_ROPE)
        wq = jnp.pad(wq, ((0, 0), (0, 0), (0, 256 - MLA_NOPE - MLA_ROPE))).reshape(MLA_Q_RANK, 1024)
        q_a, k_a, v_a = _mla_prep(p, mla_q_norm[l].reshape(1, -1), mla_kv_norm[l].reshape(1, -1),
                                  wq.astype(BF16), mla_w_kvup[l].astype(BF16), cosf, sinf)
        out_a = _mla_attn(q_a, k_a, v_a, n_ctx)

        out_b = _rwkv_mixer(p, n_ctx, rwkv_mu[l], rwkv_w0[l],
                            rwkv_w_lora[l], rwkv_a0[l], rwkv_a_lora[l], rwkv_g_lora[l], rwkv_k_k[l],
                            rwkv_k_a[l], rwkv_r_k[l], rwkv_ln_g[l], rwkv_ln_b[l])

        sinkv = jnp.broadcast_to(swa_sink[l][:, None], (SWA_HEADS, LANE))
        out_s = _swa(p, cosf, sinf, sinkv, n_ctx)

        lf = diff_lambda[l]
        lam_init = 0.8 - 0.6 * math.exp(-0.3 * l)
        lam = jnp.exp(jnp.sum(lf[0] * lf[1])) - jnp.exp(jnp.sum(lf[2] * lf[3])) + lam_init
        out_d = _diff_attn(p, cosf, sinf, jnp.full((1, LANE), lam, F32), diff_subln[l].reshape(1, LANE),
                           n_ctx, lam_init)

        o4 = jnp.stack([out_a, out_b, out_s, out_d], axis=0)
        wg, wb, wo = _gate_window_weights(w_in[l], gate_off, w_branch[l], w_out[l])
        merged = _merge_a(x_all, modl, modc, o4, wg, wb, n_ctx)
        x_all, h2p = _merge_b(merged, wo, x_all, modl, modc, ln1_g[l], ln1_b[l], n_ctx, alpha)

        idx, wts, rank, counts = _router(x_all, modl, modc, router_w[l], router_bias[l], n_ctx)
        pos, blk_e, n_used, n_rows = _moe_plan(idx[..., :TOP_K], rank[..., :TOP_K], counts[0], MOE_BLK)
        tmd = _tile(ta, 128)
        pos = pos.reshape(b * ta // tmd, 1, tmd * TOP_K)
        xg = _dispatch(pos, h2p.reshape(b * ta, d // 2), n_rows, tmd)
        y_rows = _experts(xg, blk_e, n_used, exp_w_gu, exp_w_dn, l, MOE_BLK)
        x_all = _combine(pos, y_rows, wts, h2p, sh_w_gu[l].astype(BF16), sh_w_dn[l].astype(BF16), x_all,
                         modl, modc, ln2_g[l], ln2_b[l], n_ctx, alpha, tmd, skip_ctx=(l == depth - 1))
    return x_all
```

```python
import functools
import math

import jax
import jax.numpy as jnp
import numpy as np
from jax import lax
from jax.experimental import pallas as pl
from jax.experimental.pallas import tpu as pltpu

F32 = jnp.float32
BF16 = jnp.bfloat16
HIGHEST = lax.Precision.HIGHEST

GRID_W = 64
ROPE_BASE = 10000.0
ROPE_DIM = 64
ADALN_EPS = 1e-6
POST_LN_EPS = 1e-5
NEG_INF = -1e30

MLA_HEADS = 4
MLA_Q_RANK = 384
MLA_KV_RANK = 256
MLA_NOPE = 128
MLA_ROPE = 64
MLA_V = 128

RWKV_HEADS = 8
RWKV_HEAD = 64
RWKV_W = RWKV_HEADS * RWKV_HEAD
DECAY_LORA = 32
ICLR_LORA = 32
GATE_LORA = 96
RWKV_STREAM = 3 * RWKV_W + 2 * DECAY_LORA + 2 * ICLR_LORA + GATE_LORA
RWKV_GN_EPS = 64e-5
RWKV_CHUNK = 64

SWA_HEADS = 8
SWA_KV_HEADS = 2
SWA_GROUP = SWA_HEADS // SWA_KV_HEADS
SWA_HEAD = 64
WINDOW = 128

DIFF_HEADS = 4
DIFF_HEAD = 64

N_BRANCH = 4
BRANCH_W = 512

N_EXPERTS = 64
TOP_K = 6
N_GROUPS = 8
TOPK_GROUPS = 4
GROUP_SIZE = N_EXPERTS // N_GROUPS
ROUTED_SCALE = 2.5
MOE_BLK = 256

LANE = 128
SUBLANE = 8

SEG_SW = 0
SEG_MQ = 768
SEG_MKV = 1152
SEG_DF = 1536
SEG_RW = 3584
RW_COLS = 1792
SMALL_COLS = SEG_RW + RW_COLS


def _tile(n, target, mult=SUBLANE):
    best = None
    for d in range(mult, min(n, target) + 1, mult):
        if n % d == 0:
            best = d
    assert best is not None, (n, target, mult)
    return best


def _cparams(sem, vmem_mb=48):
    return pltpu.CompilerParams(dimension_semantics=sem, vmem_limit_bytes=vmem_mb * 1024 * 1024)


def _sigmoid(x):
    return 1.0 / (1.0 + jnp.exp(-x))


def _ln(x, eps):
    mu = jnp.mean(x, -1, keepdims=True)
    xc = x - mu
    var = jnp.mean(xc * xc, -1, keepdims=True)
    return xc * lax.rsqrt(var + eps)


def _rms(x, eps):
    return x * lax.rsqrt(jnp.mean(x * x, -1, keepdims=True) + eps)


def _row_is_ctx(n_rows, row0, ctx):
    rows = row0 + lax.broadcasted_iota(jnp.int32, (n_rows, 1), 0)
    return rows < ctx


def _modulated(x, ml, mc, sh_i, sc_i, isc):
    sh = jnp.where(isc, mc[sh_i:sh_i + 1], ml[sh_i:sh_i + 1])
    sc = jnp.where(isc, mc[sc_i:sc_i + 1], ml[sc_i:sc_i + 1])
    return _ln(x, ADALN_EPS) * (1.0 + sc) + sh


def _fill_modulated(xn_ref, x_ref, ml, mc, row0, ctx):
    tm = xn_ref.shape[0]
    ch = _tile(tm, 272)
    for r in range(0, tm, ch):
        isc = _row_is_ctx(ch, row0 + r, ctx)
        xn_ref[r:r + ch, :] = _modulated(x_ref[0, r:r + ch, :], ml, mc, 0, 1, isc).astype(BF16)


def _swap32(x):
    lane = lax.broadcasted_iota(jnp.int32, x.shape, 1)
    up = pltpu.roll(x, LANE - 32, 1)
    dn = pltpu.roll(x, 32, 1)
    return jnp.where((lane % 64) < 32, up, dn)


def _rope(x, cosf, sinf):
    return x * cosf + _swap32(x) * sinf


def _pack_halves(x):
    n = x.shape[1] // 2
    lo = pltpu.bitcast(x[:, :n].astype(BF16).astype(F32), jnp.uint32)
    hi = pltpu.bitcast(x[:, n:].astype(BF16).astype(F32), jnp.uint32)
    return (hi & jnp.uint32(0xFFFF0000)) | (lo >> 16)


def _unpack_halves(u):
    lo = pltpu.bitcast(u << 16, F32)
    hi = pltpu.bitcast(u & jnp.uint32(0xFFFF0000), F32)
    return lo, hi


def _dot(a, b, **kw):
    return jnp.dot(a, b, preferred_element_type=F32, **kw)


def _dot_nt(a, b, **kw):
    return lax.dot_general(a, b, (((1,), (1,)), ((), ())), preferred_element_type=F32, **kw)


def _dot_tn(a, b, **kw):
    return lax.dot_general(a, b, (((0,), (0,)), ((), ())), preferred_element_type=F32, **kw)


def _mm_kernel(x_ref, w_ref, b_ref, o_ref):
    o_ref[...] = _dot(x_ref[...], w_ref[0], precision=HIGHEST) + b_ref[0]


def _matmul_bias(x, w, b, l):
    m, k = x.shape
    n = w.shape[2]
    tn = _tile(n, 1024, LANE)
    return pl.pallas_call(
        _mm_kernel,
        grid=(n // tn,),
        in_specs=[pl.BlockSpec((m, k), lambda j: (0, 0)),
                  pl.BlockSpec((1, k, tn), lambda j: (l, 0, j)),
                  pl.BlockSpec((1, 1, tn), lambda j: (l, 0, j))],
        out_specs=pl.BlockSpec((m, tn), lambda j: (0, j)),
        out_shape=jax.ShapeDtypeStruct((m, n), F32),
        compiler_params=_cparams(("arbitrary",)),
        name="mod_matmul",
    )(x, w, b.reshape(b.shape[0], 1, n))


def _inproj_kernel(x_ref, ml_ref, mc_ref, w_ref, o_ref, xn_ref, *, tm, ctx):
    i = pl.program_id(1)
    j = pl.program_id(2)

    @pl.when(j == 0)
    def _():
        _fill_modulated(xn_ref, x_ref, ml_ref[0], mc_ref[0], i * tm, ctx)

    o_ref[0] = _dot(xn_ref[...], w_ref[...])


def _inproj(x_all, modl, modc, w_small, ctx):
    b, ta, d = x_all.shape
    n = w_small.shape[1]
    tm = _tile(ta, 1088)
    tn = _tile(n, 1024, LANE)
    return pl.pallas_call(
        functools.partial(_inproj_kernel, tm=tm, ctx=ctx),
        grid=(b, ta // tm, n // tn),
        in_specs=[pl.BlockSpec((1, tm, d), lambda bb, i, j: (bb, i, 0)),
                  pl.BlockSpec((1, 6, d), lambda bb, i, j: (bb, 0, 0)),
                  pl.BlockSpec((1, 6, d), lambda bb, i, j: (0, 0, 0)),
                  pl.BlockSpec((d, tn), lambda bb, i, j: (0, j))],
        out_specs=pl.BlockSpec((1, tm, tn), lambda bb, i, j: (bb, i, j)),
        out_shape=jax.ShapeDtypeStruct((b, ta, n), F32),
        scratch_shapes=[pltpu.VMEM((tm, d), BF16)],
        compiler_params=_cparams(("parallel", "parallel", "arbitrary"), 56),
        name="inproj",
    )(x_all, modl, modc, w_small)


def _mla_prep_kernel(mq_ref, mkv_ref, qg_ref, kg_ref, wq_ref, wkv_ref, cos_ref, sin_ref,
                     q_ref, k_ref, v_ref):
    mq = mq_ref[0]
    mkv = mkv_ref[0]
    cosv = cos_ref[...]
    sinv = sin_ref[...]
    qn = _rms(mq, 1e-6) * qg_ref[...]
    q = _dot(qn.astype(BF16), wq_ref[...])
    kvn = _rms(mkv[:, :MLA_KV_RANK], 1e-6) * kg_ref[...]
    kv = _dot(kvn.astype(BF16), wkv_ref[...])
    kr = _rope(mkv[:, MLA_KV_RANK:MLA_KV_RANK + LANE], cosv, sinv).astype(BF16)
    scale = (MLA_NOPE + MLA_ROPE) ** -0.5 * LOG2E
    for h in range(MLA_HEADS):
        c0 = h * 256
        q_ref[0, :, c0:c0 + 128] = (q[:, c0:c0 + 128] * scale).astype(BF16)
        q_ref[0, :, c0 + 128:c0 + 256] = (_rope(q[:, c0 + 128:c0 + 256], cosv, sinv) * scale).astype(BF16)
        k_ref[0, :, c0:c0 + 128] = kv[:, c0:c0 + 128].astype(BF16)
        k_ref[0, :, c0 + 128:c0 + 256] = kr
        v_ref[0, :, h * 128:(h + 1) * 128] = kv[:, c0 + 128:c0 + 256].astype(BF16)


def _mla_prep(p, qg, kg, wq, wkv, cosf, sinf):
    b, ta, _ = p.shape
    tm = _tile(ta, 544)
    return pl.pallas_call(
        _mla_prep_kernel,
        grid=(b, ta // tm),
        in_specs=[pl.BlockSpec((1, tm, 384), lambda bb, i: (bb, i, SEG_MQ // 384)),
                  pl.BlockSpec((1, tm, 384), lambda bb, i: (bb, i, SEG_MKV // 384)),
                  pl.BlockSpec((1, MLA_Q_RANK), lambda bb, i: (0, 0)),
                  pl.BlockSpec((1, MLA_KV_RANK), lambda bb, i: (0, 0)),
                  pl.BlockSpec((MLA_Q_RANK, 1024), lambda bb, i: (0, 0)),
                  pl.BlockSpec((MLA_KV_RANK, 1024), lambda bb, i: (0, 0)),
                  pl.BlockSpec((tm, LANE), lambda bb, i: (i, 0)),
                  pl.BlockSpec((tm, LANE), lambda bb, i: (i, 0))],
        out_specs=[pl.BlockSpec((1, tm, 1024), lambda bb, i: (bb, i, 0)),
                   pl.BlockSpec((1, tm, 1024), lambda bb, i: (bb, i, 0)),
                   pl.BlockSpec((1, tm, 512), lambda bb, i: (bb, i, 0))],
        out_shape=[jax.ShapeDtypeStruct((b, ta, 1024), BF16),
                   jax.ShapeDtypeStruct((b, ta, 1024), BF16),
                   jax.ShapeDtypeStruct((b, ta, 512), BF16)],
        compiler_params=_cparams(("parallel", "parallel")),
        name="mla_prep",
    )(p, p, qg, kg, wq, wkv, cosf, sinf)


LOG2E = 1.4426950408889634


def _softmax_pv(s, v):
    m = jnp.max(s, -1, keepdims=True)
    p = jnp.exp2(s - m)
    l = jnp.sum(p, -1, keepdims=True)
    return _dot(p.astype(BF16), v) / l


def _mla_attn_kernel(q_ref, k_ref, v_ref, o_ref, *, n_ctx_tiles, ctx):
    qi = pl.program_id(2)

    def run(nk):
        s = _dot_nt(q_ref[0], k_ref[0, :nk, :])
        o_ref[0] = _softmax_pv(s, v_ref[0, :nk, :]).astype(o_ref.dtype)

    @pl.when(qi < n_ctx_tiles)
    def _():
        run(ctx)

    @pl.when(qi >= n_ctx_tiles)
    def _():
        run(k_ref.shape[1])


def _mla_attn(q, k, v, ctx):
    b, ta, _ = q.shape
    tq = _tile(math.gcd(ta, ctx), 256)
    return pl.pallas_call(
        functools.partial(_mla_attn_kernel, n_ctx_tiles=ctx // tq, ctx=ctx),
        grid=(b, MLA_HEADS, ta // tq),
        in_specs=[pl.BlockSpec((1, tq, 256), lambda bb, h, i: (bb, i, h)),
                  pl.BlockSpec((1, ta, 256), lambda bb, h, i: (bb, 0, h)),
                  pl.BlockSpec((1, ta, 128), lambda bb, h, i: (bb, 0, h))],
        out_specs=pl.BlockSpec((1, tq, 128), lambda bb, h, i: (bb, i, h)),
        out_shape=jax.ShapeDtypeStruct((b, ta, 512), BF16),
        compiler_params=_cparams(("parallel", "parallel", "arbitrary")),
        name="mla_attn",
    )(q, k, v)


def _diff_attn_kernel(q_ref, k_ref, v_ref, cq_ref, sq_ref, ck_ref, sk_ref, lam_ref, sub_ref,
                      o_ref, ks_ref, vs_ref, *, n_ctx_tiles, ctx, post_scale):
    qi = pl.program_id(2)

    @pl.when(qi == 0)
    def _():
        ks_ref[...] = _rope(k_ref[0], ck_ref[...], sk_ref[...]).astype(BF16)
        vs_ref[...] = v_ref[0].astype(BF16)

    def run(nk):
        q = _rope(q_ref[0], cq_ref[...], sq_ref[...]) * (DIFF_HEAD ** -0.5 * LOG2E)
        lane = lax.broadcasted_iota(jnp.int32, q.shape, 1)
        q1 = jnp.where(lane < 64, q, 0.0).astype(BF16)
        q2 = jnp.where(lane >= 64, q, 0.0).astype(BF16)
        k = ks_ref[:nk, :]
        v = vs_ref[:nk, :]
        o = _softmax_pv(_dot_nt(q1, k), v) - lam_ref[...] * _softmax_pv(_dot_nt(q2, k), v)
        o = _rms(o, 1e-5) * sub_ref[...] * post_scale
        o_ref[0] = o.astype(o_ref.dtype)

    @pl.when(qi < n_ctx_tiles)
    def _():
        run(ctx)

    @pl.when(qi >= n_ctx_tiles)
    def _():
        run(ks_ref.shape[0])


def _diff_attn(p, cosf, sinf, lamv, subln, ctx, lam_init):
    b, ta, _ = p.shape
    tq = _tile(math.gcd(ta, ctx), 256)
    cb = SEG_DF // LANE
    return pl.pallas_call(
        functools.partial(_diff_attn_kernel, n_ctx_tiles=ctx // tq, ctx=ctx, post_scale=1.0 - lam_init),
        grid=(b, DIFF_HEADS, ta // tq),
        in_specs=[pl.BlockSpec((1, tq, LANE), lambda bb, h, i: (bb, i, cb + h)),
                  pl.BlockSpec((1, ta, LANE), lambda bb, h, i: (bb, 0, cb + DIFF_HEADS + h)),
                  pl.BlockSpec((1, ta, LANE), lambda bb, h, i: (bb, 0, cb + 2 * DIFF_HEADS + h)),
                  pl.BlockSpec((tq, LANE), lambda bb, h, i: (i, 0)),
                  pl.BlockSpec((tq, LANE), lambda bb, h, i: (i, 0)),
                  pl.BlockSpec((ta, LANE), lambda bb, h, i: (0, 0)),
                  pl.BlockSpec((ta, LANE), lambda bb, h, i: (0, 0)),
                  pl.BlockSpec((1, LANE), lambda bb, h, i: (0, 0)),
                  pl.BlockSpec((1, LANE), lambda bb, h, i: (0, 0))],
        out_specs=pl.BlockSpec((1, tq, LANE), lambda bb, h, i: (bb, i, h)),
        out_shape=jax.ShapeDtypeStruct((b, ta, 512), BF16),
        scratch_shapes=[pltpu.VMEM((ta, LANE), BF16), pltpu.VMEM((ta, LANE), BF16)],
        compiler_params=_cparams(("parallel", "parallel", "arbitrary")),
        name="diff_attn",
    )(p, p, p, cosf, sinf, cosf, sinf, lamv, subln)


def _swa_kernel(q_ref, k_ref, v_ref, cq_ref, sq_ref, ck_ref, sk_ref, sink_ref,
                o_ref, ks_ref, vs_ref, *, ctx, nsb):
    t = pl.program_id(1)
    n_ctx_steps = ctx // (WINDOW * nsb)
    ta = ks_ref.shape[0]
    span = 3 * WINDOW

    @pl.when(t == 0)
    def _():
        ks_ref[...] = _rope(k_ref[0], ck_ref[...], sk_ref[...]).astype(BF16)
        vs_ref[...] = v_ref[0].astype(BF16)

    lane = lax.broadcasted_iota(jnp.int32, (WINDOW, LANE), 1)

    def run(local, sb):
        rows = slice(sb * WINDOW, (sb + 1) * WINDOW)
        cq = cq_ref[rows, :]
        sq = sq_ref[rows, :]
        qblk = [_rope(q_ref[0, rows, hb * LANE:(hb + 1) * LANE], cq, sq) * (SWA_HEAD ** -0.5)
                for hb in range(SWA_HEADS // 2)]

        def head_q(hk, g):
            h = hk * SWA_GROUP + g
            blk = qblk[h // 2]
            if h % 2 != hk:
                blk = pltpu.roll(blk, 64, 1)
            keep = (lane >= 64) if hk == 1 else (lane < 64)
            return jnp.where(keep, blk, 0.0)

        kc = ks_ref[:ctx, :]
        vc = vs_ref[:ctx, :]
        if local:
            n = (t - n_ctx_steps) * nsb + sb
            ws = jnp.clip(ctx + (n - 1) * WINDOW, ctx, ta - span)
            ws = pl.multiple_of(ws, WINDOW)
            kl = ks_ref[pl.ds(ws, span), :]
            vl = vs_ref[pl.ds(ws, span), :]
            kpos = (ws - ctx) + lax.broadcasted_iota(jnp.int32, (1, span), 1)
            qpos = n * WINDOW + lax.broadcasted_iota(jnp.int32, (SWA_GROUP * WINDOW, 1), 0) % WINDOW
            valid = jnp.abs(kpos - qpos) <= WINDOW
        outs = {}
        for hk in range(SWA_KV_HEADS):
            qst = jnp.concatenate([head_q(hk, g) for g in range(SWA_GROUP)], axis=0).astype(BF16)
            sink = jnp.concatenate(
                [jnp.broadcast_to(sink_ref[hk * SWA_GROUP + g:hk * SWA_GROUP + g + 1, 0:1], (WINDOW, 1))
                 for g in range(SWA_GROUP)], axis=0)
            s_c = _dot_nt(qst, kc)
            m = jnp.maximum(jnp.max(s_c, -1, keepdims=True), sink)
            if local:
                s_l = jnp.where(valid, _dot_nt(qst, kl), NEG_INF)
                m = jnp.maximum(m, jnp.max(s_l, -1, keepdims=True))
            p_c = jnp.exp(s_c - m)
            l = jnp.sum(p_c, -1, keepdims=True) + jnp.exp(sink - m)
            acc = _dot(p_c.astype(BF16), vc)
            if local:
                p_l = jnp.exp(s_l - m)
                l = l + jnp.sum(p_l, -1, keepdims=True)
                acc = acc + _dot(p_l.astype(BF16), vl)
            o = acc / l
            for g in range(SWA_GROUP):
                res = o[g * WINDOW:(g + 1) * WINDOW, :]
                if g % 2 != hk:
                    res = pltpu.roll(res, 64, 1)
                outs[hk * SWA_GROUP + g] = res
        for hb in range(SWA_HEADS // 2):
            blk = jnp.where(lane < 64, outs[2 * hb], outs[2 * hb + 1])
            o_ref[0, rows, hb * LANE:(hb + 1) * LANE] = blk.astype(o_ref.dtype)

    @pl.when(t < n_ctx_steps)
    def _():
        for sb in range(nsb):
            run(False, sb)

    @pl.when(t >= n_ctx_steps)
    def _():
        for sb in range(nsb):
            run(True, sb)


def _swa(p, cosf, sinf, sinkv, ctx):
    b, ta, _ = p.shape
    qb = SEG_SW // 512
    kb = (SEG_SW + 512) // LANE
    nsb = 2 if (ctx % (2 * WINDOW) == 0 and ta % (2 * WINDOW) == 0) else 1
    tq = nsb * WINDOW
    return pl.pallas_call(
        functools.partial(_swa_kernel, ctx=ctx, nsb=nsb),
        grid=(b, ta // tq),
        in_specs=[pl.BlockSpec((1, tq, 512), lambda bb, t: (bb, t, qb)),
                  pl.BlockSpec((1, ta, LANE), lambda bb, t: (bb, 0, kb)),
                  pl.BlockSpec((1, ta, LANE), lambda bb, t: (bb, 0, kb + 1)),
                  pl.BlockSpec((tq, LANE), lambda bb, t: (t, 0)),
                  pl.BlockSpec((tq, LANE), lambda bb, t: (t, 0)),
                  pl.BlockSpec((ta, LANE), lambda bb, t: (0, 0)),
                  pl.BlockSpec((ta, LANE), lambda bb, t: (0, 0)),
                  pl.BlockSpec((SWA_HEADS, LANE), lambda bb, t: (0, 0))],
        out_specs=pl.BlockSpec((1, tq, 512), lambda bb, t: (bb, t, 0)),
        out_shape=jax.ShapeDtypeStruct((b, ta, 512), BF16),
        scratch_shapes=[pltpu.VMEM((ta, LANE), BF16), pltpu.VMEM((ta, LANE), BF16)],
        compiler_params=_cparams(("parallel", "arbitrary")),
        name="swa_attn",
    )(p, p, p, cosf, sinf, cosf, sinf, sinkv)


def _softplus(x):
    return jnp.maximum(x, 0.0) + jnp.log(1.0 + jnp.exp(-jnp.abs(x)))


def _rwkv_prep_kernel(z_ref, hp_ref, hn_ref, mu_ref, w0_ref, a0_ref, wl_ref, al_ref, gl_ref, kk_ref, ka_ref,
                      rk_ref, bd_ref, r_o, v_o, kk_o, ld0_o, ld1_o, kd0_o, kd1_o, a0_o, a1_o, bv_o, g_o, zs_ref):
    tm = z_ref.shape[1]
    w = RWKV_W
    z = z_ref[0]
    zs_ref[SUBLANE:SUBLANE + tm, :] = z
    zs_ref[SUBLANE - 1:SUBLANE, :] = hp_ref[0, 0]
    zs_ref[SUBLANE + tm:SUBLANE + tm + 1, :] = hn_ref[0, 0]
    shifted = 0.5 * (zs_ref[SUBLANE - 1:SUBLANE - 1 + tm, :] + zs_ref[SUBLANE + 1:SUBLANE + 1 + tm, :])
    seg = z + (shifted - z) * mu_ref[...]
    r = seg[:, 0:w]
    k = seg[:, w:2 * w]
    v = seg[:, 2 * w:3 * w]
    lo = seg[:, 3 * w:3 * w + LANE]
    gl = seg[:, 3 * w + LANE:3 * w + 2 * LANE]
    hp = dict(precision=HIGHEST)
    bd = bd_ref[...]
    kk = k * kk_ref[...]
    kk = kk / jnp.maximum(jnp.sqrt(_dot(kk * kk, bd, **hp)), 1e-12)
    th = jnp.tanh(lo).astype(BF16)
    lo_b = lo.astype(BF16)
    ksum = jnp.zeros_like(k)
    for d, (ld_o, kd_o, a_o) in enumerate(((ld0_o, kd0_o, a0_o), (ld1_o, kd1_o, a1_o))):
        wd = w0_ref[d:d + 1, :] + _dot(th, wl_ref[d])
        wd = -_softplus(-wd) - 0.5
        a = _sigmoid(a0_ref[d:d + 1, :] + _dot(lo_b, al_ref[d]))
        kd = k * (1.0 + (a - 1.0) * ka_ref[...])
        ld_o[0] = -jnp.exp(wd)
        kd_o[0] = kd
        a_o[0] = a
        ksum = ksum + kd
    r_o[0] = r
    v_o[0] = v
    kk_o[0] = kk
    bv_o[0] = _dot(r * ksum * rk_ref[...], bd, **hp) * v
    g_o[0] = _dot(_sigmoid(gl).astype(BF16), gl_ref[...])


def _rwkv_prep(p, hprev, hnext, mu, w0, a0, wl, al, gl, k_k, k_a, r_k, bd, tm):
    b, ta, _ = p.shape
    w = RWKV_W
    row = lambda bb, i: (bb, i, 0)
    vec = lambda n: pl.BlockSpec((1, n), lambda bb, i: (0, 0))
    out = jax.ShapeDtypeStruct((b, ta, w), F32)
    return pl.pallas_call(
        _rwkv_prep_kernel,
        grid=(b, ta // tm),
        in_specs=[pl.BlockSpec((1, tm, RW_COLS), lambda bb, i: (bb, i, SEG_RW // RW_COLS)),
                  pl.BlockSpec((1, 1, 1, RW_COLS), lambda bb, i: (bb, i, 0, 0)),
                  pl.BlockSpec((1, 1, 1, RW_COLS), lambda bb, i: (bb, i, 0, 0)),
                  vec(RW_COLS),
                  pl.BlockSpec((2, w), lambda bb, i: (0, 0)),
                  pl.BlockSpec((2, w), lambda bb, i: (0, 0)),
                  pl.BlockSpec((2, LANE, w), lambda bb, i: (0, 0, 0)),
                  pl.BlockSpec((2, LANE, w), lambda bb, i: (0, 0, 0)),
                  pl.BlockSpec((LANE, w), lambda bb, i: (0, 0)),
                  vec(w), vec(w), vec(w),
                  pl.BlockSpec((w, w), lambda bb, i: (0, 0))],
        out_specs=[pl.BlockSpec((1, tm, w), row)] * 11,
        out_shape=[out] * 11,
        scratch_shapes=[pltpu.VMEM((tm + 2 * SUBLANE, RW_COLS), F32)],
        compiler_params=_cparams(("parallel", "parallel")),
        name="rwkv_prep",
    )(p, hprev, hnext, mu, w0, a0, wl, al, gl, k_k, k_a, r_k, bd)


def _split3(x):
    hi = x.astype(BF16)
    r1 = x - hi.astype(F32)
    mid = r1.astype(BF16)
    return hi, mid, (r1 - mid.astype(F32)).astype(BF16)


def _split2(x):
    hi = x.astype(BF16)
    return hi, (x - hi.astype(F32)).astype(BF16)


def _mm3(a, b, dot=_dot):
    return dot(a[0], b[0]) + dot(a[0], b[1]) + dot(a[1], b[0])


def _rwkv_chunk_kernel(r_ref, v_ref, kk_ref, ld0_ref, ld1_ref, kd0_ref, kd1_ref, a0_ref, a1_ref, pz_ref, wu_ref):
    C = RWKV_CHUNK
    hd = RWKV_HEAD
    row = lax.broadcasted_iota(jnp.int32, (C, C), 0)
    col = lax.broadcasted_iota(jnp.int32, (C, C), 1)
    eye = (row == col).astype(F32)
    probs = []
    for d, (ld_ref, kd_ref, a_ref) in enumerate(((ld0_ref, kd0_ref, a0_ref), (ld1_ref, kd1_ref, a1_ref))):
        incl = (row >= col) if d == 0 else (row <= col)
        strict = (row > col) if d == 0 else (row < col)
        last = C - 1 if d == 0 else 0
        for h in range(RWKV_HEADS):
            sl = slice(h * hd, (h + 1) * hd)
            probs.append(dict(d=d, h=h, incl=incl, strict=strict, last=last,
                              ones=jnp.where(incl, 1.0, 0.0).astype(BF16),
                              r=r_ref[0, :, sl], v=v_ref[0, :, sl], kk=kk_ref[0, :, sl],
                              ld=ld_ref[0, :, sl], k=kd_ref[0, :, sl], a=a_ref[0, :, sl]))
    bf = lambda z: z.astype(BF16)
    for q in probs:
        q["cs"] = sum(_dot(q["ones"], part) for part in _split3(q["ld"]))
    for q in probs:
        cs = q["cs"]
        e_cs = jnp.exp(cs)
        e_inv = jnp.exp(-cs)
        q["g_end"] = e_cs[q["last"]:q["last"] + 1, :]
        q["kkd"] = q["kk"] * jnp.exp(cs - q["ld"])
        q["rd"] = q["r"] * e_cs
        q["bi"] = bf(q["kk"] * q["a"] * e_inv)
        q["ki"] = bf(q["k"] * e_inv)
        q["lhs"] = bf(jnp.concatenate([q["kkd"], q["rd"]], axis=0))
    for q in probs:
        gb = _dot_nt(q["lhs"], q["bi"])
        gk = _dot_nt(q["lhs"], q["ki"])
        q["pw"] = -jnp.where(q["strict"], gb[:C], 0.0)
        q["q_b"] = bf(jnp.where(q["incl"], gb[C:], 0.0))
        q["akqk"] = bf(jnp.concatenate([jnp.where(q["strict"], gk[:C], 0.0),
                                        jnp.where(q["incl"], gk[C:], 0.0)], axis=0))
        q["tinv"] = eye + q["pw"]
    for _ in range(5):
        for q in probs:
            pw = bf(q["pw"])
            q["pw"] = _dot(pw, pw)
        for q in probs:
            q["tinv"] = q["tinv"] + _dot(bf(q["tinv"]), bf(q["pw"]))
    for q in probs:
        q["vs"] = bf(q["v"])
        q["av"] = _dot(q["akqk"], q["vs"])
    for q in probs:
        tinv = bf(q["tinv"])
        q["w1"] = _dot(tinv, bf(q["kkd"]))
        q["u1"] = _dot(tinv, bf(q["av"][:C]))
    for q in probs:
        w1 = bf(q["w1"])
        u1 = bf(q["u1"])
        w2 = q["rd"] - _dot(q["q_b"], w1)
        u2 = q["av"][C:] - _dot(q["q_b"], u1)
        pt = (eye - _dot_tn(w1, q["bi"])) * q["g_end"]
        zt = (_dot_tn(q["vs"], q["ki"]) - _dot_tn(u1, q["bi"])) * q["g_end"]
        pz_ref[q["d"], 0, 0, q["h"]] = jnp.concatenate([pt, zt], axis=1)
        wu_ref[q["d"], 0, 0, q["h"]] = jnp.concatenate([w2, u2], axis=1)


def _rwkv_chunks(r, v, kk, ld0, ld1, kd0, kd1, a0, a1):
    b, ta, w = r.shape
    nc = ta // RWKV_CHUNK
    spec = pl.BlockSpec((1, RWKV_CHUNK, w), lambda bb, c: (bb, c, 0))
    ospec = pl.BlockSpec((2, 1, 1, RWKV_HEADS, RWKV_HEAD, 2 * RWKV_HEAD), lambda bb, c: (0, bb, c, 0, 0, 0))
    oshape = jax.ShapeDtypeStruct((2, b, nc, RWKV_HEADS, RWKV_HEAD, 2 * RWKV_HEAD), F32)
    return pl.pallas_call(
        _rwkv_chunk_kernel,
        grid=(b, nc),
        in_specs=[spec] * 9,
        out_specs=[ospec, ospec],
        out_shape=[oshape, oshape],
        compiler_params=_cparams(("parallel", "parallel")),
        name="rwkv_chunks",
    )(r, v, kk, ld0, ld1, kd0, kd1, a0, a1)


def _rwkv_state_kernel(pz0_ref, pz1_ref, wu0_ref, wu1_ref, y0_ref, y1_ref, st_ref):
    i = pl.program_id(0)
    hd = RWKV_HEAD
    nb = st_ref.shape[1]

    @pl.when(i == 0)
    def _():
        st_ref[...] = jnp.zeros_like(st_ref)

    probs = []
    for d, (pz_ref, wu_ref) in enumerate(((pz0_ref, wu0_ref), (pz1_ref, wu1_ref))):
        for bb in range(nb):
            for h in range(RWKV_HEADS):
                pz = pz_ref[0, bb, 0, h]
                wu = wu_ref[0, bb, 0, h]
                probs.append(dict(d=d, b=bb, h=h, pt=_split2(pz[:, :hd]), zt=pz[:, hd:],
                                  w2=_split2(wu[:, :hd]), u2=wu[:, hd:], s=_split2(st_ref[d, bb, h])))
    for q in probs:
        q["y"] = _mm3(q["w2"], q["s"], _dot_nt) + q["u2"]
    for q in probs:
        st_ref[q["d"], q["b"], q["h"]] = _mm3(q["s"], q["pt"]) + q["zt"]
    for d, y_ref in enumerate((y0_ref, y1_ref)):
        for bb in range(nb):
            ys = [q["y"] for q in probs if q["d"] == d and q["b"] == bb]
            for hp in range(RWKV_HEADS // 2):
                y_ref[bb, :, hp * LANE:(hp + 1) * LANE] = jnp.concatenate([ys[2 * hp], ys[2 * hp + 1]], axis=1)


def _rwkv_state(pz, wu, ctx):
    _, b, nc, nh, hd, _ = pz.shape
    ncc = ctx // RWKV_CHUNK

    def rev(i):
        return jnp.where(i < ncc, ncc - 1 - i, nc - 1 - (i - ncc))

    blk = (1, b, 1, nh, hd, 2 * hd)
    fwd = lambda i: (0, 0, i, 0, 0, 0)
    bwd = lambda i: (1, 0, rev(i), 0, 0, 0)
    yshape = jax.ShapeDtypeStruct((b, nc * RWKV_CHUNK, nh * hd), F32)
    return pl.pallas_call(
        _rwkv_state_kernel,
        grid=(nc,),
        in_specs=[pl.BlockSpec(blk, fwd), pl.BlockSpec(blk, bwd), pl.BlockSpec(blk, fwd), pl.BlockSpec(blk, bwd)],
        out_specs=[pl.BlockSpec((b, RWKV_CHUNK, nh * hd), lambda i: (0, i, 0)),
                   pl.BlockSpec((b, RWKV_CHUNK, nh * hd), lambda i: (0, rev(i), 0))],
        out_shape=[yshape, yshape],
        scratch_shapes=[pltpu.VMEM((2, b, nh, hd, hd), F32)],
        compiler_params=_cparams(("arbitrary",)),
        name="rwkv_state",
    )(pz, pz, wu, wu)


def _rwkv_out_kernel(y0_ref, y1_ref, bv_ref, g_ref, lg_ref, lb_ref, bd_ref, o_ref):
    hp = dict(precision=HIGHEST)
    bdm = bd_ref[...] * (1.0 / RWKV_HEAD)
    y = y0_ref[0] + y1_ref[0]
    yc = y - _dot(y, bdm, **hp)
    var = _dot(yc * yc, bdm, **hp)
    yn = yc * lax.rsqrt(var + RWKV_GN_EPS) * lg_ref[...] + lb_ref[...]
    o_ref[0] = ((yn + bv_ref[0]) * g_ref[0]).astype(o_ref.dtype)


def _rwkv_out(y0, y1, bv, g, ln_g, ln_b, bd, tm):
    b, ta, w = y0.shape
    row = pl.BlockSpec((1, tm, w), lambda bb, i: (bb, i, 0))
    vec = pl.BlockSpec((1, w), lambda bb, i: (0, 0))
    return pl.pallas_call(
        _rwkv_out_kernel,
        grid=(b, ta // tm),
        in_specs=[row, row, row, row, vec, vec, pl.BlockSpec((w, w), lambda bb, i: (0, 0))],
        out_specs=row,
        out_shape=jax.ShapeDtypeStruct((b, ta, w), BF16),
        compiler_params=_cparams(("parallel", "parallel")),
        name="rwkv_out",
    )(y0, y1, bv, g, ln_g, ln_b, bd)


def _rwkv_mixer(p, ctx, mu, w0, w_lora, a0, a_lora, g_lora, k_k, k_a, r_k, ln_g, ln_b):
    b, ta, _ = p.shape
    w = RWKV_W
    tm = _tile(math.gcd(ta, ctx), 256)
    nt = ta // tm
    edge_lo = p[:, 0::tm, SEG_RW:]
    edge_hi = p[:, tm - 1::tm, SEG_RW:]
    tile_id = jnp.arange(nt)[None, :, None]
    hprev = jnp.where((tile_id == 0) | (tile_id == ctx // tm), 0.0, jnp.roll(edge_hi, 1, axis=1))
    hnext = jnp.where((tile_id == nt - 1) | (tile_id == ctx // tm - 1), 0.0, jnp.roll(edge_lo, -1, axis=1))
    def rows_at(mat, start):
        return jnp.pad(mat.astype(BF16), ((start, LANE - start - mat.shape[0]), (0, 0)))

    wl = jnp.stack([rows_at(w_lora[d], d * DECAY_LORA) for d in range(2)])
    al = jnp.stack([rows_at(a_lora[d], 2 * DECAY_LORA + d * ICLR_LORA) for d in range(2)])
    gl = rows_at(g_lora, 0)
    mu_p = jnp.pad(mu, (0, RW_COLS - RWKV_STREAM)).reshape(1, RW_COLS)
    head_id = jnp.arange(w) // RWKV_HEAD
    bd = (head_id[:, None] == head_id[None, :]).astype(F32)
    r, v, kk, ld0, ld1, kd0, kd1, a0_, a1_, bv, g = _rwkv_prep(
        p, hprev[:, :, None, :], hnext[:, :, None, :], mu_p, w0, a0, wl, al, gl,
        k_k.reshape(1, w), k_a.reshape(1, w), r_k.reshape(1, w), bd, tm)
    pz, wu = _rwkv_chunks(r, v, kk, ld0, ld1, kd0, kd1, a0_, a1_)
    y0, y1 = _rwkv_state(pz, wu, ctx)
    return _rwkv_out(y0, y1, bv, g, ln_g.reshape(1, w), ln_b.reshape(1, w), bd, tm)


def _merge_a_kernel(x_ref, ml_ref, mc_ref, o4_ref, wg_ref, wb_ref, out_ref, xn_ref, acc_ref, *, tm, ctx):
    i = pl.program_id(1)
    j = pl.program_id(2)
    br = pl.program_id(3)

    @pl.when((j == 0) & (br == 0))
    def _():
        _fill_modulated(xn_ref, x_ref, ml_ref[0], mc_ref[0], i * tm, ctx)

    val = _sigmoid(_dot(xn_ref[...], wg_ref[...])) * _dot(o4_ref[0, 0], wb_ref[0])

    @pl.when(br == 0)
    def _():
        acc_ref[...] = val

    @pl.when(br > 0)
    def _():
        acc_ref[...] += val

    @pl.when(br == N_BRANCH - 1)
    def _():
        out_ref[0] = acc_ref[...].astype(out_ref.dtype)


def _gate_window_weights(w_in_l, gate_off, w_branch_l, w_out_l):
    d = w_in_l.shape[0]
    shift = gate_off % LANE
    start = gate_off - shift
    gw = d + 2 * LANE
    tail = gw - d - shift
    wsrc = w_in_l[:, start:].astype(BF16)
    wsrc = jnp.pad(wsrc, ((0, 0), (0, (N_BRANCH - 1) * d + gw - wsrc.shape[1])))
    wg = jnp.concatenate([wsrc[:, br * d:br * d + gw] for br in range(N_BRANCH)], axis=1)
    wb = jnp.pad(w_branch_l.astype(BF16), ((0, 0), (0, 0), (shift, tail)))
    wo = jnp.pad(w_out_l.astype(BF16), ((shift, tail), (0, 0)))
    return wg, wb, wo


def _merge_a(x_all, modl, modc, o4, wg, wb, ctx):
    b, ta, d = x_all.shape
    gw = wb.shape[2]
    tm = _tile(ta, 1088)
    tn = _tile(gw, 768, LANE)
    return pl.pallas_call(
        functools.partial(_merge_a_kernel, tm=tm, ctx=ctx),
        grid=(b, ta // tm, gw // tn, N_BRANCH),
        in_specs=[pl.BlockSpec((1, tm, d), lambda bb, i, j, br: (bb, i, 0)),
                  pl.BlockSpec((1, 6, d), lambda bb, i, j, br: (bb, 0, 0)),
                  pl.BlockSpec((1, 6, d), lambda bb, i, j, br: (0, 0, 0)),
                  pl.BlockSpec((1, 1, tm, BRANCH_W), lambda bb, i, j, br: (br, bb, i, 0)),
                  pl.BlockSpec((d, tn), lambda bb, i, j, br: (0, br * (gw // tn) + j)),
                  pl.BlockSpec((1, BRANCH_W, tn), lambda bb, i, j, br: (br, 0, j))],
        out_specs=pl.BlockSpec((1, tm, tn), lambda bb, i, j, br: (bb, i, j)),
        out_shape=jax.ShapeDtypeStruct((b, ta, gw), BF16),
        scratch_shapes=[pltpu.VMEM((tm, d), BF16), pltpu.VMEM((tm, tn), F32)],
        compiler_params=_cparams(("parallel", "parallel", "arbitrary", "arbitrary"), 56),
        name="merge_gate",
    )(x_all, modl, modc, o4, wg, wb)


def _merge_b_kernel(m_ref, w_ref, x_ref, ml_ref, mc_ref, lg_ref, lb_ref, xo_ref, h2_ref, *, tm, ctx, alpha):
    i = pl.program_id(1)
    isc = _row_is_ctx(tm, i * tm, ctx)
    ml = ml_ref[0]
    mc = mc_ref[0]
    y = _dot(m_ref[0], w_ref[...])
    g1 = jnp.where(isc, mc[2:3], ml[2:3])
    xn = _ln(alpha * x_ref[0] + g1 * y, POST_LN_EPS) * lg_ref[...] + lb_ref[...]
    xo_ref[0] = xn
    h2_ref[0] = _pack_halves(_modulated(xn, ml, mc, 3, 4, isc))


def _merge_b(merged, w_out, x_all, modl, modc, ln_g, ln_b, ctx, alpha):
    b, ta, d = x_all.shape
    gw = w_out.shape[0]
    tm = _tile(ta, 272)
    row = lambda bb, i: (bb, i, 0)
    return pl.pallas_call(
        functools.partial(_merge_b_kernel, tm=tm, ctx=ctx, alpha=alpha),
        grid=(b, ta // tm),
        in_specs=[pl.BlockSpec((1, tm, gw), row),
                  pl.BlockSpec((gw, d), lambda bb, i: (0, 0)),
                  pl.BlockSpec((1, tm, d), row),
                  pl.BlockSpec((1, 6, d), lambda bb, i: (bb, 0, 0)),
                  pl.BlockSpec((1, 6, d), lambda bb, i: (0, 0, 0)),
                  pl.BlockSpec((1, d), lambda bb, i: (0, 0)),
                  pl.BlockSpec((1, d), lambda bb, i: (0, 0))],
        out_specs=[pl.BlockSpec((1, tm, d), row), pl.BlockSpec((1, tm, d // 2), row)],
        out_shape=[jax.ShapeDtypeStruct((b, ta, d), F32), jax.ShapeDtypeStruct((b, ta, d // 2), jnp.uint32)],
        compiler_params=_cparams(("parallel", "parallel")),
        name="merge_out_ln",
    )(merged, w_out, x_all, modl, modc, ln_g.reshape(1, d), ln_b.reshape(1, d))


def _router_kernel(x_ref, ml_ref, mc_ref, rw_ref, rb_ref, idx_ref, wt_ref, rank_ref, cnt_ref, carry_ref,
                   *, tm, ctx):
    i = pl.program_id(1)

    @pl.when((pl.program_id(0) == 0) & (i == 0))
    def _():
        carry_ref[...] = jnp.zeros_like(carry_ref)

    isc = _row_is_ctx(tm, i * tm, ctx)
    h2 = _modulated(x_ref[0], ml_ref[0], mc_ref[0], 3, 4, isc)
    scores = _sigmoid(_dot(h2, rw_ref[...], precision=HIGHEST))
    biased = scores + rb_ref[...]
    lane = lax.broadcasted_iota(jnp.int32, scores.shape, 1)
    grp = lane // GROUP_SIZE
    ninf = -jnp.inf

    def first_argmax(z):
        m = jnp.max(z, -1, keepdims=True)
        idx = jnp.min(jnp.where(z == m, lane, N_EXPERTS), -1, keepdims=True)
        return m, idx

    gscore = jnp.zeros_like(biased)
    for g in range(N_GROUPS):
        zg = jnp.where(grp == g, biased, ninf)
        m1, i1 = first_argmax(zg)
        m2 = jnp.max(jnp.where(lane == i1, ninf, zg), -1, keepdims=True)
        gscore = jnp.where(grp == g, m1 + m2, gscore)
    cand = jnp.where(lane % GROUP_SIZE == 0, gscore, ninf)
    gsel = jnp.zeros(scores.shape, jnp.bool_)
    for _ in range(TOPK_GROUPS):
        _, ig = first_argmax(cand)
        gsel = gsel | (grp == ig // GROUP_SIZE)
        cand = jnp.where(lane == ig, ninf, cand)
    masked = jnp.where(gsel, biased, ninf)
    out_lane = lax.broadcasted_iota(jnp.int32, (tm, LANE), 1)
    idx_out = jnp.zeros((tm, LANE), jnp.int32)
    w_out = jnp.zeros((tm, LANE), F32)
    w_sum = jnp.zeros((tm, 1), F32)
    chosen = []
    picked = jnp.zeros(scores.shape, F32)
    for kq in range(TOP_K):
        _, ie = first_argmax(masked)
        sel = lane == ie
        wk = jnp.sum(jnp.where(sel, scores, 0.0), -1, keepdims=True)
        masked = jnp.where(sel, ninf, masked)
        idx_out = jnp.where(out_lane == kq, ie, idx_out)
        w_out = jnp.where(out_lane == kq, wk, w_out)
        w_sum = w_sum + wk
        chosen.append(sel)
        picked = jnp.where(sel, 1.0, picked)
    idx_ref[0] = idx_out
    wt_ref[0] = w_out / w_sum * ROUTED_SCALE
    r_i = lax.broadcasted_iota(jnp.int32, (tm, tm), 0)
    c_i = lax.broadcasted_iota(jnp.int32, (tm, tm), 1)
    before = _dot(jnp.where(r_i > c_i, 1.0, 0.0).astype(BF16), picked.astype(BF16)) + carry_ref[...]
    rank_out = jnp.zeros((tm, LANE), jnp.int32)
    for kq in range(TOP_K):
        rk = jnp.sum(jnp.where(chosen[kq], before, 0.0), -1, keepdims=True)
        rank_out = jnp.where(out_lane == kq, rk.astype(jnp.int32), rank_out)
    rank_ref[0] = rank_out
    carry_ref[...] += jnp.sum(picked, axis=0, keepdims=True)
    cnt_ref[...] = carry_ref[...].astype(jnp.int32)


def _router(x_all, modl, modc, rw, rb, ctx):
    b, ta, d = x_all.shape
    tm = _tile(ta, 544)
    row = lambda bb, i: (bb, i, 0)
    tok = jax.ShapeDtypeStruct((b, ta, LANE), jnp.int32)
    return pl.pallas_call(
        functools.partial(_router_kernel, tm=tm, ctx=ctx),
        grid=(b, ta // tm),
        in_specs=[pl.BlockSpec((1, tm, d), row),
                  pl.BlockSpec((1, 6, d), lambda bb, i: (bb, 0, 0)),
                  pl.BlockSpec((1, 6, d), lambda bb, i: (0, 0, 0)),
                  pl.BlockSpec((d, N_EXPERTS), lambda bb, i: (0, 0)),
                  pl.BlockSpec((1, N_EXPERTS), lambda bb, i: (0, 0))],
        out_specs=[pl.BlockSpec((1, tm, LANE), row), pl.BlockSpec((1, tm, LANE), row),
                   pl.BlockSpec((1, tm, LANE), row), pl.BlockSpec((1, N_EXPERTS), lambda bb, i: (0, 0))],
        out_shape=[tok, jax.ShapeDtypeStruct((b, ta, LANE), F32), tok,
                   jax.ShapeDtypeStruct((1, N_EXPERTS), jnp.int32)],
        scratch_shapes=[pltpu.VMEM((1, N_EXPERTS), F32)],
        compiler_params=_cparams(("arbitrary", "arbitrary")),
        name="router",
    )(x_all, modl, modc, rw, rb.reshape(1, N_EXPERTS))


def _row_copy(src_ref, src_row, dst_ref, dst_row, sem):
    return pltpu.make_async_copy(src_ref.at[pl.ds(src_row, 1)], dst_ref.at[pl.ds(dst_row, 1)], sem)


def _dispatch_kernel(pos_ref, h2_ref, xg_in_ref, xg_ref, sem, *, tm):
    del xg_in_ref

    def issue(t, carry):
        for kq in range(TOP_K):
            _row_copy(h2_ref, t, xg_ref, pos_ref[0, 0, t * TOP_K + kq], sem).start()
        return carry

    lax.fori_loop(0, tm, issue, 0, unroll=4)
    for _ in range(TOP_K):
        pltpu.make_async_copy(h2_ref, xg_ref.at[pl.ds(0, tm)], sem).wait()


def _dispatch(pos, h2p, n_rows, tm):
    n, half = h2p.shape
    xg0 = jnp.zeros((n_rows, half), jnp.uint32)
    return pl.pallas_call(
        functools.partial(_dispatch_kernel, tm=tm),
        grid=(n // tm,),
        in_specs=[pl.BlockSpec((1, 1, tm * TOP_K), lambda i: (i, 0, 0), memory_space=pltpu.SMEM),
                  pl.BlockSpec((tm, half), lambda i: (i, 0)),
                  pl.BlockSpec(memory_space=pl.ANY)],
        out_specs=pl.BlockSpec(memory_space=pl.ANY),
        out_shape=jax.ShapeDtypeStruct((n_rows, half), jnp.uint32),
        scratch_shapes=[pltpu.SemaphoreType.DMA(())],
        input_output_aliases={2: 0},
        compiler_params=_cparams(("arbitrary",)),
        name="moe_dispatch",
    )(pos, h2p, xg0)


def _swiglu(x_lo, x_hi, wgu, wdn, ff):
    half = x_lo.shape[1]
    h = _dot(x_lo, wgu[:half, :]) + _dot(x_hi, wgu[half:, :])
    g = h[:, :ff]
    return _dot((g * _sigmoid(g) * h[:, ff:]).astype(BF16), wdn)


def _experts_kernel(be_ref, nu_ref, x_ref, wgu_ref, wdn_ref, o_ref, wgu_s, wdn_s, *, ff):
    i = pl.program_id(0)
    changed = (i == 0) | (be_ref[i] != be_ref[jnp.maximum(i - 1, 0)])

    @pl.when(changed)
    def _():
        wgu_s[...] = wgu_ref[0, 0].astype(BF16)
        wdn_s[...] = wdn_ref[0, 0].astype(BF16)

    @pl.when(i < nu_ref[0])
    def _():
        lo, hi = _unpack_halves(x_ref[...])
        o_ref[...] = _pack_halves(_swiglu(lo.astype(BF16), hi.astype(BF16), wgu_s[...], wdn_s[...], ff))

    @pl.when(i >= nu_ref[0])
    def _():
        o_ref[...] = jnp.zeros_like(o_ref)


def _experts(xg, blk_e, n_used, w_gu, w_dn, l, blk):
    n_rows, half = xg.shape
    d = 2 * half
    ff = w_dn.shape[2]
    n_blocks = n_rows // blk
    grid_spec = pltpu.PrefetchScalarGridSpec(
        num_scalar_prefetch=2,
        grid=(n_blocks,),
        in_specs=[pl.BlockSpec((blk, half), lambda i, be, nu: (i, 0)),
                  pl.BlockSpec((1, 1, d, 2 * ff), lambda i, be, nu: (l, be[i], 0, 0)),
                  pl.BlockSpec((1, 1, ff, d), lambda i, be, nu: (l, be[i], 0, 0))],
        out_specs=pl.BlockSpec((blk, half), lambda i, be, nu: (i, 0)),
        scratch_shapes=[pltpu.VMEM((d, 2 * ff), BF16), pltpu.VMEM((ff, d), BF16)],
    )
    return pl.pallas_call(
        functools.partial(_experts_kernel, ff=ff),
        grid_spec=grid_spec,
        out_shape=jax.ShapeDtypeStruct((n_rows, half), jnp.uint32),
        compiler_params=_cparams(("arbitrary",)),
        name="moe_experts",
    )(blk_e, n_used, xg, w_gu, w_dn)


def _moe_plan(idx, rank, counts, blk):
    n_assign = idx.size
    padded = (counts + blk - 1) // blk * blk
    pad_end = jnp.cumsum(padded)
    start_pad = pad_end - padded
    experts = jnp.arange(N_EXPERTS, dtype=jnp.int32)
    start_of = jnp.sum(jnp.where(idx[..., None] == experts, start_pad, 0), axis=-1)
    pos = (start_of + rank).astype(jnp.int32)
    n_blocks = -(-(n_assign + N_EXPERTS * (blk - 1)) // blk)
    blk_start = jnp.arange(n_blocks, dtype=jnp.int32) * blk
    blk_e = jnp.minimum(jnp.sum(blk_start[:, None] >= pad_end[None, :], axis=-1), N_EXPERTS - 1).astype(jnp.int32)
    n_used = (pad_end[-1] // blk).astype(jnp.int32).reshape(1)
    return pos, blk_e, n_used, n_blocks * blk


def _combine_kernel(pos_ref, y_ref, wt_ref, h2_ref, sgu_ref, sdn_ref, x_ref, ml_ref, mc_ref, lg_ref, lb_ref,
                    xo_ref, buf, sem, *, tm, ctx, alpha, ff, t0):
    i = pl.program_id(1) + t0

    def issue(t, carry):
        for kq in range(TOP_K):
            _row_copy(y_ref, pos_ref[0, 0, t * TOP_K + kq], buf.at[kq], t, sem).start()
        return carry

    lax.fori_loop(0, tm, issue, 0, unroll=4)
    lo, hi = _unpack_halves(h2_ref[0])
    f = _swiglu(lo.astype(BF16), hi.astype(BF16), sgu_ref[...], sdn_ref[...], ff)
    for kq in range(TOP_K):
        pltpu.make_async_copy(y_ref.at[pl.ds(0, tm)], buf.at[kq], sem).wait()
    wt = wt_ref[0]
    half = f.shape[1] // 2
    f_lo = f[:, :half]
    f_hi = f[:, half:]
    for kq in range(TOP_K):
        lo, hi = _unpack_halves(buf[kq])
        f_lo = f_lo + wt[:, kq:kq + 1] * lo
        f_hi = f_hi + wt[:, kq:kq + 1] * hi
    f = jnp.concatenate([f_lo, f_hi], axis=1)
    isc = _row_is_ctx(tm, i * tm, ctx)
    g2 = jnp.where(isc, mc_ref[0][5:6], ml_ref[0][5:6])
    xo_ref[0] = _ln(alpha * x_ref[0] + g2 * f, POST_LN_EPS) * lg_ref[...] + lb_ref[...]


def _combine(pos, y_rows, wts, h2p, sh_gu, sh_dn, x_all, modl, modc, ln_g, ln_b, ctx, alpha, tm, skip_ctx):
    b, ta, d = x_all.shape
    ff = sh_dn.shape[0]
    nt = ta // tm
    t0 = ctx // tm if skip_ctx else 0
    row = lambda bb, i: (bb, i + t0, 0)
    return pl.pallas_call(
        functools.partial(_combine_kernel, tm=tm, ctx=ctx, alpha=alpha, ff=ff, t0=t0),
        grid=(b, nt - t0),
        in_specs=[pl.BlockSpec((1, 1, tm * TOP_K), lambda bb, i: (bb * nt + i + t0, 0, 0),
                               memory_space=pltpu.SMEM),
                  pl.BlockSpec(memory_space=pl.ANY),
                  pl.BlockSpec((1, tm, LANE), row),
                  pl.BlockSpec((1, tm, d // 2), row),
                  pl.BlockSpec((d, 2 * ff), lambda bb, i: (0, 0)),
                  pl.BlockSpec((ff, d), lambda bb, i: (0, 0)),
                  pl.BlockSpec((1, tm, d), row),
                  pl.BlockSpec((1, 6, d), lambda bb, i: (bb, 0, 0)),
                  pl.BlockSpec((1, 6, d), lambda bb, i: (0, 0, 0)),
                  pl.BlockSpec((1, d), lambda bb, i: (0, 0)),
                  pl.BlockSpec((1, d), lambda bb, i: (0, 0))],
        out_specs=pl.BlockSpec((1, tm, d), lambda bb, i: (bb, i, 0)),
        out_shape=jax.ShapeDtypeStruct((b, ta - t0 * tm, d), F32),
        scratch_shapes=[pltpu.VMEM((TOP_K, tm, d // 2), jnp.uint32), pltpu.SemaphoreType.DMA(())],
        compiler_params=_cparams(("arbitrary", "arbitrary")),
        name="moe_combine_ln",
    )(pos, y_rows, wts, h2p, sh_gu, sh_dn, x_all, modl, modc, ln_g.reshape(1, d), ln_b.reshape(1, d))


def _rope_tables(n_lat, ctx):
    rows = n_lat // GRID_W
    row = jnp.repeat(jnp.arange(rows, dtype=F32), GRID_W)
    col = jnp.tile(jnp.arange(GRID_W, dtype=F32), rows)
    n_freq = ROPE_DIM // 4
    inv_freq = ROPE_BASE ** (-jnp.arange(n_freq, dtype=F32) / n_freq)
    ang = jnp.concatenate([row[:, None] * inv_freq, col[:, None] * inv_freq], -1)
    cos, sin = jnp.cos(ang), jnp.sin(ang)
    cos = jnp.concatenate([jnp.ones((ctx, ROPE_DIM // 2), F32), cos], 0)
    sin = jnp.concatenate([jnp.zeros((ctx, ROPE_DIM // 2), F32), sin], 0)
    cosf = jnp.tile(jnp.concatenate([cos, cos], -1), (1, LANE // ROPE_DIM))
    sinf = jnp.tile(jnp.concatenate([-sin, sin], -1), (1, LANE // ROPE_DIM))
    return cosf, sinf


def _pack_small_weights(w_in_l):
    d = w_in_l.shape[0]
    offs = np.cumsum([0, MLA_Q_RANK, MLA_KV_RANK + MLA_ROPE, RWKV_STREAM,
                      (SWA_HEADS + 2 * SWA_KV_HEADS) * SWA_HEAD, 3 * DIFF_HEADS * 2 * DIFF_HEAD])
    placed = sorted(zip([SEG_MQ, SEG_MKV, SEG_RW, SEG_SW, SEG_DF], range(5)))
    parts = []
    cur = 0
    for start, s in placed:
        if start > cur:
            parts.append(jnp.zeros((d, start - cur), BF16))
        parts.append(w_in_l[:, int(offs[s]):int(offs[s + 1])].astype(BF16))
        cur = start + int(offs[s + 1] - offs[s])
    parts.append(jnp.zeros((d, SMALL_COLS - cur), BF16))
    return jnp.concatenate(parts, axis=1), int(offs[5])


def kernel(x, c, ctx, c_ctx, w_mod, b_mod, w_in, mla_q_norm, mla_w_qup, mla_kv_norm, mla_w_kvup, rwkv_mu, rwkv_w0, rwkv_w_lora, rwkv_a0, rwkv_a_lora, rwkv_g_lora, rwkv_k_k, rwkv_k_a, rwkv_r_k, rwkv_ln_g, rwkv_ln_b, swa_sink, diff_lambda, diff_subln, w_branch, w_out, ln1_g, ln1_b, router_w, router_bias, exp_w_gu, exp_w_dn, sh_w_gu, sh_w_dn, ln2_g, ln2_b):
    depth = w_mod.shape[0]
    b, n_lat, d = x.shape
    n_ctx = ctx.shape[1]
    ta = n_ctx + n_lat
    alpha = (2 * depth) ** 0.25
    cosf, sinf = _rope_tables(n_lat, n_ctx)
    x_all = jnp.concatenate([ctx, x], axis=1)

    cond = jnp.concatenate([c, c_ctx[None]], axis=0)
    cond = jnp.pad(jax.nn.silu(cond), ((0, SUBLANE - (b + 1) % SUBLANE), (0, 0)))

    for l in range(depth):
        mod = _matmul_bias(cond, w_mod, b_mod, l).reshape(cond.shape[0], 6, d)
        modl, modc = mod[:b], mod[b:b + 1]

        w_small, gate_off = _pack_small_weights(w_in[l])
        p = _inproj(x_all, modl, modc, w_small, n_ctx)

        wq = mla_w_qup[l].reshape(MLA_Q_RANK, MLA_HEADS, MLA_NOPE + MLA_ROPE)
        wq = jnp.pad(wq, ((0, 0), (0, 0), (0, 256 - MLA_NOPE - MLA_ROPE))).reshape(MLA_Q_RANK, 1024)
        q_a, k_a, v_a = _mla_prep(p, mla_q_norm[l].reshape(1, -1), mla_kv_norm[l].reshape(1, -1),
                                  wq.astype(BF16), mla_w_kvup[l].astype(BF16), cosf, sinf)
        out_a = _mla_attn(q_a, k_a, v_a, n_ctx)

        out_b = _rwkv_mixer(p, n_ctx, rwkv_mu[l], rwkv_w0[l],
                            rwkv_w_lora[l], rwkv_a0[l], rwkv_a_lora[l], rwkv_g_lora[l], rwkv_k_k[l],
                            rwkv_k_a[l], rwkv_r_k[l], rwkv_ln_g[l], rwkv_ln_b[l])

        sinkv = jnp.broadcast_to(swa_sink[l][:, None], (SWA_HEADS, LANE))
        out_s = _swa(p, cosf, sinf, sinkv, n_ctx)

        lf = diff_lambda[l]
        lam_init = 0.8 - 0.6 * math.exp(-0.3 * l)
        lam = jnp.exp(jnp.sum(lf[0] * lf[1])) - jnp.exp(jnp.sum(lf[2] * lf[3])) + lam_init
        out_d = _diff_attn(p, cosf, sinf, jnp.full((1, LANE), lam, F32), diff_subln[l].reshape(1, LANE),
                           n_ctx, lam_init)

        o4 = jnp.stack([out_a, out_b, out_s, out_d], axis=0)
        wg, wb, wo = _gate_window_weights(w_in[l], gate_off, w_branch[l], w_out[l])
        merged = _merge_a(x_all, modl, modc, o4, wg, wb, n_ctx)
        x_all, h2p = _merge_b(merged, wo, x_all, modl, modc, ln1_g[l], ln1_b[l], n_ctx, alpha)

        idx, wts, rank, counts = _router(x_all, modl, modc, router_w[l], router_bias[l], n_ctx)
        pos, blk_e, n_used, n_rows = _moe_plan(idx[..., :TOP_K], rank[..., :TOP_K], counts[0], MOE_BLK)
        tmd = _tile(ta, 128)
        tmc = _tile(math.gcd(ta, n_ctx), 256)
        xg = _dispatch(pos.reshape(b * ta // tmd, 1, tmd * TOP_K), h2p.reshape(b * ta, d // 2), n_rows, tmd)
        y_rows = _experts(xg, blk_e, n_used, exp_w_gu, exp_w_dn, l, MOE_BLK)
        x_all = _combine(pos.reshape(b * ta // tmc, 1, tmc * TOP_K), y_rows, wts, h2p, sh_w_gu[l].astype(BF16),
                         sh_w_dn[l].astype(BF16), x_all, modl, modc, ln2_g[l], ln2_b[l], n_ctx, alpha, tmc,
                         skip_ctx=(l == depth - 1))
    return x_all
```

```python
import functools
import math

import jax
import jax.numpy as jnp
import numpy as np
from jax import lax
from jax.experimental import pallas as pl
from jax.experimental.pallas import tpu as pltpu

F32 = jnp.float32
BF16 = jnp.bfloat16
HIGHEST = lax.Precision.HIGHEST

GRID_W = 64
ROPE_BASE = 10000.0
ROPE_DIM = 64
ADALN_EPS = 1e-6
POST_LN_EPS = 1e-5
NEG_INF = -1e30

MLA_HEADS = 4
MLA_Q_RANK = 384
MLA_KV_RANK = 256
MLA_NOPE = 128
MLA_ROPE = 64
MLA_V = 128

RWKV_HEADS = 8
RWKV_HEAD = 64
RWKV_W = RWKV_HEADS * RWKV_HEAD
DECAY_LORA = 32
ICLR_LORA = 32
GATE_LORA = 96
RWKV_STREAM = 3 * RWKV_W + 2 * DECAY_LORA + 2 * ICLR_LORA + GATE_LORA
RWKV_GN_EPS = 64e-5
RWKV_CHUNK = 64

SWA_HEADS = 8
SWA_KV_HEADS = 2
SWA_GROUP = SWA_HEADS // SWA_KV_HEADS
SWA_HEAD = 64
WINDOW = 128

DIFF_HEADS = 4
DIFF_HEAD = 64

N_BRANCH = 4
BRANCH_W = 512

N_EXPERTS = 64
TOP_K = 6
N_GROUPS = 8
TOPK_GROUPS = 4
GROUP_SIZE = N_EXPERTS // N_GROUPS
ROUTED_SCALE = 2.5
MOE_BLK = 256

LANE = 128
SUBLANE = 8

SEG_SW = 0
SEG_MQ = 768
SEG_MKV = 1152
SEG_DF = 1536
SEG_RW = 3584
RW_COLS = 1792
SMALL_COLS = SEG_RW + RW_COLS


def _tile(n, target, mult=SUBLANE):
    best = None
    for d in range(mult, min(n, target) + 1, mult):
        if n % d == 0:
            best = d
    assert best is not None, (n, target, mult)
    return best


def _cparams(sem, vmem_mb=48):
    return pltpu.CompilerParams(dimension_semantics=sem, vmem_limit_bytes=vmem_mb * 1024 * 1024)


def _sigmoid(x):
    return 1.0 / (1.0 + jnp.exp(-x))


def _ln(x, eps):
    mu = jnp.mean(x, -1, keepdims=True)
    xc = x - mu
    var = jnp.mean(xc * xc, -1, keepdims=True)
    return xc * lax.rsqrt(var + eps)


def _rms(x, eps):
    return x * lax.rsqrt(jnp.mean(x * x, -1, keepdims=True) + eps)


def _row_is_ctx(n_rows, row0, ctx):
    rows = row0 + lax.broadcasted_iota(jnp.int32, (n_rows, 1), 0)
    return rows < ctx


def _modulated(x, ml, mc, sh_i, sc_i, isc):
    sh = jnp.where(isc, mc[sh_i:sh_i + 1], ml[sh_i:sh_i + 1])
    sc = jnp.where(isc, mc[sc_i:sc_i + 1], ml[sc_i:sc_i + 1])
    return _ln(x, ADALN_EPS) * (1.0 + sc) + sh


def _fill_modulated(xn_ref, x_ref, ml, mc, row0, ctx):
    tm = xn_ref.shape[0]
    ch = _tile(tm, 272)
    for r in range(0, tm, ch):
        isc = _row_is_ctx(ch, row0 + r, ctx)
        xn_ref[r:r + ch, :] = _modulated(x_ref[0, r:r + ch, :], ml, mc, 0, 1, isc).astype(BF16)


def _swap32(x):
    lane = lax.broadcasted_iota(jnp.int32, x.shape, 1)
    up = pltpu.roll(x, LANE - 32, 1)
    dn = pltpu.roll(x, 32, 1)
    return jnp.where((lane % 64) < 32, up, dn)


def _rope(x, cosf, sinf):
    return x * cosf + _swap32(x) * sinf


def _pack_halves(x):
    n = x.shape[1] // 2
    lo = pltpu.bitcast(x[:, :n].astype(BF16).astype(F32), jnp.uint32)
    hi = pltpu.bitcast(x[:, n:].astype(BF16).astype(F32), jnp.uint32)
    return (hi & jnp.uint32(0xFFFF0000)) | (lo >> 16)


def _unpack_halves(u):
    lo = pltpu.bitcast(u << 16, F32)
    hi = pltpu.bitcast(u & jnp.uint32(0xFFFF0000), F32)
    return lo, hi


def _dot(a, b, **kw):
    return jnp.dot(a, b, preferred_element_type=F32, **kw)


def _dot_nt(a, b, **kw):
    return lax.dot_general(a, b, (((1,), (1,)), ((), ())), preferred_element_type=F32, **kw)


def _dot_tn(a, b, **kw):
    return lax.dot_general(a, b, (((0,), (0,)), ((), ())), preferred_element_type=F32, **kw)


def _mm_kernel(x_ref, w_ref, b_ref, o_ref):
    o_ref[...] = _dot(x_ref[...], w_ref[0], precision=HIGHEST) + b_ref[0]


def _matmul_bias(x, w, b, l):
    m, k = x.shape
    n = w.shape[2]
    tn = _tile(n, 1024, LANE)
    return pl.pallas_call(
        _mm_kernel,
        grid=(n // tn,),
        in_specs=[pl.BlockSpec((m, k), lambda j: (0, 0)),
                  pl.BlockSpec((1, k, tn), lambda j: (l, 0, j)),
                  pl.BlockSpec((1, 1, tn), lambda j: (l, 0, j))],
        out_specs=pl.BlockSpec((m, tn), lambda j: (0, j)),
        out_shape=jax.ShapeDtypeStruct((m, n), F32),
        compiler_params=_cparams(("arbitrary",)),
        name="mod_matmul",
    )(x, w, b.reshape(b.shape[0], 1, n))


def _inproj_kernel(x_ref, ml_ref, mc_ref, w_ref, o_ref, xn_ref, *, tm, ctx):
    i = pl.program_id(1)
    j = pl.program_id(2)

    @pl.when(j == 0)
    def _():
        _fill_modulated(xn_ref, x_ref, ml_ref[0], mc_ref[0], i * tm, ctx)

    o_ref[0] = _dot(xn_ref[...], w_ref[...])


def _inproj(x_all, modl, modc, w_small, ctx):
    b, ta, d = x_all.shape
    n = w_small.shape[1]
    tm = _tile(ta, 1088)
    tn = _tile(n, 1024, LANE)
    return pl.pallas_call(
        functools.partial(_inproj_kernel, tm=tm, ctx=ctx),
        grid=(b, ta // tm, n // tn),
        in_specs=[pl.BlockSpec((1, tm, d), lambda bb, i, j: (bb, i, 0)),
                  pl.BlockSpec((1, 6, d), lambda bb, i, j: (bb, 0, 0)),
                  pl.BlockSpec((1, 6, d), lambda bb, i, j: (0, 0, 0)),
                  pl.BlockSpec((d, tn), lambda bb, i, j: (0, j))],
        out_specs=pl.BlockSpec((1, tm, tn), lambda bb, i, j: (bb, i, j)),
        out_shape=jax.ShapeDtypeStruct((b, ta, n), F32),
        scratch_shapes=[pltpu.VMEM((tm, d), BF16)],
        compiler_params=_cparams(("parallel", "parallel", "arbitrary"), 56),
        name="inproj",
    )(x_all, modl, modc, w_small)


def _mla_prep_kernel(mq_ref, mkv_ref, qg_ref, kg_ref, wq_ref, wkv_ref, cos_ref, sin_ref,
                     q_ref, k_ref, v_ref):
    mq = mq_ref[0]
    mkv = mkv_ref[0]
    cosv = cos_ref[...]
    sinv = sin_ref[...]
    qn = _rms(mq, 1e-6) * qg_ref[...]
    q = _dot(qn.astype(BF16), wq_ref[...])
    kvn = _rms(mkv[:, :MLA_KV_RANK], 1e-6) * kg_ref[...]
    kv = _dot(kvn.astype(BF16), wkv_ref[...])
    kr = _rope(mkv[:, MLA_KV_RANK:MLA_KV_RANK + LANE], cosv, sinv).astype(BF16)
    scale = (MLA_NOPE + MLA_ROPE) ** -0.5 * LOG2E
    for h in range(MLA_HEADS):
        c0 = h * 256
        q_ref[0, :, c0:c0 + 128] = (q[:, c0:c0 + 128] * scale).astype(BF16)
        q_ref[0, :, c0 + 128:c0 + 256] = (_rope(q[:, c0 + 128:c0 + 256], cosv, sinv) * scale).astype(BF16)
        k_ref[0, :, c0:c0 + 128] = kv[:, c0:c0 + 128].astype(BF16)
        k_ref[0, :, c0 + 128:c0 + 256] = kr
        v_ref[0, :, h * 128:(h + 1) * 128] = kv[:, c0 + 128:c0 + 256].astype(BF16)


def _mla_prep(p, qg, kg, wq, wkv, cosf, sinf):
    b, ta, _ = p.shape
    tm = _tile(ta, 544)
    return pl.pallas_call(
        _mla_prep_kernel,
        grid=(b, ta // tm),
        in_specs=[pl.BlockSpec((1, tm, 384), lambda bb, i: (bb, i, SEG_MQ // 384)),
                  pl.BlockSpec((1, tm, 384), lambda bb, i: (bb, i, SEG_MKV // 384)),
                  pl.BlockSpec((1, MLA_Q_RANK), lambda bb, i: (0, 0)),
                  pl.BlockSpec((1, MLA_KV_RANK), lambda bb, i: (0, 0)),
                  pl.BlockSpec((MLA_Q_RANK, 1024), lambda bb, i: (0, 0)),
                  pl.BlockSpec((MLA_KV_RANK, 1024), lambda bb, i: (0, 0)),
                  pl.BlockSpec((tm, LANE), lambda bb, i: (i, 0)),
                  pl.BlockSpec((tm, LANE), lambda bb, i: (i, 0))],
        out_specs=[pl.BlockSpec((1, tm, 1024), lambda bb, i: (bb, i, 0)),
                   pl.BlockSpec((1, tm, 1024), lambda bb, i: (bb, i, 0)),
                   pl.BlockSpec((1, tm, 512), lambda bb, i: (bb, i, 0))],
        out_shape=[jax.ShapeDtypeStruct((b, ta, 1024), BF16),
                   jax.ShapeDtypeStruct((b, ta, 1024), BF16),
                   jax.ShapeDtypeStruct((b, ta, 512), BF16)],
        compiler_params=_cparams(("parallel", "parallel")),
        name="mla_prep",
    )(p, p, qg, kg, wq, wkv, cosf, sinf)


LOG2E = 1.4426950408889634


def _softmax_pv(s, v):
    m = jnp.max(s, -1, keepdims=True)
    p = jnp.exp2(s - m)
    l = jnp.sum(p, -1, keepdims=True)
    return _dot(p.astype(BF16), v) / l


def _mla_attn_kernel(q_ref, k_ref, v_ref, o_ref, *, n_ctx_tiles, ctx):
    qi = pl.program_id(2)

    def run(nk):
        s = _dot_nt(q_ref[0], k_ref[0, :nk, :])
        o_ref[0] = _softmax_pv(s, v_ref[0, :nk, :]).astype(o_ref.dtype)

    @pl.when(qi < n_ctx_tiles)
    def _():
        run(ctx)

    @pl.when(qi >= n_ctx_tiles)
    def _():
        run(k_ref.shape[1])


def _mla_attn(q, k, v, ctx):
    b, ta, _ = q.shape
    tq = _tile(math.gcd(ta, ctx), 256)
    return pl.pallas_call(
        functools.partial(_mla_attn_kernel, n_ctx_tiles=ctx // tq, ctx=ctx),
        grid=(b, MLA_HEADS, ta // tq),
        in_specs=[pl.BlockSpec((1, tq, 256), lambda bb, h, i: (bb, i, h)),
                  pl.BlockSpec((1, ta, 256), lambda bb, h, i: (bb, 0, h)),
                  pl.BlockSpec((1, ta, 128), lambda bb, h, i: (bb, 0, h))],
        out_specs=pl.BlockSpec((1, tq, 128), lambda bb, h, i: (bb, i, h)),
        out_shape=jax.ShapeDtypeStruct((b, ta, 512), BF16),
        compiler_params=_cparams(("parallel", "parallel", "arbitrary")),
        name="mla_attn",
    )(q, k, v)


def _diff_attn_kernel(q_ref, k_ref, v_ref, cq_ref, sq_ref, ck_ref, sk_ref, lam_ref, sub_ref,
                      o_ref, ks_ref, vs_ref, *, n_ctx_tiles, ctx, post_scale):
    qi = pl.program_id(2)

    @pl.when(qi == 0)
    def _():
        ks_ref[...] = _rope(k_ref[0], ck_ref[...], sk_ref[...]).astype(BF16)
        vs_ref[...] = v_ref[0].astype(BF16)

    def run(nk):
        q = _rope(q_ref[0], cq_ref[...], sq_ref[...]) * (DIFF_HEAD ** -0.5 * LOG2E)
        lane = lax.broadcasted_iota(jnp.int32, q.shape, 1)
        q1 = jnp.where(lane < 64, q, 0.0).astype(BF16)
        q2 = jnp.where(lane >= 64, q, 0.0).astype(BF16)
        k = ks_ref[:nk, :]
        v = vs_ref[:nk, :]
        o = _softmax_pv(_dot_nt(q1, k), v) - lam_ref[...] * _softmax_pv(_dot_nt(q2, k), v)
        o = _rms(o, 1e-5) * sub_ref[...] * post_scale
        o_ref[0] = o.astype(o_ref.dtype)

    @pl.when(qi < n_ctx_tiles)
    def _():
        run(ctx)

    @pl.when(qi >= n_ctx_tiles)
    def _():
        run(ks_ref.shape[0])


def _diff_attn(p, cosf, sinf, lamv, subln, ctx, lam_init):
    b, ta, _ = p.shape
    tq = _tile(math.gcd(ta, ctx), 256)
    cb = SEG_DF // LANE
    return pl.pallas_call(
        functools.partial(_diff_attn_kernel, n_ctx_tiles=ctx // tq, ctx=ctx, post_scale=1.0 - lam_init),
        grid=(b, DIFF_HEADS, ta // tq),
        in_specs=[pl.BlockSpec((1, tq, LANE), lambda bb, h, i: (bb, i, cb + h)),
                  pl.BlockSpec((1, ta, LANE), lambda bb, h, i: (bb, 0, cb + DIFF_HEADS + h)),
                  pl.BlockSpec((1, ta, LANE), lambda bb, h, i: (bb, 0, cb + 2 * DIFF_HEADS + h)),
                  pl.BlockSpec((tq, LANE), lambda bb, h, i: (i, 0)),
                  pl.BlockSpec((tq, LANE), lambda bb, h, i: (i, 0)),
                  pl.BlockSpec((ta, LANE), lambda bb, h, i: (0, 0)),
                  pl.BlockSpec((ta, LANE), lambda bb, h, i: (0, 0)),
                  pl.BlockSpec((1, LANE), lambda bb, h, i: (0, 0)),
                  pl.BlockSpec((1, LANE), lambda bb, h, i: (0, 0))],
        out_specs=pl.BlockSpec((1, tq, LANE), lambda bb, h, i: (bb, i, h)),
        out_shape=jax.ShapeDtypeStruct((b, ta, 512), BF16),
        scratch_shapes=[pltpu.VMEM((ta, LANE), BF16), pltpu.VMEM((ta, LANE), BF16)],
        compiler_params=_cparams(("parallel", "parallel", "arbitrary")),
        name="diff_attn",
    )(p, p, p, cosf, sinf, cosf, sinf, lamv, subln)


def _swa_kernel(q_ref, k_ref, v_ref, cq_ref, sq_ref, ck_ref, sk_ref, sink_ref,
                o_ref, ks_ref, vs_ref, *, ctx, nsb):
    t = pl.program_id(1)
    n_ctx_steps = ctx // (WINDOW * nsb)
    ta = ks_ref.shape[0]
    span = 3 * WINDOW

    @pl.when(t == 0)
    def _():
        ks_ref[...] = _rope(k_ref[0], ck_ref[...], sk_ref[...]).astype(BF16)
        vs_ref[...] = v_ref[0].astype(BF16)

    lane = lax.broadcasted_iota(jnp.int32, (WINDOW, LANE), 1)

    def run(local, sb):
        rows = slice(sb * WINDOW, (sb + 1) * WINDOW)
        cq = cq_ref[rows, :]
        sq = sq_ref[rows, :]
        qblk = [_rope(q_ref[0, rows, hb * LANE:(hb + 1) * LANE], cq, sq) * (SWA_HEAD ** -0.5)
                for hb in range(SWA_HEADS // 2)]

        def head_q(hk, g):
            h = hk * SWA_GROUP + g
            blk = qblk[h // 2]
            if h % 2 != hk:
                blk = pltpu.roll(blk, 64, 1)
            keep = (lane >= 64) if hk == 1 else (lane < 64)
            return jnp.where(keep, blk, 0.0)

        kc = ks_ref[:ctx, :]
        vc = vs_ref[:ctx, :]
        if local:
            n = (t - n_ctx_steps) * nsb + sb
            ws = jnp.clip(ctx + (n - 1) * WINDOW, ctx, ta - span)
            ws = pl.multiple_of(ws, WINDOW)
            kl = ks_ref[pl.ds(ws, span), :]
            vl = vs_ref[pl.ds(ws, span), :]
            kpos = (ws - ctx) + lax.broadcasted_iota(jnp.int32, (1, span), 1)
            qpos = n * WINDOW + lax.broadcasted_iota(jnp.int32, (SWA_GROUP * WINDOW, 1), 0) % WINDOW
            valid = jnp.abs(kpos - qpos) <= WINDOW
        outs = {}
        for hk in range(SWA_KV_HEADS):
            qst = jnp.concatenate([head_q(hk, g) for g in range(SWA_GROUP)], axis=0).astype(BF16)
            sink = jnp.concatenate(
                [jnp.broadcast_to(sink_ref[hk * SWA_GROUP + g:hk * SWA_GROUP + g + 1, 0:1], (WINDOW, 1))
                 for g in range(SWA_GROUP)], axis=0)
            s_c = _dot_nt(qst, kc)
            m = jnp.maximum(jnp.max(s_c, -1, keepdims=True), sink)
            if local:
                s_l = jnp.where(valid, _dot_nt(qst, kl), NEG_INF)
                m = jnp.maximum(m, jnp.max(s_l, -1, keepdims=True))
            p_c = jnp.exp(s_c - m)
            l = jnp.sum(p_c, -1, keepdims=True) + jnp.exp(sink - m)
            acc = _dot(p_c.astype(BF16), vc)
            if local:
                p_l = jnp.exp(s_l - m)
                l = l + jnp.sum(p_l, -1, keepdims=True)
                acc = acc + _dot(p_l.astype(BF16), vl)
            o = acc / l
            for g in range(SWA_GROUP):
                res = o[g * WINDOW:(g + 1) * WINDOW, :]
                if g % 2 != hk:
                    res = pltpu.roll(res, 64, 1)
                outs[hk * SWA_GROUP + g] = res
        for hb in range(SWA_HEADS // 2):
            blk = jnp.where(lane < 64, outs[2 * hb], outs[2 * hb + 1])
            o_ref[0, rows, hb * LANE:(hb + 1) * LANE] = blk.astype(o_ref.dtype)

    @pl.when(t < n_ctx_steps)
    def _():
        for sb in range(nsb):
            run(False, sb)

    @pl.when(t >= n_ctx_steps)
    def _():
        for sb in range(nsb):
            run(True, sb)


def _swa(p, cosf, sinf, sinkv, ctx):
    b, ta, _ = p.shape
    qb = SEG_SW // 512
    kb = (SEG_SW + 512) // LANE
    nsb = 2 if (ctx % (2 * WINDOW) == 0 and ta % (2 * WINDOW) == 0) else 1
    tq = nsb * WINDOW
    return pl.pallas_call(
        functools.partial(_swa_kernel, ctx=ctx, nsb=nsb),
        grid=(b, ta // tq),
        in_specs=[pl.BlockSpec((1, tq, 512), lambda bb, t: (bb, t, qb)),
                  pl.BlockSpec((1, ta, LANE), lambda bb, t: (bb, 0, kb)),
                  pl.BlockSpec((1, ta, LANE), lambda bb, t: (bb, 0, kb + 1)),
                  pl.BlockSpec((tq, LANE), lambda bb, t: (t, 0)),
                  pl.BlockSpec((tq, LANE), lambda bb, t: (t, 0)),
                  pl.BlockSpec((ta, LANE), lambda bb, t: (0, 0)),
                  pl.BlockSpec((ta, LANE), lambda bb, t: (0, 0)),
                  pl.BlockSpec((SWA_HEADS, LANE), lambda bb, t: (0, 0))],
        out_specs=pl.BlockSpec((1, tq, 512), lambda bb, t: (bb, t, 0)),
        out_shape=jax.ShapeDtypeStruct((b, ta, 512), BF16),
        scratch_shapes=[pltpu.VMEM((ta, LANE), BF16), pltpu.VMEM((ta, LANE), BF16)],
        compiler_params=_cparams(("parallel", "arbitrary")),
        name="swa_attn",
    )(p, p, p, cosf, sinf, cosf, sinf, sinkv)


def _softplus(x):
    return jnp.maximum(x, 0.0) + jnp.log(1.0 + jnp.exp(-jnp.abs(x)))


def _rwkv_prep_kernel(z_ref, hp_ref, hn_ref, mu_ref, w0_ref, a0_ref, wl_ref, al_ref, gl_ref, kk_ref, ka_ref,
                      rk_ref, bd_ref, r_o, v_o, kk_o, ld0_o, ld1_o, kd0_o, kd1_o, a0_o, a1_o, bv_o, g_o, zs_ref):
    tm = z_ref.shape[1]
    w = RWKV_W
    z = z_ref[0]
    zs_ref[SUBLANE:SUBLANE + tm, :] = z
    zs_ref[SUBLANE - 1:SUBLANE, :] = hp_ref[0, 0]
    zs_ref[SUBLANE + tm:SUBLANE + tm + 1, :] = hn_ref[0, 0]
    shifted = 0.5 * (zs_ref[SUBLANE - 1:SUBLANE - 1 + tm, :] + zs_ref[SUBLANE + 1:SUBLANE + 1 + tm, :])
    seg = z + (shifted - z) * mu_ref[...]
    r = seg[:, 0:w]
    k = seg[:, w:2 * w]
    v = seg[:, 2 * w:3 * w]
    lo = seg[:, 3 * w:3 * w + LANE]
    gl = seg[:, 3 * w + LANE:3 * w + 2 * LANE]
    hp = dict(precision=HIGHEST)
    bd = bd_ref[...]
    kk = k * kk_ref[...]
    kk = kk / jnp.maximum(jnp.sqrt(_dot(kk * kk, bd, **hp)), 1e-12)
    th = jnp.tanh(lo).astype(BF16)
    lo_b = lo.astype(BF16)
    ksum = jnp.zeros_like(k)
    for d, (ld_o, kd_o, a_o) in enumerate(((ld0_o, kd0_o, a0_o), (ld1_o, kd1_o, a1_o))):
        wd = w0_ref[d:d + 1, :] + _dot(th, wl_ref[d])
        wd = -_softplus(-wd) - 0.5
        a = _sigmoid(a0_ref[d:d + 1, :] + _dot(lo_b, al_ref[d]))
        kd = k * (1.0 + (a - 1.0) * ka_ref[...])
        ld_o[0] = -jnp.exp(wd)
        kd_o[0] = kd
        a_o[0] = a
        ksum = ksum + kd
    r_o[0] = r
    v_o[0] = v
    kk_o[0] = kk
    bv_o[0] = _dot(r * ksum * rk_ref[...], bd, **hp) * v
    g_o[0] = _dot(_sigmoid(gl).astype(BF16), gl_ref[...])


def _rwkv_prep(p, hprev, hnext, mu, w0, a0, wl, al, gl, k_k, k_a, r_k, bd, tm):
    b, ta, _ = p.shape
    w = RWKV_W
    row = lambda bb, i: (bb, i, 0)
    vec = lambda n: pl.BlockSpec((1, n), lambda bb, i: (0, 0))
    out = jax.ShapeDtypeStruct((b, ta, w), F32)
    return pl.pallas_call(
        _rwkv_prep_kernel,
        grid=(b, ta // tm),
        in_specs=[pl.BlockSpec((1, tm, RW_COLS), lambda bb, i: (bb, i, SEG_RW // RW_COLS)),
                  pl.BlockSpec((1, 1, 1, RW_COLS), lambda bb, i: (bb, i, 0, 0)),
                  pl.BlockSpec((1, 1, 1, RW_COLS), lambda bb, i: (bb, i, 0, 0)),
                  vec(RW_COLS),
                  pl.BlockSpec((2, w), lambda bb, i: (0, 0)),
                  pl.BlockSpec((2, w), lambda bb, i: (0, 0)),
                  pl.BlockSpec((2, LANE, w), lambda bb, i: (0, 0, 0)),
                  pl.BlockSpec((2, LANE, w), lambda bb, i: (0, 0, 0)),
                  pl.BlockSpec((LANE, w), lambda bb, i: (0, 0)),
                  vec(w), vec(w), vec(w),
                  pl.BlockSpec((w, w), lambda bb, i: (0, 0))],
        out_specs=[pl.BlockSpec((1, tm, w), row)] * 11,
        out_shape=[out] * 11,
        scratch_shapes=[pltpu.VMEM((tm + 2 * SUBLANE, RW_COLS), F32)],
        compiler_params=_cparams(("parallel", "parallel")),
        name="rwkv_prep",
    )(p, hprev, hnext, mu, w0, a0, wl, al, gl, k_k, k_a, r_k, bd)


def _split3(x):
    hi = x.astype(BF16)
    r1 = x - hi.astype(F32)
    mid = r1.astype(BF16)
    return hi, mid, (r1 - mid.astype(F32)).astype(BF16)


def _split2(x):
    hi = x.astype(BF16)
    return hi, (x - hi.astype(F32)).astype(BF16)


def _mm3(a, b, dot=_dot):
    return dot(a[0], b[0]) + dot(a[0], b[1]) + dot(a[1], b[0])


def _rwkv_chunk_kernel(r_ref, v_ref, kk_ref, ld0_ref, ld1_ref, kd0_ref, kd1_ref, a0_ref, a1_ref, pz_ref, wu_ref):
    C = RWKV_CHUNK
    hd = RWKV_HEAD
    row = lax.broadcasted_iota(jnp.int32, (C, C), 0)
    col = lax.broadcasted_iota(jnp.int32, (C, C), 1)
    eye = (row == col).astype(F32)
    probs = []
    for d, (ld_ref, kd_ref, a_ref) in enumerate(((ld0_ref, kd0_ref, a0_ref), (ld1_ref, kd1_ref, a1_ref))):
        incl = (row >= col) if d == 0 else (row <= col)
        strict = (row > col) if d == 0 else (row < col)
        last = C - 1 if d == 0 else 0
        for h in range(RWKV_HEADS):
            sl = slice(h * hd, (h + 1) * hd)
            probs.append(dict(d=d, h=h, incl=incl, strict=strict, last=last,
                              ones=jnp.where(incl, 1.0, 0.0).astype(BF16),
                              r=r_ref[0, :, sl], v=v_ref[0, :, sl], kk=kk_ref[0, :, sl],
                              ld=ld_ref[0, :, sl], k=kd_ref[0, :, sl], a=a_ref[0, :, sl]))
    bf = lambda z: z.astype(BF16)
    for q in probs:
        q["cs"] = sum(_dot(q["ones"], part) for part in _split3(q["ld"]))
    for q in probs:
        cs = q["cs"]
        e_cs = jnp.exp(cs)
        e_inv = jnp.exp(-cs)
        q["g_end"] = e_cs[q["last"]:q["last"] + 1, :]
        q["kkd"] = q["kk"] * jnp.exp(cs - q["ld"])
        q["rd"] = q["r"] * e_cs
        q["bi"] = bf(q["kk"] * q["a"] * e_inv)
        q["ki"] = bf(q["k"] * e_inv)
        q["lhs"] = bf(jnp.concatenate([q["kkd"], q["rd"]], axis=0))
    for q in probs:
        gb = _dot_nt(q["lhs"], q["bi"])
        gk = _dot_nt(q["lhs"], q["ki"])
        q["pw"] = -jnp.where(q["strict"], gb[:C], 0.0)
        q["q_b"] = bf(jnp.where(q["incl"], gb[C:], 0.0))
        q["akqk"] = bf(jnp.concatenate([jnp.where(q["strict"], gk[:C], 0.0),
                                        jnp.where(q["incl"], gk[C:], 0.0)], axis=0))
        q["tinv"] = eye + q["pw"]
    for _ in range(5):
        for q in probs:
            pw = bf(q["pw"])
            q["pw"] = _dot(pw, pw)
        for q in probs:
            q["tinv"] = q["tinv"] + _dot(bf(q["tinv"]), bf(q["pw"]))
    for q in probs:
        q["vs"] = bf(q["v"])
        q["av"] = _dot(q["akqk"], q["vs"])
    for q in probs:
        tinv = bf(q["tinv"])
        q["w1"] = _dot(tinv, bf(q["kkd"]))
        q["u1"] = _dot(tinv, bf(q["av"][:C]))
    for q in probs:
        w1 = bf(q["w1"])
        u1 = bf(q["u1"])
        w2 = q["rd"] - _dot(q["q_b"], w1)
        u2 = q["av"][C:] - _dot(q["q_b"], u1)
        pt = (eye - _dot_tn(w1, q["bi"])) * q["g_end"]
        zt = (_dot_tn(q["vs"], q["ki"]) - _dot_tn(u1, q["bi"])) * q["g_end"]
        pz_ref[q["d"], 0, 0, q["h"]] = jnp.concatenate([pt, zt], axis=1)
        wu_ref[q["d"], 0, 0, q["h"]] = jnp.concatenate([w2, u2], axis=1)


def _rwkv_chunks(r, v, kk, ld0, ld1, kd0, kd1, a0, a1):
    b, ta, w = r.shape
    nc = ta // RWKV_CHUNK
    spec = pl.BlockSpec((1, RWKV_CHUNK, w), lambda bb, c: (bb, c, 0))
    ospec = pl.BlockSpec((2, 1, 1, RWKV_HEADS, RWKV_HEAD, 2 * RWKV_HEAD), lambda bb, c: (0, bb, c, 0, 0, 0))
    oshape = jax.ShapeDtypeStruct((2, b, nc, RWKV_HEADS, RWKV_HEAD, 2 * RWKV_HEAD), F32)
    return pl.pallas_call(
        _rwkv_chunk_kernel,
        grid=(b, nc),
        in_specs=[spec] * 9,
        out_specs=[ospec, ospec],
        out_shape=[oshape, oshape],
        compiler_params=_cparams(("parallel", "parallel")),
        name="rwkv_chunks",
    )(r, v, kk, ld0, ld1, kd0, kd1, a0, a1)


def _rwkv_state_kernel(pz0_ref, pz1_ref, wu0_ref, wu1_ref, y0_ref, y1_ref, st_ref):
    i = pl.program_id(0)
    hd = RWKV_HEAD
    nb = st_ref.shape[1]

    @pl.when(i == 0)
    def _():
        st_ref[...] = jnp.zeros_like(st_ref)

    probs = []
    for d, (pz_ref, wu_ref) in enumerate(((pz0_ref, wu0_ref), (pz1_ref, wu1_ref))):
        for bb in range(nb):
            for h in range(RWKV_HEADS):
                pz = pz_ref[0, bb, 0, h]
                wu = wu_ref[0, bb, 0, h]
                probs.append(dict(d=d, b=bb, h=h, pt=_split2(pz[:, :hd]), zt=pz[:, hd:],
                                  w2=_split2(wu[:, :hd]), u2=wu[:, hd:], s=_split2(st_ref[d, bb, h])))
    for q in probs:
        q["y"] = _mm3(q["w2"], q["s"], _dot_nt) + q["u2"]
    for q in probs:
        st_ref[q["d"], q["b"], q["h"]] = _mm3(q["s"], q["pt"]) + q["zt"]
    for d, y_ref in enumerate((y0_ref, y1_ref)):
        for bb in range(nb):
            ys = [q["y"] for q in probs if q["d"] == d and q["b"] == bb]
            for hp in range(RWKV_HEADS // 2):
                y_ref[bb, :, hp * LANE:(hp + 1) * LANE] = jnp.concatenate([ys[2 * hp], ys[2 * hp + 1]], axis=1)


def _rwkv_state(pz, wu, ctx):
    _, b, nc, nh, hd, _ = pz.shape
    ncc = ctx // RWKV_CHUNK

    def rev(i):
        return jnp.where(i < ncc, ncc - 1 - i, nc - 1 - (i - ncc))

    blk = (1, b, 1, nh, hd, 2 * hd)
    fwd = lambda i: (0, 0, i, 0, 0, 0)
    bwd = lambda i: (1, 0, rev(i), 0, 0, 0)
    yshape = jax.ShapeDtypeStruct((b, nc * RWKV_CHUNK, nh * hd), F32)
    return pl.pallas_call(
        _rwkv_state_kernel,
        grid=(nc,),
        in_specs=[pl.BlockSpec(blk, fwd), pl.BlockSpec(blk, bwd), pl.BlockSpec(blk, fwd), pl.BlockSpec(blk, bwd)],
        out_specs=[pl.BlockSpec((b, RWKV_CHUNK, nh * hd), lambda i: (0, i, 0)),
                   pl.BlockSpec((b, RWKV_CHUNK, nh * hd), lambda i: (0, rev(i), 0))],
        out_shape=[yshape, yshape],
        scratch_shapes=[pltpu.VMEM((2, b, nh, hd, hd), F32)],
        compiler_params=_cparams(("arbitrary",)),
        name="rwkv_state",
    )(pz, pz, wu, wu)


def _rwkv_out_kernel(y0_ref, y1_ref, bv_ref, g_ref, lg_ref, lb_ref, bd_ref, o_ref):
    hp = dict(precision=HIGHEST)
    bdm = bd_ref[...] * (1.0 / RWKV_HEAD)
    y = y0_ref[0] + y1_ref[0]
    yc = y - _dot(y, bdm, **hp)
    var = _dot(yc * yc, bdm, **hp)
    yn = yc * lax.rsqrt(var + RWKV_GN_EPS) * lg_ref[...] + lb_ref[...]
    o_ref[0] = ((yn + bv_ref[0]) * g_ref[0]).astype(o_ref.dtype)


def _rwkv_out(y0, y1, bv, g, ln_g, ln_b, bd, tm):
    b, ta, w = y0.shape
    row = pl.BlockSpec((1, tm, w), lambda bb, i: (bb, i, 0))
    vec = pl.BlockSpec((1, w), lambda bb, i: (0, 0))
    return pl.pallas_call(
        _rwkv_out_kernel,
        grid=(b, ta // tm),
        in_specs=[row, row, row, row, vec, vec, pl.BlockSpec((w, w), lambda bb, i: (0, 0))],
        out_specs=row,
        out_shape=jax.ShapeDtypeStruct((b, ta, w), BF16),
        compiler_params=_cparams(("parallel", "parallel")),
        name="rwkv_out",
    )(y0, y1, bv, g, ln_g, ln_b, bd)


def _rwkv_mixer(p, ctx, mu, w0, w_lora, a0, a_lora, g_lora, k_k, k_a, r_k, ln_g, ln_b):
    b, ta, _ = p.shape
    w = RWKV_W
    tm = _tile(math.gcd(ta, ctx), 256)
    nt = ta // tm
    edge_lo = p[:, 0::tm, SEG_RW:]
    edge_hi = p[:, tm - 1::tm, SEG_RW:]
    tile_id = jnp.arange(nt)[None, :, None]
    hprev = jnp.where((tile_id == 0) | (tile_id == ctx // tm), 0.0, jnp.roll(edge_hi, 1, axis=1))
    hnext = jnp.where((tile_id == nt - 1) | (tile_id == ctx // tm - 1), 0.0, jnp.roll(edge_lo, -1, axis=1))
    def rows_at(mat, start):
        return jnp.pad(mat.astype(BF16), ((start, LANE - start - mat.shape[0]), (0, 0)))

    wl = jnp.stack([rows_at(w_lora[d], d * DECAY_LORA) for d in range(2)])
    al = jnp.stack([rows_at(a_lora[d], 2 * DECAY_LORA + d * ICLR_LORA) for d in range(2)])
    gl = rows_at(g_lora, 0)
    mu_p = jnp.pad(mu, (0, RW_COLS - RWKV_STREAM)).reshape(1, RW_COLS)
    head_id = jnp.arange(w) // RWKV_HEAD
    bd = (head_id[:, None] == head_id[None, :]).astype(F32)
    r, v, kk, ld0, ld1, kd0, kd1, a0_, a1_, bv, g = _rwkv_prep(
        p, hprev[:, :, None, :], hnext[:, :, None, :], mu_p, w0, a0, wl, al, gl,
        k_k.reshape(1, w), k_a.reshape(1, w), r_k.reshape(1, w), bd, tm)
    pz, wu = _rwkv_chunks(r, v, kk, ld0, ld1, kd0, kd1, a0_, a1_)
    y0, y1 = _rwkv_state(pz, wu, ctx)
    return _rwkv_out(y0, y1, bv, g, ln_g.reshape(1, w), ln_b.reshape(1, w), bd, tm)


def _merge_a_kernel(x_ref, ml_ref, mc_ref, o4_ref, wg_ref, wb_ref, out_ref, xn_ref, acc_ref, *, tm, ctx):
    i = pl.program_id(1)
    j = pl.program_id(2)
    br = pl.program_id(3)

    @pl.when((j == 0) & (br == 0))
    def _():
        _fill_modulated(xn_ref, x_ref, ml_ref[0], mc_ref[0], i * tm, ctx)

    val = _sigmoid(_dot(xn_ref[...], wg_ref[...])) * _dot(o4_ref[0, 0], wb_ref[0])

    @pl.when(br == 0)
    def _():
        acc_ref[...] = val

    @pl.when(br > 0)
    def _():
        acc_ref[...] += val

    @pl.when(br == N_BRANCH - 1)
    def _():
        out_ref[0] = acc_ref[...].astype(out_ref.dtype)


def _gate_window_weights(w_in_l, gate_off, w_branch_l, w_out_l):
    d = w_in_l.shape[0]
    shift = gate_off % LANE
    start = gate_off - shift
    gw = d + 2 * LANE
    tail = gw - d - shift
    wsrc = w_in_l[:, start:].astype(BF16)
    wsrc = jnp.pad(wsrc, ((0, 0), (0, (N_BRANCH - 1) * d + gw - wsrc.shape[1])))
    wg = jnp.concatenate([wsrc[:, br * d:br * d + gw] for br in range(N_BRANCH)], axis=1)
    wb = jnp.pad(w_branch_l.astype(BF16), ((0, 0), (0, 0), (shift, tail)))
    wo = jnp.pad(w_out_l.astype(BF16), ((shift, tail), (0, 0)))
    return wg, wb, wo


def _merge_a(x_all, modl, modc, o4, wg, wb, ctx):
    b, ta, d = x_all.shape
    gw = wb.shape[2]
    tm = _tile(ta, 1088)
    tn = _tile(gw, 768, LANE)
    return pl.pallas_call(
        functools.partial(_merge_a_kernel, tm=tm, ctx=ctx),
        grid=(b, ta // tm, gw // tn, N_BRANCH),
        in_specs=[pl.BlockSpec((1, tm, d), lambda bb, i, j, br: (bb, i, 0)),
                  pl.BlockSpec((1, 6, d), lambda bb, i, j, br: (bb, 0, 0)),
                  pl.BlockSpec((1, 6, d), lambda bb, i, j, br: (0, 0, 0)),
                  pl.BlockSpec((1, 1, tm, BRANCH_W), lambda bb, i, j, br: (br, bb, i, 0)),
                  pl.BlockSpec((d, tn), lambda bb, i, j, br: (0, br * (gw // tn) + j)),
                  pl.BlockSpec((1, BRANCH_W, tn), lambda bb, i, j, br: (br, 0, j))],
        out_specs=pl.BlockSpec((1, tm, tn), lambda bb, i, j, br: (bb, i, j)),
        out_shape=jax.ShapeDtypeStruct((b, ta, gw), BF16),
        scratch_shapes=[pltpu.VMEM((tm, d), BF16), pltpu.VMEM((tm, tn), F32)],
        compiler_params=_cparams(("parallel", "parallel", "arbitrary", "arbitrary"), 56),
        name="merge_gate",
    )(x_all, modl, modc, o4, wg, wb)


def _merge_b_kernel(m_ref, w_ref, x_ref, ml_ref, mc_ref, lg_ref, lb_ref, xo_ref, h2_ref, *, tm, ctx, alpha):
    i = pl.program_id(1)
    isc = _row_is_ctx(tm, i * tm, ctx)
    ml = ml_ref[0]
    mc = mc_ref[0]
    y = _dot(m_ref[0], w_ref[...])
    g1 = jnp.where(isc, mc[2:3], ml[2:3])
    xn = _ln(alpha * x_ref[0] + g1 * y, POST_LN_EPS) * lg_ref[...] + lb_ref[...]
    xo_ref[0] = xn
    h2_ref[0] = _pack_halves(_modulated(xn, ml, mc, 3, 4, isc))


def _merge_b(merged, w_out, x_all, modl, modc, ln_g, ln_b, ctx, alpha):
    b, ta, d = x_all.shape
    gw = w_out.shape[0]
    tm = _tile(ta, 272)
    row = lambda bb, i: (bb, i, 0)
    return pl.pallas_call(
        functools.partial(_merge_b_kernel, tm=tm, ctx=ctx, alpha=alpha),
        grid=(b, ta // tm),
        in_specs=[pl.BlockSpec((1, tm, gw), row),
                  pl.BlockSpec((gw, d), lambda bb, i: (0, 0)),
                  pl.BlockSpec((1, tm, d), row),
                  pl.BlockSpec((1, 6, d), lambda bb, i: (bb, 0, 0)),
                  pl.BlockSpec((1, 6, d), lambda bb, i: (0, 0, 0)),
                  pl.BlockSpec((1, d), lambda bb, i: (0, 0)),
                  pl.BlockSpec((1, d), lambda bb, i: (0, 0))],
        out_specs=[pl.BlockSpec((1, tm, d), row), pl.BlockSpec((1, tm, d // 2), row)],
        out_shape=[jax.ShapeDtypeStruct((b, ta, d), F32), jax.ShapeDtypeStruct((b, ta, d // 2), jnp.uint32)],
        compiler_params=_cparams(("parallel", "parallel")),
        name="merge_out_ln",
    )(merged, w_out, x_all, modl, modc, ln_g.reshape(1, d), ln_b.reshape(1, d))


def _router_kernel(x_ref, ml_ref, mc_ref, rw_ref, rb_ref, idx_ref, wt_ref, rank_ref, cnt_ref, carry_ref,
                   *, tm, ctx):
    i = pl.program_id(1)

    @pl.when((pl.program_id(0) == 0) & (i == 0))
    def _():
        carry_ref[...] = jnp.zeros_like(carry_ref)

    isc = _row_is_ctx(tm, i * tm, ctx)
    h2 = _modulated(x_ref[0], ml_ref[0], mc_ref[0], 3, 4, isc)
    scores = _sigmoid(_dot(h2, rw_ref[...], precision=HIGHEST))
    biased = scores + rb_ref[...]
    lane = lax.broadcasted_iota(jnp.int32, scores.shape, 1)
    grp = lane // GROUP_SIZE
    ninf = -jnp.inf

    def first_argmax(z):
        m = jnp.max(z, -1, keepdims=True)
        idx = jnp.min(jnp.where(z == m, lane, N_EXPERTS), -1, keepdims=True)
        return m, idx

    gscore = jnp.zeros_like(biased)
    for g in range(N_GROUPS):
        zg = jnp.where(grp == g, biased, ninf)
        m1, i1 = first_argmax(zg)
        m2 = jnp.max(jnp.where(lane == i1, ninf, zg), -1, keepdims=True)
        gscore = jnp.where(grp == g, m1 + m2, gscore)
    cand = jnp.where(lane % GROUP_SIZE == 0, gscore, ninf)
    gsel = jnp.zeros(scores.shape, jnp.bool_)
    for _ in range(TOPK_GROUPS):
        _, ig = first_argmax(cand)
        gsel = gsel | (grp == ig // GROUP_SIZE)
        cand = jnp.where(lane == ig, ninf, cand)
    masked = jnp.where(gsel, biased, ninf)
    out_lane = lax.broadcasted_iota(jnp.int32, (tm, LANE), 1)
    idx_out = jnp.zeros((tm, LANE), jnp.int32)
    w_out = jnp.zeros((tm, LANE), F32)
    w_sum = jnp.zeros((tm, 1), F32)
    chosen = []
    picked = jnp.zeros(scores.shape, F32)
    for kq in range(TOP_K):
        _, ie = first_argmax(masked)
        sel = lane == ie
        wk = jnp.sum(jnp.where(sel, scores, 0.0), -1, keepdims=True)
        masked = jnp.where(sel, ninf, masked)
        idx_out = jnp.where(out_lane == kq, ie, idx_out)
        w_out = jnp.where(out_lane == kq, wk, w_out)
        w_sum = w_sum + wk
        chosen.append(sel)
        picked = jnp.where(sel, 1.0, picked)
    idx_ref[0] = idx_out
    wt_ref[0] = w_out / w_sum * ROUTED_SCALE
    r_i = lax.broadcasted_iota(jnp.int32, (tm, tm), 0)
    c_i = lax.broadcasted_iota(jnp.int32, (tm, tm), 1)
    before = _dot(jnp.where(r_i > c_i, 1.0, 0.0).astype(BF16), picked.astype(BF16)) + carry_ref[...]
    rank_out = jnp.zeros((tm, LANE), jnp.int32)
    for kq in range(TOP_K):
        rk = jnp.sum(jnp.where(chosen[kq], before, 0.0), -1, keepdims=True)
        rank_out = jnp.where(out_lane == kq, rk.astype(jnp.int32), rank_out)
    rank_ref[0] = rank_out
    carry_ref[...] += jnp.sum(picked, axis=0, keepdims=True)
    cnt_ref[...] = carry_ref[...].astype(jnp.int32)


def _router(x_all, modl, modc, rw, rb, ctx):
    b, ta, d = x_all.shape
    tm = _tile(ta, 544)
    row = lambda bb, i: (bb, i, 0)
    tok = jax.ShapeDtypeStruct((b, ta, LANE), jnp.int32)
    return pl.pallas_call(
        functools.partial(_router_kernel, tm=tm, ctx=ctx),
        grid=(b, ta // tm),
        in_specs=[pl.BlockSpec((1, tm, d), row),
                  pl.BlockSpec((1, 6, d), lambda bb, i: (bb, 0, 0)),
                  pl.BlockSpec((1, 6, d), lambda bb, i: (0, 0, 0)),
                  pl.BlockSpec((d, N_EXPERTS), lambda bb, i: (0, 0)),
                  pl.BlockSpec((1, N_EXPERTS), lambda bb, i: (0, 0))],
        out_specs=[pl.BlockSpec((1, tm, LANE), row), pl.BlockSpec((1, tm, LANE), row),
                   pl.BlockSpec((1, tm, LANE), row), pl.BlockSpec((1, N_EXPERTS), lambda bb, i: (0, 0))],
        out_shape=[tok, jax.ShapeDtypeStruct((b, ta, LANE), F32), tok,
                   jax.ShapeDtypeStruct((1, N_EXPERTS), jnp.int32)],
        scratch_shapes=[pltpu.VMEM((1, N_EXPERTS), F32)],
        compiler_params=_cparams(("arbitrary", "arbitrary")),
        name="router",
    )(x_all, modl, modc, rw, rb.reshape(1, N_EXPERTS))


def _row_copy(src_ref, src_row, dst_ref, dst_row, sem):
    return pltpu.make_async_copy(src_ref.at[pl.ds(src_row, 1)], dst_ref.at[pl.ds(dst_row, 1)], sem)


def _dispatch_kernel(pos_ref, h2_ref, xg_in_ref, xg_ref, sem, *, tm):
    del xg_in_ref

    def issue(t, carry):
        for kq in range(TOP_K):
            _row_copy(h2_ref, t, xg_ref, pos_ref[0, 0, t * TOP_K + kq], sem).start()
        return carry

    lax.fori_loop(0, tm, issue, 0, unroll=4)
    for _ in range(TOP_K):
        pltpu.make_async_copy(h2_ref, xg_ref.at[pl.ds(0, tm)], sem).wait()


def _dispatch(pos, h2p, n_rows, tm):
    n, half = h2p.shape
    xg0 = jnp.zeros((n_rows, half), jnp.uint32)
    return pl.pallas_call(
        functools.partial(_dispatch_kernel, tm=tm),
        grid=(n // tm,),
        in_specs=[pl.BlockSpec((1, 1, tm * TOP_K), lambda i: (i, 0, 0), memory_space=pltpu.SMEM),
                  pl.BlockSpec((tm, half), lambda i: (i, 0)),
                  pl.BlockSpec(memory_space=pl.ANY)],
        out_specs=pl.BlockSpec(memory_space=pl.ANY),
        out_shape=jax.ShapeDtypeStruct((n_rows, half), jnp.uint32),
        scratch_shapes=[pltpu.SemaphoreType.DMA(())],
        input_output_aliases={2: 0},
        compiler_params=_cparams(("arbitrary",)),
        name="moe_dispatch",
    )(pos, h2p, xg0)


def _swiglu(x_lo, x_hi, wgu, wdn, ff):
    half = x_lo.shape[1]
    h = _dot(x_lo, wgu[:half, :]) + _dot(x_hi, wgu[half:, :])
    g = h[:, :ff]
    return _dot((g * _sigmoid(g) * h[:, ff:]).astype(BF16), wdn)


def _experts_kernel(be_ref, nu_ref, first_ref, slot_ref, nxt_ref, x_ref, wgu_hbm, wdn_hbm, o_ref,
                    gu_buf, dn_buf, wgu_s, wdn_s, sems, *, ff, l):
    i = pl.program_id(0)

    def weight_copies(e, s):
        return (pltpu.make_async_copy(wgu_hbm.at[l, e], gu_buf.at[s], sems.at[0, s]),
                pltpu.make_async_copy(wdn_hbm.at[l, e], dn_buf.at[s], sems.at[1, s]))

    @pl.when(i == 0)
    def _():
        for cp in weight_copies(be_ref[0], slot_ref[0]):
            cp.start()

    @pl.when(first_ref[i] == 1)
    def _():
        s = slot_ref[i]
        for cp in weight_copies(be_ref[i], s):
            cp.wait()
        wgu_s[...] = gu_buf[s].astype(BF16)
        wdn_s[...] = dn_buf[s].astype(BF16)

        @pl.when(nxt_ref[i] >= 0)
        def _():
            for cp in weight_copies(nxt_ref[i], 1 - s):
                cp.start()

    @pl.when(i < nu_ref[0])
    def _():
        lo, hi = _unpack_halves(x_ref[...])
        o_ref[...] = _pack_halves(_swiglu(lo.astype(BF16), hi.astype(BF16), wgu_s[...], wdn_s[...], ff))

    @pl.when(i >= nu_ref[0])
    def _():
        o_ref[...] = jnp.zeros_like(o_ref)


def _experts(xg, blk_e, n_used, w_gu, w_dn, l, blk):
    n_rows, half = xg.shape
    d = 2 * half
    ff = w_dn.shape[2]
    n_blocks = n_rows // blk
    first = jnp.concatenate([jnp.ones((1,), jnp.int32), (blk_e[1:] != blk_e[:-1]).astype(jnp.int32)])
    slot = (jnp.cumsum(first) - 1) % 2
    n_le = jnp.sum(blk_e[None, :] <= blk_e[:, None], axis=1)
    pick = jnp.arange(n_blocks)[None, :] == n_le[:, None]
    nxt = jnp.where(n_le < n_blocks, jnp.sum(jnp.where(pick, blk_e[None, :], 0), axis=1), -1)
    grid_spec = pltpu.PrefetchScalarGridSpec(
        num_scalar_prefetch=5,
        grid=(n_blocks,),
        in_specs=[pl.BlockSpec((blk, half), lambda i, *_: (i, 0)),
                  pl.BlockSpec(memory_space=pl.ANY),
                  pl.BlockSpec(memory_space=pl.ANY)],
        out_specs=pl.BlockSpec((blk, half), lambda i, *_: (i, 0)),
        scratch_shapes=[pltpu.VMEM((2, d, 2 * ff), F32), pltpu.VMEM((2, ff, d), F32),
                        pltpu.VMEM((d, 2 * ff), BF16), pltpu.VMEM((ff, d), BF16),
                        pltpu.SemaphoreType.DMA((2, 2))],
    )
    return pl.pallas_call(
        functools.partial(_experts_kernel, ff=ff, l=l),
        grid_spec=grid_spec,
        out_shape=jax.ShapeDtypeStruct((n_rows, half), jnp.uint32),
        compiler_params=_cparams(("arbitrary",), 56),
        name="moe_experts",
    )(blk_e, n_used, first, slot.astype(jnp.int32), nxt.astype(jnp.int32), xg, w_gu, w_dn)


def _moe_plan(idx, rank, counts, blk):
    n_assign = idx.size
    padded = (counts + blk - 1) // blk * blk
    pad_end = jnp.cumsum(padded)
    start_pad = pad_end - padded
    experts = jnp.arange(N_EXPERTS, dtype=jnp.int32)
    start_of = jnp.sum(jnp.where(idx[..., None] == experts, start_pad, 0), axis=-1)
    pos = (start_of + rank).astype(jnp.int32)
    n_blocks = -(-(n_assign + N_EXPERTS * (blk - 1)) // blk)
    blk_start = jnp.arange(n_blocks, dtype=jnp.int32) * blk
    blk_e = jnp.minimum(jnp.sum(blk_start[:, None] >= pad_end[None, :], axis=-1), N_EXPERTS - 1).astype(jnp.int32)
    n_used = (pad_end[-1] // blk).astype(jnp.int32).reshape(1)
    return pos, blk_e, n_used, n_blocks * blk


def _combine_kernel(pos_ref, y_ref, wt_ref, h2_ref, sgu_ref, sdn_ref, x_ref, ml_ref, mc_ref, lg_ref, lb_ref,
                    xo_ref, buf, sem, *, tm, ctx, alpha, ff, t0):
    i = pl.program_id(1) + t0

    def issue(t, carry):
        for kq in range(TOP_K):
            _row_copy(y_ref, pos_ref[0, 0, t * TOP_K + kq], buf.at[kq], t, sem).start()
        return carry

    lax.fori_loop(0, tm, issue, 0, unroll=4)
    lo, hi = _unpack_halves(h2_ref[0])
    f = _swiglu(lo.astype(BF16), hi.astype(BF16), sgu_ref[...], sdn_ref[...], ff)
    for kq in range(TOP_K):
        pltpu.make_async_copy(y_ref.at[pl.ds(0, tm)], buf.at[kq], sem).wait()
    wt = wt_ref[0]
    half = f.shape[1] // 2
    f_lo = f[:, :half]
    f_hi = f[:, half:]
    for kq in range(TOP_K):
        lo, hi = _unpack_halves(buf[kq])
        f_lo = f_lo + wt[:, kq:kq + 1] * lo
        f_hi = f_hi + wt[:, kq:kq + 1] * hi
    f = jnp.concatenate([f_lo, f_hi], axis=1)
    isc = _row_is_ctx(tm, i * tm, ctx)
    g2 = jnp.where(isc, mc_ref[0][5:6], ml_ref[0][5:6])
    xo_ref[0] = _ln(alpha * x_ref[0] + g2 * f, POST_LN_EPS) * lg_ref[...] + lb_ref[...]


def _combine(pos, y_rows, wts, h2p, sh_gu, sh_dn, x_all, modl, modc, ln_g, ln_b, ctx, alpha, tm, skip_ctx):
    b, ta, d = x_all.shape
    ff = sh_dn.shape[0]
    nt = ta // tm
    t0 = ctx // tm if skip_ctx else 0
    row = lambda bb, i: (bb, i + t0, 0)
    return pl.pallas_call(
        functools.partial(_combine_kernel, tm=tm, ctx=ctx, alpha=alpha, ff=ff, t0=t0),
        grid=(b, nt - t0),
        in_specs=[pl.BlockSpec((1, 1, tm * TOP_K), lambda bb, i: (bb * nt + i + t0, 0, 0),
                               memory_space=pltpu.SMEM),
                  pl.BlockSpec(memory_space=pl.ANY),
                  pl.BlockSpec((1, tm, LANE), row),
                  pl.BlockSpec((1, tm, d // 2), row),
                  pl.BlockSpec((d, 2 * ff), lambda bb, i: (0, 0)),
                  pl.BlockSpec((ff, d), lambda bb, i: (0, 0)),
                  pl.BlockSpec((1, tm, d), row),
                  pl.BlockSpec((1, 6, d), lambda bb, i: (bb, 0, 0)),
                  pl.BlockSpec((1, 6, d), lambda bb, i: (0, 0, 0)),
                  pl.BlockSpec((1, d), lambda bb, i: (0, 0)),
                  pl.BlockSpec((1, d), lambda bb, i: (0, 0))],
        out_specs=pl.BlockSpec((1, tm, d), lambda bb, i: (bb, i, 0)),
        out_shape=jax.ShapeDtypeStruct((b, ta - t0 * tm, d), F32),
        scratch_shapes=[pltpu.VMEM((TOP_K, tm, d // 2), jnp.uint32), pltpu.SemaphoreType.DMA(())],
        compiler_params=_cparams(("arbitrary", "arbitrary")),
        name="moe_combine_ln",
    )(pos, y_rows, wts, h2p, sh_gu, sh_dn, x_all, modl, modc, ln_g.reshape(1, d), ln_b.reshape(1, d))


def _rope_tables(n_lat, ctx):
    rows = n_lat // GRID_W
    row = jnp.repeat(jnp.arange(rows, dtype=F32), GRID_W)
    col = jnp.tile(jnp.arange(GRID_W, dtype=F32), rows)
    n_freq = ROPE_DIM // 4
    inv_freq = ROPE_BASE ** (-jnp.arange(n_freq, dtype=F32) / n_freq)
    ang = jnp.concatenate([row[:, None] * inv_freq, col[:, None] * inv_freq], -1)
    cos, sin = jnp.cos(ang), jnp.sin(ang)
    cos = jnp.concatenate([jnp.ones((ctx, ROPE_DIM // 2), F32), cos], 0)
    sin = jnp.concatenate([jnp.zeros((ctx, ROPE_DIM // 2), F32), sin], 0)
    cosf = jnp.tile(jnp.concatenate([cos, cos], -1), (1, LANE // ROPE_DIM))
    sinf = jnp.tile(jnp.concatenate([-sin, sin], -1), (1, LANE // ROPE_DIM))
    return cosf, sinf


def _pack_small_weights(w_in_l):
    d = w_in_l.shape[0]
    offs = np.cumsum([0, MLA_Q_RANK, MLA_KV_RANK + MLA_ROPE, RWKV_STREAM,
                      (SWA_HEADS + 2 * SWA_KV_HEADS) * SWA_HEAD, 3 * DIFF_HEADS * 2 * DIFF_HEAD])
    placed = sorted(zip([SEG_MQ, SEG_MKV, SEG_RW, SEG_SW, SEG_DF], range(5)))
    parts = []
    cur = 0
    for start, s in placed:
        if start > cur:
            parts.append(jnp.zeros((d, start - cur), BF16))
        parts.append(w_in_l[:, int(offs[s]):int(offs[s + 1])].astype(BF16))
        cur = start + int(offs[s + 1] - offs[s])
    parts.append(jnp.zeros((d, SMALL_COLS - cur), BF16))
    return jnp.concatenate(parts, axis=1), int(offs[5])


def kernel(x, c, ctx, c_ctx, w_mod, b_mod, w_in, mla_q_norm, mla_w_qup, mla_kv_norm, mla_w_kvup, rwkv_mu, rwkv_w0, rwkv_w_lora, rwkv_a0, rwkv_a_lora, rwkv_g_lora, rwkv_k_k, rwkv_k_a, rwkv_r_k, rwkv_ln_g, rwkv_ln_b, swa_sink, diff_lambda, diff_subln, w_branch, w_out, ln1_g, ln1_b, router_w, router_bias, exp_w_gu, exp_w_dn, sh_w_gu, sh_w_dn, ln2_g, ln2_b):
    depth = w_mod.shape[0]
    b, n_lat, d = x.shape
    n_ctx = ctx.shape[1]
    ta = n_ctx + n_lat
    alpha = (2 * depth) ** 0.25
    cosf, sinf = _rope_tables(n_lat, n_ctx)
    x_all = jnp.concatenate([ctx, x], axis=1)

    cond = jnp.concatenate([c, c_ctx[None]], axis=0)
    cond = jnp.pad(jax.nn.silu(cond), ((0, SUBLANE - (b + 1) % SUBLANE), (0, 0)))

    for l in range(depth):
        mod = _matmul_bias(cond, w_mod, b_mod, l).reshape(cond.shape[0], 6, d)
        modl, modc = mod[:b], mod[b:b + 1]

        w_small, gate_off = _pack_small_weights(w_in[l])
        p = _inproj(x_all, modl, modc, w_small, n_ctx)

        wq = mla_w_qup[l].reshape(MLA_Q_RANK, MLA_HEADS, MLA_NOPE + MLA_ROPE)
        wq = jnp.pad(wq, ((0, 0), (0, 0), (0, 256 - MLA_NOPE - MLA_ROPE))).reshape(MLA_Q_RANK, 1024)
        q_a, k_a, v_a = _mla_prep(p, mla_q_norm[l].reshape(1, -1), mla_kv_norm[l].reshape(1, -1),
                                  wq.astype(BF16), mla_w_kvup[l].astype(BF16), cosf, sinf)
        out_a = _mla_attn(q_a, k_a, v_a, n_ctx)

        out_b = _rwkv_mixer(p, n_ctx, rwkv_mu[l], rwkv_w0[l],
                            rwkv_w_lora[l], rwkv_a0[l], rwkv_a_lora[l], rwkv_g_lora[l], rwkv_k_k[l],
                            rwkv_k_a[l], rwkv_r_k[l], rwkv_ln_g[l], rwkv_ln_b[l])

        sinkv = jnp.broadcast_to(swa_sink[l][:, None], (SWA_HEADS, LANE))
        out_s = _swa(p, cosf, sinf, sinkv, n_ctx)

        lf = diff_lambda[l]
        lam_init = 0.8 - 0.6 * math.exp(-0.3 * l)
        lam = jnp.exp(jnp.sum(lf[0] * lf[1])) - jnp.exp(jnp.sum(lf[2] * lf[3])) + lam_init
        out_d = _diff_attn(p, cosf, sinf, jnp.full((1, LANE), lam, F32), diff_subln[l].reshape(1, LANE),
                           n_ctx, lam_init)

        o4 = jnp.stack([out_a, out_b, out_s, out_d], axis=0)
        wg, wb, wo = _gate_window_weights(w_in[l], gate_off, w_branch[l], w_out[l])
        merged = _merge_a(x_all, modl, modc, o4, wg, wb, n_ctx)
        x_all, h2p = _merge_b(merged, wo, x_all, modl, modc, ln1_g[l], ln1_b[l], n_ctx, alpha)

        idx, wts, rank, counts = _router(x_all, modl, modc, router_w[l], router_bias[l], n_ctx)
        pos, blk_e, n_used, n_rows = _moe_plan(idx[..., :TOP_K], rank[..., :TOP_K], counts[0], MOE_BLK)
        tmd = _tile(ta, 128)
        tmc = _tile(math.gcd(ta, n_ctx), 256)
        xg = _dispatch(pos.reshape(b * ta // tmd, 1, tmd * TOP_K), h2p.reshape(b * ta, d // 2), n_rows, tmd)
        y_rows = _experts(xg, blk_e, n_used, exp_w_gu, exp_w_dn, l, MOE_BLK)
        x_all = _combine(pos.reshape(b * ta // tmc, 1, tmc * TOP_K), y_rows, wts, h2p, sh_w_gu[l].astype(BF16),
                         sh_w_dn[l].astype(BF16), x_all, modl, modc, ln2_g[l], ln2_b[l], n_ctx, alpha, tmc,
                         skip_ctx=(l == depth - 1))
    return x_all
```

```python
import functools
import math

import jax
import jax.numpy as jnp
import numpy as np
from jax import lax
from jax.experimental import pallas as pl
from jax.experimental.pallas import tpu as pltpu

F32 = jnp.float32
BF16 = jnp.bfloat16
HIGHEST = lax.Precision.HIGHEST

GRID_W = 64
ROPE_BASE = 10000.0
ROPE_DIM = 64
ADALN_EPS = 1e-6
POST_LN_EPS = 1e-5
NEG_INF = -1e30

MLA_HEADS = 4
MLA_Q_RANK = 384
MLA_KV_RANK = 256
MLA_NOPE = 128
MLA_ROPE = 64
MLA_V = 128

RWKV_HEADS = 8
RWKV_HEAD = 64
RWKV_W = RWKV_HEADS * RWKV_HEAD
DECAY_LORA = 32
ICLR_LORA = 32
GATE_LORA = 96
RWKV_STREAM = 3 * RWKV_W + 2 * DECAY_LORA + 2 * ICLR_LORA + GATE_LORA
RWKV_GN_EPS = 64e-5
RWKV_CHUNK = 64

SWA_HEADS = 8
SWA_KV_HEADS = 2
SWA_GROUP = SWA_HEADS // SWA_KV_HEADS
SWA_HEAD = 64
WINDOW = 128

DIFF_HEADS = 4
DIFF_HEAD = 64

N_BRANCH = 4
BRANCH_W = 512

N_EXPERTS = 64
TOP_K = 6
N_GROUPS = 8
TOPK_GROUPS = 4
GROUP_SIZE = N_EXPERTS // N_GROUPS
ROUTED_SCALE = 2.5
MOE_BLK = 256

LANE = 128
SUBLANE = 8

SEG_SW = 0
SEG_MQ = 768
SEG_MKV = 1152
SEG_DF = 1536
SEG_RW = 3584
RW_COLS = 1792
SMALL_COLS = SEG_RW + RW_COLS


def _tile(n, target, mult=SUBLANE):
    best = None
    for d in range(mult, min(n, target) + 1, mult):
        if n % d == 0:
            best = d
    assert best is not None, (n, target, mult)
    return best


def _cparams(sem, vmem_mb=48):
    return pltpu.CompilerParams(dimension_semantics=sem, vmem_limit_bytes=vmem_mb * 1024 * 1024)


def _sigmoid(x):
    return 1.0 / (1.0 + jnp.exp(-x))


def _ln(x, eps):
    mu = jnp.mean(x, -1, keepdims=True)
    xc = x - mu
    var = jnp.mean(xc * xc, -1, keepdims=True)
    return xc * lax.rsqrt(var + eps)


def _rms(x, eps):
    return x * lax.rsqrt(jnp.mean(x * x, -1, keepdims=True) + eps)


def _row_is_ctx(n_rows, row0, ctx):
    rows = row0 + lax.broadcasted_iota(jnp.int32, (n_rows, 1), 0)
    return rows < ctx


def _modulated(x, ml, mc, sh_i, sc_i, isc):
    sh = jnp.where(isc, mc[sh_i:sh_i + 1], ml[sh_i:sh_i + 1])
    sc = jnp.where(isc, mc[sc_i:sc_i + 1], ml[sc_i:sc_i + 1])
    return _ln(x, ADALN_EPS) * (1.0 + sc) + sh


def _fill_modulated(xn_ref, x_ref, ml, mc, row0, ctx):
    tm = xn_ref.shape[0]
    ch = _tile(tm, 272)
    for r in range(0, tm, ch):
        isc = _row_is_ctx(ch, row0 + r, ctx)
        xn_ref[r:r + ch, :] = _modulated(x_ref[0, r:r + ch, :], ml, mc, 0, 1, isc).astype(BF16)


def _swap32(x):
    lane = lax.broadcasted_iota(jnp.int32, x.shape, 1)
    up = pltpu.roll(x, LANE - 32, 1)
    dn = pltpu.roll(x, 32, 1)
    return jnp.where((lane % 64) < 32, up, dn)


def _rope(x, cosf, sinf):
    return x * cosf + _swap32(x) * sinf


def _pack_halves(x):
    n = x.shape[1] // 2
    lo = pltpu.bitcast(x[:, :n].astype(BF16).astype(F32), jnp.uint32)
    hi = pltpu.bitcast(x[:, n:].astype(BF16).astype(F32), jnp.uint32)
    return (hi & jnp.uint32(0xFFFF0000)) | (lo >> 16)


def _unpack_halves(u):
    lo = pltpu.bitcast(u << 16, F32)
    hi = pltpu.bitcast(u & jnp.uint32(0xFFFF0000), F32)
    return lo, hi


def _dot(a, b, **kw):
    return jnp.dot(a, b, preferred_element_type=F32, **kw)


def _dot_nt(a, b, **kw):
    return lax.dot_general(a, b, (((1,), (1,)), ((), ())), preferred_element_type=F32, **kw)


def _dot_tn(a, b, **kw):
    return lax.dot_general(a, b, (((0,), (0,)), ((), ())), preferred_element_type=F32, **kw)


def _mm_kernel(x_ref, w_ref, b_ref, o_ref):
    w = w_ref[0].astype(BF16)
    o_ref[...] = sum(_dot(part, w) for part in _split3(x_ref[...])) + b_ref[0]


def _matmul_bias(x, w, b, l):
    m, k = x.shape
    n = w.shape[2]
    tn = _tile(n, 1024, LANE)
    return pl.pallas_call(
        _mm_kernel,
        grid=(n // tn,),
        in_specs=[pl.BlockSpec((m, k), lambda j: (0, 0)),
                  pl.BlockSpec((1, k, tn), lambda j: (l, 0, j)),
                  pl.BlockSpec((1, 1, tn), lambda j: (l, 0, j))],
        out_specs=pl.BlockSpec((m, tn), lambda j: (0, j)),
        out_shape=jax.ShapeDtypeStruct((m, n), F32),
        compiler_params=_cparams(("arbitrary",)),
        name="mod_matmul",
    )(x, w, b.reshape(b.shape[0], 1, n))


def _inproj_kernel(x_ref, ml_ref, mc_ref, w_ref, o_ref, xn_ref, *, tm, ctx):
    i = pl.program_id(1)
    j = pl.program_id(2)

    @pl.when(j == 0)
    def _():
        _fill_modulated(xn_ref, x_ref, ml_ref[0], mc_ref[0], i * tm, ctx)

    o_ref[0] = _dot(xn_ref[...], w_ref[...])


def _inproj(x_all, modl, modc, w_small, ctx):
    b, ta, d = x_all.shape
    n = w_small.shape[1]
    tm = _tile(ta, 1088)
    tn = _tile(n, 1024, LANE)
    return pl.pallas_call(
        functools.partial(_inproj_kernel, tm=tm, ctx=ctx),
        grid=(b, ta // tm, n // tn),
        in_specs=[pl.BlockSpec((1, tm, d), lambda bb, i, j: (bb, i, 0)),
                  pl.BlockSpec((1, 6, d), lambda bb, i, j: (bb, 0, 0)),
                  pl.BlockSpec((1, 6, d), lambda bb, i, j: (0, 0, 0)),
                  pl.BlockSpec((d, tn), lambda bb, i, j: (0, j))],
        out_specs=pl.BlockSpec((1, tm, tn), lambda bb, i, j: (bb, i, j)),
        out_shape=jax.ShapeDtypeStruct((b, ta, n), F32),
        scratch_shapes=[pltpu.VMEM((tm, d), BF16)],
        compiler_params=_cparams(("parallel", "parallel", "arbitrary"), 56),
        name="inproj",
    )(x_all, modl, modc, w_small)


def _mla_prep_kernel(mq_ref, mkv_ref, qg_ref, kg_ref, wq_ref, wkv_ref, cos_ref, sin_ref,
                     q_ref, k_ref, v_ref):
    mq = mq_ref[0]
    mkv = mkv_ref[0]
    cosv = cos_ref[...]
    sinv = sin_ref[...]
    qn = _rms(mq, 1e-6) * qg_ref[...]
    q = _dot(qn.astype(BF16), wq_ref[...])
    kvn = _rms(mkv[:, :MLA_KV_RANK], 1e-6) * kg_ref[...]
    kv = _dot(kvn.astype(BF16), wkv_ref[...])
    kr = _rope(mkv[:, MLA_KV_RANK:MLA_KV_RANK + LANE], cosv, sinv).astype(BF16)
    scale = (MLA_NOPE + MLA_ROPE) ** -0.5 * LOG2E
    for h in range(MLA_HEADS):
        c0 = h * 256
        q_ref[0, :, c0:c0 + 128] = (q[:, c0:c0 + 128] * scale).astype(BF16)
        q_ref[0, :, c0 + 128:c0 + 256] = (_rope(q[:, c0 + 128:c0 + 256], cosv, sinv) * scale).astype(BF16)
        k_ref[0, :, c0:c0 + 128] = kv[:, c0:c0 + 128].astype(BF16)
        k_ref[0, :, c0 + 128:c0 + 256] = kr
        v_ref[0, :, h * 128:(h + 1) * 128] = kv[:, c0 + 128:c0 + 256].astype(BF16)


def _mla_prep(p, qg, kg, wq, wkv, cosf, sinf):
    b, ta, _ = p.shape
    tm = _tile(ta, 544)
    return pl.pallas_call(
        _mla_prep_kernel,
        grid=(b, ta // tm),
        in_specs=[pl.BlockSpec((1, tm, 384), lambda bb, i: (bb, i, SEG_MQ // 384)),
                  pl.BlockSpec((1, tm, 384), lambda bb, i: (bb, i, SEG_MKV // 384)),
                  pl.BlockSpec((1, MLA_Q_RANK), lambda bb, i: (0, 0)),
                  pl.BlockSpec((1, MLA_KV_RANK), lambda bb, i: (0, 0)),
                  pl.BlockSpec((MLA_Q_RANK, 1024), lambda bb, i: (0, 0)),
                  pl.BlockSpec((MLA_KV_RANK, 1024), lambda bb, i: (0, 0)),
                  pl.BlockSpec((tm, LANE), lambda bb, i: (i, 0)),
                  pl.BlockSpec((tm, LANE), lambda bb, i: (i, 0))],
        out_specs=[pl.BlockSpec((1, tm, 1024), lambda bb, i: (bb, i, 0)),
                   pl.BlockSpec((1, tm, 1024), lambda bb, i: (bb, i, 0)),
                   pl.BlockSpec((1, tm, 512), lambda bb, i: (bb, i, 0))],
        out_shape=[jax.ShapeDtypeStruct((b, ta, 1024), BF16),
                   jax.ShapeDtypeStruct((b, ta, 1024), BF16),
                   jax.ShapeDtypeStruct((b, ta, 512), BF16)],
        compiler_params=_cparams(("parallel", "parallel")),
        name="mla_prep",
    )(p, p, qg, kg, wq, wkv, cosf, sinf)


LOG2E = 1.4426950408889634


def _softmax_pv(s, v):
    m = jnp.max(s, -1, keepdims=True)
    p = jnp.exp2(s - m)
    l = jnp.sum(p, -1, keepdims=True)
    return _dot(p.astype(BF16), v) / l


def _mla_attn_kernel(q_ref, k_ref, v_ref, o_ref, *, n_ctx_tiles, ctx):
    qi = pl.program_id(2)

    def run(nk):
        s = _dot_nt(q_ref[0], k_ref[0, :nk, :])
        o_ref[0] = _softmax_pv(s, v_ref[0, :nk, :]).astype(o_ref.dtype)

    @pl.when(qi < n_ctx_tiles)
    def _():
        run(ctx)

    @pl.when(qi >= n_ctx_tiles)
    def _():
        run(k_ref.shape[1])


def _mla_attn(q, k, v, ctx):
    b, ta, _ = q.shape
    tq = _tile(math.gcd(ta, ctx), 256)
    return pl.pallas_call(
        functools.partial(_mla_attn_kernel, n_ctx_tiles=ctx // tq, ctx=ctx),
        grid=(b, MLA_HEADS, ta // tq),
        in_specs=[pl.BlockSpec((1, tq, 256), lambda bb, h, i: (bb, i, h)),
                  pl.BlockSpec((1, ta, 256), lambda bb, h, i: (bb, 0, h)),
                  pl.BlockSpec((1, ta, 128), lambda bb, h, i: (bb, 0, h))],
        out_specs=pl.BlockSpec((1, tq, 128), lambda bb, h, i: (bb, i, h)),
        out_shape=jax.ShapeDtypeStruct((b, ta, 512), BF16),
        compiler_params=_cparams(("parallel", "parallel", "arbitrary")),
        name="mla_attn",
    )(q, k, v)


def _diff_attn_kernel(q_ref, k_ref, v_ref, cq_ref, sq_ref, ck_ref, sk_ref, lam_ref, sub_ref,
                      o_ref, ks_ref, vs_ref, *, n_ctx_tiles, ctx, post_scale):
    qi = pl.program_id(2)

    @pl.when(qi == 0)
    def _():
        ks_ref[...] = _rope(k_ref[0], ck_ref[...], sk_ref[...]).astype(BF16)
        vs_ref[...] = v_ref[0].astype(BF16)

    def run(nk):
        q = _rope(q_ref[0], cq_ref[...], sq_ref[...]) * (DIFF_HEAD ** -0.5 * LOG2E)
        lane = lax.broadcasted_iota(jnp.int32, q.shape, 1)
        q1 = jnp.where(lane < 64, q, 0.0).astype(BF16)
        q2 = jnp.where(lane >= 64, q, 0.0).astype(BF16)
        k = ks_ref[:nk, :]
        v = vs_ref[:nk, :]
        o = _softmax_pv(_dot_nt(q1, k), v) - lam_ref[...] * _softmax_pv(_dot_nt(q2, k), v)
        o = _rms(o, 1e-5) * sub_ref[...] * post_scale
        o_ref[0] = o.astype(o_ref.dtype)

    @pl.when(qi < n_ctx_tiles)
    def _():
        run(ctx)

    @pl.when(qi >= n_ctx_tiles)
    def _():
        run(ks_ref.shape[0])


def _diff_attn(p, cosf, sinf, lamv, subln, ctx, lam_init):
    b, ta, _ = p.shape
    tq = _tile(math.gcd(ta, ctx), 256)
    cb = SEG_DF // LANE
    return pl.pallas_call(
        functools.partial(_diff_attn_kernel, n_ctx_tiles=ctx // tq, ctx=ctx, post_scale=1.0 - lam_init),
        grid=(b, DIFF_HEADS, ta // tq),
        in_specs=[pl.BlockSpec((1, tq, LANE), lambda bb, h, i: (bb, i, cb + h)),
                  pl.BlockSpec((1, ta, LANE), lambda bb, h, i: (bb, 0, cb + DIFF_HEADS + h)),
                  pl.BlockSpec((1, ta, LANE), lambda bb, h, i: (bb, 0, cb + 2 * DIFF_HEADS + h)),
                  pl.BlockSpec((tq, LANE), lambda bb, h, i: (i, 0)),
                  pl.BlockSpec((tq, LANE), lambda bb, h, i: (i, 0)),
                  pl.BlockSpec((ta, LANE), lambda bb, h, i: (0, 0)),
                  pl.BlockSpec((ta, LANE), lambda bb, h, i: (0, 0)),
                  pl.BlockSpec((1, LANE), lambda bb, h, i: (0, 0)),
                  pl.BlockSpec((1, LANE), lambda bb, h, i: (0, 0))],
        out_specs=pl.BlockSpec((1, tq, LANE), lambda bb, h, i: (bb, i, h)),
        out_shape=jax.ShapeDtypeStruct((b, ta, 512), BF16),
        scratch_shapes=[pltpu.VMEM((ta, LANE), BF16), pltpu.VMEM((ta, LANE), BF16)],
        compiler_params=_cparams(("parallel", "parallel", "arbitrary")),
        name="diff_attn",
    )(p, p, p, cosf, sinf, cosf, sinf, lamv, subln)


def _swa_kernel(q_ref, k_ref, v_ref, cq_ref, sq_ref, ck_ref, sk_ref, sink_ref,
                o_ref, ks_ref, vs_ref, *, ctx, nsb):
    t = pl.program_id(1)
    n_ctx_steps = ctx // (WINDOW * nsb)
    ta = ks_ref.shape[0]
    span = 3 * WINDOW

    @pl.when(t == 0)
    def _():
        ks_ref[...] = _rope(k_ref[0], ck_ref[...], sk_ref[...]).astype(BF16)
        vs_ref[...] = v_ref[0].astype(BF16)

    lane = lax.broadcasted_iota(jnp.int32, (WINDOW, LANE), 1)

    def run(local, sb):
        rows = slice(sb * WINDOW, (sb + 1) * WINDOW)
        cq = cq_ref[rows, :]
        sq = sq_ref[rows, :]
        qblk = [_rope(q_ref[0, rows, hb * LANE:(hb + 1) * LANE], cq, sq) * (SWA_HEAD ** -0.5)
                for hb in range(SWA_HEADS // 2)]

        def head_q(hk, g):
            h = hk * SWA_GROUP + g
            blk = qblk[h // 2]
            if h % 2 != hk:
                blk = pltpu.roll(blk, 64, 1)
            keep = (lane >= 64) if hk == 1 else (lane < 64)
            return jnp.where(keep, blk, 0.0)

        kc = ks_ref[:ctx, :]
        vc = vs_ref[:ctx, :]
        if local:
            n = (t - n_ctx_steps) * nsb + sb
            ws = jnp.clip(ctx + (n - 1) * WINDOW, ctx, ta - span)
            ws = pl.multiple_of(ws, WINDOW)
            kl = ks_ref[pl.ds(ws, span), :]
            vl = vs_ref[pl.ds(ws, span), :]
            kpos = (ws - ctx) + lax.broadcasted_iota(jnp.int32, (1, span), 1)
            qpos = n * WINDOW + lax.broadcasted_iota(jnp.int32, (SWA_GROUP * WINDOW, 1), 0) % WINDOW
            valid = jnp.abs(kpos - qpos) <= WINDOW
        outs = {}
        for hk in range(SWA_KV_HEADS):
            qst = jnp.concatenate([head_q(hk, g) for g in range(SWA_GROUP)], axis=0).astype(BF16)
            sink = jnp.concatenate(
                [jnp.broadcast_to(sink_ref[hk * SWA_GROUP + g:hk * SWA_GROUP + g + 1, 0:1], (WINDOW, 1))
                 for g in range(SWA_GROUP)], axis=0)
            s_c = _dot_nt(qst, kc)
            m = jnp.maximum(jnp.max(s_c, -1, keepdims=True), sink)
            if local:
                s_l = jnp.where(valid, _dot_nt(qst, kl), NEG_INF)
                m = jnp.maximum(m, jnp.max(s_l, -1, keepdims=True))
            p_c = jnp.exp(s_c - m)
            l = jnp.sum(p_c, -1, keepdims=True) + jnp.exp(sink - m)
            acc = _dot(p_c.astype(BF16), vc)
            if local:
                p_l = jnp.exp(s_l - m)
                l = l + jnp.sum(p_l, -1, keepdims=True)
                acc = acc + _dot(p_l.astype(BF16), vl)
            o = acc / l
            for g in range(SWA_GROUP):
                res = o[g * WINDOW:(g + 1) * WINDOW, :]
                if g % 2 != hk:
                    res = pltpu.roll(res, 64, 1)
                outs[hk * SWA_GROUP + g] = res
        for hb in range(SWA_HEADS // 2):
            blk = jnp.where(lane < 64, outs[2 * hb], outs[2 * hb + 1])
            o_ref[0, rows, hb * LANE:(hb + 1) * LANE] = blk.astype(o_ref.dtype)

    @pl.when(t < n_ctx_steps)
    def _():
        for sb in range(nsb):
            run(False, sb)

    @pl.when(t >= n_ctx_steps)
    def _():
        for sb in range(nsb):
            run(True, sb)


def _swa(p, cosf, sinf, sinkv, ctx):
    b, ta, _ = p.shape
    qb = SEG_SW // 512
    kb = (SEG_SW + 512) // LANE
    nsb = 2 if (ctx % (2 * WINDOW) == 0 and ta % (2 * WINDOW) == 0) else 1
    tq = nsb * WINDOW
    return pl.pallas_call(
        functools.partial(_swa_kernel, ctx=ctx, nsb=nsb),
        grid=(b, ta // tq),
        in_specs=[pl.BlockSpec((1, tq, 512), lambda bb, t: (bb, t, qb)),
                  pl.BlockSpec((1, ta, LANE), lambda bb, t: (bb, 0, kb)),
                  pl.BlockSpec((1, ta, LANE), lambda bb, t: (bb, 0, kb + 1)),
                  pl.BlockSpec((tq, LANE), lambda bb, t: (t, 0)),
                  pl.BlockSpec((tq, LANE), lambda bb, t: (t, 0)),
                  pl.BlockSpec((ta, LANE), lambda bb, t: (0, 0)),
                  pl.BlockSpec((ta, LANE), lambda bb, t: (0, 0)),
                  pl.BlockSpec((SWA_HEADS, LANE), lambda bb, t: (0, 0))],
        out_specs=pl.BlockSpec((1, tq, 512), lambda bb, t: (bb, t, 0)),
        out_shape=jax.ShapeDtypeStruct((b, ta, 512), BF16),
        scratch_shapes=[pltpu.VMEM((ta, LANE), BF16), pltpu.VMEM((ta, LANE), BF16)],
        compiler_params=_cparams(("parallel", "arbitrary")),
        name="swa_attn",
    )(p, p, p, cosf, sinf, cosf, sinf, sinkv)


def _softplus(x):
    return jnp.maximum(x, 0.0) + jnp.log(1.0 + jnp.exp(-jnp.abs(x)))


def _rwkv_prep_kernel(z_ref, hp_ref, hn_ref, mu_ref, w0_ref, a0_ref, wl_ref, al_ref, gl_ref, kk_ref, ka_ref,
                      rk_ref, bd_ref, r_o, v_o, kk_o, ld0_o, ld1_o, kd0_o, kd1_o, a0_o, a1_o, bv_o, g_o, zs_ref):
    tm = z_ref.shape[1]
    w = RWKV_W
    z = z_ref[0]
    zs_ref[SUBLANE:SUBLANE + tm, :] = z
    zs_ref[SUBLANE - 1:SUBLANE, :] = hp_ref[0, 0]
    zs_ref[SUBLANE + tm:SUBLANE + tm + 1, :] = hn_ref[0, 0]
    shifted = 0.5 * (zs_ref[SUBLANE - 1:SUBLANE - 1 + tm, :] + zs_ref[SUBLANE + 1:SUBLANE + 1 + tm, :])
    seg = z + (shifted - z) * mu_ref[...]
    r = seg[:, 0:w]
    k = seg[:, w:2 * w]
    v = seg[:, 2 * w:3 * w]
    lo = seg[:, 3 * w:3 * w + LANE]
    gl = seg[:, 3 * w + LANE:3 * w + 2 * LANE]
    hp = dict(precision=HIGHEST)
    bd = bd_ref[...]
    kk = k * kk_ref[...]
    kk = kk / jnp.maximum(jnp.sqrt(_dot(kk * kk, bd, **hp)), 1e-12)
    th = jnp.tanh(lo).astype(BF16)
    lo_b = lo.astype(BF16)
    ksum = jnp.zeros_like(k)
    for d, (ld_o, kd_o, a_o) in enumerate(((ld0_o, kd0_o, a0_o), (ld1_o, kd1_o, a1_o))):
        wd = w0_ref[d:d + 1, :] + _dot(th, wl_ref[d])
        wd = -_softplus(-wd) - 0.5
        a = _sigmoid(a0_ref[d:d + 1, :] + _dot(lo_b, al_ref[d]))
        kd = k * (1.0 + (a - 1.0) * ka_ref[...])
        ld_o[0] = -jnp.exp(wd)
        kd_o[0] = kd
        a_o[0] = a
        ksum = ksum + kd
    r_o[0] = r
    v_o[0] = v
    kk_o[0] = kk
    bv_o[0] = _dot(r * ksum * rk_ref[...], bd, **hp) * v
    g_o[0] = _dot(_sigmoid(gl).astype(BF16), gl_ref[...])


def _rwkv_prep(p, hprev, hnext, mu, w0, a0, wl, al, gl, k_k, k_a, r_k, bd, tm):
    b, ta, _ = p.shape
    w = RWKV_W
    row = lambda bb, i: (bb, i, 0)
    vec = lambda n: pl.BlockSpec((1, n), lambda bb, i: (0, 0))
    out = jax.ShapeDtypeStruct((b, ta, w), F32)
    return pl.pallas_call(
        _rwkv_prep_kernel,
        grid=(b, ta // tm),
        in_specs=[pl.BlockSpec((1, tm, RW_COLS), lambda bb, i: (bb, i, SEG_RW // RW_COLS)),
                  pl.BlockSpec((1, 1, 1, RW_COLS), lambda bb, i: (bb, i, 0, 0)),
                  pl.BlockSpec((1, 1, 1, RW_COLS), lambda bb, i: (bb, i, 0, 0)),
                  vec(RW_COLS),
                  pl.BlockSpec((2, w), lambda bb, i: (0, 0)),
                  pl.BlockSpec((2, w), lambda bb, i: (0, 0)),
                  pl.BlockSpec((2, LANE, w), lambda bb, i: (0, 0, 0)),
                  pl.BlockSpec((2, LANE, w), lambda bb, i: (0, 0, 0)),
                  pl.BlockSpec((LANE, w), lambda bb, i: (0, 0)),
                  vec(w), vec(w), vec(w),
                  pl.BlockSpec((w, w), lambda bb, i: (0, 0))],
        out_specs=[pl.BlockSpec((1, tm, w), row)] * 11,
        out_shape=[out] * 11,
        scratch_shapes=[pltpu.VMEM((tm + 2 * SUBLANE, RW_COLS), F32)],
        compiler_params=_cparams(("parallel", "parallel")),
        name="rwkv_prep",
    )(p, hprev, hnext, mu, w0, a0, wl, al, gl, k_k, k_a, r_k, bd)


def _split3(x):
    hi = x.astype(BF16)
    r1 = x - hi.astype(F32)
    mid = r1.astype(BF16)
    return hi, mid, (r1 - mid.astype(F32)).astype(BF16)


def _split2(x):
    hi = x.astype(BF16)
    return hi, (x - hi.astype(F32)).astype(BF16)


def _mm3(a, b, dot=_dot):
    return dot(a[0], b[0]) + dot(a[0], b[1]) + dot(a[1], b[0])


def _rwkv_chunk_kernel(r_ref, v_ref, kk_ref, ld0_ref, ld1_ref, kd0_ref, kd1_ref, a0_ref, a1_ref, pz_ref, wu_ref):
    C = RWKV_CHUNK
    hd = RWKV_HEAD
    row = lax.broadcasted_iota(jnp.int32, (C, C), 0)
    col = lax.broadcasted_iota(jnp.int32, (C, C), 1)
    eye = (row == col).astype(F32)
    probs = []
    for d, (ld_ref, kd_ref, a_ref) in enumerate(((ld0_ref, kd0_ref, a0_ref), (ld1_ref, kd1_ref, a1_ref))):
        incl = (row >= col) if d == 0 else (row <= col)
        strict = (row > col) if d == 0 else (row < col)
        last = C - 1 if d == 0 else 0
        for h in range(RWKV_HEADS):
            sl = slice(h * hd, (h + 1) * hd)
            probs.append(dict(d=d, h=h, incl=incl, strict=strict, last=last,
                              ones=jnp.where(incl, 1.0, 0.0).astype(BF16),
                              r=r_ref[0, :, sl], v=v_ref[0, :, sl], kk=kk_ref[0, :, sl],
                              ld=ld_ref[0, :, sl], k=kd_ref[0, :, sl], a=a_ref[0, :, sl]))
    bf = lambda z: z.astype(BF16)
    for q in probs:
        q["cs"] = sum(_dot(q["ones"], part) for part in _split3(q["ld"]))
    for q in probs:
        cs = q["cs"]
        e_cs = jnp.exp(cs)
        e_inv = jnp.exp(-cs)
        q["g_end"] = e_cs[q["last"]:q["last"] + 1, :]
        q["kkd"] = q["kk"] * jnp.exp(cs - q["ld"])
        q["rd"] = q["r"] * e_cs
        q["bi"] = bf(q["kk"] * q["a"] * e_inv)
        q["ki"] = bf(q["k"] * e_inv)
        q["lhs"] = bf(jnp.concatenate([q["kkd"], q["rd"]], axis=0))
    for q in probs:
        gb = _dot_nt(q["lhs"], q["bi"])
        gk = _dot_nt(q["lhs"], q["ki"])
        q["pw"] = -jnp.where(q["strict"], gb[:C], 0.0)
        q["q_b"] = bf(jnp.where(q["incl"], gb[C:], 0.0))
        q["akqk"] = bf(jnp.concatenate([jnp.where(q["strict"], gk[:C], 0.0),
                                        jnp.where(q["incl"], gk[C:], 0.0)], axis=0))
        q["tinv"] = eye + q["pw"]
    for _ in range(5):
        for q in probs:
            pw = bf(q["pw"])
            q["pw"] = _dot(pw, pw)
        for q in probs:
            q["tinv"] = q["tinv"] + _dot(bf(q["tinv"]), bf(q["pw"]))
    for q in probs:
        q["vs"] = bf(q["v"])
        q["av"] = _dot(q["akqk"], q["vs"])
    for q in probs:
        tinv = bf(q["tinv"])
        q["w1"] = _dot(tinv, bf(q["kkd"]))
        q["u1"] = _dot(tinv, bf(q["av"][:C]))
    for q in probs:
        w1 = bf(q["w1"])
        u1 = bf(q["u1"])
        w2 = q["rd"] - _dot(q["q_b"], w1)
        u2 = q["av"][C:] - _dot(q["q_b"], u1)
        pt = (eye - _dot_tn(w1, q["bi"])) * q["g_end"]
        zt = (_dot_tn(q["vs"], q["ki"]) - _dot_tn(u1, q["bi"])) * q["g_end"]
        pz_ref[q["d"], 0, 0, q["h"]] = jnp.concatenate([pt, zt], axis=1)
        wu_ref[q["d"], 0, 0, q["h"]] = jnp.concatenate([w2, u2], axis=1)


def _rwkv_chunks(r, v, kk, ld0, ld1, kd0, kd1, a0, a1):
    b, ta, w = r.shape
    nc = ta // RWKV_CHUNK
    spec = pl.BlockSpec((1, RWKV_CHUNK, w), lambda bb, c: (bb, c, 0))
    ospec = pl.BlockSpec((2, 1, 1, RWKV_HEADS, RWKV_HEAD, 2 * RWKV_HEAD), lambda bb, c: (0, bb, c, 0, 0, 0))
    oshape = jax.ShapeDtypeStruct((2, b, nc, RWKV_HEADS, RWKV_HEAD, 2 * RWKV_HEAD), F32)
    return pl.pallas_call(
        _rwkv_chunk_kernel,
        grid=(b, nc),
        in_specs=[spec] * 9,
        out_specs=[ospec, ospec],
        out_shape=[oshape, oshape],
        compiler_params=_cparams(("parallel", "parallel")),
        name="rwkv_chunks",
    )(r, v, kk, ld0, ld1, kd0, kd1, a0, a1)


def _rwkv_state_kernel(pz0_ref, pz1_ref, wu0_ref, wu1_ref, y0_ref, y1_ref, st_ref):
    i = pl.program_id(0)
    hd = RWKV_HEAD
    nb = st_ref.shape[1]

    @pl.when(i == 0)
    def _():
        st_ref[...] = jnp.zeros_like(st_ref)

    probs = []
    for d, (pz_ref, wu_ref) in enumerate(((pz0_ref, wu0_ref), (pz1_ref, wu1_ref))):
        for bb in range(nb):
            for h in range(RWKV_HEADS):
                pz = pz_ref[0, bb, 0, h]
                wu = wu_ref[0, bb, 0, h]
                probs.append(dict(d=d, b=bb, h=h, pt=_split2(pz[:, :hd]), zt=pz[:, hd:],
                                  w2=_split2(wu[:, :hd]), u2=wu[:, hd:], s=_split2(st_ref[d, bb, h])))
    for q in probs:
        q["y"] = _mm3(q["w2"], q["s"], _dot_nt) + q["u2"]
    for q in probs:
        st_ref[q["d"], q["b"], q["h"]] = _mm3(q["s"], q["pt"]) + q["zt"]
    for d, y_ref in enumerate((y0_ref, y1_ref)):
        for bb in range(nb):
            ys = [q["y"] for q in probs if q["d"] == d and q["b"] == bb]
            for hp in range(RWKV_HEADS // 2):
                y_ref[bb, :, hp * LANE:(hp + 1) * LANE] = jnp.concatenate([ys[2 * hp], ys[2 * hp + 1]], axis=1)


def _rwkv_state(pz, wu, ctx):
    _, b, nc, nh, hd, _ = pz.shape
    ncc = ctx // RWKV_CHUNK

    def rev(i):
        return jnp.where(i < ncc, ncc - 1 - i, nc - 1 - (i - ncc))

    blk = (1, b, 1, nh, hd, 2 * hd)
    fwd = lambda i: (0, 0, i, 0, 0, 0)
    bwd = lambda i: (1, 0, rev(i), 0, 0, 0)
    yshape = jax.ShapeDtypeStruct((b, nc * RWKV_CHUNK, nh * hd), F32)
    return pl.pallas_call(
        _rwkv_state_kernel,
        grid=(nc,),
        in_specs=[pl.BlockSpec(blk, fwd), pl.BlockSpec(blk, bwd), pl.BlockSpec(blk, fwd), pl.BlockSpec(blk, bwd)],
        out_specs=[pl.BlockSpec((b, RWKV_CHUNK, nh * hd), lambda i: (0, i, 0)),
                   pl.BlockSpec((b, RWKV_CHUNK, nh * hd), lambda i: (0, rev(i), 0))],
        out_shape=[yshape, yshape],
        scratch_shapes=[pltpu.VMEM((2, b, nh, hd, hd), F32)],
        compiler_params=_cparams(("arbitrary",)),
        name="rwkv_state",
    )(pz, pz, wu, wu)


def _rwkv_out_kernel(y0_ref, y1_ref, bv_ref, g_ref, lg_ref, lb_ref, bd_ref, o_ref):
    hp = dict(precision=HIGHEST)
    bdm = bd_ref[...] * (1.0 / RWKV_HEAD)
    y = y0_ref[0] + y1_ref[0]
    yc = y - _dot(y, bdm, **hp)
    var = _dot(yc * yc, bdm, **hp)
    yn = yc * lax.rsqrt(var + RWKV_GN_EPS) * lg_ref[...] + lb_ref[...]
    o_ref[0] = ((yn + bv_ref[0]) * g_ref[0]).astype(o_ref.dtype)


def _rwkv_out(y0, y1, bv, g, ln_g, ln_b, bd, tm):
    b, ta, w = y0.shape
    row = pl.BlockSpec((1, tm, w), lambda bb, i: (bb, i, 0))
    vec = pl.BlockSpec((1, w), lambda bb, i: (0, 0))
    return pl.pallas_call(
        _rwkv_out_kernel,
        grid=(b, ta // tm),
        in_specs=[row, row, row, row, vec, vec, pl.BlockSpec((w, w), lambda bb, i: (0, 0))],
        out_specs=row,
        out_shape=jax.ShapeDtypeStruct((b, ta, w), BF16),
        compiler_params=_cparams(("parallel", "parallel")),
        name="rwkv_out",
    )(y0, y1, bv, g, ln_g, ln_b, bd)


def _rwkv_mixer(p, ctx, mu, w0, w_lora, a0, a_lora, g_lora, k_k, k_a, r_k, ln_g, ln_b):
    b, ta, _ = p.shape
    w = RWKV_W
    tm = _tile(math.gcd(ta, ctx), 256)
    nt = ta // tm
    edge_lo = p[:, 0::tm, SEG_RW:]
    edge_hi = p[:, tm - 1::tm, SEG_RW:]
    tile_id = jnp.arange(nt)[None, :, None]
    hprev = jnp.where((tile_id == 0) | (tile_id == ctx // tm), 0.0, jnp.roll(edge_hi, 1, axis=1))
    hnext = jnp.where((tile_id == nt - 1) | (tile_id == ctx // tm - 1), 0.0, jnp.roll(edge_lo, -1, axis=1))
    def rows_at(mat, start):
        return jnp.pad(mat.astype(BF16), ((start, LANE - start - mat.shape[0]), (0, 0)))

    wl = jnp.stack([rows_at(w_lora[d], d * DECAY_LORA) for d in range(2)])
    al = jnp.stack([rows_at(a_lora[d], 2 * DECAY_LORA + d * ICLR_LORA) for d in range(2)])
    gl = rows_at(g_lora, 0)
    mu_p = jnp.pad(mu, (0, RW_COLS - RWKV_STREAM)).reshape(1, RW_COLS)
    head_id = jnp.arange(w) // RWKV_HEAD
    bd = (head_id[:, None] == head_id[None, :]).astype(F32)
    r, v, kk, ld0, ld1, kd0, kd1, a0_, a1_, bv, g = _rwkv_prep(
        p, hprev[:, :, None, :], hnext[:, :, None, :], mu_p, w0, a0, wl, al, gl,
        k_k.reshape(1, w), k_a.reshape(1, w), r_k.reshape(1, w), bd, tm)
    pz, wu = _rwkv_chunks(r, v, kk, ld0, ld1, kd0, kd1, a0_, a1_)
    y0, y1 = _rwkv_state(pz, wu, ctx)
    return _rwkv_out(y0, y1, bv, g, ln_g.reshape(1, w), ln_b.reshape(1, w), bd, tm)


def _merge_a_kernel(x_ref, ml_ref, mc_ref, o4_ref, wg_ref, wb_ref, out_ref, xn_ref, acc_ref, *, tm, ctx):
    i = pl.program_id(1)
    j = pl.program_id(2)
    br = pl.program_id(3)

    @pl.when((j == 0) & (br == 0))
    def _():
        _fill_modulated(xn_ref, x_ref, ml_ref[0], mc_ref[0], i * tm, ctx)

    val = _sigmoid(_dot(xn_ref[...], wg_ref[...])) * _dot(o4_ref[0, 0], wb_ref[0])

    @pl.when(br == 0)
    def _():
        acc_ref[...] = val

    @pl.when(br > 0)
    def _():
        acc_ref[...] += val

    @pl.when(br == N_BRANCH - 1)
    def _():
        out_ref[0] = acc_ref[...].astype(out_ref.dtype)


def _gate_window_weights(w_in_l, gate_off, w_branch_l, w_out_l):
    d = w_in_l.shape[0]
    shift = gate_off % LANE
    start = gate_off - shift
    gw = d + 2 * LANE
    tail = gw - d - shift
    wsrc = w_in_l[:, start:].astype(BF16)
    wsrc = jnp.pad(wsrc, ((0, 0), (0, (N_BRANCH - 1) * d + gw - wsrc.shape[1])))
    wg = jnp.concatenate([wsrc[:, br * d:br * d + gw] for br in range(N_BRANCH)], axis=1)
    wb = jnp.pad(w_branch_l.astype(BF16), ((0, 0), (0, 0), (shift, tail)))
    wo = jnp.pad(w_out_l.astype(BF16), ((shift, tail), (0, 0)))
    return wg, wb, wo


def _merge_a(x_all, modl, modc, o4, wg, wb, ctx):
    b, ta, d = x_all.shape
    gw = wb.shape[2]
    tm = _tile(ta, 1088)
    tn = _tile(gw, 768, LANE)
    return pl.pallas_call(
        functools.partial(_merge_a_kernel, tm=tm, ctx=ctx),
        grid=(b, ta // tm, gw // tn, N_BRANCH),
        in_specs=[pl.BlockSpec((1, tm, d), lambda bb, i, j, br: (bb, i, 0)),
                  pl.BlockSpec((1, 6, d), lambda bb, i, j, br: (bb, 0, 0)),
                  pl.BlockSpec((1, 6, d), lambda bb, i, j, br: (0, 0, 0)),
                  pl.BlockSpec((1, 1, tm, BRANCH_W), lambda bb, i, j, br: (br, bb, i, 0)),
                  pl.BlockSpec((d, tn), lambda bb, i, j, br: (0, br * (gw // tn) + j)),
                  pl.BlockSpec((1, BRANCH_W, tn), lambda bb, i, j, br: (br, 0, j))],
        out_specs=pl.BlockSpec((1, tm, tn), lambda bb, i, j, br: (bb, i, j)),
        out_shape=jax.ShapeDtypeStruct((b, ta, gw), BF16),
        scratch_shapes=[pltpu.VMEM((tm, d), BF16), pltpu.VMEM((tm, tn), F32)],
        compiler_params=_cparams(("parallel", "parallel", "arbitrary", "arbitrary"), 56),
        name="merge_gate",
    )(x_all, modl, modc, o4, wg, wb)


def _merge_b_kernel(m_ref, w_ref, x_ref, ml_ref, mc_ref, lg_ref, lb_ref, xo_ref, h2_ref, *, tm, ctx, alpha):
    i = pl.program_id(1)
    isc = _row_is_ctx(tm, i * tm, ctx)
    ml = ml_ref[0]
    mc = mc_ref[0]
    y = _dot(m_ref[0], w_ref[...])
    g1 = jnp.where(isc, mc[2:3], ml[2:3])
    xn = _ln(alpha * x_ref[0] + g1 * y, POST_LN_EPS) * lg_ref[...] + lb_ref[...]
    xo_ref[0] = xn
    h2_ref[0] = _pack_halves(_modulated(xn, ml, mc, 3, 4, isc))


def _merge_b(merged, w_out, x_all, modl, modc, ln_g, ln_b, ctx, alpha):
    b, ta, d = x_all.shape
    gw = w_out.shape[0]
    tm = _tile(ta, 272)
    row = lambda bb, i: (bb, i, 0)
    return pl.pallas_call(
        functools.partial(_merge_b_kernel, tm=tm, ctx=ctx, alpha=alpha),
        grid=(b, ta // tm),
        in_specs=[pl.BlockSpec((1, tm, gw), row),
                  pl.BlockSpec((gw, d), lambda bb, i: (0, 0)),
                  pl.BlockSpec((1, tm, d), row),
                  pl.BlockSpec((1, 6, d), lambda bb, i: (bb, 0, 0)),
                  pl.BlockSpec((1, 6, d), lambda bb, i: (0, 0, 0)),
                  pl.BlockSpec((1, d), lambda bb, i: (0, 0)),
                  pl.BlockSpec((1, d), lambda bb, i: (0, 0))],
        out_specs=[pl.BlockSpec((1, tm, d), row), pl.BlockSpec((1, tm, d // 2), row)],
        out_shape=[jax.ShapeDtypeStruct((b, ta, d), F32), jax.ShapeDtypeStruct((b, ta, d // 2), jnp.uint32)],
        compiler_params=_cparams(("parallel", "parallel")),
        name="merge_out_ln",
    )(merged, w_out, x_all, modl, modc, ln_g.reshape(1, d), ln_b.reshape(1, d))


def _router_kernel(x_ref, ml_ref, mc_ref, rw_ref, rb_ref, idx_ref, wt_ref, rank_ref, cnt_ref, carry_ref,
                   *, tm, ctx):
    i = pl.program_id(1)

    @pl.when((pl.program_id(0) == 0) & (i == 0))
    def _():
        carry_ref[...] = jnp.zeros_like(carry_ref)

    isc = _row_is_ctx(tm, i * tm, ctx)
    h2 = _modulated(x_ref[0], ml_ref[0], mc_ref[0], 3, 4, isc)
    scores = _sigmoid(_dot(h2, rw_ref[...], precision=HIGHEST))
    biased = scores + rb_ref[...]
    lane = lax.broadcasted_iota(jnp.int32, scores.shape, 1)
    grp = lane // GROUP_SIZE
    ninf = -jnp.inf

    def first_argmax(z):
        m = jnp.max(z, -1, keepdims=True)
        idx = jnp.min(jnp.where(z == m, lane, N_EXPERTS), -1, keepdims=True)
        return m, idx

    gscore = jnp.zeros_like(biased)
    for g in range(N_GROUPS):
        zg = jnp.where(grp == g, biased, ninf)
        m1, i1 = first_argmax(zg)
        m2 = jnp.max(jnp.where(lane == i1, ninf, zg), -1, keepdims=True)
        gscore = jnp.where(grp == g, m1 + m2, gscore)
    cand = jnp.where(lane % GROUP_SIZE == 0, gscore, ninf)
    gsel = jnp.zeros(scores.shape, jnp.bool_)
    for _ in range(TOPK_GROUPS):
        _, ig = first_argmax(cand)
        gsel = gsel | (grp == ig // GROUP_SIZE)
        cand = jnp.where(lane == ig, ninf, cand)
    masked = jnp.where(gsel, biased, ninf)
    out_lane = lax.broadcasted_iota(jnp.int32, (tm, LANE), 1)
    idx_out = jnp.zeros((tm, LANE), jnp.int32)
    w_out = jnp.zeros((tm, LANE), F32)
    w_sum = jnp.zeros((tm, 1), F32)
    chosen = []
    picked = jnp.zeros(scores.shape, F32)
    for kq in range(TOP_K):
        _, ie = first_argmax(masked)
        sel = lane == ie
        wk = jnp.sum(jnp.where(sel, scores, 0.0), -1, keepdims=True)
        masked = jnp.where(sel, ninf, masked)
        idx_out = jnp.where(out_lane == kq, ie, idx_out)
        w_out = jnp.where(out_lane == kq, wk, w_out)
        w_sum = w_sum + wk
        chosen.append(sel)
        picked = jnp.where(sel, 1.0, picked)
    idx_ref[0] = idx_out
    wt_ref[0] = w_out / w_sum * ROUTED_SCALE
    r_i = lax.broadcasted_iota(jnp.int32, (tm, tm), 0)
    c_i = lax.broadcasted_iota(jnp.int32, (tm, tm), 1)
    before = _dot(jnp.where(r_i > c_i, 1.0, 0.0).astype(BF16), picked.astype(BF16)) + carry_ref[...]
    rank_out = jnp.zeros((tm, LANE), jnp.int32)
    for kq in range(TOP_K):
        rk = jnp.sum(jnp.where(chosen[kq], before, 0.0), -1, keepdims=True)
        rank_out = jnp.where(out_lane == kq, rk.astype(jnp.int32), rank_out)
    rank_ref[0] = rank_out
    carry_ref[...] += jnp.sum(picked, axis=0, keepdims=True)
    cnt_ref[...] = carry_ref[...].astype(jnp.int32)


def _router(x_all, modl, modc, rw, rb, ctx):
    b, ta, d = x_all.shape
    tm = _tile(ta, 544)
    row = lambda bb, i: (bb, i, 0)
    tok = jax.ShapeDtypeStruct((b, ta, LANE), jnp.int32)
    return pl.pallas_call(
        functools.partial(_router_kernel, tm=tm, ctx=ctx),
        grid=(b, ta // tm),
        in_specs=[pl.BlockSpec((1, tm, d), row),
                  pl.BlockSpec((1, 6, d), lambda bb, i: (bb, 0, 0)),
                  pl.BlockSpec((1, 6, d), lambda bb, i: (0, 0, 0)),
                  pl.BlockSpec((d, N_EXPERTS), lambda bb, i: (0, 0)),
                  pl.BlockSpec((1, N_EXPERTS), lambda bb, i: (0, 0))],
        out_specs=[pl.BlockSpec((1, tm, LANE), row), pl.BlockSpec((1, tm, LANE), row),
                   pl.BlockSpec((1, tm, LANE), row), pl.BlockSpec((1, N_EXPERTS), lambda bb, i: (0, 0))],
        out_shape=[tok, jax.ShapeDtypeStruct((b, ta, LANE), F32), tok,
                   jax.ShapeDtypeStruct((1, N_EXPERTS), jnp.int32)],
        scratch_shapes=[pltpu.VMEM((1, N_EXPERTS), F32)],
        compiler_params=_cparams(("arbitrary", "arbitrary")),
        name="router",
    )(x_all, modl, modc, rw, rb.reshape(1, N_EXPERTS))


def _row_copy(src_ref, src_row, dst_ref, dst_row, sem):
    return pltpu.make_async_copy(src_ref.at[pl.ds(src_row, 1)], dst_ref.at[pl.ds(dst_row, 1)], sem)


def _dispatch_kernel(pos_ref, h2_ref, xg_in_ref, xg_ref, sem, *, tm):
    del xg_in_ref

    def issue(t, carry):
        for kq in range(TOP_K):
            _row_copy(h2_ref, t, xg_ref, pos_ref[0, 0, t * TOP_K + kq], sem).start()
        return carry

    lax.fori_loop(0, tm, issue, 0, unroll=4)
    for _ in range(TOP_K):
        pltpu.make_async_copy(h2_ref, xg_ref.at[pl.ds(0, tm)], sem).wait()


def _dispatch(pos, h2p, n_rows, tm):
    n, half = h2p.shape
    xg0 = jnp.zeros((n_rows, half), jnp.uint32)
    return pl.pallas_call(
        functools.partial(_dispatch_kernel, tm=tm),
        grid=(n // tm,),
        in_specs=[pl.BlockSpec((1, 1, tm * TOP_K), lambda i: (i, 0, 0), memory_space=pltpu.SMEM),
                  pl.BlockSpec((tm, half), lambda i: (i, 0)),
                  pl.BlockSpec(memory_space=pl.ANY)],
        out_specs=pl.BlockSpec(memory_space=pl.ANY),
        out_shape=jax.ShapeDtypeStruct((n_rows, half), jnp.uint32),
        scratch_shapes=[pltpu.SemaphoreType.DMA(())],
        input_output_aliases={2: 0},
        compiler_params=_cparams(("arbitrary",)),
        name="moe_dispatch",
    )(pos, h2p, xg0)


def _swiglu(x_lo, x_hi, wgu, wdn, ff):
    half = x_lo.shape[1]
    h = _dot(x_lo, wgu[:half, :]) + _dot(x_hi, wgu[half:, :])
    g = h[:, :ff]
    return _dot((g * _sigmoid(g) * h[:, ff:]).astype(BF16), wdn)


def _experts_kernel(be_ref, nu_ref, first_ref, slot_ref, nxt_ref, x_ref, wgu_hbm, wdn_hbm, o_ref,
                    gu_buf, dn_buf, wgu_s, wdn_s, sems, *, ff, l):
    i = pl.program_id(0)

    def weight_copies(e, s):
        return (pltpu.make_async_copy(wgu_hbm.at[l, e], gu_buf.at[s], sems.at[0, s]),
                pltpu.make_async_copy(wdn_hbm.at[l, e], dn_buf.at[s], sems.at[1, s]))

    @pl.when(i == 0)
    def _():
        for cp in weight_copies(be_ref[0], slot_ref[0]):
            cp.start()

    @pl.when(first_ref[i] == 1)
    def _():
        s = slot_ref[i]
        for cp in weight_copies(be_ref[i], s):
            cp.wait()
        wgu_s[...] = gu_buf[s].astype(BF16)
        wdn_s[...] = dn_buf[s].astype(BF16)

        @pl.when(nxt_ref[i] >= 0)
        def _():
            for cp in weight_copies(nxt_ref[i], 1 - s):
                cp.start()

    @pl.when(i < nu_ref[0])
    def _():
        lo, hi = _unpack_halves(x_ref[...])
        o_ref[...] = _pack_halves(_swiglu(lo.astype(BF16), hi.astype(BF16), wgu_s[...], wdn_s[...], ff))

    @pl.when(i >= nu_ref[0])
    def _():
        o_ref[...] = jnp.zeros_like(o_ref)


def _experts(xg, blk_e, n_used, w_gu, w_dn, l, blk):
    n_rows, half = xg.shape
    d = 2 * half
    ff = w_dn.shape[2]
    n_blocks = n_rows // blk
    first = jnp.concatenate([jnp.ones((1,), jnp.int32), (blk_e[1:] != blk_e[:-1]).astype(jnp.int32)])
    slot = (jnp.cumsum(first) - 1) % 2
    n_le = jnp.sum(blk_e[None, :] <= blk_e[:, None], axis=1)
    pick = jnp.arange(n_blocks)[None, :] == n_le[:, None]
    nxt = jnp.where(n_le < n_blocks, jnp.sum(jnp.where(pick, blk_e[None, :], 0), axis=1), -1)
    grid_spec = pltpu.PrefetchScalarGridSpec(
        num_scalar_prefetch=5,
        grid=(n_blocks,),
        in_specs=[pl.BlockSpec((blk, half), lambda i, *_: (i, 0)),
                  pl.BlockSpec(memory_space=pl.ANY),
                  pl.BlockSpec(memory_space=pl.ANY)],
        out_specs=pl.BlockSpec((blk, half), lambda i, *_: (i, 0)),
        scratch_shapes=[pltpu.VMEM((2, d, 2 * ff), F32), pltpu.VMEM((2, ff, d), F32),
                        pltpu.VMEM((d, 2 * ff), BF16), pltpu.VMEM((ff, d), BF16),
                        pltpu.SemaphoreType.DMA((2, 2))],
    )
    return pl.pallas_call(
        functools.partial(_experts_kernel, ff=ff, l=l),
        grid_spec=grid_spec,
        out_shape=jax.ShapeDtypeStruct((n_rows, half), jnp.uint32),
        compiler_params=_cparams(("arbitrary",), 56),
        name="moe_experts",
    )(blk_e, n_used, first, slot.astype(jnp.int32), nxt.astype(jnp.int32), xg, w_gu, w_dn)


def _moe_plan(idx, rank, counts, blk):
    n_assign = idx.size
    padded = (counts + blk - 1) // blk * blk
    pad_end = jnp.cumsum(padded)
    start_pad = pad_end - padded
    experts = jnp.arange(N_EXPERTS, dtype=jnp.int32)
    start_of = jnp.sum(jnp.where(idx[..., None] == experts, start_pad, 0), axis=-1)
    pos = (start_of + rank).astype(jnp.int32)
    n_blocks = -(-(n_assign + N_EXPERTS * (blk - 1)) // blk)
    blk_start = jnp.arange(n_blocks, dtype=jnp.int32) * blk
    blk_e = jnp.minimum(jnp.sum(blk_start[:, None] >= pad_end[None, :], axis=-1), N_EXPERTS - 1).astype(jnp.int32)
    n_used = (pad_end[-1] // blk).astype(jnp.int32).reshape(1)
    return pos, blk_e, n_used, n_blocks * blk


def _combine_kernel(pos_ref, y_ref, wt_ref, h2_ref, sgu_ref, sdn_ref, x_ref, ml_ref, mc_ref, lg_ref, lb_ref,
                    xo_ref, buf, sem, *, tm, ctx, alpha, ff, t0):
    i = pl.program_id(1) + t0

    def issue(t, carry):
        for kq in range(TOP_K):
            _row_copy(y_ref, pos_ref[0, 0, t * TOP_K + kq], buf.at[kq], t, sem).start()
        return carry

    lax.fori_loop(0, tm, issue, 0, unroll=4)
    lo, hi = _unpack_halves(h2_ref[0])
    f = _swiglu(lo.astype(BF16), hi.astype(BF16), sgu_ref[...], sdn_ref[...], ff)
    for kq in range(TOP_K):
        pltpu.make_async_copy(y_ref.at[pl.ds(0, tm)], buf.at[kq], sem).wait()
    wt = wt_ref[0]
    half = f.shape[1] // 2
    f_lo = f[:, :half]
    f_hi = f[:, half:]
    for kq in range(TOP_K):
        lo, hi = _unpack_halves(buf[kq])
        f_lo = f_lo + wt[:, kq:kq + 1] * lo
        f_hi = f_hi + wt[:, kq:kq + 1] * hi
    f = jnp.concatenate([f_lo, f_hi], axis=1)
    isc = _row_is_ctx(tm, i * tm, ctx)
    g2 = jnp.where(isc, mc_ref[0][5:6], ml_ref[0][5:6])
    xo_ref[0] = _ln(alpha * x_ref[0] + g2 * f, POST_LN_EPS) * lg_ref[...] + lb_ref[...]


def _combine(pos, y_rows, wts, h2p, sh_gu, sh_dn, x_all, modl, modc, ln_g, ln_b, ctx, alpha, tm, skip_ctx):
    b, ta, d = x_all.shape
    ff = sh_dn.shape[0]
    nt = ta // tm
    t0 = ctx // tm if skip_ctx else 0
    row = lambda bb, i: (bb, i + t0, 0)
    return pl.pallas_call(
        functools.partial(_combine_kernel, tm=tm, ctx=ctx, alpha=alpha, ff=ff, t0=t0),
        grid=(b, nt - t0),
        in_specs=[pl.BlockSpec((1, 1, tm * TOP_K), lambda bb, i: (bb * nt + i + t0, 0, 0),
                               memory_space=pltpu.SMEM),
                  pl.BlockSpec(memory_space=pl.ANY),
                  pl.BlockSpec((1, tm, LANE), row),
                  pl.BlockSpec((1, tm, d // 2), row),
                  pl.BlockSpec((d, 2 * ff), lambda bb, i: (0, 0)),
                  pl.BlockSpec((ff, d), lambda bb, i: (0, 0)),
                  pl.BlockSpec((1, tm, d), row),
                  pl.BlockSpec((1, 6, d), lambda bb, i: (bb, 0, 0)),
                  pl.BlockSpec((1, 6, d), lambda bb, i: (0, 0, 0)),
                  pl.BlockSpec((1, d), lambda bb, i: (0, 0)),
                  pl.BlockSpec((1, d), lambda bb, i: (0, 0))],
        out_specs=pl.BlockSpec((1, tm, d), lambda bb, i: (bb, i, 0)),
        out_shape=jax.ShapeDtypeStruct((b, ta - t0 * tm, d), F32),
        scratch_shapes=[pltpu.VMEM((TOP_K, tm, d // 2), jnp.uint32), pltpu.SemaphoreType.DMA(())],
        compiler_params=_cparams(("arbitrary", "arbitrary")),
        name="moe_combine_ln",
    )(pos, y_rows, wts, h2p, sh_gu, sh_dn, x_all, modl, modc, ln_g.reshape(1, d), ln_b.reshape(1, d))


def _rope_tables(n_lat, ctx):
    rows = n_lat // GRID_W
    row = jnp.repeat(jnp.arange(rows, dtype=F32), GRID_W)
    col = jnp.tile(jnp.arange(GRID_W, dtype=F32), rows)
    n_freq = ROPE_DIM // 4
    inv_freq = ROPE_BASE ** (-jnp.arange(n_freq, dtype=F32) / n_freq)
    ang = jnp.concatenate([row[:, None] * inv_freq, col[:, None] * inv_freq], -1)
    cos, sin = jnp.cos(ang), jnp.sin(ang)
    cos = jnp.concatenate([jnp.ones((ctx, ROPE_DIM // 2), F32), cos], 0)
    sin = jnp.concatenate([jnp.zeros((ctx, ROPE_DIM // 2), F32), sin], 0)
    cosf = jnp.tile(jnp.concatenate([cos, cos], -1), (1, LANE // ROPE_DIM))
    sinf = jnp.tile(jnp.concatenate([-sin, sin], -1), (1, LANE // ROPE_DIM))
    return cosf, sinf


def _pack_small_weights(w_in_l):
    d = w_in_l.shape[0]
    offs = np.cumsum([0, MLA_Q_RANK, MLA_KV_RANK + MLA_ROPE, RWKV_STREAM,
                      (SWA_HEADS + 2 * SWA_KV_HEADS) * SWA_HEAD, 3 * DIFF_HEADS * 2 * DIFF_HEAD])
    placed = sorted(zip([SEG_MQ, SEG_MKV, SEG_RW, SEG_SW, SEG_DF], range(5)))
    parts = []
    cur = 0
    for start, s in placed:
        if start > cur:
            parts.append(jnp.zeros((d, start - cur), BF16))
        parts.append(w_in_l[:, int(offs[s]):int(offs[s + 1])].astype(BF16))
        cur = start + int(offs[s + 1] - offs[s])
    parts.append(jnp.zeros((d, SMALL_COLS - cur), BF16))
    return jnp.concatenate(parts, axis=1), int(offs[5])


def kernel(x, c, ctx, c_ctx, w_mod, b_mod, w_in, mla_q_norm, mla_w_qup, mla_kv_norm, mla_w_kvup, rwkv_mu, rwkv_w0, rwkv_w_lora, rwkv_a0, rwkv_a_lora, rwkv_g_lora, rwkv_k_k, rwkv_k_a, rwkv_r_k, rwkv_ln_g, rwkv_ln_b, swa_sink, diff_lambda, diff_subln, w_branch, w_out, ln1_g, ln1_b, router_w, router_bias, exp_w_gu, exp_w_dn, sh_w_gu, sh_w_dn, ln2_g, ln2_b):
    depth = w_mod.shape[0]
    b, n_lat, d = x.shape
    n_ctx = ctx.shape[1]
    ta = n_ctx + n_lat
    alpha = (2 * depth) ** 0.25
    cosf, sinf = _rope_tables(n_lat, n_ctx)
    x_all = jnp.concatenate([ctx, x], axis=1)

    cond = jnp.concatenate([c, c_ctx[None]], axis=0)
    cond = jnp.pad(jax.nn.silu(cond), ((0, SUBLANE - (b + 1) % SUBLANE), (0, 0)))

    for l in range(depth):
        mod = _matmul_bias(cond, w_mod, b_mod, l).reshape(cond.shape[0], 6, d)
        modl, modc = mod[:b], mod[b:b + 1]

        w_small, gate_off = _pack_small_weights(w_in[l])
        p = _inproj(x_all, modl, modc, w_small, n_ctx)

        wq = mla_w_qup[l].reshape(MLA_Q_RANK, MLA_HEADS, MLA_NOPE + MLA_ROPE)
        wq = jnp.pad(wq, ((0, 0), (0, 0), (0, 256 - MLA_NOPE - MLA_ROPE))).reshape(MLA_Q_RANK, 1024)
        q_a, k_a, v_a = _mla_prep(p, mla_q_norm[l].reshape(1, -1), mla_kv_norm[l].reshape(1, -1),
                                  wq.astype(BF16), mla_w_kvup[l].astype(BF16), cosf, sinf)
        out_a = _mla_attn(q_a, k_a, v_a, n_ctx)

        out_b = _rwkv_mixer(p, n_ctx, rwkv_mu[l], rwkv_w0[l],
                            rwkv_w_lora[l], rwkv_a0[l], rwkv_a_lora[l], rwkv_g_lora[l], rwkv_k_k[l],
                            rwkv_k_a[l], rwkv_r_k[l], rwkv_ln_g[l], rwkv_ln_b[l])

        sinkv = jnp.broadcast_to(swa_sink[l][:, None], (SWA_HEADS, LANE))
        out_s = _swa(p, cosf, sinf, sinkv, n_ctx)

        lf = diff_lambda[l]
        lam_init = 0.8 - 0.6 * math.exp(-0.3 * l)
        lam = jnp.exp(jnp.sum(lf[0] * lf[1])) - jnp.exp(jnp.sum(lf[2] * lf[3])) + lam_init
        out_d = _diff_attn(p, cosf, sinf, jnp.full((1, LANE), lam, F32), diff_subln[l].reshape(1, LANE),
                           n_ctx, lam_init)

        o4 = jnp.stack([out_a, out_b, out_s, out_d], axis=0)
        wg, wb, wo = _gate_window_weights(w_in[l], gate_off, w_branch[l], w_out[l])
        merged = _merge_a(x_all, modl, modc, o4, wg, wb, n_ctx)
        x_all, h2p = _merge_b(merged, wo, x_all, modl, modc, ln1_g[l], ln1_b[l], n_ctx, alpha)

        idx, wts, rank, counts = _router(x_all, modl, modc, router_w[l], router_bias[l], n_ctx)
        pos, blk_e, n_used, n_rows = _moe_plan(idx[..., :TOP_K], rank[..., :TOP_K], counts[0], MOE_BLK)
        tmd = _tile(ta, 128)
        tmc = _tile(math.gcd(ta, n_ctx), 256)
        xg = _dispatch(pos.reshape(b * ta // tmd, 1, tmd * TOP_K), h2p.reshape(b * ta, d // 2), n_rows, tmd)
        y_rows = _experts(xg, blk_e, n_used, exp_w_gu, exp_w_dn, l, MOE_BLK)
        x_all = _combine(pos.reshape(b * ta // tmc, 1, tmc * TOP_K), y_rows, wts, h2p, sh_w_gu[l].astype(BF16),
                         sh_w_dn[l].astype(BF16), x_all, modl, modc, ln2_g[l], ln2_b[l], n_ctx, alpha, tmc,
                         skip_ctx=(l == depth - 1))
    return x_all
```

```python
import functools
import math

import jax
import jax.numpy as jnp
import numpy as np
from jax import lax
from jax.experimental import pallas as pl
from jax.experimental.pallas import tpu as pltpu

F32 = jnp.float32
BF16 = jnp.bfloat16
HIGHEST = lax.Precision.HIGHEST

GRID_W = 64
ROPE_BASE = 10000.0
ROPE_DIM = 64
ADALN_EPS = 1e-6
POST_LN_EPS = 1e-5
NEG_INF = -1e30

MLA_HEADS = 4
MLA_Q_RANK = 384
MLA_KV_RANK = 256
MLA_NOPE = 128
MLA_ROPE = 64
MLA_V = 128

RWKV_HEADS = 8
RWKV_HEAD = 64
RWKV_W = RWKV_HEADS * RWKV_HEAD
DECAY_LORA = 32
ICLR_LORA = 32
GATE_LORA = 96
RWKV_STREAM = 3 * RWKV_W + 2 * DECAY_LORA + 2 * ICLR_LORA + GATE_LORA
RWKV_GN_EPS = 64e-5
RWKV_CHUNK = 64

SWA_HEADS = 8
SWA_KV_HEADS = 2
SWA_GROUP = SWA_HEADS // SWA_KV_HEADS
SWA_HEAD = 64
WINDOW = 128

DIFF_HEADS = 4
DIFF_HEAD = 64

N_BRANCH = 4
BRANCH_W = 512

N_EXPERTS = 64
TOP_K = 6
N_GROUPS = 8
TOPK_GROUPS = 4
GROUP_SIZE = N_EXPERTS // N_GROUPS
ROUTED_SCALE = 2.5
MOE_BLK = 256

LANE = 128
SUBLANE = 8

SEG_SW = 0
SEG_MQ = 768
SEG_MKV = 1152
SEG_DF = 1536
SEG_RW = 3584
RW_COLS = 1792
SMALL_COLS = SEG_RW + RW_COLS


def _tile(n, target, mult=SUBLANE):
    best = None
    for d in range(mult, min(n, target) + 1, mult):
        if n % d == 0:
            best = d
    assert best is not None, (n, target, mult)
    return best


def _cparams(sem, vmem_mb=48):
    return pltpu.CompilerParams(dimension_semantics=sem, vmem_limit_bytes=vmem_mb * 1024 * 1024)


def _sigmoid(x):
    return 1.0 / (1.0 + jnp.exp(-x))


def _ln(x, eps):
    mu = jnp.mean(x, -1, keepdims=True)
    xc = x - mu
    var = jnp.mean(xc * xc, -1, keepdims=True)
    return xc * lax.rsqrt(var + eps)


def _rms(x, eps):
    return x * lax.rsqrt(jnp.mean(x * x, -1, keepdims=True) + eps)


def _row_is_ctx(n_rows, row0, ctx):
    rows = row0 + lax.broadcasted_iota(jnp.int32, (n_rows, 1), 0)
    return rows < ctx


def _modulated(x, ml, mc, sh_i, sc_i, isc):
    sh = jnp.where(isc, mc[sh_i:sh_i + 1], ml[sh_i:sh_i + 1])
    sc = jnp.where(isc, mc[sc_i:sc_i + 1], ml[sc_i:sc_i + 1])
    return _ln(x, ADALN_EPS) * (1.0 + sc) + sh


def _fill_modulated(xn_ref, x_ref, ml, mc, row0, ctx):
    tm = xn_ref.shape[0]
    ch = _tile(tm, 272)
    for r in range(0, tm, ch):
        isc = _row_is_ctx(ch, row0 + r, ctx)
        xn_ref[r:r + ch, :] = _modulated(x_ref[0, r:r + ch, :], ml, mc, 0, 1, isc).astype(BF16)


def _swap32(x):
    lane = lax.broadcasted_iota(jnp.int32, x.shape, 1)
    up = pltpu.roll(x, LANE - 32, 1)
    dn = pltpu.roll(x, 32, 1)
    return jnp.where((lane % 64) < 32, up, dn)


def _rope(x, cosf, sinf):
    return x * cosf + _swap32(x) * sinf


def _pack_halves(x):
    n = x.shape[1] // 2
    lo = pltpu.bitcast(x[:, :n].astype(BF16).astype(F32), jnp.uint32)
    hi = pltpu.bitcast(x[:, n:].astype(BF16).astype(F32), jnp.uint32)
    return (hi & jnp.uint32(0xFFFF0000)) | (lo >> 16)


def _unpack_halves(u):
    lo = pltpu.bitcast(u << 16, F32)
    hi = pltpu.bitcast(u & jnp.uint32(0xFFFF0000), F32)
    return lo, hi


def _dot(a, b, **kw):
    return jnp.dot(a, b, preferred_element_type=F32, **kw)


def _dot_nt(a, b, **kw):
    return lax.dot_general(a, b, (((1,), (1,)), ((), ())), preferred_element_type=F32, **kw)


def _dot_tn(a, b, **kw):
    return lax.dot_general(a, b, (((0,), (0,)), ((), ())), preferred_element_type=F32, **kw)


def _mm_kernel(x_ref, w_ref, b_ref, o_ref):
    w = w_ref[0].astype(BF16)
    o_ref[...] = sum(_dot(part, w) for part in _split3(x_ref[...])) + b_ref[0]


def _matmul_bias(x, w, b, l):
    m, k = x.shape
    n = w.shape[2]
    tn = _tile(n, 1024, LANE)
    return pl.pallas_call(
        _mm_kernel,
        grid=(n // tn,),
        in_specs=[pl.BlockSpec((m, k), lambda j: (0, 0)),
                  pl.BlockSpec((1, k, tn), lambda j: (l, 0, j)),
                  pl.BlockSpec((1, 1, tn), lambda j: (l, 0, j))],
        out_specs=pl.BlockSpec((m, tn), lambda j: (0, j)),
        out_shape=jax.ShapeDtypeStruct((m, n), F32),
        compiler_params=_cparams(("arbitrary",)),
        name="mod_matmul",
    )(x, w, b.reshape(b.shape[0], 1, n))


def _inproj_kernel(x_ref, ml_ref, mc_ref, w_ref, o_ref, xn_ref, *, tm, ctx):
    i = pl.program_id(1)
    j = pl.program_id(2)

    @pl.when(j == 0)
    def _():
        _fill_modulated(xn_ref, x_ref, ml_ref[0], mc_ref[0], i * tm, ctx)

    o_ref[0] = _dot(xn_ref[...], w_ref[...])


def _inproj(x_all, modl, modc, w_small, ctx):
    b, ta, d = x_all.shape
    n = w_small.shape[1]
    tm = _tile(ta, 1088)
    tn = _tile(n, 1024, LANE)
    return pl.pallas_call(
        functools.partial(_inproj_kernel, tm=tm, ctx=ctx),
        grid=(b, ta // tm, n // tn),
        in_specs=[pl.BlockSpec((1, tm, d), lambda bb, i, j: (bb, i, 0)),
                  pl.BlockSpec((1, 6, d), lambda bb, i, j: (bb, 0, 0)),
                  pl.BlockSpec((1, 6, d), lambda bb, i, j: (0, 0, 0)),
                  pl.BlockSpec((d, tn), lambda bb, i, j: (0, j))],
        out_specs=pl.BlockSpec((1, tm, tn), lambda bb, i, j: (bb, i, j)),
        out_shape=jax.ShapeDtypeStruct((b, ta, n), F32),
        scratch_shapes=[pltpu.VMEM((tm, d), BF16)],
        compiler_params=_cparams(("parallel", "parallel", "arbitrary"), 56),
        name="inproj",
    )(x_all, modl, modc, w_small)


def _mla_prep_kernel(mq_ref, mkv_ref, qg_ref, kg_ref, wq_ref, wkv_ref, cos_ref, sin_ref,
                     q_ref, k_ref, v_ref):
    mq = mq_ref[0]
    mkv = mkv_ref[0]
    cosv = cos_ref[...]
    sinv = sin_ref[...]
    qn = _rms(mq, 1e-6) * qg_ref[...]
    q = _dot(qn.astype(BF16), wq_ref[...])
    kvn = _rms(mkv[:, :MLA_KV_RANK], 1e-6) * kg_ref[...]
    kv = _dot(kvn.astype(BF16), wkv_ref[...])
    kr = _rope(mkv[:, MLA_KV_RANK:MLA_KV_RANK + LANE], cosv, sinv).astype(BF16)
    scale = (MLA_NOPE + MLA_ROPE) ** -0.5 * LOG2E
    for h in range(MLA_HEADS):
        c0 = h * 256
        q_ref[0, :, c0:c0 + 128] = (q[:, c0:c0 + 128] * scale).astype(BF16)
        q_ref[0, :, c0 + 128:c0 + 256] = (_rope(q[:, c0 + 128:c0 + 256], cosv, sinv) * scale).astype(BF16)
        k_ref[0, :, c0:c0 + 128] = kv[:, c0:c0 + 128].astype(BF16)
        k_ref[0, :, c0 + 128:c0 + 256] = kr
        v_ref[0, :, h * 128:(h + 1) * 128] = kv[:, c0 + 128:c0 + 256].astype(BF16)


def _mla_prep(p, qg, kg, wq, wkv, cosf, sinf):
    b, ta, _ = p.shape
    tm = _tile(ta, 544)
    return pl.pallas_call(
        _mla_prep_kernel,
        grid=(b, ta // tm),
        in_specs=[pl.BlockSpec((1, tm, 384), lambda bb, i: (bb, i, SEG_MQ // 384)),
                  pl.BlockSpec((1, tm, 384), lambda bb, i: (bb, i, SEG_MKV // 384)),
                  pl.BlockSpec((1, MLA_Q_RANK), lambda bb, i: (0, 0)),
                  pl.BlockSpec((1, MLA_KV_RANK), lambda bb, i: (0, 0)),
                  pl.BlockSpec((MLA_Q_RANK, 1024), lambda bb, i: (0, 0)),
                  pl.BlockSpec((MLA_KV_RANK, 1024), lambda bb, i: (0, 0)),
                  pl.BlockSpec((tm, LANE), lambda bb, i: (i, 0)),
                  pl.BlockSpec((tm, LANE), lambda bb, i: (i, 0))],
        out_specs=[pl.BlockSpec((1, tm, 1024), lambda bb, i: (bb, i, 0)),
                   pl.BlockSpec((1, tm, 1024), lambda bb, i: (bb, i, 0)),
                   pl.BlockSpec((1, tm, 512), lambda bb, i: (bb, i, 0))],
        out_shape=[jax.ShapeDtypeStruct((b, ta, 1024), BF16),
                   jax.ShapeDtypeStruct((b, ta, 1024), BF16),
                   jax.ShapeDtypeStruct((b, ta, 512), BF16)],
        compiler_params=_cparams(("parallel", "parallel")),
        name="mla_prep",
    )(p, p, qg, kg, wq, wkv, cosf, sinf)


LOG2E = 1.4426950408889634


def _softmax_pv(s, v):
    m = jnp.max(s, -1, keepdims=True)
    p = jnp.exp2(s - m)
    l = jnp.sum(p, -1, keepdims=True)
    return _dot(p.astype(BF16), v) / l


def _mla_attn_kernel(q_ref, k_ref, v_ref, o_ref, *, n_ctx_tiles, ctx):
    qi = pl.program_id(2)

    def run(nk):
        s = _dot_nt(q_ref[0], k_ref[0, :nk, :])
        o_ref[0] = _softmax_pv(s, v_ref[0, :nk, :]).astype(o_ref.dtype)

    @pl.when(qi < n_ctx_tiles)
    def _():
        run(ctx)

    @pl.when(qi >= n_ctx_tiles)
    def _():
        run(k_ref.shape[1])


def _mla_attn(q, k, v, ctx):
    b, ta, _ = q.shape
    tq = _tile(math.gcd(ta, ctx), 256)
    return pl.pallas_call(
        functools.partial(_mla_attn_kernel, n_ctx_tiles=ctx // tq, ctx=ctx),
        grid=(b, MLA_HEADS, ta // tq),
        in_specs=[pl.BlockSpec((1, tq, 256), lambda bb, h, i: (bb, i, h)),
                  pl.BlockSpec((1, ta, 256), lambda bb, h, i: (bb, 0, h)),
                  pl.BlockSpec((1, ta, 128), lambda bb, h, i: (bb, 0, h))],
        out_specs=pl.BlockSpec((1, tq, 128), lambda bb, h, i: (bb, i, h)),
        out_shape=jax.ShapeDtypeStruct((b, ta, 512), BF16),
        compiler_params=_cparams(("parallel", "parallel", "arbitrary")),
        name="mla_attn",
    )(q, k, v)


def _diff_attn_kernel(q_ref, k_ref, v_ref, cq_ref, sq_ref, ck_ref, sk_ref, lam_ref, sub_ref,
                      o_ref, ks_ref, vs_ref, *, n_ctx_tiles, ctx, post_scale):
    qi = pl.program_id(2)

    @pl.when(qi == 0)
    def _():
        ks_ref[...] = _rope(k_ref[0], ck_ref[...], sk_ref[...]).astype(BF16)
        vs_ref[...] = v_ref[0].astype(BF16)

    def run(nk):
        q = _rope(q_ref[0], cq_ref[...], sq_ref[...]) * (DIFF_HEAD ** -0.5 * LOG2E)
        lane = lax.broadcasted_iota(jnp.int32, q.shape, 1)
        q1 = jnp.where(lane < 64, q, 0.0).astype(BF16)
        q2 = jnp.where(lane >= 64, q, 0.0).astype(BF16)
        k = ks_ref[:nk, :]
        v = vs_ref[:nk, :]
        o = _softmax_pv(_dot_nt(q1, k), v) - lam_ref[...] * _softmax_pv(_dot_nt(q2, k), v)
        o = _rms(o, 1e-5) * sub_ref[...] * post_scale
        o_ref[0] = o.astype(o_ref.dtype)

    @pl.when(qi < n_ctx_tiles)
    def _():
        run(ctx)

    @pl.when(qi >= n_ctx_tiles)
    def _():
        run(ks_ref.shape[0])


def _diff_attn(p, cosf, sinf, lamv, subln, ctx, lam_init):
    b, ta, _ = p.shape
    tq = _tile(math.gcd(ta, ctx), 256)
    cb = SEG_DF // LANE
    return pl.pallas_call(
        functools.partial(_diff_attn_kernel, n_ctx_tiles=ctx // tq, ctx=ctx, post_scale=1.0 - lam_init),
        grid=(b, DIFF_HEADS, ta // tq),
        in_specs=[pl.BlockSpec((1, tq, LANE), lambda bb, h, i: (bb, i, cb + h)),
                  pl.BlockSpec((1, ta, LANE), lambda bb, h, i: (bb, 0, cb + DIFF_HEADS + h)),
                  pl.BlockSpec((1, ta, LANE), lambda bb, h, i: (bb, 0, cb + 2 * DIFF_HEADS + h)),
                  pl.BlockSpec((tq, LANE), lambda bb, h, i: (i, 0)),
                  pl.BlockSpec((tq, LANE), lambda bb, h, i: (i, 0)),
                  pl.BlockSpec((ta, LANE), lambda bb, h, i: (0, 0)),
                  pl.BlockSpec((ta, LANE), lambda bb, h, i: (0, 0)),
                  pl.BlockSpec((1, LANE), lambda bb, h, i: (0, 0)),
                  pl.BlockSpec((1, LANE), lambda bb, h, i: (0, 0))],
        out_specs=pl.BlockSpec((1, tq, LANE), lambda bb, h, i: (bb, i, h)),
        out_shape=jax.ShapeDtypeStruct((b, ta, 512), BF16),
        scratch_shapes=[pltpu.VMEM((ta, LANE), BF16), pltpu.VMEM((ta, LANE), BF16)],
        compiler_params=_cparams(("parallel", "parallel", "arbitrary")),
        name="diff_attn",
    )(p, p, p, cosf, sinf, cosf, sinf, lamv, subln)


def _swa_kernel(q_ref, k_ref, v_ref, cq_ref, sq_ref, ck_ref, sk_ref, sink_ref,
                o_ref, ks_ref, vs_ref, *, ctx, nsb):
    t = pl.program_id(1)
    n_ctx_steps = ctx // (WINDOW * nsb)
    ta = ks_ref.shape[0]
    span = 3 * WINDOW

    @pl.when(t == 0)
    def _():
        ks_ref[...] = _rope(k_ref[0], ck_ref[...], sk_ref[...]).astype(BF16)
        vs_ref[...] = v_ref[0].astype(BF16)

    lane = lax.broadcasted_iota(jnp.int32, (WINDOW, LANE), 1)

    def run(local, sb):
        rows = slice(sb * WINDOW, (sb + 1) * WINDOW)
        cq = cq_ref[rows, :]
        sq = sq_ref[rows, :]
        qblk = [_rope(q_ref[0, rows, hb * LANE:(hb + 1) * LANE], cq, sq) * (SWA_HEAD ** -0.5)
                for hb in range(SWA_HEADS // 2)]

        def head_q(hk, g):
            h = hk * SWA_GROUP + g
            blk = qblk[h // 2]
            if h % 2 != hk:
                blk = pltpu.roll(blk, 64, 1)
            keep = (lane >= 64) if hk == 1 else (lane < 64)
            return jnp.where(keep, blk, 0.0)

        kc = ks_ref[:ctx, :]
        vc = vs_ref[:ctx, :]
        if local:
            n = (t - n_ctx_steps) * nsb + sb
            ws = jnp.clip(ctx + (n - 1) * WINDOW, ctx, ta - span)
            ws = pl.multiple_of(ws, WINDOW)
            kl = ks_ref[pl.ds(ws, span), :]
            vl = vs_ref[pl.ds(ws, span), :]
            kpos = (ws - ctx) + lax.broadcasted_iota(jnp.int32, (1, span), 1)
            qpos = n * WINDOW + lax.broadcasted_iota(jnp.int32, (SWA_GROUP * WINDOW, 1), 0) % WINDOW
            valid = jnp.abs(kpos - qpos) <= WINDOW
        outs = {}
        for hk in range(SWA_KV_HEADS):
            qst = jnp.concatenate([head_q(hk, g) for g in range(SWA_GROUP)], axis=0).astype(BF16)
            sink = jnp.concatenate(
                [jnp.broadcast_to(sink_ref[hk * SWA_GROUP + g:hk * SWA_GROUP + g + 1, 0:1], (WINDOW, 1))
                 for g in range(SWA_GROUP)], axis=0)
            s_c = _dot_nt(qst, kc)
            m = jnp.maximum(jnp.max(s_c, -1, keepdims=True), sink)
            if local:
                s_l = jnp.where(valid, _dot_nt(qst, kl), NEG_INF)
                m = jnp.maximum(m, jnp.max(s_l, -1, keepdims=True))
            p_c = jnp.exp(s_c - m)
            l = jnp.sum(p_c, -1, keepdims=True) + jnp.exp(sink - m)
            acc = _dot(p_c.astype(BF16), vc)
            if local:
                p_l = jnp.exp(s_l - m)
                l = l + jnp.sum(p_l, -1, keepdims=True)
                acc = acc + _dot(p_l.astype(BF16), vl)
            o = acc / l
            for g in range(SWA_GROUP):
                res = o[g * WINDOW:(g + 1) * WINDOW, :]
                if g % 2 != hk:
                    res = pltpu.roll(res, 64, 1)
                outs[hk * SWA_GROUP + g] = res
        for hb in range(SWA_HEADS // 2):
            blk = jnp.where(lane < 64, outs[2 * hb], outs[2 * hb + 1])
            o_ref[0, rows, hb * LANE:(hb + 1) * LANE] = blk.astype(o_ref.dtype)

    @pl.when(t < n_ctx_steps)
    def _():
        for sb in range(nsb):
            run(False, sb)

    @pl.when(t >= n_ctx_steps)
    def _():
        for sb in range(nsb):
            run(True, sb)


def _swa(p, cosf, sinf, sinkv, ctx):
    b, ta, _ = p.shape
    qb = SEG_SW // 512
    kb = (SEG_SW + 512) // LANE
    nsb = 2 if (ctx % (2 * WINDOW) == 0 and ta % (2 * WINDOW) == 0) else 1
    tq = nsb * WINDOW
    return pl.pallas_call(
        functools.partial(_swa_kernel, ctx=ctx, nsb=nsb),
        grid=(b, ta // tq),
        in_specs=[pl.BlockSpec((1, tq, 512), lambda bb, t: (bb, t, qb)),
                  pl.BlockSpec((1, ta, LANE), lambda bb, t: (bb, 0, kb)),
                  pl.BlockSpec((1, ta, LANE), lambda bb, t: (bb, 0, kb + 1)),
                  pl.BlockSpec((tq, LANE), lambda bb, t: (t, 0)),
                  pl.BlockSpec((tq, LANE), lambda bb, t: (t, 0)),
                  pl.BlockSpec((ta, LANE), lambda bb, t: (0, 0)),
                  pl.BlockSpec((ta, LANE), lambda bb, t: (0, 0)),
                  pl.BlockSpec((SWA_HEADS, LANE), lambda bb, t: (0, 0))],
        out_specs=pl.BlockSpec((1, tq, 512), lambda bb, t: (bb, t, 0)),
        out_shape=jax.ShapeDtypeStruct((b, ta, 512), BF16),
        scratch_shapes=[pltpu.VMEM((ta, LANE), BF16), pltpu.VMEM((ta, LANE), BF16)],
        compiler_params=_cparams(("parallel", "arbitrary")),
        name="swa_attn",
    )(p, p, p, cosf, sinf, cosf, sinf, sinkv)


def _softplus(x):
    return jnp.maximum(x, 0.0) + jnp.log(1.0 + jnp.exp(-jnp.abs(x)))


def _rwkv_prep_kernel(z_ref, hp_ref, hn_ref, mu_ref, w0_ref, a0_ref, wl_ref, al_ref, gl_ref, kk_ref, ka_ref,
                      rk_ref, bd_ref, r_o, v_o, kk_o, ld0_o, ld1_o, kd0_o, kd1_o, a0_o, a1_o, bv_o, g_o, zs_ref):
    tm = z_ref.shape[1]
    w = RWKV_W
    z = z_ref[0]
    zs_ref[SUBLANE:SUBLANE + tm, :] = z
    zs_ref[SUBLANE - 1:SUBLANE, :] = hp_ref[0, 0]
    zs_ref[SUBLANE + tm:SUBLANE + tm + 1, :] = hn_ref[0, 0]
    shifted = 0.5 * (zs_ref[SUBLANE - 1:SUBLANE - 1 + tm, :] + zs_ref[SUBLANE + 1:SUBLANE + 1 + tm, :])
    seg = z + (shifted - z) * mu_ref[...]
    r = seg[:, 0:w]
    k = seg[:, w:2 * w]
    v = seg[:, 2 * w:3 * w]
    lo = seg[:, 3 * w:3 * w + LANE]
    gl = seg[:, 3 * w + LANE:3 * w + 2 * LANE]
    hp = dict(precision=HIGHEST)
    bd = bd_ref[...]
    kk = k * kk_ref[...]
    kk = kk / jnp.maximum(jnp.sqrt(_dot(kk * kk, bd, **hp)), 1e-12)
    th = jnp.tanh(lo).astype(BF16)
    lo_b = lo.astype(BF16)
    ksum = jnp.zeros_like(k)
    for d, (ld_o, kd_o, a_o) in enumerate(((ld0_o, kd0_o, a0_o), (ld1_o, kd1_o, a1_o))):
        wd = w0_ref[d:d + 1, :] + _dot(th, wl_ref[d])
        wd = -_softplus(-wd) - 0.5
        a = _sigmoid(a0_ref[d:d + 1, :] + _dot(lo_b, al_ref[d]))
        kd = k * (1.0 + (a - 1.0) * ka_ref[...])
        ld_o[0] = -jnp.exp(wd)
        kd_o[0] = kd
        a_o[0] = a
        ksum = ksum + kd
    r_o[0] = r
    v_o[0] = v
    kk_o[0] = kk
    bv_o[0] = _dot(r * ksum * rk_ref[...], bd, **hp) * v
    g_o[0] = _dot(_sigmoid(gl).astype(BF16), gl_ref[...])


def _rwkv_prep(p, hprev, hnext, mu, w0, a0, wl, al, gl, k_k, k_a, r_k, bd, tm):
    b, ta, _ = p.shape
    w = RWKV_W
    row = lambda bb, i: (bb, i, 0)
    vec = lambda n: pl.BlockSpec((1, n), lambda bb, i: (0, 0))
    out = jax.ShapeDtypeStruct((b, ta, w), F32)
    return pl.pallas_call(
        _rwkv_prep_kernel,
        grid=(b, ta // tm),
        in_specs=[pl.BlockSpec((1, tm, RW_COLS), lambda bb, i: (bb, i, SEG_RW // RW_COLS)),
                  pl.BlockSpec((1, 1, 1, RW_COLS), lambda bb, i: (bb, i, 0, 0)),
                  pl.BlockSpec((1, 1, 1, RW_COLS), lambda bb, i: (bb, i, 0, 0)),
                  vec(RW_COLS),
                  pl.BlockSpec((2, w), lambda bb, i: (0, 0)),
                  pl.BlockSpec((2, w), lambda bb, i: (0, 0)),
                  pl.BlockSpec((2, LANE, w), lambda bb, i: (0, 0, 0)),
                  pl.BlockSpec((2, LANE, w), lambda bb, i: (0, 0, 0)),
                  pl.BlockSpec((LANE, w), lambda bb, i: (0, 0)),
                  vec(w), vec(w), vec(w),
                  pl.BlockSpec((w, w), lambda bb, i: (0, 0))],
        out_specs=[pl.BlockSpec((1, tm, w), row)] * 11,
        out_shape=[out] * 11,
        scratch_shapes=[pltpu.VMEM((tm + 2 * SUBLANE, RW_COLS), F32)],
        compiler_params=_cparams(("parallel", "parallel")),
        name="rwkv_prep",
    )(p, hprev, hnext, mu, w0, a0, wl, al, gl, k_k, k_a, r_k, bd)


def _split3(x):
    hi = x.astype(BF16)
    r1 = x - hi.astype(F32)
    mid = r1.astype(BF16)
    return hi, mid, (r1 - mid.astype(F32)).astype(BF16)


def _split2(x):
    hi = x.astype(BF16)
    return hi, (x - hi.astype(F32)).astype(BF16)


def _mm3(a, b, dot=_dot):
    return dot(a[0], b[0]) + dot(a[0], b[1]) + dot(a[1], b[0])


def _rwkv_chunk_kernel(r_ref, v_ref, kk_ref, ld0_ref, ld1_ref, kd0_ref, kd1_ref, a0_ref, a1_ref, pz_ref, wu_ref):
    C = RWKV_CHUNK
    hd = RWKV_HEAD
    row = lax.broadcasted_iota(jnp.int32, (C, C), 0)
    col = lax.broadcasted_iota(jnp.int32, (C, C), 1)
    eye = (row == col).astype(F32)
    probs = []
    for d, (ld_ref, kd_ref, a_ref) in enumerate(((ld0_ref, kd0_ref, a0_ref), (ld1_ref, kd1_ref, a1_ref))):
        incl = (row >= col) if d == 0 else (row <= col)
        strict = (row > col) if d == 0 else (row < col)
        last = C - 1 if d == 0 else 0
        ones = jnp.where(incl, 1.0, 0.0).astype(BF16)
        cs_all = sum(_dot(ones, part) for part in _split3(ld_ref[0]))
        for h in range(RWKV_HEADS):
            sl = slice(h * hd, (h + 1) * hd)
            probs.append(dict(d=d, h=h, incl=incl, strict=strict, last=last, cs=cs_all[:, sl],
                              r=r_ref[0, :, sl], v=v_ref[0, :, sl], kk=kk_ref[0, :, sl],
                              ld=ld_ref[0, :, sl], k=kd_ref[0, :, sl], a=a_ref[0, :, sl]))
    bf = lambda z: z.astype(BF16)
    for q in probs:
        cs = q["cs"]
        e_cs = jnp.exp(cs)
        e_inv = jnp.exp(-cs)
        q["g_end"] = e_cs[q["last"]:q["last"] + 1, :]
        q["kkd"] = q["kk"] * jnp.exp(cs - q["ld"])
        q["rd"] = q["r"] * e_cs
        q["bi"] = bf(q["kk"] * q["a"] * e_inv)
        q["ki"] = bf(q["k"] * e_inv)
        q["lhs"] = bf(jnp.concatenate([q["kkd"], q["rd"]], axis=0))
    for q in probs:
        gb = _dot_nt(q["lhs"], q["bi"])
        gk = _dot_nt(q["lhs"], q["ki"])
        q["pw"] = -jnp.where(q["strict"], gb[:C], 0.0)
        q["q_b"] = bf(jnp.where(q["incl"], gb[C:], 0.0))
        q["akqk"] = bf(jnp.concatenate([jnp.where(q["strict"], gk[:C], 0.0),
                                        jnp.where(q["incl"], gk[C:], 0.0)], axis=0))
        q["tinv"] = eye + q["pw"]
    for _ in range(C.bit_length() - 2):
        for q in probs:
            pw = bf(q["pw"])
            q["pw"] = _dot(pw, pw)
        for q in probs:
            q["tinv"] = q["tinv"] + _dot(bf(q["tinv"]), bf(q["pw"]))
    for q in probs:
        q["vs"] = bf(q["v"])
        q["av"] = _dot(q["akqk"], q["vs"])
    for q in probs:
        tinv = bf(q["tinv"])
        q["w1"] = _dot(tinv, bf(q["kkd"]))
        q["u1"] = _dot(tinv, bf(q["av"][:C]))
    for q in probs:
        w1 = bf(q["w1"])
        u1 = bf(q["u1"])
        w2 = q["rd"] - _dot(q["q_b"], w1)
        u2 = q["av"][C:] - _dot(q["q_b"], u1)
        pt = (eye[:hd, :hd] - _dot_tn(w1, q["bi"])) * q["g_end"]
        zt = (_dot_tn(q["vs"], q["ki"]) - _dot_tn(u1, q["bi"])) * q["g_end"]
        pz_ref[q["d"], 0, 0, q["h"]] = jnp.concatenate([pt, zt], axis=1)
        wu_ref[q["d"], 0, 0, q["h"]] = jnp.concatenate([w2, u2], axis=1)


def _rwkv_chunks(r, v, kk, ld0, ld1, kd0, kd1, a0, a1):
    b, ta, w = r.shape
    nc = ta // RWKV_CHUNK
    spec = pl.BlockSpec((1, RWKV_CHUNK, w), lambda bb, c: (bb, c, 0))

    def out(rows):
        return (pl.BlockSpec((2, 1, 1, RWKV_HEADS, rows, 2 * RWKV_HEAD), lambda bb, c: (0, bb, c, 0, 0, 0)),
                jax.ShapeDtypeStruct((2, b, nc, RWKV_HEADS, rows, 2 * RWKV_HEAD), F32))

    (pz_spec, pz_shape), (wu_spec, wu_shape) = out(RWKV_HEAD), out(RWKV_CHUNK)
    return pl.pallas_call(
        _rwkv_chunk_kernel,
        grid=(b, nc),
        in_specs=[spec] * 9,
        out_specs=[pz_spec, wu_spec],
        out_shape=[pz_shape, wu_shape],
        compiler_params=_cparams(("parallel", "parallel")),
        name="rwkv_chunks",
    )(r, v, kk, ld0, ld1, kd0, kd1, a0, a1)


def _rwkv_state_kernel(pz0_ref, pz1_ref, wu0_ref, wu1_ref, y0_ref, y1_ref, st_ref):
    i = pl.program_id(0)
    hd = RWKV_HEAD
    nb = st_ref.shape[1]

    @pl.when(i == 0)
    def _():
        st_ref[...] = jnp.zeros_like(st_ref)

    probs = []
    for d, (pz_ref, wu_ref) in enumerate(((pz0_ref, wu0_ref), (pz1_ref, wu1_ref))):
        for bb in range(nb):
            for h in range(RWKV_HEADS):
                pz = pz_ref[0, bb, 0, h]
                wu = wu_ref[0, bb, 0, h]
                probs.append(dict(d=d, b=bb, h=h, pt=_split2(pz[:, :hd]), zt=pz[:, hd:],
                                  w2=_split2(wu[:, :hd]), u2=wu[:, hd:], s=_split2(st_ref[d, bb, h])))
    for q in probs:
        q["y"] = _mm3(q["w2"], q["s"], _dot_nt) + q["u2"]
    for q in probs:
        st_ref[q["d"], q["b"], q["h"]] = _mm3(q["s"], q["pt"]) + q["zt"]
    for d, y_ref in enumerate((y0_ref, y1_ref)):
        for bb in range(nb):
            ys = [q["y"] for q in probs if q["d"] == d and q["b"] == bb]
            for hp in range(RWKV_HEADS // 2):
                y_ref[bb, :, hp * LANE:(hp + 1) * LANE] = jnp.concatenate([ys[2 * hp], ys[2 * hp + 1]], axis=1)


def _rwkv_state(pz, wu, ctx):
    _, b, nc, nh, hd, _ = pz.shape
    ncc = ctx // RWKV_CHUNK

    def rev(i):
        return jnp.where(i < ncc, ncc - 1 - i, nc - 1 - (i - ncc))

    blk = (1, b, 1, nh, hd, 2 * hd)
    wblk = (1, b, 1, nh, RWKV_CHUNK, 2 * hd)
    fwd = lambda i: (0, 0, i, 0, 0, 0)
    bwd = lambda i: (1, 0, rev(i), 0, 0, 0)
    yshape = jax.ShapeDtypeStruct((b, nc * RWKV_CHUNK, nh * hd), F32)
    return pl.pallas_call(
        _rwkv_state_kernel,
        grid=(nc,),
        in_specs=[pl.BlockSpec(blk, fwd), pl.BlockSpec(blk, bwd), pl.BlockSpec(wblk, fwd), pl.BlockSpec(wblk, bwd)],
        out_specs=[pl.BlockSpec((b, RWKV_CHUNK, nh * hd), lambda i: (0, i, 0)),
                   pl.BlockSpec((b, RWKV_CHUNK, nh * hd), lambda i: (0, rev(i), 0))],
        out_shape=[yshape, yshape],
        scratch_shapes=[pltpu.VMEM((2, b, nh, hd, hd), F32)],
        compiler_params=_cparams(("arbitrary",)),
        name="rwkv_state",
    )(pz, pz, wu, wu)


def _rwkv_out_kernel(y0_ref, y1_ref, bv_ref, g_ref, lg_ref, lb_ref, bd_ref, o_ref):
    hp = dict(precision=HIGHEST)
    bdm = bd_ref[...] * (1.0 / RWKV_HEAD)
    y = y0_ref[0] + y1_ref[0]
    yc = y - _dot(y, bdm, **hp)
    var = _dot(yc * yc, bdm, **hp)
    yn = yc * lax.rsqrt(var + RWKV_GN_EPS) * lg_ref[...] + lb_ref[...]
    o_ref[0] = ((yn + bv_ref[0]) * g_ref[0]).astype(o_ref.dtype)


def _rwkv_out(y0, y1, bv, g, ln_g, ln_b, bd, tm):
    b, ta, w = y0.shape
    row = pl.BlockSpec((1, tm, w), lambda bb, i: (bb, i, 0))
    vec = pl.BlockSpec((1, w), lambda bb, i: (0, 0))
    return pl.pallas_call(
        _rwkv_out_kernel,
        grid=(b, ta // tm),
        in_specs=[row, row, row, row, vec, vec, pl.BlockSpec((w, w), lambda bb, i: (0, 0))],
        out_specs=row,
        out_shape=jax.ShapeDtypeStruct((b, ta, w), BF16),
        compiler_params=_cparams(("parallel", "parallel")),
        name="rwkv_out",
    )(y0, y1, bv, g, ln_g, ln_b, bd)


def _rwkv_mixer(p, ctx, mu, w0, w_lora, a0, a_lora, g_lora, k_k, k_a, r_k, ln_g, ln_b):
    b, ta, _ = p.shape
    w = RWKV_W
    tm = _tile(math.gcd(ta, ctx), 256)
    nt = ta // tm
    edge_lo = p[:, 0::tm, SEG_RW:]
    edge_hi = p[:, tm - 1::tm, SEG_RW:]
    tile_id = jnp.arange(nt)[None, :, None]
    hprev = jnp.where((tile_id == 0) | (tile_id == ctx // tm), 0.0, jnp.roll(edge_hi, 1, axis=1))
    hnext = jnp.where((tile_id == nt - 1) | (tile_id == ctx // tm - 1), 0.0, jnp.roll(edge_lo, -1, axis=1))
    def rows_at(mat, start):
        return jnp.pad(mat.astype(BF16), ((start, LANE - start - mat.shape[0]), (0, 0)))

    wl = jnp.stack([rows_at(w_lora[d], d * DECAY_LORA) for d in range(2)])
    al = jnp.stack([rows_at(a_lora[d], 2 * DECAY_LORA + d * ICLR_LORA) for d in range(2)])
    gl = rows_at(g_lora, 0)
    mu_p = jnp.pad(mu, (0, RW_COLS - RWKV_STREAM)).reshape(1, RW_COLS)
    head_id = jnp.arange(w) // RWKV_HEAD
    bd = (head_id[:, None] == head_id[None, :]).astype(F32)
    r, v, kk, ld0, ld1, kd0, kd1, a0_, a1_, bv, g = _rwkv_prep(
        p, hprev[:, :, None, :], hnext[:, :, None, :], mu_p, w0, a0, wl, al, gl,
        k_k.reshape(1, w), k_a.reshape(1, w), r_k.reshape(1, w), bd, tm)
    pz, wu = _rwkv_chunks(r, v, kk, ld0, ld1, kd0, kd1, a0_, a1_)
    y0, y1 = _rwkv_state(pz, wu, ctx)
    return _rwkv_out(y0, y1, bv, g, ln_g.reshape(1, w), ln_b.reshape(1, w), bd, tm)


def _merge_a_kernel(x_ref, ml_ref, mc_ref, o4_ref, wg_ref, wb_ref, out_ref, xn_ref, acc_ref, *, tm, ctx):
    i = pl.program_id(1)
    j = pl.program_id(2)
    br = pl.program_id(3)

    @pl.when((j == 0) & (br == 0))
    def _():
        _fill_modulated(xn_ref, x_ref, ml_ref[0], mc_ref[0], i * tm, ctx)

    val = _sigmoid(_dot(xn_ref[...], wg_ref[...])) * _dot(o4_ref[0, 0], wb_ref[0])

    @pl.when(br == 0)
    def _():
        acc_ref[...] = val

    @pl.when(br > 0)
    def _():
        acc_ref[...] += val

    @pl.when(br == N_BRANCH - 1)
    def _():
        out_ref[0] = acc_ref[...].astype(out_ref.dtype)


def _gate_window_weights(w_in_l, gate_off, w_branch_l, w_out_l):
    d = w_in_l.shape[0]
    shift = gate_off % LANE
    start = gate_off - shift
    gw = d + 2 * LANE
    tail = gw - d - shift
    wsrc = w_in_l[:, start:].astype(BF16)
    wsrc = jnp.pad(wsrc, ((0, 0), (0, (N_BRANCH - 1) * d + gw - wsrc.shape[1])))
    wg = jnp.concatenate([wsrc[:, br * d:br * d + gw] for br in range(N_BRANCH)], axis=1)
    wb = jnp.pad(w_branch_l.astype(BF16), ((0, 0), (0, 0), (shift, tail)))
    wo = jnp.pad(w_out_l.astype(BF16), ((shift, tail), (0, 0)))
    return wg, wb, wo


def _merge_a(x_all, modl, modc, o4, wg, wb, ctx):
    b, ta, d = x_all.shape
    gw = wb.shape[2]
    tm = _tile(ta, 1088)
    tn = _tile(gw, 768, LANE)
    return pl.pallas_call(
        functools.partial(_merge_a_kernel, tm=tm, ctx=ctx),
        grid=(b, ta // tm, gw // tn, N_BRANCH),
        in_specs=[pl.BlockSpec((1, tm, d), lambda bb, i, j, br: (bb, i, 0)),
                  pl.BlockSpec((1, 6, d), lambda bb, i, j, br: (bb, 0, 0)),
                  pl.BlockSpec((1, 6, d), lambda bb, i, j, br: (0, 0, 0)),
                  pl.BlockSpec((1, 1, tm, BRANCH_W), lambda bb, i, j, br: (br, bb, i, 0)),
                  pl.BlockSpec((d, tn), lambda bb, i, j, br: (0, br * (gw // tn) + j)),
                  pl.BlockSpec((1, BRANCH_W, tn), lambda bb, i, j, br: (br, 0, j))],
        out_specs=pl.BlockSpec((1, tm, tn), lambda bb, i, j, br: (bb, i, j)),
        out_shape=jax.ShapeDtypeStruct((b, ta, gw), BF16),
        scratch_shapes=[pltpu.VMEM((tm, d), BF16), pltpu.VMEM((tm, tn), F32)],
        compiler_params=_cparams(("parallel", "parallel", "arbitrary", "arbitrary"), 56),
        name="merge_gate",
    )(x_all, modl, modc, o4, wg, wb)


def _merge_b_kernel(m_ref, w_ref, x_ref, ml_ref, mc_ref, lg_ref, lb_ref, xo_ref, h2_ref, *, tm, ctx, alpha):
    i = pl.program_id(1)
    isc = _row_is_ctx(tm, i * tm, ctx)
    ml = ml_ref[0]
    mc = mc_ref[0]
    y = _dot(m_ref[0], w_ref[...])
    g1 = jnp.where(isc, mc[2:3], ml[2:3])
    xn = _ln(alpha * x_ref[0] + g1 * y, POST_LN_EPS) * lg_ref[...] + lb_ref[...]
    xo_ref[0] = xn
    h2_ref[0] = _pack_halves(_modulated(xn, ml, mc, 3, 4, isc))


def _merge_b(merged, w_out, x_all, modl, modc, ln_g, ln_b, ctx, alpha):
    b, ta, d = x_all.shape
    gw = w_out.shape[0]
    tm = _tile(ta, 272)
    row = lambda bb, i: (bb, i, 0)
    return pl.pallas_call(
        functools.partial(_merge_b_kernel, tm=tm, ctx=ctx, alpha=alpha),
        grid=(b, ta // tm),
        in_specs=[pl.BlockSpec((1, tm, gw), row),
                  pl.BlockSpec((gw, d), lambda bb, i: (0, 0)),
                  pl.BlockSpec((1, tm, d), row),
                  pl.BlockSpec((1, 6, d), lambda bb, i: (bb, 0, 0)),
                  pl.BlockSpec((1, 6, d), lambda bb, i: (0, 0, 0)),
                  pl.BlockSpec((1, d), lambda bb, i: (0, 0)),
                  pl.BlockSpec((1, d), lambda bb, i: (0, 0))],
        out_specs=[pl.BlockSpec((1, tm, d), row), pl.BlockSpec((1, tm, d // 2), row)],
        out_shape=[jax.ShapeDtypeStruct((b, ta, d), F32), jax.ShapeDtypeStruct((b, ta, d // 2), jnp.uint32)],
        compiler_params=_cparams(("parallel", "parallel")),
        name="merge_out_ln",
    )(merged, w_out, x_all, modl, modc, ln_g.reshape(1, d), ln_b.reshape(1, d))


def _router_kernel(x_ref, ml_ref, mc_ref, rw_ref, rb_ref, idx_ref, wt_ref, rank_ref, cnt_ref, carry_ref,
                   *, tm, ctx):
    i = pl.program_id(1)

    @pl.when((pl.program_id(0) == 0) & (i == 0))
    def _():
        carry_ref[...] = jnp.zeros_like(carry_ref)

    isc = _row_is_ctx(tm, i * tm, ctx)
    h2 = _modulated(x_ref[0], ml_ref[0], mc_ref[0], 3, 4, isc)
    scores = _sigmoid(_dot(h2, rw_ref[...], precision=HIGHEST))
    biased = scores + rb_ref[...]
    lane = lax.broadcasted_iota(jnp.int32, scores.shape, 1)
    grp = lane // GROUP_SIZE
    ninf = -jnp.inf

    def first_argmax(z):
        m = jnp.max(z, -1, keepdims=True)
        idx = jnp.min(jnp.where(z == m, lane, N_EXPERTS), -1, keepdims=True)
        return m, idx

    gscore = jnp.zeros_like(biased)
    for g in range(N_GROUPS):
        zg = jnp.where(grp == g, biased, ninf)
        m1, i1 = first_argmax(zg)
        m2 = jnp.max(jnp.where(lane == i1, ninf, zg), -1, keepdims=True)
        gscore = jnp.where(grp == g, m1 + m2, gscore)
    cand = jnp.where(lane % GROUP_SIZE == 0, gscore, ninf)
    gsel = jnp.zeros(scores.shape, jnp.bool_)
    for _ in range(TOPK_GROUPS):
        _, ig = first_argmax(cand)
        gsel = gsel | (grp == ig // GROUP_SIZE)
        cand = jnp.where(lane == ig, ninf, cand)
    masked = jnp.where(gsel, biased, ninf)
    out_lane = lax.broadcasted_iota(jnp.int32, (tm, LANE), 1)
    idx_out = jnp.zeros((tm, LANE), jnp.int32)
    w_out = jnp.zeros((tm, LANE), F32)
    w_sum = jnp.zeros((tm, 1), F32)
    chosen = []
    picked = jnp.zeros(scores.shape, F32)
    for kq in range(TOP_K):
        _, ie = first_argmax(masked)
        sel = lane == ie
        wk = jnp.sum(jnp.where(sel, scores, 0.0), -1, keepdims=True)
        masked = jnp.where(sel, ninf, masked)
        idx_out = jnp.where(out_lane == kq, ie, idx_out)
        w_out = jnp.where(out_lane == kq, wk, w_out)
        w_sum = w_sum + wk
        chosen.append(sel)
        picked = jnp.where(sel, 1.0, picked)
    idx_ref[0] = idx_out
    wt_ref[0] = w_out / w_sum * ROUTED_SCALE
    r_i = lax.broadcasted_iota(jnp.int32, (tm, tm), 0)
    c_i = lax.broadcasted_iota(jnp.int32, (tm, tm), 1)
    before = _dot(jnp.where(r_i > c_i, 1.0, 0.0).astype(BF16), picked.astype(BF16)) + carry_ref[...]
    rank_out = jnp.zeros((tm, LANE), jnp.int32)
    for kq in range(TOP_K):
        rk = jnp.sum(jnp.where(chosen[kq], before, 0.0), -1, keepdims=True)
        rank_out = jnp.where(out_lane == kq, rk.astype(jnp.int32), rank_out)
    rank_ref[0] = rank_out
    carry_ref[...] += jnp.sum(picked, axis=0, keepdims=True)
    cnt_ref[...] = carry_ref[...].astype(jnp.int32)


def _router(x_all, modl, modc, rw, rb, ctx):
    b, ta, d = x_all.shape
    tm = _tile(ta, 544)
    row = lambda bb, i: (bb, i, 0)
    tok = jax.ShapeDtypeStruct((b, ta, LANE), jnp.int32)
    return pl.pallas_call(
        functools.partial(_router_kernel, tm=tm, ctx=ctx),
        grid=(b, ta // tm),
        in_specs=[pl.BlockSpec((1, tm, d), row),
                  pl.BlockSpec((1, 6, d), lambda bb, i: (bb, 0, 0)),
                  pl.BlockSpec((1, 6, d), lambda bb, i: (0, 0, 0)),
                  pl.BlockSpec((d, N_EXPERTS), lambda bb, i: (0, 0)),
                  pl.BlockSpec((1, N_EXPERTS), lambda bb, i: (0, 0))],
        out_specs=[pl.BlockSpec((1, tm, LANE), row), pl.BlockSpec((1, tm, LANE), row),
                   pl.BlockSpec((1, tm, LANE), row), pl.BlockSpec((1, N_EXPERTS), lambda bb, i: (0, 0))],
        out_shape=[tok, jax.ShapeDtypeStruct((b, ta, LANE), F32), tok,
                   jax.ShapeDtypeStruct((1, N_EXPERTS), jnp.int32)],
        scratch_shapes=[pltpu.VMEM((1, N_EXPERTS), F32)],
        compiler_params=_cparams(("arbitrary", "arbitrary")),
        name="router",
    )(x_all, modl, modc, rw, rb.reshape(1, N_EXPERTS))


def _row_copy(src_ref, src_row, dst_ref, dst_row, sem):
    return pltpu.make_async_copy(src_ref.at[pl.ds(src_row, 1)], dst_ref.at[pl.ds(dst_row, 1)], sem)


def _dispatch_kernel(pos_ref, h2_ref, xg_in_ref, xg_ref, sem, *, tm):
    del xg_in_ref

    def issue(t, carry):
        for kq in range(TOP_K):
            _row_copy(h2_ref, t, xg_ref, pos_ref[0, 0, t * TOP_K + kq], sem).start()
        return carry

    lax.fori_loop(0, tm, issue, 0, unroll=4)
    for _ in range(TOP_K):
        pltpu.make_async_copy(h2_ref, xg_ref.at[pl.ds(0, tm)], sem).wait()


def _dispatch(pos, h2p, n_rows, tm):
    n, half = h2p.shape
    xg0 = jnp.zeros((n_rows, half), jnp.uint32)
    return pl.pallas_call(
        functools.partial(_dispatch_kernel, tm=tm),
        grid=(n // tm,),
        in_specs=[pl.BlockSpec((1, 1, tm * TOP_K), lambda i: (i, 0, 0), memory_space=pltpu.SMEM),
                  pl.BlockSpec((tm, half), lambda i: (i, 0)),
                  pl.BlockSpec(memory_space=pl.ANY)],
        out_specs=pl.BlockSpec(memory_space=pl.ANY),
        out_shape=jax.ShapeDtypeStruct((n_rows, half), jnp.uint32),
        scratch_shapes=[pltpu.SemaphoreType.DMA(())],
        input_output_aliases={2: 0},
        compiler_params=_cparams(("arbitrary",)),
        name="moe_dispatch",
    )(pos, h2p, xg0)


def _swiglu(x_lo, x_hi, wgu, wdn, ff):
    half = x_lo.shape[1]
    h = _dot(x_lo, wgu[:half, :]) + _dot(x_hi, wgu[half:, :])
    g = h[:, :ff]
    return _dot((g * _sigmoid(g) * h[:, ff:]).astype(BF16), wdn)


def _experts_kernel(be_ref, nu_ref, first_ref, slot_ref, nxt_ref, x_ref, wgu_hbm, wdn_hbm, o_ref,
                    gu_buf, dn_buf, wgu_s, wdn_s, sems, *, ff, l):
    i = pl.program_id(0)

    def weight_copies(e, s):
        return (pltpu.make_async_copy(wgu_hbm.at[l, e], gu_buf.at[s], sems.at[0, s]),
                pltpu.make_async_copy(wdn_hbm.at[l, e], dn_buf.at[s], sems.at[1, s]))

    @pl.when(i == 0)
    def _():
        for cp in weight_copies(be_ref[0], slot_ref[0]):
            cp.start()

    @pl.when(first_ref[i] == 1)
    def _():
        s = slot_ref[i]
        for cp in weight_copies(be_ref[i], s):
            cp.wait()
        wgu_s[...] = gu_buf[s].astype(BF16)
        wdn_s[...] = dn_buf[s].astype(BF16)

        @pl.when(nxt_ref[i] >= 0)
        def _():
            for cp in weight_copies(nxt_ref[i], 1 - s):
                cp.start()

    @pl.when(i < nu_ref[0])
    def _():
        lo, hi = _unpack_halves(x_ref[...])
        o_ref[...] = _pack_halves(_swiglu(lo.astype(BF16), hi.astype(BF16), wgu_s[...], wdn_s[...], ff))

    @pl.when(i >= nu_ref[0])
    def _():
        o_ref[...] = jnp.zeros_like(o_ref)


def _experts(xg, blk_e, n_used, w_gu, w_dn, l, blk):
    n_rows, half = xg.shape
    d = 2 * half
    ff = w_dn.shape[2]
    n_blocks = n_rows // blk
    first = jnp.concatenate([jnp.ones((1,), jnp.int32), (blk_e[1:] != blk_e[:-1]).astype(jnp.int32)])
    slot = (jnp.cumsum(first) - 1) % 2
    n_le = jnp.sum(blk_e[None, :] <= blk_e[:, None], axis=1)
    pick = jnp.arange(n_blocks)[None, :] == n_le[:, None]
    nxt = jnp.where(n_le < n_blocks, jnp.sum(jnp.where(pick, blk_e[None, :], 0), axis=1), -1)
    grid_spec = pltpu.PrefetchScalarGridSpec(
        num_scalar_prefetch=5,
        grid=(n_blocks,),
        in_specs=[pl.BlockSpec((blk, half), lambda i, *_: (i, 0)),
                  pl.BlockSpec(memory_space=pl.ANY),
                  pl.BlockSpec(memory_space=pl.ANY)],
        out_specs=pl.BlockSpec((blk, half), lambda i, *_: (i, 0)),
        scratch_shapes=[pltpu.VMEM((2, d, 2 * ff), F32), pltpu.VMEM((2, ff, d), F32),
                        pltpu.VMEM((d, 2 * ff), BF16), pltpu.VMEM((ff, d), BF16),
                        pltpu.SemaphoreType.DMA((2, 2))],
    )
    return pl.pallas_call(
        functools.partial(_experts_kernel, ff=ff, l=l),
        grid_spec=grid_spec,
        out_shape=jax.ShapeDtypeStruct((n_rows, half), jnp.uint32),
        compiler_params=_cparams(("arbitrary",), 56),
        name="moe_experts",
    )(blk_e, n_used, first, slot.astype(jnp.int32), nxt.astype(jnp.int32), xg, w_gu, w_dn)


def _moe_plan(idx, rank, counts, blk):
    n_assign = idx.size
    padded = (counts + blk - 1) // blk * blk
    pad_end = jnp.cumsum(padded)
    start_pad = pad_end - padded
    experts = jnp.arange(N_EXPERTS, dtype=jnp.int32)
    start_of = jnp.sum(jnp.where(idx[..., None] == experts, start_pad, 0), axis=-1)
    pos = (start_of + rank).astype(jnp.int32)
    n_blocks = -(-(n_assign + N_EXPERTS * (blk - 1)) // blk)
    blk_start = jnp.arange(n_blocks, dtype=jnp.int32) * blk
    blk_e = jnp.minimum(jnp.sum(blk_start[:, None] >= pad_end[None, :], axis=-1), N_EXPERTS - 1).astype(jnp.int32)
    n_used = (pad_end[-1] // blk).astype(jnp.int32).reshape(1)
    return pos, blk_e, n_used, n_blocks * blk


def _combine_kernel(pos_ref, y_ref, wt_ref, h2_ref, sgu_ref, sdn_ref, x_ref, ml_ref, mc_ref, lg_ref, lb_ref,
                    xo_ref, buf, sem, *, tm, ctx, alpha, ff, t0):
    i = pl.program_id(1) + t0

    def issue(t, carry):
        for kq in range(TOP_K):
            _row_copy(y_ref, pos_ref[0, 0, t * TOP_K + kq], buf.at[kq], t, sem).start()
        return carry

    lax.fori_loop(0, tm, issue, 0, unroll=4)
    lo, hi = _unpack_halves(h2_ref[0])
    f = _swiglu(lo.astype(BF16), hi.astype(BF16), sgu_ref[...], sdn_ref[...], ff)
    for kq in range(TOP_K):
        pltpu.make_async_copy(y_ref.at[pl.ds(0, tm)], buf.at[kq], sem).wait()
    wt = wt_ref[0]
    half = f.shape[1] // 2
    f_lo = f[:, :half]
    f_hi = f[:, half:]
    for kq in range(TOP_K):
        lo, hi = _unpack_halves(buf[kq])
        f_lo = f_lo + wt[:, kq:kq + 1] * lo
        f_hi = f_hi + wt[:, kq:kq + 1] * hi
    f = jnp.concatenate([f_lo, f_hi], axis=1)
    isc = _row_is_ctx(tm, i * tm, ctx)
    g2 = jnp.where(isc, mc_ref[0][5:6], ml_ref[0][5:6])
    xo_ref[0] = _ln(alpha * x_ref[0] + g2 * f, POST_LN_EPS) * lg_ref[...] + lb_ref[...]


def _combine(pos, y_rows, wts, h2p, sh_gu, sh_dn, x_all, modl, modc, ln_g, ln_b, ctx, alpha, tm, skip_ctx):
    b, ta, d = x_all.shape
    ff = sh_dn.shape[0]
    nt = ta // tm
    t0 = ctx // tm if skip_ctx else 0
    row = lambda bb, i: (bb, i + t0, 0)
    return pl.pallas_call(
        functools.partial(_combine_kernel, tm=tm, ctx=ctx, alpha=alpha, ff=ff, t0=t0),
        grid=(b, nt - t0),
        in_specs=[pl.BlockSpec((1, 1, tm * TOP_K), lambda bb, i: (bb * nt + i + t0, 0, 0),
                               memory_space=pltpu.SMEM),
                  pl.BlockSpec(memory_space=pl.ANY),
                  pl.BlockSpec((1, tm, LANE), row),
                  pl.BlockSpec((1, tm, d // 2), row),
                  pl.BlockSpec((d, 2 * ff), lambda bb, i: (0, 0)),
                  pl.BlockSpec((ff, d), lambda bb, i: (0, 0)),
                  pl.BlockSpec((1, tm, d), row),
                  pl.BlockSpec((1, 6, d), lambda bb, i: (bb, 0, 0)),
                  pl.BlockSpec((1, 6, d), lambda bb, i: (0, 0, 0)),
                  pl.BlockSpec((1, d), lambda bb, i: (0, 0)),
                  pl.BlockSpec((1, d), lambda bb, i: (0, 0))],
        out_specs=pl.BlockSpec((1, tm, d), lambda bb, i: (bb, i, 0)),
        out_shape=jax.ShapeDtypeStruct((b, ta - t0 * tm, d), F32),
        scratch_shapes=[pltpu.VMEM((TOP_K, tm, d // 2), jnp.uint32), pltpu.SemaphoreType.DMA(())],
        compiler_params=_cparams(("arbitrary", "arbitrary")),
        name="moe_combine_ln",
    )(pos, y_rows, wts, h2p, sh_gu, sh_dn, x_all, modl, modc, ln_g.reshape(1, d), ln_b.reshape(1, d))


def _rope_tables(n_lat, ctx):
    rows = n_lat // GRID_W
    row = jnp.repeat(jnp.arange(rows, dtype=F32), GRID_W)
    col = jnp.tile(jnp.arange(GRID_W, dtype=F32), rows)
    n_freq = ROPE_DIM // 4
    inv_freq = ROPE_BASE ** (-jnp.arange(n_freq, dtype=F32) / n_freq)
    ang = jnp.concatenate([row[:, None] * inv_freq, col[:, None] * inv_freq], -1)
    cos, sin = jnp.cos(ang), jnp.sin(ang)
    cos = jnp.concatenate([jnp.ones((ctx, ROPE_DIM // 2), F32), cos], 0)
    sin = jnp.concatenate([jnp.zeros((ctx, ROPE_DIM // 2), F32), sin], 0)
    cosf = jnp.tile(jnp.concatenate([cos, cos], -1), (1, LANE // ROPE_DIM))
    sinf = jnp.tile(jnp.concatenate([-sin, sin], -1), (1, LANE // ROPE_DIM))
    return cosf, sinf


def _pack_small_weights(w_in_l):
    d = w_in_l.shape[0]
    offs = np.cumsum([0, MLA_Q_RANK, MLA_KV_RANK + MLA_ROPE, RWKV_STREAM,
                      (SWA_HEADS + 2 * SWA_KV_HEADS) * SWA_HEAD, 3 * DIFF_HEADS * 2 * DIFF_HEAD])
    placed = sorted(zip([SEG_MQ, SEG_MKV, SEG_RW, SEG_SW, SEG_DF], range(5)))
    parts = []
    cur = 0
    for start, s in placed:
        if start > cur:
            parts.append(jnp.zeros((d, start - cur), BF16))
        parts.append(w_in_l[:, int(offs[s]):int(offs[s + 1])].astype(BF16))
        cur = start + int(offs[s + 1] - offs[s])
    parts.append(jnp.zeros((d, SMALL_COLS - cur), BF16))
    return jnp.concatenate(parts, axis=1), int(offs[5])


def kernel(x, c, ctx, c_ctx, w_mod, b_mod, w_in, mla_q_norm, mla_w_qup, mla_kv_norm, mla_w_kvup, rwkv_mu, rwkv_w0, rwkv_w_lora, rwkv_a0, rwkv_a_lora, rwkv_g_lora, rwkv_k_k, rwkv_k_a, rwkv_r_k, rwkv_ln_g, rwkv_ln_b, swa_sink, diff_lambda, diff_subln, w_branch, w_out, ln1_g, ln1_b, router_w, router_bias, exp_w_gu, exp_w_dn, sh_w_gu, sh_w_dn, ln2_g, ln2_b):
    depth = w_mod.shape[0]
    b, n_lat, d = x.shape
    n_ctx = ctx.shape[1]
    ta = n_ctx + n_lat
    alpha = (2 * depth) ** 0.25
    cosf, sinf = _rope_tables(n_lat, n_ctx)
    x_all = jnp.concatenate([ctx, x], axis=1)

    cond = jnp.concatenate([c, c_ctx[None]], axis=0)
    cond = jnp.pad(jax.nn.silu(cond), ((0, SUBLANE - (b + 1) % SUBLANE), (0, 0)))

    for l in range(depth):
        mod = _matmul_bias(cond, w_mod, b_mod, l).reshape(cond.shape[0], 6, d)
        modl, modc = mod[:b], mod[b:b + 1]

        w_small, gate_off = _pack_small_weights(w_in[l])
        p = _inproj(x_all, modl, modc, w_small, n_ctx)

        wq = mla_w_qup[l].reshape(MLA_Q_RANK, MLA_HEADS, MLA_NOPE + MLA_ROPE)
        wq = jnp.pad(wq, ((0, 0), (0, 0), (0, 256 - MLA_NOPE - MLA_ROPE))).reshape(MLA_Q_RANK, 1024)
        q_a, k_a, v_a = _mla_prep(p, mla_q_norm[l].reshape(1, -1), mla_kv_norm[l].reshape(1, -1),
                                  wq.astype(BF16), mla_w_kvup[l].astype(BF16), cosf, sinf)
        out_a = _mla_attn(q_a, k_a, v_a, n_ctx)

        out_b = _rwkv_mixer(p, n_ctx, rwkv_mu[l], rwkv_w0[l],
                            rwkv_w_lora[l], rwkv_a0[l], rwkv_a_lora[l], rwkv_g_lora[l], rwkv_k_k[l],
                            rwkv_k_a[l], rwkv_r_k[l], rwkv_ln_g[l], rwkv_ln_b[l])

        sinkv = jnp.broadcast_to(swa_sink[l][:, None], (SWA_HEADS, LANE))
        out_s = _swa(p, cosf, sinf, sinkv, n_ctx)

        lf = diff_lambda[l]
        lam_init = 0.8 - 0.6 * math.exp(-0.3 * l)
        lam = jnp.exp(jnp.sum(lf[0] * lf[1])) - jnp.exp(jnp.sum(lf[2] * lf[3])) + lam_init
        out_d = _diff_attn(p, cosf, sinf, jnp.full((1, LANE), lam, F32), diff_subln[l].reshape(1, LANE),
                           n_ctx, lam_init)

        o4 = jnp.stack([out_a, out_b, out_s, out_d], axis=0)
        wg, wb, wo = _gate_window_weights(w_in[l], gate_off, w_branch[l], w_out[l])
        merged = _merge_a(x_all, modl, modc, o4, wg, wb, n_ctx)
        x_all, h2p = _merge_b(merged, wo, x_all, modl, modc, ln1_g[l], ln1_b[l], n_ctx, alpha)

        idx, wts, rank, counts = _router(x_all, modl, modc, router_w[l], router_bias[l], n_ctx)
        pos, blk_e, n_used, n_rows = _moe_plan(idx[..., :TOP_K], rank[..., :TOP_K], counts[0], MOE_BLK)
        tmd = _tile(ta, 128)
        tmc = _tile(math.gcd(ta, n_ctx), 256)
        xg = _dispatch(pos.reshape(b * ta // tmd, 1, tmd * TOP_K), h2p.reshape(b * ta, d // 2), n_rows, tmd)
        y_rows = _experts(xg, blk_e, n_used, exp_w_gu, exp_w_dn, l, MOE_BLK)
        x_all = _combine(pos.reshape(b * ta // tmc, 1, tmc * TOP_K), y_rows, wts, h2p, sh_w_gu[l].astype(BF16),
                         sh_w_dn[l].astype(BF16), x_all, modl, modc, ln2_g[l], ln2_b[l], n_ctx, alpha, tmc,
                         skip_ctx=(l == depth - 1))
    return x_all
```

```python
import functools
import math

import jax
import jax.numpy as jnp
import numpy as np
from jax import lax
from jax.experimental import pallas as pl
from jax.experimental.pallas import tpu as pltpu

F32 = jnp.float32
BF16 = jnp.bfloat16
HIGHEST = lax.Precision.HIGHEST

GRID_W = 64
ROPE_BASE = 10000.0
ROPE_DIM = 64
ADALN_EPS = 1e-6
POST_LN_EPS = 1e-5
NEG_INF = -1e30

MLA_HEADS = 4
MLA_Q_RANK = 384
MLA_KV_RANK = 256
MLA_NOPE = 128
MLA_ROPE = 64
MLA_V = 128

RWKV_HEADS = 8
RWKV_HEAD = 64
RWKV_W = RWKV_HEADS * RWKV_HEAD
DECAY_LORA = 32
ICLR_LORA = 32
GATE_LORA = 96
RWKV_STREAM = 3 * RWKV_W + 2 * DECAY_LORA + 2 * ICLR_LORA + GATE_LORA
RWKV_GN_EPS = 64e-5
RWKV_CHUNK = 64

SWA_HEADS = 8
SWA_KV_HEADS = 2
SWA_GROUP = SWA_HEADS // SWA_KV_HEADS
SWA_HEAD = 64
WINDOW = 128

DIFF_HEADS = 4
DIFF_HEAD = 64

N_BRANCH = 4
BRANCH_W = 512

N_EXPERTS = 64
TOP_K = 6
N_GROUPS = 8
TOPK_GROUPS = 4
GROUP_SIZE = N_EXPERTS // N_GROUPS
ROUTED_SCALE = 2.5
MOE_BLK = 256

LANE = 128
SUBLANE = 8

SEG_SW = 0
SEG_MQ = 768
SEG_MKV = 1152
SEG_DF = 1536
SEG_RW = 3584
RW_COLS = 1792
SMALL_COLS = SEG_RW + RW_COLS


def _tile(n, target, mult=SUBLANE):
    best = None
    for d in range(mult, min(n, target) + 1, mult):
        if n % d == 0:
            best = d
    assert best is not None, (n, target, mult)
    return best


def _cparams(sem, vmem_mb=48):
    return pltpu.CompilerParams(dimension_semantics=sem, vmem_limit_bytes=vmem_mb * 1024 * 1024)


def _sigmoid(x):
    return 1.0 / (1.0 + jnp.exp(-x))


def _ln(x, eps):
    mu = jnp.mean(x, -1, keepdims=True)
    xc = x - mu
    var = jnp.mean(xc * xc, -1, keepdims=True)
    return xc * lax.rsqrt(var + eps)


def _rms(x, eps):
    return x * lax.rsqrt(jnp.mean(x * x, -1, keepdims=True) + eps)


def _row_is_ctx(n_rows, row0, ctx):
    rows = row0 + lax.broadcasted_iota(jnp.int32, (n_rows, 1), 0)
    return rows < ctx


def _modulated(x, ml, mc, sh_i, sc_i, isc):
    sh = jnp.where(isc, mc[sh_i:sh_i + 1], ml[sh_i:sh_i + 1])
    sc = jnp.where(isc, mc[sc_i:sc_i + 1], ml[sc_i:sc_i + 1])
    return _ln(x, ADALN_EPS) * (1.0 + sc) + sh


def _fill_modulated(xn_ref, x_ref, ml, mc, row0, ctx):
    tm = xn_ref.shape[0]
    ch = _tile(tm, 272)
    for r in range(0, tm, ch):
        isc = _row_is_ctx(ch, row0 + r, ctx)
        xn_ref[r:r + ch, :] = _modulated(x_ref[0, r:r + ch, :], ml, mc, 0, 1, isc).astype(BF16)


def _swap32(x):
    lane = lax.broadcasted_iota(jnp.int32, x.shape, 1)
    up = pltpu.roll(x, LANE - 32, 1)
    dn = pltpu.roll(x, 32, 1)
    return jnp.where((lane % 64) < 32, up, dn)


def _rope(x, cosf, sinf):
    return x * cosf + _swap32(x) * sinf


def _pack_halves(x):
    n = x.shape[1] // 2
    lo = pltpu.bitcast(x[:, :n].astype(BF16).astype(F32), jnp.uint32)
    hi = pltpu.bitcast(x[:, n:].astype(BF16).astype(F32), jnp.uint32)
    return (hi & jnp.uint32(0xFFFF0000)) | (lo >> 16)


def _unpack_halves(u):
    lo = pltpu.bitcast(u << 16, F32)
    hi = pltpu.bitcast(u & jnp.uint32(0xFFFF0000), F32)
    return lo, hi


def _dot(a, b, **kw):
    return jnp.dot(a, b, preferred_element_type=F32, **kw)


def _dot_nt(a, b, **kw):
    return lax.dot_general(a, b, (((1,), (1,)), ((), ())), preferred_element_type=F32, **kw)


def _dot_tn(a, b, **kw):
    return lax.dot_general(a, b, (((0,), (0,)), ((), ())), preferred_element_type=F32, **kw)


def _mm_kernel(x_ref, w_ref, b_ref, o_ref):
    w = w_ref[0].astype(BF16)
    o_ref[...] = sum(_dot(part, w) for part in _split3(x_ref[...])) + b_ref[0]


def _matmul_bias(x, w, b, l):
    m, k = x.shape
    n = w.shape[2]
    tn = _tile(n, 1024, LANE)
    return pl.pallas_call(
        _mm_kernel,
        grid=(n // tn,),
        in_specs=[pl.BlockSpec((m, k), lambda j: (0, 0)),
                  pl.BlockSpec((1, k, tn), lambda j: (l, 0, j)),
                  pl.BlockSpec((1, 1, tn), lambda j: (l, 0, j))],
        out_specs=pl.BlockSpec((m, tn), lambda j: (0, j)),
        out_shape=jax.ShapeDtypeStruct((m, n), F32),
        compiler_params=_cparams(("arbitrary",)),
        name="mod_matmul",
    )(x, w, b.reshape(b.shape[0], 1, n))


def _inproj_kernel(x_ref, ml_ref, mc_ref, w_ref, o_ref, xn_ref, *, tm, ctx):
    i = pl.program_id(1)
    j = pl.program_id(2)

    @pl.when(j == 0)
    def _():
        _fill_modulated(xn_ref, x_ref, ml_ref[0], mc_ref[0], i * tm, ctx)

    o_ref[0] = _dot(xn_ref[...], w_ref[...])


def _inproj(x_all, modl, modc, w_small, ctx):
    b, ta, d = x_all.shape
    n = w_small.shape[1]
    tm = _tile(ta, 1088)
    tn = _tile(n, 1024, LANE)
    return pl.pallas_call(
        functools.partial(_inproj_kernel, tm=tm, ctx=ctx),
        grid=(b, ta // tm, n // tn),
        in_specs=[pl.BlockSpec((1, tm, d), lambda bb, i, j: (bb, i, 0)),
                  pl.BlockSpec((1, 6, d), lambda bb, i, j: (bb, 0, 0)),
                  pl.BlockSpec((1, 6, d), lambda bb, i, j: (0, 0, 0)),
                  pl.BlockSpec((d, tn), lambda bb, i, j: (0, j))],
        out_specs=pl.BlockSpec((1, tm, tn), lambda bb, i, j: (bb, i, j)),
        out_shape=jax.ShapeDtypeStruct((b, ta, n), F32),
        scratch_shapes=[pltpu.VMEM((tm, d), BF16)],
        compiler_params=_cparams(("parallel", "parallel", "arbitrary"), 56),
        name="inproj",
    )(x_all, modl, modc, w_small)


def _mla_prep_kernel(mq_ref, mkv_ref, qg_ref, kg_ref, wq_ref, wkv_ref, cos_ref, sin_ref,
                     q_ref, k_ref, v_ref):
    mq = mq_ref[0]
    mkv = mkv_ref[0]
    cosv = cos_ref[...]
    sinv = sin_ref[...]
    qn = _rms(mq, 1e-6) * qg_ref[...]
    q = _dot(qn.astype(BF16), wq_ref[...])
    kvn = _rms(mkv[:, :MLA_KV_RANK], 1e-6) * kg_ref[...]
    kv = _dot(kvn.astype(BF16), wkv_ref[...])
    kr = _rope(mkv[:, MLA_KV_RANK:MLA_KV_RANK + LANE], cosv, sinv).astype(BF16)
    scale = (MLA_NOPE + MLA_ROPE) ** -0.5 * LOG2E
    for h in range(MLA_HEADS):
        c0 = h * 256
        q_ref[0, :, c0:c0 + 128] = (q[:, c0:c0 + 128] * scale).astype(BF16)
        q_ref[0, :, c0 + 128:c0 + 256] = (_rope(q[:, c0 + 128:c0 + 256], cosv, sinv) * scale).astype(BF16)
        k_ref[0, :, c0:c0 + 128] = kv[:, c0:c0 + 128].astype(BF16)
        k_ref[0, :, c0 + 128:c0 + 256] = kr
        v_ref[0, :, h * 128:(h + 1) * 128] = kv[:, c0 + 128:c0 + 256].astype(BF16)


def _mla_prep(p, qg, kg, wq, wkv, cosf, sinf):
    b, ta, _ = p.shape
    tm = _tile(ta, 544)
    return pl.pallas_call(
        _mla_prep_kernel,
        grid=(b, ta // tm),
        in_specs=[pl.BlockSpec((1, tm, 384), lambda bb, i: (bb, i, SEG_MQ // 384)),
                  pl.BlockSpec((1, tm, 384), lambda bb, i: (bb, i, SEG_MKV // 384)),
                  pl.BlockSpec((1, MLA_Q_RANK), lambda bb, i: (0, 0)),
                  pl.BlockSpec((1, MLA_KV_RANK), lambda bb, i: (0, 0)),
                  pl.BlockSpec((MLA_Q_RANK, 1024), lambda bb, i: (0, 0)),
                  pl.BlockSpec((MLA_KV_RANK, 1024), lambda bb, i: (0, 0)),
                  pl.BlockSpec((tm, LANE), lambda bb, i: (i, 0)),
                  pl.BlockSpec((tm, LANE), lambda bb, i: (i, 0))],
        out_specs=[pl.BlockSpec((1, tm, 1024), lambda bb, i: (bb, i, 0)),
                   pl.BlockSpec((1, tm, 1024), lambda bb, i: (bb, i, 0)),
                   pl.BlockSpec((1, tm, 512), lambda bb, i: (bb, i, 0))],
        out_shape=[jax.ShapeDtypeStruct((b, ta, 1024), BF16),
                   jax.ShapeDtypeStruct((b, ta, 1024), BF16),
                   jax.ShapeDtypeStruct((b, ta, 512), BF16)],
        compiler_params=_cparams(("parallel", "parallel")),
        name="mla_prep",
    )(p, p, qg, kg, wq, wkv, cosf, sinf)


LOG2E = 1.4426950408889634


def _softmax_pv(s, v):
    m = jnp.max(s, -1, keepdims=True)
    p = jnp.exp2(s - m)
    l = jnp.sum(p, -1, keepdims=True)
    return _dot(p.astype(BF16), v) / l


def _mla_attn_kernel(q_ref, k_ref, v_ref, o_ref, *, n_ctx_tiles, ctx):
    qi = pl.program_id(2)

    def run(nk):
        s = _dot_nt(q_ref[0], k_ref[0, :nk, :])
        o_ref[0] = _softmax_pv(s, v_ref[0, :nk, :]).astype(o_ref.dtype)

    @pl.when(qi < n_ctx_tiles)
    def _():
        run(ctx)

    @pl.when(qi >= n_ctx_tiles)
    def _():
        run(k_ref.shape[1])


def _mla_attn(q, k, v, ctx):
    b, ta, _ = q.shape
    tq = _tile(math.gcd(ta, ctx), 256)
    return pl.pallas_call(
        functools.partial(_mla_attn_kernel, n_ctx_tiles=ctx // tq, ctx=ctx),
        grid=(b, MLA_HEADS, ta // tq),
        in_specs=[pl.BlockSpec((1, tq, 256), lambda bb, h, i: (bb, i, h)),
                  pl.BlockSpec((1, ta, 256), lambda bb, h, i: (bb, 0, h)),
                  pl.BlockSpec((1, ta, 128), lambda bb, h, i: (bb, 0, h))],
        out_specs=pl.BlockSpec((1, tq, 128), lambda bb, h, i: (bb, i, h)),
        out_shape=jax.ShapeDtypeStruct((b, ta, 512), BF16),
        compiler_params=_cparams(("parallel", "parallel", "arbitrary")),
        name="mla_attn",
    )(q, k, v)


def _diff_attn_kernel(q_ref, k_ref, v_ref, cq_ref, sq_ref, ck_ref, sk_ref, lam_ref, sub_ref,
                      o_ref, ks_ref, vs_ref, *, n_ctx_tiles, ctx, post_scale):
    qi = pl.program_id(2)

    @pl.when(qi == 0)
    def _():
        ks_ref[...] = _rope(k_ref[0], ck_ref[...], sk_ref[...]).astype(BF16)
        vs_ref[...] = v_ref[0].astype(BF16)

    def run(nk):
        q = _rope(q_ref[0], cq_ref[...], sq_ref[...]) * (DIFF_HEAD ** -0.5 * LOG2E)
        lane = lax.broadcasted_iota(jnp.int32, q.shape, 1)
        q1 = jnp.where(lane < 64, q, 0.0).astype(BF16)
        q2 = jnp.where(lane >= 64, q, 0.0).astype(BF16)
        k = ks_ref[:nk, :]
        v = vs_ref[:nk, :]
        o = _softmax_pv(_dot_nt(q1, k), v) - lam_ref[...] * _softmax_pv(_dot_nt(q2, k), v)
        o = _rms(o, 1e-5) * sub_ref[...] * post_scale
        o_ref[0] = o.astype(o_ref.dtype)

    @pl.when(qi < n_ctx_tiles)
    def _():
        run(ctx)

    @pl.when(qi >= n_ctx_tiles)
    def _():
        run(ks_ref.shape[0])


def _diff_attn(p, cosf, sinf, lamv, subln, ctx, lam_init):
    b, ta, _ = p.shape
    tq = _tile(math.gcd(ta, ctx), 256)
    cb = SEG_DF // LANE
    return pl.pallas_call(
        functools.partial(_diff_attn_kernel, n_ctx_tiles=ctx // tq, ctx=ctx, post_scale=1.0 - lam_init),
        grid=(b, DIFF_HEADS, ta // tq),
        in_specs=[pl.BlockSpec((1, tq, LANE), lambda bb, h, i: (bb, i, cb + h)),
                  pl.BlockSpec((1, ta, LANE), lambda bb, h, i: (bb, 0, cb + DIFF_HEADS + h)),
                  pl.BlockSpec((1, ta, LANE), lambda bb, h, i: (bb, 0, cb + 2 * DIFF_HEADS + h)),
                  pl.BlockSpec((tq, LANE), lambda bb, h, i: (i, 0)),
                  pl.BlockSpec((tq, LANE), lambda bb, h, i: (i, 0)),
                  pl.BlockSpec((ta, LANE), lambda bb, h, i: (0, 0)),
                  pl.BlockSpec((ta, LANE), lambda bb, h, i: (0, 0)),
                  pl.BlockSpec((1, LANE), lambda bb, h, i: (0, 0)),
                  pl.BlockSpec((1, LANE), lambda bb, h, i: (0, 0))],
        out_specs=pl.BlockSpec((1, tq, LANE), lambda bb, h, i: (bb, i, h)),
        out_shape=jax.ShapeDtypeStruct((b, ta, 512), BF16),
        scratch_shapes=[pltpu.VMEM((ta, LANE), BF16), pltpu.VMEM((ta, LANE), BF16)],
        compiler_params=_cparams(("parallel", "parallel", "arbitrary")),
        name="diff_attn",
    )(p, p, p, cosf, sinf, cosf, sinf, lamv, subln)


def _swa_kernel(q_ref, k_ref, v_ref, cq_ref, sq_ref, ck_ref, sk_ref, sink_ref,
                o_ref, ks_ref, vs_ref, *, ctx, nsb):
    t = pl.program_id(1)
    n_ctx_steps = ctx // (WINDOW * nsb)
    ta = ks_ref.shape[0]
    span = 3 * WINDOW

    @pl.when(t == 0)
    def _():
        ks_ref[...] = _rope(k_ref[0], ck_ref[...], sk_ref[...]).astype(BF16)
        vs_ref[...] = v_ref[0].astype(BF16)

    lane = lax.broadcasted_iota(jnp.int32, (WINDOW, LANE), 1)

    def run(local, sb):
        rows = slice(sb * WINDOW, (sb + 1) * WINDOW)
        cq = cq_ref[rows, :]
        sq = sq_ref[rows, :]
        qblk = [_rope(q_ref[0, rows, hb * LANE:(hb + 1) * LANE], cq, sq) * (SWA_HEAD ** -0.5)
                for hb in range(SWA_HEADS // 2)]

        def head_q(hk, g):
            h = hk * SWA_GROUP + g
            blk = qblk[h // 2]
            if h % 2 != hk:
                blk = pltpu.roll(blk, 64, 1)
            keep = (lane >= 64) if hk == 1 else (lane < 64)
            return jnp.where(keep, blk, 0.0)

        kc = ks_ref[:ctx, :]
        vc = vs_ref[:ctx, :]
        if local:
            n = (t - n_ctx_steps) * nsb + sb
            ws = jnp.clip(ctx + (n - 1) * WINDOW, ctx, ta - span)
            ws = pl.multiple_of(ws, WINDOW)
            kl = ks_ref[pl.ds(ws, span), :]
            vl = vs_ref[pl.ds(ws, span), :]
            kpos = (ws - ctx) + lax.broadcasted_iota(jnp.int32, (1, span), 1)
            qpos = n * WINDOW + lax.broadcasted_iota(jnp.int32, (SWA_GROUP * WINDOW, 1), 0) % WINDOW
            valid = jnp.abs(kpos - qpos) <= WINDOW
        outs = {}
        for hk in range(SWA_KV_HEADS):
            qst = jnp.concatenate([head_q(hk, g) for g in range(SWA_GROUP)], axis=0).astype(BF16)
            sink = jnp.concatenate(
                [jnp.broadcast_to(sink_ref[hk * SWA_GROUP + g:hk * SWA_GROUP + g + 1, 0:1], (WINDOW, 1))
                 for g in range(SWA_GROUP)], axis=0)
            s_c = _dot_nt(qst, kc)
            m = jnp.maximum(jnp.max(s_c, -1, keepdims=True), sink)
            if local:
                s_l = jnp.where(valid, _dot_nt(qst, kl), NEG_INF)
                m = jnp.maximum(m, jnp.max(s_l, -1, keepdims=True))
            p_c = jnp.exp(s_c - m)
            l = jnp.sum(p_c, -1, keepdims=True) + jnp.exp(sink - m)
            acc = _dot(p_c.astype(BF16), vc)
            if local:
                p_l = jnp.exp(s_l - m)
                l = l + jnp.sum(p_l, -1, keepdims=True)
                acc = acc + _dot(p_l.astype(BF16), vl)
            o = acc / l
            for g in range(SWA_GROUP):
                res = o[g * WINDOW:(g + 1) * WINDOW, :]
                if g % 2 != hk:
                    res = pltpu.roll(res, 64, 1)
                outs[hk * SWA_GROUP + g] = res
        for hb in range(SWA_HEADS // 2):
            blk = jnp.where(lane < 64, outs[2 * hb], outs[2 * hb + 1])
            o_ref[0, rows, hb * LANE:(hb + 1) * LANE] = blk.astype(o_ref.dtype)

    @pl.when(t < n_ctx_steps)
    def _():
        for sb in range(nsb):
            run(False, sb)

    @pl.when(t >= n_ctx_steps)
    def _():
        for sb in range(nsb):
            run(True, sb)


def _swa(p, cosf, sinf, sinkv, ctx):
    b, ta, _ = p.shape
    qb = SEG_SW // 512
    kb = (SEG_SW + 512) // LANE
    nsb = 2 if (ctx % (2 * WINDOW) == 0 and ta % (2 * WINDOW) == 0) else 1
    tq = nsb * WINDOW
    return pl.pallas_call(
        functools.partial(_swa_kernel, ctx=ctx, nsb=nsb),
        grid=(b, ta // tq),
        in_specs=[pl.BlockSpec((1, tq, 512), lambda bb, t: (bb, t, qb)),
                  pl.BlockSpec((1, ta, LANE), lambda bb, t: (bb, 0, kb)),
                  pl.BlockSpec((1, ta, LANE), lambda bb, t: (bb, 0, kb + 1)),
                  pl.BlockSpec((tq, LANE), lambda bb, t: (t, 0)),
                  pl.BlockSpec((tq, LANE), lambda bb, t: (t, 0)),
                  pl.BlockSpec((ta, LANE), lambda bb, t: (0, 0)),
                  pl.BlockSpec((ta, LANE), lambda bb, t: (0, 0)),
                  pl.BlockSpec((SWA_HEADS, LANE), lambda bb, t: (0, 0))],
        out_specs=pl.BlockSpec((1, tq, 512), lambda bb, t: (bb, t, 0)),
        out_shape=jax.ShapeDtypeStruct((b, ta, 512), BF16),
        scratch_shapes=[pltpu.VMEM((ta, LANE), BF16), pltpu.VMEM((ta, LANE), BF16)],
        compiler_params=_cparams(("parallel", "arbitrary")),
        name="swa_attn",
    )(p, p, p, cosf, sinf, cosf, sinf, sinkv)


def _softplus(x):
    return jnp.maximum(x, 0.0) + jnp.log(1.0 + jnp.exp(-jnp.abs(x)))


def _rwkv_prep_kernel(z_ref, hp_ref, hn_ref, mu_ref, w0_ref, a0_ref, wl_ref, al_ref, gl_ref, kk_ref, ka_ref,
                      rk_ref, bd_ref, r_o, v_o, kk_o, ld0_o, ld1_o, kd0_o, kd1_o, a0_o, a1_o, bv_o, g_o, zs_ref):
    tm = z_ref.shape[1]
    w = RWKV_W
    z = z_ref[0]
    zs_ref[SUBLANE:SUBLANE + tm, :] = z
    zs_ref[SUBLANE - 1:SUBLANE, :] = hp_ref[0, 0]
    zs_ref[SUBLANE + tm:SUBLANE + tm + 1, :] = hn_ref[0, 0]
    shifted = 0.5 * (zs_ref[SUBLANE - 1:SUBLANE - 1 + tm, :] + zs_ref[SUBLANE + 1:SUBLANE + 1 + tm, :])
    seg = z + (shifted - z) * mu_ref[...]
    r = seg[:, 0:w]
    k = seg[:, w:2 * w]
    v = seg[:, 2 * w:3 * w]
    lo = seg[:, 3 * w:3 * w + LANE]
    gl = seg[:, 3 * w + LANE:3 * w + 2 * LANE]
    hp = dict(precision=HIGHEST)
    bd = bd_ref[...]
    kk = k * kk_ref[...]
    kk = kk / jnp.maximum(jnp.sqrt(_dot(kk * kk, bd, **hp)), 1e-12)
    th = jnp.tanh(lo).astype(BF16)
    lo_b = lo.astype(BF16)
    ksum = jnp.zeros_like(k)
    for d, (ld_o, kd_o, a_o) in enumerate(((ld0_o, kd0_o, a0_o), (ld1_o, kd1_o, a1_o))):
        wd = w0_ref[d:d + 1, :] + _dot(th, wl_ref[d])
        wd = -_softplus(-wd) - 0.5
        a = _sigmoid(a0_ref[d:d + 1, :] + _dot(lo_b, al_ref[d]))
        kd = k * (1.0 + (a - 1.0) * ka_ref[...])
        ld_o[0] = -jnp.exp(wd)
        kd_o[0] = kd
        a_o[0] = a
        ksum = ksum + kd
    r_o[0] = r
    v_o[0] = v
    kk_o[0] = kk
    bv_o[0] = _dot(r * ksum * rk_ref[...], bd, **hp) * v
    g_o[0] = _dot(_sigmoid(gl).astype(BF16), gl_ref[...])


def _rwkv_prep(p, hprev, hnext, mu, w0, a0, wl, al, gl, k_k, k_a, r_k, bd, tm):
    b, ta, _ = p.shape
    w = RWKV_W
    row = lambda bb, i: (bb, i, 0)
    vec = lambda n: pl.BlockSpec((1, n), lambda bb, i: (0, 0))
    out = jax.ShapeDtypeStruct((b, ta, w), F32)
    return pl.pallas_call(
        _rwkv_prep_kernel,
        grid=(b, ta // tm),
        in_specs=[pl.BlockSpec((1, tm, RW_COLS), lambda bb, i: (bb, i, SEG_RW // RW_COLS)),
                  pl.BlockSpec((1, 1, 1, RW_COLS), lambda bb, i: (bb, i, 0, 0)),
                  pl.BlockSpec((1, 1, 1, RW_COLS), lambda bb, i: (bb, i, 0, 0)),
                  vec(RW_COLS),
                  pl.BlockSpec((2, w), lambda bb, i: (0, 0)),
                  pl.BlockSpec((2, w), lambda bb, i: (0, 0)),
                  pl.BlockSpec((2, LANE, w), lambda bb, i: (0, 0, 0)),
                  pl.BlockSpec((2, LANE, w), lambda bb, i: (0, 0, 0)),
                  pl.BlockSpec((LANE, w), lambda bb, i: (0, 0)),
                  vec(w), vec(w), vec(w),
                  pl.BlockSpec((w, w), lambda bb, i: (0, 0))],
        out_specs=[pl.BlockSpec((1, tm, w), row)] * 11,
        out_shape=[out] * 11,
        scratch_shapes=[pltpu.VMEM((tm + 2 * SUBLANE, RW_COLS), F32)],
        compiler_params=_cparams(("parallel", "parallel")),
        name="rwkv_prep",
    )(p, hprev, hnext, mu, w0, a0, wl, al, gl, k_k, k_a, r_k, bd)


def _split3(x):
    hi = x.astype(BF16)
    r1 = x - hi.astype(F32)
    mid = r1.astype(BF16)
    return hi, mid, (r1 - mid.astype(F32)).astype(BF16)


def _rwkv_chunk_kernel(r_ref, v_ref, kk_ref, ld0_ref, ld1_ref, kd0_ref, kd1_ref, a0_ref, a1_ref, pz_ref, wu_ref):
    C = RWKV_CHUNK
    hd = RWKV_HEAD
    row = lax.broadcasted_iota(jnp.int32, (C, C), 0)
    col = lax.broadcasted_iota(jnp.int32, (C, C), 1)
    eye = (row == col).astype(F32)
    probs = []
    for d, (ld_ref, kd_ref, a_ref) in enumerate(((ld0_ref, kd0_ref, a0_ref), (ld1_ref, kd1_ref, a1_ref))):
        incl = (row >= col) if d == 0 else (row <= col)
        strict = (row > col) if d == 0 else (row < col)
        last = C - 1 if d == 0 else 0
        ones = jnp.where(incl, 1.0, 0.0).astype(BF16)
        cs_all = sum(_dot(ones, part) for part in _split3(ld_ref[0]))
        for h in range(RWKV_HEADS):
            sl = slice(h * hd, (h + 1) * hd)
            probs.append(dict(d=d, h=h, incl=incl, strict=strict, last=last, cs=cs_all[:, sl],
                              r=r_ref[0, :, sl], v=v_ref[0, :, sl], kk=kk_ref[0, :, sl],
                              ld=ld_ref[0, :, sl], k=kd_ref[0, :, sl], a=a_ref[0, :, sl]))
    bf = lambda z: z.astype(BF16)
    for q in probs:
        cs = q["cs"]
        e_cs = jnp.exp(cs)
        e_inv = jnp.exp(-cs)
        q["g_end"] = e_cs[q["last"]:q["last"] + 1, :]
        q["kkd"] = q["kk"] * jnp.exp(cs - q["ld"])
        q["rd"] = q["r"] * e_cs
        q["bi"] = bf(q["kk"] * q["a"] * e_inv)
        q["ki"] = bf(q["k"] * e_inv)
        q["lhs"] = bf(jnp.concatenate([q["kkd"], q["rd"]], axis=0))
    for q in probs:
        gb = _dot_nt(q["lhs"], q["bi"])
        gk = _dot_nt(q["lhs"], q["ki"])
        q["pw"] = -jnp.where(q["strict"], gb[:C], 0.0)
        q["q_b"] = bf(jnp.where(q["incl"], gb[C:], 0.0))
        q["akqk"] = bf(jnp.concatenate([jnp.where(q["strict"], gk[:C], 0.0),
                                        jnp.where(q["incl"], gk[C:], 0.0)], axis=0))
        q["tinv"] = eye + q["pw"]
    for _ in range(C.bit_length() - 2):
        for q in probs:
            pw = bf(q["pw"])
            q["pw"] = _dot(pw, pw)
        for q in probs:
            q["tinv"] = q["tinv"] + _dot(bf(q["tinv"]), bf(q["pw"]))
    for q in probs:
        q["vs"] = bf(q["v"])
        q["av"] = _dot(q["akqk"], q["vs"])
    for q in probs:
        tinv = bf(q["tinv"])
        q["w1"] = _dot(tinv, bf(q["kkd"]))
        q["u1"] = _dot(tinv, bf(q["av"][:C]))
    for q in probs:
        w1 = bf(q["w1"])
        u1 = bf(q["u1"])
        w2 = q["rd"] - _dot(q["q_b"], w1)
        u2 = q["av"][C:] - _dot(q["q_b"], u1)
        pt = (eye[:hd, :hd] - _dot_tn(w1, q["bi"])) * q["g_end"]
        zt = (_dot_tn(q["vs"], q["ki"]) - _dot_tn(u1, q["bi"])) * q["g_end"]
        pz_ref[q["d"], 0, 0, q["h"]] = jnp.concatenate([pt, zt], axis=1)
        wu_ref[q["d"], 0, 0, q["h"]] = jnp.concatenate([w2, u2], axis=1)


def _rwkv_chunks(r, v, kk, ld0, ld1, kd0, kd1, a0, a1):
    b, ta, w = r.shape
    nc = ta // RWKV_CHUNK
    spec = pl.BlockSpec((1, RWKV_CHUNK, w), lambda bb, c: (bb, c, 0))

    def out(rows):
        return (pl.BlockSpec((2, 1, 1, RWKV_HEADS, rows, 2 * RWKV_HEAD), lambda bb, c: (0, bb, c, 0, 0, 0)),
                jax.ShapeDtypeStruct((2, b, nc, RWKV_HEADS, rows, 2 * RWKV_HEAD), F32))

    (pz_spec, pz_shape), (wu_spec, wu_shape) = out(RWKV_HEAD), out(RWKV_CHUNK)
    return pl.pallas_call(
        _rwkv_chunk_kernel,
        grid=(b, nc),
        in_specs=[spec] * 9,
        out_specs=[pz_spec, wu_spec],
        out_shape=[pz_shape, wu_shape],
        compiler_params=_cparams(("parallel", "parallel")),
        name="rwkv_chunks",
    )(r, v, kk, ld0, ld1, kd0, kd1, a0, a1)


def _rwkv_state_kernel(pz0_ref, pz1_ref, wu0_ref, wu1_ref, y0_ref, y1_ref, st_ref):
    i = pl.program_id(0)
    hd = RWKV_HEAD
    nb = st_ref.shape[1]

    @pl.when(i == 0)
    def _():
        st_ref[...] = jnp.zeros_like(st_ref)

    probs = []
    for d, (pz_ref, wu_ref) in enumerate(((pz0_ref, wu0_ref), (pz1_ref, wu1_ref))):
        for bb in range(nb):
            for h in range(RWKV_HEADS):
                pz = pz_ref[0, bb, 0, h]
                wu = wu_ref[0, bb, 0, h]
                probs.append(dict(d=d, b=bb, h=h, pt=pz[:, :hd].astype(BF16), zt=pz[:, hd:],
                                  w2=wu[:, :hd].astype(BF16), u2=wu[:, hd:], s=st_ref[d, bb, h].astype(BF16)))
    for q in probs:
        q["y"] = _dot_nt(q["w2"], q["s"]) + q["u2"]
    for q in probs:
        st_ref[q["d"], q["b"], q["h"]] = _dot(q["s"], q["pt"]) + q["zt"]
    for d, y_ref in enumerate((y0_ref, y1_ref)):
        for bb in range(nb):
            ys = [q["y"] for q in probs if q["d"] == d and q["b"] == bb]
            for hp in range(RWKV_HEADS // 2):
                y_ref[bb, :, hp * LANE:(hp + 1) * LANE] = jnp.concatenate([ys[2 * hp], ys[2 * hp + 1]], axis=1)


def _rwkv_state(pz, wu, ctx):
    _, b, nc, nh, hd, _ = pz.shape
    ncc = ctx // RWKV_CHUNK

    def rev(i):
        return jnp.where(i < ncc, ncc - 1 - i, nc - 1 - (i - ncc))

    blk = (1, b, 1, nh, hd, 2 * hd)
    wblk = (1, b, 1, nh, RWKV_CHUNK, 2 * hd)
    fwd = lambda i: (0, 0, i, 0, 0, 0)
    bwd = lambda i: (1, 0, rev(i), 0, 0, 0)
    yshape = jax.ShapeDtypeStruct((b, nc * RWKV_CHUNK, nh * hd), F32)
    return pl.pallas_call(
        _rwkv_state_kernel,
        grid=(nc,),
        in_specs=[pl.BlockSpec(blk, fwd), pl.BlockSpec(blk, bwd), pl.BlockSpec(wblk, fwd), pl.BlockSpec(wblk, bwd)],
        out_specs=[pl.BlockSpec((b, RWKV_CHUNK, nh * hd), lambda i: (0, i, 0)),
                   pl.BlockSpec((b, RWKV_CHUNK, nh * hd), lambda i: (0, rev(i), 0))],
        out_shape=[yshape, yshape],
        scratch_shapes=[pltpu.VMEM((2, b, nh, hd, hd), F32)],
        compiler_params=_cparams(("arbitrary",)),
        name="rwkv_state",
    )(pz, pz, wu, wu)


def _rwkv_out_kernel(y0_ref, y1_ref, bv_ref, g_ref, lg_ref, lb_ref, bd_ref, o_ref):
    hp = dict(precision=HIGHEST)
    bdm = bd_ref[...] * (1.0 / RWKV_HEAD)
    y = y0_ref[0] + y1_ref[0]
    yc = y - _dot(y, bdm, **hp)
    var = _dot(yc * yc, bdm, **hp)
    yn = yc * lax.rsqrt(var + RWKV_GN_EPS) * lg_ref[...] + lb_ref[...]
    o_ref[0] = ((yn + bv_ref[0]) * g_ref[0]).astype(o_ref.dtype)


def _rwkv_out(y0, y1, bv, g, ln_g, ln_b, bd, tm):
    b, ta, w = y0.shape
    row = pl.BlockSpec((1, tm, w), lambda bb, i: (bb, i, 0))
    vec = pl.BlockSpec((1, w), lambda bb, i: (0, 0))
    return pl.pallas_call(
        _rwkv_out_kernel,
        grid=(b, ta // tm),
        in_specs=[row, row, row, row, vec, vec, pl.BlockSpec((w, w), lambda bb, i: (0, 0))],
        out_specs=row,
        out_shape=jax.ShapeDtypeStruct((b, ta, w), BF16),
        compiler_params=_cparams(("parallel", "parallel")),
        name="rwkv_out",
    )(y0, y1, bv, g, ln_g, ln_b, bd)


def _rwkv_mixer(p, ctx, mu, w0, w_lora, a0, a_lora, g_lora, k_k, k_a, r_k, ln_g, ln_b):
    b, ta, _ = p.shape
    w = RWKV_W
    tm = _tile(math.gcd(ta, ctx), 256)
    nt = ta // tm
    edge_lo = p[:, 0::tm, SEG_RW:]
    edge_hi = p[:, tm - 1::tm, SEG_RW:]
    tile_id = jnp.arange(nt)[None, :, None]
    hprev = jnp.where((tile_id == 0) | (tile_id == ctx // tm), 0.0, jnp.roll(edge_hi, 1, axis=1))
    hnext = jnp.where((tile_id == nt - 1) | (tile_id == ctx // tm - 1), 0.0, jnp.roll(edge_lo, -1, axis=1))
    def rows_at(mat, start):
        return jnp.pad(mat.astype(BF16), ((start, LANE - start - mat.shape[0]), (0, 0)))

    wl = jnp.stack([rows_at(w_lora[d], d * DECAY_LORA) for d in range(2)])
    al = jnp.stack([rows_at(a_lora[d], 2 * DECAY_LORA + d * ICLR_LORA) for d in range(2)])
    gl = rows_at(g_lora, 0)
    mu_p = jnp.pad(mu, (0, RW_COLS - RWKV_STREAM)).reshape(1, RW_COLS)
    head_id = jnp.arange(w) // RWKV_HEAD
    bd = (head_id[:, None] == head_id[None, :]).astype(F32)
    r, v, kk, ld0, ld1, kd0, kd1, a0_, a1_, bv, g = _rwkv_prep(
        p, hprev[:, :, None, :], hnext[:, :, None, :], mu_p, w0, a0, wl, al, gl,
        k_k.reshape(1, w), k_a.reshape(1, w), r_k.reshape(1, w), bd, tm)
    pz, wu = _rwkv_chunks(r, v, kk, ld0, ld1, kd0, kd1, a0_, a1_)
    y0, y1 = _rwkv_state(pz, wu, ctx)
    return _rwkv_out(y0, y1, bv, g, ln_g.reshape(1, w), ln_b.reshape(1, w), bd, tm)


def _merge_a_kernel(x_ref, ml_ref, mc_ref, o4_ref, wg_ref, wb_ref, out_ref, xn_ref, acc_ref, *, tm, ctx):
    i = pl.program_id(1)
    j = pl.program_id(2)
    br = pl.program_id(3)

    @pl.when((j == 0) & (br == 0))
    def _():
        _fill_modulated(xn_ref, x_ref, ml_ref[0], mc_ref[0], i * tm, ctx)

    val = _sigmoid(_dot(xn_ref[...], wg_ref[...])) * _dot(o4_ref[0, 0], wb_ref[0])

    @pl.when(br == 0)
    def _():
        acc_ref[...] = val

    @pl.when(br > 0)
    def _():
        acc_ref[...] += val

    @pl.when(br == N_BRANCH - 1)
    def _():
        out_ref[0] = acc_ref[...].astype(out_ref.dtype)


def _gate_window_weights(w_in_l, gate_off, w_branch_l, w_out_l):
    d = w_in_l.shape[0]
    shift = gate_off % LANE
    start = gate_off - shift
    gw = d + 2 * LANE
    tail = gw - d - shift
    wsrc = w_in_l[:, start:].astype(BF16)
    wsrc = jnp.pad(wsrc, ((0, 0), (0, (N_BRANCH - 1) * d + gw - wsrc.shape[1])))
    wg = jnp.concatenate([wsrc[:, br * d:br * d + gw] for br in range(N_BRANCH)], axis=1)
    wb = jnp.pad(w_branch_l.astype(BF16), ((0, 0), (0, 0), (shift, tail)))
    wo = jnp.pad(w_out_l.astype(BF16), ((shift, tail), (0, 0)))
    return wg, wb, wo


def _merge_a(x_all, modl, modc, o4, wg, wb, ctx):
    b, ta, d = x_all.shape
    gw = wb.shape[2]
    tm = _tile(ta, 1088)
    tn = _tile(gw, 768, LANE)
    return pl.pallas_call(
        functools.partial(_merge_a_kernel, tm=tm, ctx=ctx),
        grid=(b, ta // tm, gw // tn, N_BRANCH),
        in_specs=[pl.BlockSpec((1, tm, d), lambda bb, i, j, br: (bb, i, 0)),
                  pl.BlockSpec((1, 6, d), lambda bb, i, j, br: (bb, 0, 0)),
                  pl.BlockSpec((1, 6, d), lambda bb, i, j, br: (0, 0, 0)),
                  pl.BlockSpec((1, 1, tm, BRANCH_W), lambda bb, i, j, br: (br, bb, i, 0)),
                  pl.BlockSpec((d, tn), lambda bb, i, j, br: (0, br * (gw // tn) + j)),
                  pl.BlockSpec((1, BRANCH_W, tn), lambda bb, i, j, br: (br, 0, j))],
        out_specs=pl.BlockSpec((1, tm, tn), lambda bb, i, j, br: (bb, i, j)),
        out_shape=jax.ShapeDtypeStruct((b, ta, gw), BF16),
        scratch_shapes=[pltpu.VMEM((tm, d), BF16), pltpu.VMEM((tm, tn), F32)],
        compiler_params=_cparams(("parallel", "parallel", "arbitrary", "arbitrary"), 56),
        name="merge_gate",
    )(x_all, modl, modc, o4, wg, wb)


def _merge_b_kernel(m_ref, w_ref, x_ref, ml_ref, mc_ref, lg_ref, lb_ref, xo_ref, h2_ref, *, tm, ctx, alpha):
    i = pl.program_id(1)
    isc = _row_is_ctx(tm, i * tm, ctx)
    ml = ml_ref[0]
    mc = mc_ref[0]
    y = _dot(m_ref[0], w_ref[...])
    g1 = jnp.where(isc, mc[2:3], ml[2:3])
    xn = _ln(alpha * x_ref[0] + g1 * y, POST_LN_EPS) * lg_ref[...] + lb_ref[...]
    xo_ref[0] = xn
    h2_ref[0] = _pack_halves(_modulated(xn, ml, mc, 3, 4, isc))


def _merge_b(merged, w_out, x_all, modl, modc, ln_g, ln_b, ctx, alpha):
    b, ta, d = x_all.shape
    gw = w_out.shape[0]
    tm = _tile(ta, 272)
    row = lambda bb, i: (bb, i, 0)
    return pl.pallas_call(
        functools.partial(_merge_b_kernel, tm=tm, ctx=ctx, alpha=alpha),
        grid=(b, ta // tm),
        in_specs=[pl.BlockSpec((1, tm, gw), row),
                  pl.BlockSpec((gw, d), lambda bb, i: (0, 0)),
                  pl.BlockSpec((1, tm, d), row),
                  pl.BlockSpec((1, 6, d), lambda bb, i: (bb, 0, 0)),
                  pl.BlockSpec((1, 6, d), lambda bb, i: (0, 0, 0)),
                  pl.BlockSpec((1, d), lambda bb, i: (0, 0)),
                  pl.BlockSpec((1, d), lambda bb, i: (0, 0))],
        out_specs=[pl.BlockSpec((1, tm, d), row), pl.BlockSpec((1, tm, d // 2), row)],
        out_shape=[jax.ShapeDtypeStruct((b, ta, d), F32), jax.ShapeDtypeStruct((b, ta, d // 2), jnp.uint32)],
        compiler_params=_cparams(("parallel", "parallel")),
        name="merge_out_ln",
    )(merged, w_out, x_all, modl, modc, ln_g.reshape(1, d), ln_b.reshape(1, d))


def _router_kernel(x_ref, ml_ref, mc_ref, rw_ref, rb_ref, idx_ref, wt_ref, rank_ref, cnt_ref, carry_ref,
                   *, tm, ctx):
    i = pl.program_id(1)

    @pl.when((pl.program_id(0) == 0) & (i == 0))
    def _():
        carry_ref[...] = jnp.zeros_like(carry_ref)

    isc = _row_is_ctx(tm, i * tm, ctx)
    h2 = _modulated(x_ref[0], ml_ref[0], mc_ref[0], 3, 4, isc)
    scores = _sigmoid(_dot(h2, rw_ref[...], precision=HIGHEST))
    biased = scores + rb_ref[...]
    lane = lax.broadcasted_iota(jnp.int32, scores.shape, 1)
    grp = lane // GROUP_SIZE
    ninf = -jnp.inf

    def first_argmax(z):
        m = jnp.max(z, -1, keepdims=True)
        idx = jnp.min(jnp.where(z == m, lane, N_EXPERTS), -1, keepdims=True)
        return m, idx

    gscore = jnp.zeros_like(biased)
    for g in range(N_GROUPS):
        zg = jnp.where(grp == g, biased, ninf)
        m1, i1 = first_argmax(zg)
        m2 = jnp.max(jnp.where(lane == i1, ninf, zg), -1, keepdims=True)
        gscore = jnp.where(grp == g, m1 + m2, gscore)
    cand = jnp.where(lane % GROUP_SIZE == 0, gscore, ninf)
    gsel = jnp.zeros(scores.shape, jnp.bool_)
    for _ in range(TOPK_GROUPS):
        _, ig = first_argmax(cand)
        gsel = gsel | (grp == ig // GROUP_SIZE)
        cand = jnp.where(lane == ig, ninf, cand)
    masked = jnp.where(gsel, biased, ninf)
    out_lane = lax.broadcasted_iota(jnp.int32, (tm, LANE), 1)
    idx_out = jnp.zeros((tm, LANE), jnp.int32)
    w_out = jnp.zeros((tm, LANE), F32)
    w_sum = jnp.zeros((tm, 1), F32)
    chosen = []
    picked = jnp.zeros(scores.shape, F32)
    for kq in range(TOP_K):
        _, ie = first_argmax(masked)
        sel = lane == ie
        wk = jnp.sum(jnp.where(sel, scores, 0.0), -1, keepdims=True)
        masked = jnp.where(sel, ninf, masked)
        idx_out = jnp.where(out_lane == kq, ie, idx_out)
        w_out = jnp.where(out_lane == kq, wk, w_out)
        w_sum = w_sum + wk
        chosen.append(sel)
        picked = jnp.where(sel, 1.0, picked)
    idx_ref[0] = idx_out
    wt_ref[0] = w_out / w_sum * ROUTED_SCALE
    r_i = lax.broadcasted_iota(jnp.int32, (tm, tm), 0)
    c_i = lax.broadcasted_iota(jnp.int32, (tm, tm), 1)
    before = _dot(jnp.where(r_i > c_i, 1.0, 0.0).astype(BF16), picked.astype(BF16)) + carry_ref[...]
    rank_out = jnp.zeros((tm, LANE), jnp.int32)
    for kq in range(TOP_K):
        rk = jnp.sum(jnp.where(chosen[kq], before, 0.0), -1, keepdims=True)
        rank_out = jnp.where(out_lane == kq, rk.astype(jnp.int32), rank_out)
    rank_ref[0] = rank_out
    carry_ref[...] += jnp.sum(picked, axis=0, keepdims=True)
    cnt_ref[...] = carry_ref[...].astype(jnp.int32)


def _router(x_all, modl, modc, rw, rb, ctx):
    b, ta, d = x_all.shape
    tm = _tile(ta, 544)
    row = lambda bb, i: (bb, i, 0)
    tok = jax.ShapeDtypeStruct((b, ta, LANE), jnp.int32)
    return pl.pallas_call(
        functools.partial(_router_kernel, tm=tm, ctx=ctx),
        grid=(b, ta // tm),
        in_specs=[pl.BlockSpec((1, tm, d), row),
                  pl.BlockSpec((1, 6, d), lambda bb, i: (bb, 0, 0)),
                  pl.BlockSpec((1, 6, d), lambda bb, i: (0, 0, 0)),
                  pl.BlockSpec((d, N_EXPERTS), lambda bb, i: (0, 0)),
                  pl.BlockSpec((1, N_EXPERTS), lambda bb, i: (0, 0))],
        out_specs=[pl.BlockSpec((1, tm, LANE), row), pl.BlockSpec((1, tm, LANE), row),
                   pl.BlockSpec((1, tm, LANE), row), pl.BlockSpec((1, N_EXPERTS), lambda bb, i: (0, 0))],
        out_shape=[tok, jax.ShapeDtypeStruct((b, ta, LANE), F32), tok,
                   jax.ShapeDtypeStruct((1, N_EXPERTS), jnp.int32)],
        scratch_shapes=[pltpu.VMEM((1, N_EXPERTS), F32)],
        compiler_params=_cparams(("arbitrary", "arbitrary")),
        name="router",
    )(x_all, modl, modc, rw, rb.reshape(1, N_EXPERTS))


def _row_copy(src_ref, src_row, dst_ref, dst_row, sem):
    return pltpu.make_async_copy(src_ref.at[pl.ds(src_row, 1)], dst_ref.at[pl.ds(dst_row, 1)], sem)


def _dispatch_kernel(pos_ref, h2_ref, xg_in_ref, xg_ref, sem, *, tm):
    del xg_in_ref

    def issue(t, carry):
        for kq in range(TOP_K):
            _row_copy(h2_ref, t, xg_ref, pos_ref[0, 0, t * TOP_K + kq], sem).start()
        return carry

    lax.fori_loop(0, tm, issue, 0, unroll=4)
    for _ in range(TOP_K):
        pltpu.make_async_copy(h2_ref, xg_ref.at[pl.ds(0, tm)], sem).wait()


def _dispatch(pos, h2p, n_rows, tm):
    n, half = h2p.shape
    xg0 = jnp.zeros((n_rows, half), jnp.uint32)
    return pl.pallas_call(
        functools.partial(_dispatch_kernel, tm=tm),
        grid=(n // tm,),
        in_specs=[pl.BlockSpec((1, 1, tm * TOP_K), lambda i: (i, 0, 0), memory_space=pltpu.SMEM),
                  pl.BlockSpec((tm, half), lambda i: (i, 0)),
                  pl.BlockSpec(memory_space=pl.ANY)],
        out_specs=pl.BlockSpec(memory_space=pl.ANY),
        out_shape=jax.ShapeDtypeStruct((n_rows, half), jnp.uint32),
        scratch_shapes=[pltpu.SemaphoreType.DMA(())],
        input_output_aliases={2: 0},
        compiler_params=_cparams(("arbitrary",)),
        name="moe_dispatch",
    )(pos, h2p, xg0)


def _swiglu(x_lo, x_hi, wgu, wdn, ff):
    half = x_lo.shape[1]
    h = _dot(x_lo, wgu[:half, :]) + _dot(x_hi, wgu[half:, :])
    g = h[:, :ff]
    return _dot((g * _sigmoid(g) * h[:, ff:]).astype(BF16), wdn)


def _experts_kernel(be_ref, nu_ref, first_ref, slot_ref, nxt_ref, x_ref, wgu_hbm, wdn_hbm, o_ref,
                    gu_buf, dn_buf, wgu_s, wdn_s, sems, *, ff, l):
    i = pl.program_id(0)

    def weight_copies(e, s):
        return (pltpu.make_async_copy(wgu_hbm.at[l, e], gu_buf.at[s], sems.at[0, s]),
                pltpu.make_async_copy(wdn_hbm.at[l, e], dn_buf.at[s], sems.at[1, s]))

    @pl.when(i == 0)
    def _():
        for cp in weight_copies(be_ref[0], slot_ref[0]):
            cp.start()

    @pl.when(first_ref[i] == 1)
    def _():
        s = slot_ref[i]
        for cp in weight_copies(be_ref[i], s):
            cp.wait()
        wgu_s[...] = gu_buf[s].astype(BF16)
        wdn_s[...] = dn_buf[s].astype(BF16)

        @pl.when(nxt_ref[i] >= 0)
        def _():
            for cp in weight_copies(nxt_ref[i], 1 - s):
                cp.start()

    @pl.when(i < nu_ref[0])
    def _():
        lo, hi = _unpack_halves(x_ref[...])
        o_ref[...] = _pack_halves(_swiglu(lo.astype(BF16), hi.astype(BF16), wgu_s[...], wdn_s[...], ff))

    @pl.when(i >= nu_ref[0])
    def _():
        o_ref[...] = jnp.zeros_like(o_ref)


def _experts(xg, blk_e, n_used, w_gu, w_dn, l, blk):
    n_rows, half = xg.shape
    d = 2 * half
    ff = w_dn.shape[2]
    n_blocks = n_rows // blk
    first = jnp.concatenate([jnp.ones((1,), jnp.int32), (blk_e[1:] != blk_e[:-1]).astype(jnp.int32)])
    slot = (jnp.cumsum(first) - 1) % 2
    n_le = jnp.sum(blk_e[None, :] <= blk_e[:, None], axis=1)
    pick = jnp.arange(n_blocks)[None, :] == n_le[:, None]
    nxt = jnp.where(n_le < n_blocks, jnp.sum(jnp.where(pick, blk_e[None, :], 0), axis=1), -1)
    grid_spec = pltpu.PrefetchScalarGridSpec(
        num_scalar_prefetch=5,
        grid=(n_blocks,),
        in_specs=[pl.BlockSpec((blk, half), lambda i, *_: (i, 0)),
                  pl.BlockSpec(memory_space=pl.ANY),
                  pl.BlockSpec(memory_space=pl.ANY)],
        out_specs=pl.BlockSpec((blk, half), lambda i, *_: (i, 0)),
        scratch_shapes=[pltpu.VMEM((2, d, 2 * ff), F32), pltpu.VMEM((2, ff, d), F32),
                        pltpu.VMEM((d, 2 * ff), BF16), pltpu.VMEM((ff, d), BF16),
                        pltpu.SemaphoreType.DMA((2, 2))],
    )
    return pl.pallas_call(
        functools.partial(_experts_kernel, ff=ff, l=l),
        grid_spec=grid_spec,
        out_shape=jax.ShapeDtypeStruct((n_rows, half), jnp.uint32),
        compiler_params=_cparams(("arbitrary",), 56),
        name="moe_experts",
    )(blk_e, n_used, first, slot.astype(jnp.int32), nxt.astype(jnp.int32), xg, w_gu, w_dn)


def _moe_plan(idx, rank, counts, blk):
    n_assign = idx.size
    padded = (counts + blk - 1) // blk * blk
    pad_end = jnp.cumsum(padded)
    start_pad = pad_end - padded
    experts = jnp.arange(N_EXPERTS, dtype=jnp.int32)
    start_of = jnp.sum(jnp.where(idx[..., None] == experts, start_pad, 0), axis=-1)
    pos = (start_of + rank).astype(jnp.int32)
    n_blocks = -(-(n_assign + N_EXPERTS * (blk - 1)) // blk)
    blk_start = jnp.arange(n_blocks, dtype=jnp.int32) * blk
    blk_e = jnp.minimum(jnp.sum(blk_start[:, None] >= pad_end[None, :], axis=-1), N_EXPERTS - 1).astype(jnp.int32)
    n_used = (pad_end[-1] // blk).astype(jnp.int32).reshape(1)
    return pos, blk_e, n_used, n_blocks * blk


def _combine_kernel(pos_ref, y_ref, wt_ref, h2_ref, sgu_ref, sdn_ref, x_ref, ml_ref, mc_ref, lg_ref, lb_ref,
                    xo_ref, buf, sem, *, tm, ctx, alpha, ff, t0):
    i = pl.program_id(1) + t0

    def issue(t, carry):
        for kq in range(TOP_K):
            _row_copy(y_ref, pos_ref[0, 0, t * TOP_K + kq], buf.at[kq], t, sem).start()
        return carry

    lax.fori_loop(0, tm, issue, 0, unroll=4)
    lo, hi = _unpack_halves(h2_ref[0])
    f = _swiglu(lo.astype(BF16), hi.astype(BF16), sgu_ref[...], sdn_ref[...], ff)
    for kq in range(TOP_K):
        pltpu.make_async_copy(y_ref.at[pl.ds(0, tm)], buf.at[kq], sem).wait()
    wt = wt_ref[0]
    half = f.shape[1] // 2
    f_lo = f[:, :half]
    f_hi = f[:, half:]
    for kq in range(TOP_K):
        lo, hi = _unpack_halves(buf[kq])
        f_lo = f_lo + wt[:, kq:kq + 1] * lo
        f_hi = f_hi + wt[:, kq:kq + 1] * hi
    f = jnp.concatenate([f_lo, f_hi], axis=1)
    isc = _row_is_ctx(tm, i * tm, ctx)
    g2 = jnp.where(isc, mc_ref[0][5:6], ml_ref[0][5:6])
    xo_ref[0] = _ln(alpha * x_ref[0] + g2 * f, POST_LN_EPS) * lg_ref[...] + lb_ref[...]


def _combine(pos, y_rows, wts, h2p, sh_gu, sh_dn, x_all, modl, modc, ln_g, ln_b, ctx, alpha, tm, skip_ctx):
    b, ta, d = x_all.shape
    ff = sh_dn.shape[0]
    nt = ta // tm
    t0 = ctx // tm if skip_ctx else 0
    row = lambda bb, i: (bb, i + t0, 0)
    return pl.pallas_call(
        functools.partial(_combine_kernel, tm=tm, ctx=ctx, alpha=alpha, ff=ff, t0=t0),
        grid=(b, nt - t0),
        in_specs=[pl.BlockSpec((1, 1, tm * TOP_K), lambda bb, i: (bb * nt + i + t0, 0, 0),
                               memory_space=pltpu.SMEM),
                  pl.BlockSpec(memory_space=pl.ANY),
                  pl.BlockSpec((1, tm, LANE), row),
                  pl.BlockSpec((1, tm, d // 2), row),
                  pl.BlockSpec((d, 2 * ff), lambda bb, i: (0, 0)),
                  pl.BlockSpec((ff, d), lambda bb, i: (0, 0)),
                  pl.BlockSpec((1, tm, d), row),
                  pl.BlockSpec((1, 6, d), lambda bb, i: (bb, 0, 0)),
                  pl.BlockSpec((1, 6, d), lambda bb, i: (0, 0, 0)),
                  pl.BlockSpec((1, d), lambda bb, i: (0, 0)),
                  pl.BlockSpec((1, d), lambda bb, i: (0, 0))],
        out_specs=pl.BlockSpec((1, tm, d), lambda bb, i: (bb, i, 0)),
        out_shape=jax.ShapeDtypeStruct((b, ta - t0 * tm, d), F32),
        scratch_shapes=[pltpu.VMEM((TOP_K, tm, d // 2), jnp.uint32), pltpu.SemaphoreType.DMA(())],
        compiler_params=_cparams(("arbitrary", "arbitrary")),
        name="moe_combine_ln",
    )(pos, y_rows, wts, h2p, sh_gu, sh_dn, x_all, modl, modc, ln_g.reshape(1, d), ln_b.reshape(1, d))


def _rope_tables(n_lat, ctx):
    rows = n_lat // GRID_W
    row = jnp.repeat(jnp.arange(rows, dtype=F32), GRID_W)
    col = jnp.tile(jnp.arange(GRID_W, dtype=F32), rows)
    n_freq = ROPE_DIM // 4
    inv_freq = ROPE_BASE ** (-jnp.arange(n_freq, dtype=F32) / n_freq)
    ang = jnp.concatenate([row[:, None] * inv_freq, col[:, None] * inv_freq], -1)
    cos, sin = jnp.cos(ang), jnp.sin(ang)
    cos = jnp.concatenate([jnp.ones((ctx, ROPE_DIM // 2), F32), cos], 0)
    sin = jnp.concatenate([jnp.zeros((ctx, ROPE_DIM // 2), F32), sin], 0)
    cosf = jnp.tile(jnp.concatenate([cos, cos], -1), (1, LANE // ROPE_DIM))
    sinf = jnp.tile(jnp.concatenate([-sin, sin], -1), (1, LANE // ROPE_DIM))
    return cosf, sinf


def _pack_small_weights(w_in_l):
    d = w_in_l.shape[0]
    offs = np.cumsum([0, MLA_Q_RANK, MLA_KV_RANK + MLA_ROPE, RWKV_STREAM,
                      (SWA_HEADS + 2 * SWA_KV_HEADS) * SWA_HEAD, 3 * DIFF_HEADS * 2 * DIFF_HEAD])
    placed = sorted(zip([SEG_MQ, SEG_MKV, SEG_RW, SEG_SW, SEG_DF], range(5)))
    parts = []
    cur = 0
    for start, s in placed:
        if start > cur:
            parts.append(jnp.zeros((d, start - cur), BF16))
        parts.append(w_in_l[:, int(offs[s]):int(offs[s + 1])].astype(BF16))
        cur = start + int(offs[s + 1] - offs[s])
    parts.append(jnp.zeros((d, SMALL_COLS - cur), BF16))
    return jnp.concatenate(parts, axis=1), int(offs[5])


def kernel(x, c, ctx, c_ctx, w_mod, b_mod, w_in, mla_q_norm, mla_w_qup, mla_kv_norm, mla_w_kvup, rwkv_mu, rwkv_w0, rwkv_w_lora, rwkv_a0, rwkv_a_lora, rwkv_g_lora, rwkv_k_k, rwkv_k_a, rwkv_r_k, rwkv_ln_g, rwkv_ln_b, swa_sink, diff_lambda, diff_subln, w_branch, w_out, ln1_g, ln1_b, router_w, router_bias, exp_w_gu, exp_w_dn, sh_w_gu, sh_w_dn, ln2_g, ln2_b):
    depth = w_mod.shape[0]
    b, n_lat, d = x.shape
    n_ctx = ctx.shape[1]
    ta = n_ctx + n_lat
    alpha = (2 * depth) ** 0.25
    cosf, sinf = _rope_tables(n_lat, n_ctx)
    x_all = jnp.concatenate([ctx, x], axis=1)

    cond = jnp.concatenate([c, c_ctx[None]], axis=0)
    cond = jnp.pad(jax.nn.silu(cond), ((0, SUBLANE - (b + 1) % SUBLANE), (0, 0)))

    for l in range(depth):
        mod = _matmul_bias(cond, w_mod, b_mod, l).reshape(cond.shape[0], 6, d)
        modl, modc = mod[:b], mod[b:b + 1]

        w_small, gate_off = _pack_small_weights(w_in[l])
        p = _inproj(x_all, modl, modc, w_small, n_ctx)

        wq = mla_w_qup[l].reshape(MLA_Q_RANK, MLA_HEADS, MLA_NOPE + MLA_ROPE)
        wq = jnp.pad(wq, ((0, 0), (0, 0), (0, 256 - MLA_NOPE - MLA_ROPE))).reshape(MLA_Q_RANK, 1024)
        q_a, k_a, v_a = _mla_prep(p, mla_q_norm[l].reshape(1, -1), mla_kv_norm[l].reshape(1, -1),
                                  wq.astype(BF16), mla_w_kvup[l].astype(BF16), cosf, sinf)
        out_a = _mla_attn(q_a, k_a, v_a, n_ctx)

        out_b = _rwkv_mixer(p, n_ctx, rwkv_mu[l], rwkv_w0[l],
                            rwkv_w_lora[l], rwkv_a0[l], rwkv_a_lora[l], rwkv_g_lora[l], rwkv_k_k[l],
                            rwkv_k_a[l], rwkv_r_k[l], rwkv_ln_g[l], rwkv_ln_b[l])

        sinkv = jnp.broadcast_to(swa_sink[l][:, None], (SWA_HEADS, LANE))
        out_s = _swa(p, cosf, sinf, sinkv, n_ctx)

        lf = diff_lambda[l]
        lam_init = 0.8 - 0.6 * math.exp(-0.3 * l)
        lam = jnp.exp(jnp.sum(lf[0] * lf[1])) - jnp.exp(jnp.sum(lf[2] * lf[3])) + lam_init
        out_d = _diff_attn(p, cosf, sinf, jnp.full((1, LANE), lam, F32), diff_subln[l].reshape(1, LANE),
                           n_ctx, lam_init)

        o4 = jnp.stack([out_a, out_b, out_s, out_d], axis=0)
        wg, wb, wo = _gate_window_weights(w_in[l], gate_off, w_branch[l], w_out[l])
        merged = _merge_a(x_all, modl, modc, o4, wg, wb, n_ctx)
        x_all, h2p = _merge_b(merged, wo, x_all, modl, modc, ln1_g[l], ln1_b[l], n_ctx, alpha)

        idx, wts, rank, counts = _router(x_all, modl, modc, router_w[l], router_bias[l], n_ctx)
        pos, blk_e, n_used, n_rows = _moe_plan(idx[..., :TOP_K], rank[..., :TOP_K], counts[0], MOE_BLK)
        tmd = _tile(ta, 128)
        tmc = _tile(math.gcd(ta, n_ctx), 256)
        xg = _dispatch(pos.reshape(b * ta // tmd, 1, tmd * TOP_K), h2p.reshape(b * ta, d // 2), n_rows, tmd)
        y_rows = _experts(xg, blk_e, n_used, exp_w_gu, exp_w_dn, l, MOE_BLK)
        x_all = _combine(pos.reshape(b * ta // tmc, 1, tmc * TOP_K), y_rows, wts, h2p, sh_w_gu[l].astype(BF16),
                         sh_w_dn[l].astype(BF16), x_all, modl, modc, ln2_g[l], ln2_b[l], n_ctx, alpha, tmc,
                         skip_ctx=(l == depth - 1))
    return x_all
```

```python
import functools
import math

import jax
import jax.numpy as jnp
import numpy as np
from jax import lax
from jax.experimental import pallas as pl
from jax.experimental.pallas import tpu as pltpu

F32 = jnp.float32
BF16 = jnp.bfloat16
HIGHEST = lax.Precision.HIGHEST

GRID_W = 64
ROPE_BASE = 10000.0
ROPE_DIM = 64
ADALN_EPS = 1e-6
POST_LN_EPS = 1e-5
NEG_INF = -1e30

MLA_HEADS = 4
MLA_Q_RANK = 384
MLA_KV_RANK = 256
MLA_NOPE = 128
MLA_ROPE = 64
MLA_V = 128

RWKV_HEADS = 8
RWKV_HEAD = 64
RWKV_W = RWKV_HEADS * RWKV_HEAD
DECAY_LORA = 32
ICLR_LORA = 32
GATE_LORA = 96
RWKV_STREAM = 3 * RWKV_W + 2 * DECAY_LORA + 2 * ICLR_LORA + GATE_LORA
RWKV_GN_EPS = 64e-5
RWKV_CHUNK = 64

SWA_HEADS = 8
SWA_KV_HEADS = 2
SWA_GROUP = SWA_HEADS // SWA_KV_HEADS
SWA_HEAD = 64
WINDOW = 128

DIFF_HEADS = 4
DIFF_HEAD = 64

N_BRANCH = 4
BRANCH_W = 512

N_EXPERTS = 64
TOP_K = 6
N_GROUPS = 8
TOPK_GROUPS = 4
GROUP_SIZE = N_EXPERTS // N_GROUPS
ROUTED_SCALE = 2.5
MOE_BLK = 256

LANE = 128
SUBLANE = 8

SEG_SW = 0
SEG_MQ = 768
SEG_MKV = 1152
SEG_DF = 1536
SEG_RW = 3584
RW_COLS = 1792
SMALL_COLS = SEG_RW + RW_COLS


def _tile(n, target, mult=SUBLANE):
    best = None
    for d in range(mult, min(n, target) + 1, mult):
        if n % d == 0:
            best = d
    assert best is not None, (n, target, mult)
    return best


def _cparams(sem, vmem_mb=48):
    return pltpu.CompilerParams(dimension_semantics=sem, vmem_limit_bytes=vmem_mb * 1024 * 1024)


def _sigmoid(x):
    return 1.0 / (1.0 + jnp.exp(-x))


def _ln(x, eps):
    mu = jnp.mean(x, -1, keepdims=True)
    xc = x - mu
    var = jnp.mean(xc * xc, -1, keepdims=True)
    return xc * lax.rsqrt(var + eps)


def _rms(x, eps):
    return x * lax.rsqrt(jnp.mean(x * x, -1, keepdims=True) + eps)


def _row_is_ctx(n_rows, row0, ctx):
    rows = row0 + lax.broadcasted_iota(jnp.int32, (n_rows, 1), 0)
    return rows < ctx


def _modulated(x, ml, mc, sh_i, sc_i, isc):
    sh = jnp.where(isc, mc[sh_i:sh_i + 1], ml[sh_i:sh_i + 1])
    sc = jnp.where(isc, mc[sc_i:sc_i + 1], ml[sc_i:sc_i + 1])
    return _ln(x, ADALN_EPS) * (1.0 + sc) + sh


def _fill_modulated(xn_ref, x_ref, ml, mc, row0, ctx):
    tm = xn_ref.shape[0]
    ch = _tile(tm, 272)
    for r in range(0, tm, ch):
        isc = _row_is_ctx(ch, row0 + r, ctx)
        xn_ref[r:r + ch, :] = _modulated(x_ref[0, r:r + ch, :], ml, mc, 0, 1, isc).astype(BF16)


def _swap32(x):
    lane = lax.broadcasted_iota(jnp.int32, x.shape, 1)
    up = pltpu.roll(x, LANE - 32, 1)
    dn = pltpu.roll(x, 32, 1)
    return jnp.where((lane % 64) < 32, up, dn)


def _rope(x, cosf, sinf):
    return x * cosf + _swap32(x) * sinf


def _pack_halves(x):
    n = x.shape[1] // 2
    lo = pltpu.bitcast(x[:, :n].astype(BF16).astype(F32), jnp.uint32)
    hi = pltpu.bitcast(x[:, n:].astype(BF16).astype(F32), jnp.uint32)
    return (hi & jnp.uint32(0xFFFF0000)) | (lo >> 16)


def _unpack_halves(u):
    lo = pltpu.bitcast(u << 16, F32)
    hi = pltpu.bitcast(u & jnp.uint32(0xFFFF0000), F32)
    return lo, hi


def _dot(a, b, **kw):
    return jnp.dot(a, b, preferred_element_type=F32, **kw)


def _dot_nt(a, b, **kw):
    return lax.dot_general(a, b, (((1,), (1,)), ((), ())), preferred_element_type=F32, **kw)


def _dot_tn(a, b, **kw):
    return lax.dot_general(a, b, (((0,), (0,)), ((), ())), preferred_element_type=F32, **kw)


def _mm_kernel(x_ref, w_ref, b_ref, o_ref):
    w = w_ref[0].astype(BF16)
    o_ref[...] = sum(_dot(part, w) for part in _split3(x_ref[...])) + b_ref[0]


def _matmul_bias(x, w, b, l):
    m, k = x.shape
    n = w.shape[2]
    tn = _tile(n, 1024, LANE)
    return pl.pallas_call(
        _mm_kernel,
        grid=(n // tn,),
        in_specs=[pl.BlockSpec((m, k), lambda j: (0, 0)),
                  pl.BlockSpec((1, k, tn), lambda j: (l, 0, j)),
                  pl.BlockSpec((1, 1, tn), lambda j: (l, 0, j))],
        out_specs=pl.BlockSpec((m, tn), lambda j: (0, j)),
        out_shape=jax.ShapeDtypeStruct((m, n), F32),
        compiler_params=_cparams(("arbitrary",)),
        name="mod_matmul",
    )(x, w, b.reshape(b.shape[0], 1, n))


def _inproj_kernel(x_ref, ml_ref, mc_ref, w_ref, o_ref, xn_ref, *, tm, ctx):
    i = pl.program_id(1)
    j = pl.program_id(2)

    @pl.when(j == 0)
    def _():
        _fill_modulated(xn_ref, x_ref, ml_ref[0], mc_ref[0], i * tm, ctx)

    o_ref[0] = _dot(xn_ref[...], w_ref[...])


def _inproj(x_all, modl, modc, w_small, ctx):
    b, ta, d = x_all.shape
    n = w_small.shape[1]
    tm = _tile(ta, 1088)
    tn = _tile(n, 1024, LANE)
    return pl.pallas_call(
        functools.partial(_inproj_kernel, tm=tm, ctx=ctx),
        grid=(b, ta // tm, n // tn),
        in_specs=[pl.BlockSpec((1, tm, d), lambda bb, i, j: (bb, i, 0)),
                  pl.BlockSpec((1, 6, d), lambda bb, i, j: (bb, 0, 0)),
                  pl.BlockSpec((1, 6, d), lambda bb, i, j: (0, 0, 0)),
                  pl.BlockSpec((d, tn), lambda bb, i, j: (0, j))],
        out_specs=pl.BlockSpec((1, tm, tn), lambda bb, i, j: (bb, i, j)),
        out_shape=jax.ShapeDtypeStruct((b, ta, n), F32),
        scratch_shapes=[pltpu.VMEM((tm, d), BF16)],
        compiler_params=_cparams(("parallel", "parallel", "arbitrary"), 56),
        name="inproj",
    )(x_all, modl, modc, w_small)


def _mla_prep_kernel(mq_ref, mkv_ref, qg_ref, kg_ref, wq_ref, wkv_ref, cos_ref, sin_ref,
                     q_ref, k_ref, v_ref):
    mq = mq_ref[0]
    mkv = mkv_ref[0]
    cosv = cos_ref[...]
    sinv = sin_ref[...]
    qn = _rms(mq, 1e-6) * qg_ref[...]
    q = _dot(qn.astype(BF16), wq_ref[...])
    kvn = _rms(mkv[:, :MLA_KV_RANK], 1e-6) * kg_ref[...]
    kv = _dot(kvn.astype(BF16), wkv_ref[...])
    kr = _rope(mkv[:, MLA_KV_RANK:MLA_KV_RANK + LANE], cosv, sinv).astype(BF16)
    scale = (MLA_NOPE + MLA_ROPE) ** -0.5 * LOG2E
    for h in range(MLA_HEADS):
        c0 = h * 256
        q_ref[0, :, c0:c0 + 128] = (q[:, c0:c0 + 128] * scale).astype(BF16)
        q_ref[0, :, c0 + 128:c0 + 256] = (_rope(q[:, c0 + 128:c0 + 256], cosv, sinv) * scale).astype(BF16)
        k_ref[0, :, c0:c0 + 128] = kv[:, c0:c0 + 128].astype(BF16)
        k_ref[0, :, c0 + 128:c0 + 256] = kr
        v_ref[0, :, h * 128:(h + 1) * 128] = kv[:, c0 + 128:c0 + 256].astype(BF16)


def _mla_prep(p, qg, kg, wq, wkv, cosf, sinf):
    b, ta, _ = p.shape
    tm = _tile(ta, 544)
    return pl.pallas_call(
        _mla_prep_kernel,
        grid=(b, ta // tm),
        in_specs=[pl.BlockSpec((1, tm, 384), lambda bb, i: (bb, i, SEG_MQ // 384)),
                  pl.BlockSpec((1, tm, 384), lambda bb, i: (bb, i, SEG_MKV // 384)),
                  pl.BlockSpec((1, MLA_Q_RANK), lambda bb, i: (0, 0)),
                  pl.BlockSpec((1, MLA_KV_RANK), lambda bb, i: (0, 0)),
                  pl.BlockSpec((MLA_Q_RANK, 1024), lambda bb, i: (0, 0)),
                  pl.BlockSpec((MLA_KV_RANK, 1024), lambda bb, i: (0, 0)),
                  pl.BlockSpec((tm, LANE), lambda bb, i: (i, 0)),
                  pl.BlockSpec((tm, LANE), lambda bb, i: (i, 0))],
        out_specs=[pl.BlockSpec((1, tm, 1024), lambda bb, i: (bb, i, 0)),
                   pl.BlockSpec((1, tm, 1024), lambda bb, i: (bb, i, 0)),
                   pl.BlockSpec((1, tm, 512), lambda bb, i: (bb, i, 0))],
        out_shape=[jax.ShapeDtypeStruct((b, ta, 1024), BF16),
                   jax.ShapeDtypeStruct((b, ta, 1024), BF16),
                   jax.ShapeDtypeStruct((b, ta, 512), BF16)],
        compiler_params=_cparams(("parallel", "parallel")),
        name="mla_prep",
    )(p, p, qg, kg, wq, wkv, cosf, sinf)


LOG2E = 1.4426950408889634


def _softmax_pv(s, v):
    m = jnp.max(s, -1, keepdims=True)
    p = jnp.exp2(s - m)
    l = jnp.sum(p, -1, keepdims=True)
    return _dot(p.astype(BF16), v) / l


def _mla_attn_kernel(q_ref, k_ref, v_ref, o_ref, *, n_ctx_tiles, ctx):
    qi = pl.program_id(2)

    def run(nk):
        s = _dot_nt(q_ref[0], k_ref[0, :nk, :])
        o_ref[0] = _softmax_pv(s, v_ref[0, :nk, :]).astype(o_ref.dtype)

    @pl.when(qi < n_ctx_tiles)
    def _():
        run(ctx)

    @pl.when(qi >= n_ctx_tiles)
    def _():
        run(k_ref.shape[1])


def _mla_attn(q, k, v, ctx):
    b, ta, _ = q.shape
    tq = _tile(math.gcd(ta, ctx), 256)
    return pl.pallas_call(
        functools.partial(_mla_attn_kernel, n_ctx_tiles=ctx // tq, ctx=ctx),
        grid=(b, MLA_HEADS, ta // tq),
        in_specs=[pl.BlockSpec((1, tq, 256), lambda bb, h, i: (bb, i, h)),
                  pl.BlockSpec((1, ta, 256), lambda bb, h, i: (bb, 0, h)),
                  pl.BlockSpec((1, ta, 128), lambda bb, h, i: (bb, 0, h))],
        out_specs=pl.BlockSpec((1, tq, 128), lambda bb, h, i: (bb, i, h)),
        out_shape=jax.ShapeDtypeStruct((b, ta, 512), BF16),
        compiler_params=_cparams(("parallel", "parallel", "arbitrary")),
        name="mla_attn",
    )(q, k, v)


def _diff_attn_kernel(q_ref, k_ref, v_ref, cq_ref, sq_ref, ck_ref, sk_ref, lam_ref, sub_ref,
                      o_ref, ks_ref, vs_ref, *, n_ctx_tiles, ctx, post_scale):
    qi = pl.program_id(2)

    @pl.when(qi == 0)
    def _():
        ks_ref[...] = _rope(k_ref[0], ck_ref[...], sk_ref[...]).astype(BF16)
        vs_ref[...] = v_ref[0].astype(BF16)

    def run(nk):
        q = _rope(q_ref[0], cq_ref[...], sq_ref[...]) * (DIFF_HEAD ** -0.5 * LOG2E)
        lane = lax.broadcasted_iota(jnp.int32, q.shape, 1)
        q1 = jnp.where(lane < 64, q, 0.0).astype(BF16)
        q2 = jnp.where(lane >= 64, q, 0.0).astype(BF16)
        k = ks_ref[:nk, :]
        v = vs_ref[:nk, :]
        o = _softmax_pv(_dot_nt(q1, k), v) - lam_ref[...] * _softmax_pv(_dot_nt(q2, k), v)
        o = _rms(o, 1e-5) * sub_ref[...] * post_scale
        o_ref[0] = o.astype(o_ref.dtype)

    @pl.when(qi < n_ctx_tiles)
    def _():
        run(ctx)

    @pl.when(qi >= n_ctx_tiles)
    def _():
        run(ks_ref.shape[0])


def _diff_attn(p, cosf, sinf, lamv, subln, ctx, lam_init):
    b, ta, _ = p.shape
    tq = _tile(math.gcd(ta, ctx), 256)
    cb = SEG_DF // LANE
    return pl.pallas_call(
        functools.partial(_diff_attn_kernel, n_ctx_tiles=ctx // tq, ctx=ctx, post_scale=1.0 - lam_init),
        grid=(b, DIFF_HEADS, ta // tq),
        in_specs=[pl.BlockSpec((1, tq, LANE), lambda bb, h, i: (bb, i, cb + h)),
                  pl.BlockSpec((1, ta, LANE), lambda bb, h, i: (bb, 0, cb + DIFF_HEADS + h)),
                  pl.BlockSpec((1, ta, LANE), lambda bb, h, i: (bb, 0, cb + 2 * DIFF_HEADS + h)),
                  pl.BlockSpec((tq, LANE), lambda bb, h, i: (i, 0)),
                  pl.BlockSpec((tq, LANE), lambda bb, h, i: (i, 0)),
                  pl.BlockSpec((ta, LANE), lambda bb, h, i: (0, 0)),
                  pl.BlockSpec((ta, LANE), lambda bb, h, i: (0, 0)),
                  pl.BlockSpec((1, LANE), lambda bb, h, i: (0, 0)),
                  pl.BlockSpec((1, LANE), lambda bb, h, i: (0, 0))],
        out_specs=pl.BlockSpec((1, tq, LANE), lambda bb, h, i: (bb, i, h)),
        out_shape=jax.ShapeDtypeStruct((b, ta, 512), BF16),
        scratch_shapes=[pltpu.VMEM((ta, LANE), BF16), pltpu.VMEM((ta, LANE), BF16)],
        compiler_params=_cparams(("parallel", "parallel", "arbitrary")),
        name="diff_attn",
    )(p, p, p, cosf, sinf, cosf, sinf, lamv, subln)


def _swa_kernel(q_ref, k_ref, v_ref, cq_ref, sq_ref, ck_ref, sk_ref, sink_ref,
                o_ref, ks_ref, vs_ref, *, ctx, nsb):
    t = pl.program_id(1)
    n_ctx_steps = ctx // (WINDOW * nsb)
    ta = ks_ref.shape[0]
    span = 3 * WINDOW

    @pl.when(t == 0)
    def _():
        ks_ref[...] = _rope(k_ref[0], ck_ref[...], sk_ref[...]).astype(BF16)
        vs_ref[...] = v_ref[0].astype(BF16)

    lane = lax.broadcasted_iota(jnp.int32, (WINDOW, LANE), 1)

    def run(local, sb):
        rows = slice(sb * WINDOW, (sb + 1) * WINDOW)
        cq = cq_ref[rows, :]
        sq = sq_ref[rows, :]
        qblk = [_rope(q_ref[0, rows, hb * LANE:(hb + 1) * LANE], cq, sq) * (SWA_HEAD ** -0.5)
                for hb in range(SWA_HEADS // 2)]

        def head_q(hk, g):
            h = hk * SWA_GROUP + g
            blk = qblk[h // 2]
            if h % 2 != hk:
                blk = pltpu.roll(blk, 64, 1)
            keep = (lane >= 64) if hk == 1 else (lane < 64)
            return jnp.where(keep, blk, 0.0)

        kc = ks_ref[:ctx, :]
        vc = vs_ref[:ctx, :]
        if local:
            n = (t - n_ctx_steps) * nsb + sb
            ws = jnp.clip(ctx + (n - 1) * WINDOW, ctx, ta - span)
            ws = pl.multiple_of(ws, WINDOW)
            kl = ks_ref[pl.ds(ws, span), :]
            vl = vs_ref[pl.ds(ws, span), :]
            kpos = (ws - ctx) + lax.broadcasted_iota(jnp.int32, (1, span), 1)
            qpos = n * WINDOW + lax.broadcasted_iota(jnp.int32, (SWA_GROUP * WINDOW, 1), 0) % WINDOW
            valid = jnp.abs(kpos - qpos) <= WINDOW
        outs = {}
        for hk in range(SWA_KV_HEADS):
            qst = jnp.concatenate([head_q(hk, g) for g in range(SWA_GROUP)], axis=0).astype(BF16)
            sink = jnp.concatenate(
                [jnp.broadcast_to(sink_ref[hk * SWA_GROUP + g:hk * SWA_GROUP + g + 1, 0:1], (WINDOW, 1))
                 for g in range(SWA_GROUP)], axis=0)
            s_c = _dot_nt(qst, kc)
            m = jnp.maximum(jnp.max(s_c, -1, keepdims=True), sink)
            if local:
                s_l = jnp.where(valid, _dot_nt(qst, kl), NEG_INF)
                m = jnp.maximum(m, jnp.max(s_l, -1, keepdims=True))
            p_c = jnp.exp(s_c - m)
            l = jnp.sum(p_c, -1, keepdims=True) + jnp.exp(sink - m)
            acc = _dot(p_c.astype(BF16), vc)
            if local:
                p_l = jnp.exp(s_l - m)
                l = l + jnp.sum(p_l, -1, keepdims=True)
                acc = acc + _dot(p_l.astype(BF16), vl)
            o = acc / l
            for g in range(SWA_GROUP):
                res = o[g * WINDOW:(g + 1) * WINDOW, :]
                if g % 2 != hk:
                    res = pltpu.roll(res, 64, 1)
                outs[hk * SWA_GROUP + g] = res
        for hb in range(SWA_HEADS // 2):
            blk = jnp.where(lane < 64, outs[2 * hb], outs[2 * hb + 1])
            o_ref[0, rows, hb * LANE:(hb + 1) * LANE] = blk.astype(o_ref.dtype)

    @pl.when(t < n_ctx_steps)
    def _():
        for sb in range(nsb):
            run(False, sb)

    @pl.when(t >= n_ctx_steps)
    def _():
        for sb in range(nsb):
            run(True, sb)


def _swa(p, cosf, sinf, sinkv, ctx):
    b, ta, _ = p.shape
    qb = SEG_SW // 512
    kb = (SEG_SW + 512) // LANE
    nsb = 2 if (ctx % (2 * WINDOW) == 0 and ta % (2 * WINDOW) == 0) else 1
    tq = nsb * WINDOW
    return pl.pallas_call(
        functools.partial(_swa_kernel, ctx=ctx, nsb=nsb),
        grid=(b, ta // tq),
        in_specs=[pl.BlockSpec((1, tq, 512), lambda bb, t: (bb, t, qb)),
                  pl.BlockSpec((1, ta, LANE), lambda bb, t: (bb, 0, kb)),
                  pl.BlockSpec((1, ta, LANE), lambda bb, t: (bb, 0, kb + 1)),
                  pl.BlockSpec((tq, LANE), lambda bb, t: (t, 0)),
                  pl.BlockSpec((tq, LANE), lambda bb, t: (t, 0)),
                  pl.BlockSpec((ta, LANE), lambda bb, t: (0, 0)),
                  pl.BlockSpec((ta, LANE), lambda bb, t: (0, 0)),
                  pl.BlockSpec((SWA_HEADS, LANE), lambda bb, t: (0, 0))],
        out_specs=pl.BlockSpec((1, tq, 512), lambda bb, t: (bb, t, 0)),
        out_shape=jax.ShapeDtypeStruct((b, ta, 512), BF16),
        scratch_shapes=[pltpu.VMEM((ta, LANE), BF16), pltpu.VMEM((ta, LANE), BF16)],
        compiler_params=_cparams(("parallel", "arbitrary")),
        name="swa_attn",
    )(p, p, p, cosf, sinf, cosf, sinf, sinkv)


def _softplus(x):
    return jnp.maximum(x, 0.0) + jnp.log(1.0 + jnp.exp(-jnp.abs(x)))


def _rwkv_prep_kernel(z_ref, hp_ref, hn_ref, mu_ref, w0_ref, a0_ref, wl_ref, al_ref, gl_ref, kk_ref, ka_ref,
                      rk_ref, bd_ref, r_o, v_o, kk_o, ld0_o, ld1_o, kd0_o, kd1_o, a0_o, a1_o, bv_o, g_o, zs_ref):
    tm = z_ref.shape[1]
    w = RWKV_W
    z = z_ref[0]
    zs_ref[SUBLANE:SUBLANE + tm, :] = z
    zs_ref[SUBLANE - 1:SUBLANE, :] = hp_ref[0, 0]
    zs_ref[SUBLANE + tm:SUBLANE + tm + 1, :] = hn_ref[0, 0]
    shifted = 0.5 * (zs_ref[SUBLANE - 1:SUBLANE - 1 + tm, :] + zs_ref[SUBLANE + 1:SUBLANE + 1 + tm, :])
    seg = z + (shifted - z) * mu_ref[...]
    r = seg[:, 0:w]
    k = seg[:, w:2 * w]
    v = seg[:, 2 * w:3 * w]
    lo = seg[:, 3 * w:3 * w + LANE]
    gl = seg[:, 3 * w + LANE:3 * w + 2 * LANE]
    hp = dict(precision=HIGHEST)
    bd = bd_ref[...]
    kk = k * kk_ref[...]
    kk = kk / jnp.maximum(jnp.sqrt(_dot(kk * kk, bd, **hp)), 1e-12)
    th = jnp.tanh(lo).astype(BF16)
    lo_b = lo.astype(BF16)
    ksum = jnp.zeros_like(k)
    for d, (ld_o, kd_o, a_o) in enumerate(((ld0_o, kd0_o, a0_o), (ld1_o, kd1_o, a1_o))):
        wd = w0_ref[d:d + 1, :] + _dot(th, wl_ref[d])
        wd = -_softplus(-wd) - 0.5
        a = _sigmoid(a0_ref[d:d + 1, :] + _dot(lo_b, al_ref[d]))
        kd = k * (1.0 + (a - 1.0) * ka_ref[...])
        ld_o[0] = -jnp.exp(wd)
        kd_o[0] = kd
        a_o[0] = a
        ksum = ksum + kd
    r_o[0] = r
    v_o[0] = v
    kk_o[0] = kk
    bv_o[0] = _dot(r * ksum * rk_ref[...], bd, **hp) * v
    g_o[0] = _dot(_sigmoid(gl).astype(BF16), gl_ref[...])


def _rwkv_prep(p, hprev, hnext, mu, w0, a0, wl, al, gl, k_k, k_a, r_k, bd, tm):
    b, ta, _ = p.shape
    w = RWKV_W
    row = lambda bb, i: (bb, i, 0)
    vec = lambda n: pl.BlockSpec((1, n), lambda bb, i: (0, 0))
    out = jax.ShapeDtypeStruct((b, ta, w), F32)
    return pl.pallas_call(
        _rwkv_prep_kernel,
        grid=(b, ta // tm),
        in_specs=[pl.BlockSpec((1, tm, RW_COLS), lambda bb, i: (bb, i, SEG_RW // RW_COLS)),
                  pl.BlockSpec((1, 1, 1, RW_COLS), lambda bb, i: (bb, i, 0, 0)),
                  pl.BlockSpec((1, 1, 1, RW_COLS), lambda bb, i: (bb, i, 0, 0)),
                  vec(RW_COLS),
                  pl.BlockSpec((2, w), lambda bb, i: (0, 0)),
                  pl.BlockSpec((2, w), lambda bb, i: (0, 0)),
                  pl.BlockSpec((2, LANE, w), lambda bb, i: (0, 0, 0)),
                  pl.BlockSpec((2, LANE, w), lambda bb, i: (0, 0, 0)),
                  pl.BlockSpec((LANE, w), lambda bb, i: (0, 0)),
                  vec(w), vec(w), vec(w),
                  pl.BlockSpec((w, w), lambda bb, i: (0, 0))],
        out_specs=[pl.BlockSpec((1, tm, w), row)] * 11,
        out_shape=[out] * 11,
        scratch_shapes=[pltpu.VMEM((tm + 2 * SUBLANE, RW_COLS), F32)],
        compiler_params=_cparams(("parallel", "parallel")),
        name="rwkv_prep",
    )(p, hprev, hnext, mu, w0, a0, wl, al, gl, k_k, k_a, r_k, bd)


def _split3(x):
    hi = x.astype(BF16)
    r1 = x - hi.astype(F32)
    mid = r1.astype(BF16)
    return hi, mid, (r1 - mid.astype(F32)).astype(BF16)


def _rwkv_chunk_kernel(r_ref, v_ref, kk_ref, ld0_ref, ld1_ref, kd0_ref, kd1_ref, a0_ref, a1_ref, pz_ref, wu_ref):
    C = RWKV_CHUNK
    hd = RWKV_HEAD
    row = lax.broadcasted_iota(jnp.int32, (C, C), 0)
    col = lax.broadcasted_iota(jnp.int32, (C, C), 1)
    eye = (row == col).astype(F32)
    probs = []
    for d, (ld_ref, kd_ref, a_ref) in enumerate(((ld0_ref, kd0_ref, a0_ref), (ld1_ref, kd1_ref, a1_ref))):
        incl = (row >= col) if d == 0 else (row <= col)
        strict = (row > col) if d == 0 else (row < col)
        last = C - 1 if d == 0 else 0
        ones = jnp.where(incl, 1.0, 0.0).astype(BF16)
        cs_all = sum(_dot(ones, part) for part in _split3(ld_ref[0]))
        for h in range(RWKV_HEADS):
            sl = slice(h * hd, (h + 1) * hd)
            probs.append(dict(d=d, h=h, incl=incl, strict=strict, last=last, cs=cs_all[:, sl],
                              r=r_ref[0, :, sl], v=v_ref[0, :, sl], kk=kk_ref[0, :, sl],
                              ld=ld_ref[0, :, sl], k=kd_ref[0, :, sl], a=a_ref[0, :, sl]))
    bf = lambda z: z.astype(BF16)
    for q in probs:
        cs = q["cs"]
        e_cs = jnp.exp(cs)
        e_inv = jnp.exp(-cs)
        q["g_end"] = e_cs[q["last"]:q["last"] + 1, :]
        q["kkd"] = q["kk"] * jnp.exp(cs - q["ld"])
        q["rd"] = q["r"] * e_cs
        q["bi"] = bf(q["kk"] * q["a"] * e_inv)
        q["ki"] = bf(q["k"] * e_inv)
        q["lhs"] = bf(jnp.concatenate([q["kkd"], q["rd"]], axis=0))
    for q in probs:
        gb = _dot_nt(q["lhs"], q["bi"])
        gk = _dot_nt(q["lhs"], q["ki"])
        q["pw"] = -jnp.where(q["strict"], gb[:C], 0.0)
        q["q_b"] = bf(jnp.where(q["incl"], gb[C:], 0.0))
        q["akqk"] = bf(jnp.concatenate([jnp.where(q["strict"], gk[:C], 0.0),
                                        jnp.where(q["incl"], gk[C:], 0.0)], axis=0))
        q["tinv"] = eye + q["pw"]
    for _ in range(C.bit_length() - 2):
        for q in probs:
            pw = bf(q["pw"])
            q["pw"] = _dot(pw, pw)
        for q in probs:
            q["tinv"] = q["tinv"] + _dot(bf(q["tinv"]), bf(q["pw"]))
    for q in probs:
        q["vs"] = bf(q["v"])
        q["av"] = _dot(q["akqk"], q["vs"])
    for q in probs:
        tinv = bf(q["tinv"])
        q["w1"] = _dot(tinv, bf(q["kkd"]))
        q["u1"] = _dot(tinv, bf(q["av"][:C]))
    for q in probs:
        w1 = bf(q["w1"])
        u1 = bf(q["u1"])
        w2 = q["rd"] - _dot(q["q_b"], w1)
        u2 = q["av"][C:] - _dot(q["q_b"], u1)
        pt = (eye[:hd, :hd] - _dot_tn(w1, q["bi"])) * q["g_end"]
        zt = (_dot_tn(q["vs"], q["ki"]) - _dot_tn(u1, q["bi"])) * q["g_end"]
        pz_ref[q["d"], 0, 0, q["h"]] = jnp.concatenate([pt, zt], axis=1)
        wu_ref[q["d"], 0, 0, q["h"]] = jnp.concatenate([w2, u2], axis=1)


def _rwkv_chunks(r, v, kk, ld0, ld1, kd0, kd1, a0, a1):
    b, ta, w = r.shape
    nc = ta // RWKV_CHUNK
    spec = pl.BlockSpec((1, RWKV_CHUNK, w), lambda bb, c: (bb, c, 0))

    def out(rows):
        return (pl.BlockSpec((2, 1, 1, RWKV_HEADS, rows, 2 * RWKV_HEAD), lambda bb, c: (0, bb, c, 0, 0, 0)),
                jax.ShapeDtypeStruct((2, b, nc, RWKV_HEADS, rows, 2 * RWKV_HEAD), F32))

    (pz_spec, pz_shape), (wu_spec, wu_shape) = out(RWKV_HEAD), out(RWKV_CHUNK)
    return pl.pallas_call(
        _rwkv_chunk_kernel,
        grid=(b, nc),
        in_specs=[spec] * 9,
        out_specs=[pz_spec, wu_spec],
        out_shape=[pz_shape, wu_shape],
        compiler_params=_cparams(("parallel", "parallel")),
        name="rwkv_chunks",
    )(r, v, kk, ld0, ld1, kd0, kd1, a0, a1)


def _rwkv_state_kernel(pz0_ref, pz1_ref, wu0_ref, wu1_ref, y0_ref, y1_ref, st_ref):
    i = pl.program_id(0)
    hd = RWKV_HEAD
    nb = st_ref.shape[1]

    @pl.when(i == 0)
    def _():
        st_ref[...] = jnp.zeros_like(st_ref)

    probs = []
    for d, (pz_ref, wu_ref) in enumerate(((pz0_ref, wu0_ref), (pz1_ref, wu1_ref))):
        for bb in range(nb):
            for h in range(RWKV_HEADS):
                pz = pz_ref[0, bb, 0, h]
                wu = wu_ref[0, bb, 0, h]
                probs.append(dict(d=d, b=bb, h=h, pt=pz[:, :hd].astype(BF16), zt=pz[:, hd:],
                                  w2=wu[:, :hd].astype(BF16), u2=wu[:, hd:], s=st_ref[d, bb, h].astype(BF16)))
    for q in probs:
        q["y"] = _dot_nt(q["w2"], q["s"]) + q["u2"]
    for q in probs:
        st_ref[q["d"], q["b"], q["h"]] = _dot(q["s"], q["pt"]) + q["zt"]
    for d, y_ref in enumerate((y0_ref, y1_ref)):
        for bb in range(nb):
            ys = [q["y"] for q in probs if q["d"] == d and q["b"] == bb]
            for hp in range(RWKV_HEADS // 2):
                y_ref[bb, :, hp * LANE:(hp + 1) * LANE] = jnp.concatenate([ys[2 * hp], ys[2 * hp + 1]], axis=1)


def _rwkv_state(pz, wu, ctx):
    _, b, nc, nh, hd, _ = pz.shape
    ncc = ctx // RWKV_CHUNK

    def rev(i):
        return jnp.where(i < ncc, ncc - 1 - i, nc - 1 - (i - ncc))

    blk = (1, b, 1, nh, hd, 2 * hd)
    wblk = (1, b, 1, nh, RWKV_CHUNK, 2 * hd)
    fwd = lambda i: (0, 0, i, 0, 0, 0)
    bwd = lambda i: (1, 0, rev(i), 0, 0, 0)
    yshape = jax.ShapeDtypeStruct((b, nc * RWKV_CHUNK, nh * hd), F32)
    return pl.pallas_call(
        _rwkv_state_kernel,
        grid=(nc,),
        in_specs=[pl.BlockSpec(blk, fwd), pl.BlockSpec(blk, bwd), pl.BlockSpec(wblk, fwd), pl.BlockSpec(wblk, bwd)],
        out_specs=[pl.BlockSpec((b, RWKV_CHUNK, nh * hd), lambda i: (0, i, 0)),
                   pl.BlockSpec((b, RWKV_CHUNK, nh * hd), lambda i: (0, rev(i), 0))],
        out_shape=[yshape, yshape],
        scratch_shapes=[pltpu.VMEM((2, b, nh, hd, hd), F32)],
        compiler_params=_cparams(("arbitrary",)),
        name="rwkv_state",
    )(pz, pz, wu, wu)


def _rwkv_out_kernel(y0_ref, y1_ref, bv_ref, g_ref, lg_ref, lb_ref, bd_ref, o_ref):
    hp = dict(precision=HIGHEST)
    bdm = bd_ref[...] * (1.0 / RWKV_HEAD)
    y = y0_ref[0] + y1_ref[0]
    yc = y - _dot(y, bdm, **hp)
    var = _dot(yc * yc, bdm, **hp)
    yn = yc * lax.rsqrt(var + RWKV_GN_EPS) * lg_ref[...] + lb_ref[...]
    o_ref[0] = ((yn + bv_ref[0]) * g_ref[0]).astype(o_ref.dtype)


def _rwkv_out(y0, y1, bv, g, ln_g, ln_b, bd, tm):
    b, ta, w = y0.shape
    row = pl.BlockSpec((1, tm, w), lambda bb, i: (bb, i, 0))
    vec = pl.BlockSpec((1, w), lambda bb, i: (0, 0))
    return pl.pallas_call(
        _rwkv_out_kernel,
        grid=(b, ta // tm),
        in_specs=[row, row, row, row, vec, vec, pl.BlockSpec((w, w), lambda bb, i: (0, 0))],
        out_specs=row,
        out_shape=jax.ShapeDtypeStruct((b, ta, w), BF16),
        compiler_params=_cparams(("parallel", "parallel")),
        name="rwkv_out",
    )(y0, y1, bv, g, ln_g, ln_b, bd)


def _rwkv_mixer(p, ctx, mu, w0, w_lora, a0, a_lora, g_lora, k_k, k_a, r_k, ln_g, ln_b):
    b, ta, _ = p.shape
    w = RWKV_W
    tm = _tile(math.gcd(ta, ctx), 256)
    nt = ta // tm
    edge_lo = p[:, 0::tm, SEG_RW:]
    edge_hi = p[:, tm - 1::tm, SEG_RW:]
    tile_id = jnp.arange(nt)[None, :, None]
    hprev = jnp.where((tile_id == 0) | (tile_id == ctx // tm), 0.0, jnp.roll(edge_hi, 1, axis=1))
    hnext = jnp.where((tile_id == nt - 1) | (tile_id == ctx // tm - 1), 0.0, jnp.roll(edge_lo, -1, axis=1))
    def rows_at(mat, start):
        return jnp.pad(mat.astype(BF16), ((start, LANE - start - mat.shape[0]), (0, 0)))

    wl = jnp.stack([rows_at(w_lora[d], d * DECAY_LORA) for d in range(2)])
    al = jnp.stack([rows_at(a_lora[d], 2 * DECAY_LORA + d * ICLR_LORA) for d in range(2)])
    gl = rows_at(g_lora, 0)
    mu_p = jnp.pad(mu, (0, RW_COLS - RWKV_STREAM)).reshape(1, RW_COLS)
    head_id = jnp.arange(w) // RWKV_HEAD
    bd = (head_id[:, None] == head_id[None, :]).astype(F32)
    r, v, kk, ld0, ld1, kd0, kd1, a0_, a1_, bv, g = _rwkv_prep(
        p, hprev[:, :, None, :], hnext[:, :, None, :], mu_p, w0, a0, wl, al, gl,
        k_k.reshape(1, w), k_a.reshape(1, w), r_k.reshape(1, w), bd, tm)
    pz, wu = _rwkv_chunks(r, v, kk, ld0, ld1, kd0, kd1, a0_, a1_)
    y0, y1 = _rwkv_state(pz, wu, ctx)
    return _rwkv_out(y0, y1, bv, g, ln_g.reshape(1, w), ln_b.reshape(1, w), bd, tm)


def _merge_a_kernel(x_ref, ml_ref, mc_ref, o4_ref, wg_ref, wb_ref, out_ref, xn_ref, acc_ref, *, tm, ctx):
    i = pl.program_id(1)
    j = pl.program_id(2)
    br = pl.program_id(3)

    @pl.when((j == 0) & (br == 0))
    def _():
        _fill_modulated(xn_ref, x_ref, ml_ref[0], mc_ref[0], i * tm, ctx)

    val = _sigmoid(_dot(xn_ref[...], wg_ref[...])) * _dot(o4_ref[0, 0], wb_ref[0])

    @pl.when(br == 0)
    def _():
        acc_ref[...] = val

    @pl.when(br > 0)
    def _():
        acc_ref[...] += val

    @pl.when(br == N_BRANCH - 1)
    def _():
        out_ref[0] = acc_ref[...].astype(out_ref.dtype)


def _gate_window_weights(w_in_l, gate_off, w_branch_l, w_out_l):
    d = w_in_l.shape[0]
    shift = gate_off % LANE
    start = gate_off - shift
    gw = d + 2 * LANE
    tail = gw - d - shift
    wsrc = w_in_l[:, start:].astype(BF16)
    wsrc = jnp.pad(wsrc, ((0, 0), (0, (N_BRANCH - 1) * d + gw - wsrc.shape[1])))
    wg = jnp.concatenate([wsrc[:, br * d:br * d + gw] for br in range(N_BRANCH)], axis=1)
    wb = jnp.pad(w_branch_l.astype(BF16), ((0, 0), (0, 0), (shift, tail)))
    wo = jnp.pad(w_out_l.astype(BF16), ((shift, tail), (0, 0)))
    return wg, wb, wo


def _merge_a(x_all, modl, modc, o4, wg, wb, ctx):
    b, ta, d = x_all.shape
    gw = wb.shape[2]
    tm = _tile(ta, 1088)
    tn = _tile(gw, 768, LANE)
    return pl.pallas_call(
        functools.partial(_merge_a_kernel, tm=tm, ctx=ctx),
        grid=(b, ta // tm, gw // tn, N_BRANCH),
        in_specs=[pl.BlockSpec((1, tm, d), lambda bb, i, j, br: (bb, i, 0)),
                  pl.BlockSpec((1, 6, d), lambda bb, i, j, br: (bb, 0, 0)),
                  pl.BlockSpec((1, 6, d), lambda bb, i, j, br: (0, 0, 0)),
                  pl.BlockSpec((1, 1, tm, BRANCH_W), lambda bb, i, j, br: (br, bb, i, 0)),
                  pl.BlockSpec((d, tn), lambda bb, i, j, br: (0, br * (gw // tn) + j)),
                  pl.BlockSpec((1, BRANCH_W, tn), lambda bb, i, j, br: (br, 0, j))],
        out_specs=pl.BlockSpec((1, tm, tn), lambda bb, i, j, br: (bb, i, j)),
        out_shape=jax.ShapeDtypeStruct((b, ta, gw), BF16),
        scratch_shapes=[pltpu.VMEM((tm, d), BF16), pltpu.VMEM((tm, tn), F32)],
        compiler_params=_cparams(("parallel", "parallel", "arbitrary", "arbitrary"), 56),
        name="merge_gate",
    )(x_all, modl, modc, o4, wg, wb)


def _merge_b_kernel(m_ref, w_ref, x_ref, ml_ref, mc_ref, lg_ref, lb_ref, xo_ref, h2_ref, *, tm, ctx, alpha):
    i = pl.program_id(1)
    isc = _row_is_ctx(tm, i * tm, ctx)
    ml = ml_ref[0]
    mc = mc_ref[0]
    y = _dot(m_ref[0], w_ref[...])
    g1 = jnp.where(isc, mc[2:3], ml[2:3])
    xn = _ln(alpha * x_ref[0] + g1 * y, POST_LN_EPS) * lg_ref[...] + lb_ref[...]
    xo_ref[0] = xn
    h2_ref[0] = _pack_halves(_modulated(xn, ml, mc, 3, 4, isc))


def _merge_b(merged, w_out, x_all, modl, modc, ln_g, ln_b, ctx, alpha):
    b, ta, d = x_all.shape
    gw = w_out.shape[0]
    tm = _tile(ta, 272)
    row = lambda bb, i: (bb, i, 0)
    return pl.pallas_call(
        functools.partial(_merge_b_kernel, tm=tm, ctx=ctx, alpha=alpha),
        grid=(b, ta // tm),
        in_specs=[pl.BlockSpec((1, tm, gw), row),
                  pl.BlockSpec((gw, d), lambda bb, i: (0, 0)),
                  pl.BlockSpec((1, tm, d), row),
                  pl.BlockSpec((1, 6, d), lambda bb, i: (bb, 0, 0)),
                  pl.BlockSpec((1, 6, d), lambda bb, i: (0, 0, 0)),
                  pl.BlockSpec((1, d), lambda bb, i: (0, 0)),
                  pl.BlockSpec((1, d), lambda bb, i: (0, 0))],
        out_specs=[pl.BlockSpec((1, tm, d), row), pl.BlockSpec((1, tm, d // 2), row)],
        out_shape=[jax.ShapeDtypeStruct((b, ta, d), F32), jax.ShapeDtypeStruct((b, ta, d // 2), jnp.uint32)],
        compiler_params=_cparams(("parallel", "parallel")),
        name="merge_out_ln",
    )(merged, w_out, x_all, modl, modc, ln_g.reshape(1, d), ln_b.reshape(1, d))


def _router_kernel(x_ref, ml_ref, mc_ref, rw_ref, rb_ref, idx_ref, wt_ref, rank_ref, cnt_ref, carry_ref,
                   *, tm, ctx):
    i = pl.program_id(1)

    @pl.when((pl.program_id(0) == 0) & (i == 0))
    def _():
        carry_ref[...] = jnp.zeros_like(carry_ref)

    isc = _row_is_ctx(tm, i * tm, ctx)
    h2 = _modulated(x_ref[0], ml_ref[0], mc_ref[0], 3, 4, isc)
    scores = _sigmoid(_dot(h2, rw_ref[...], precision=HIGHEST))
    biased = scores + rb_ref[...]
    lane = lax.broadcasted_iota(jnp.int32, scores.shape, 1)
    grp = lane // GROUP_SIZE
    ninf = -jnp.inf

    def first_argmax(z):
        m = jnp.max(z, -1, keepdims=True)
        idx = jnp.min(jnp.where(z == m, lane, N_EXPERTS), -1, keepdims=True)
        return m, idx

    gscore = jnp.zeros_like(biased)
    for g in range(N_GROUPS):
        zg = jnp.where(grp == g, biased, ninf)
        m1, i1 = first_argmax(zg)
        m2 = jnp.max(jnp.where(lane == i1, ninf, zg), -1, keepdims=True)
        gscore = jnp.where(grp == g, m1 + m2, gscore)
    cand = jnp.where(lane % GROUP_SIZE == 0, gscore, ninf)
    gsel = jnp.zeros(scores.shape, jnp.bool_)
    for _ in range(TOPK_GROUPS):
        _, ig = first_argmax(cand)
        gsel = gsel | (grp == ig // GROUP_SIZE)
        cand = jnp.where(lane == ig, ninf, cand)
    masked = jnp.where(gsel, biased, ninf)
    out_lane = lax.broadcasted_iota(jnp.int32, (tm, LANE), 1)
    idx_out = jnp.zeros((tm, LANE), jnp.int32)
    w_out = jnp.zeros((tm, LANE), F32)
    w_sum = jnp.zeros((tm, 1), F32)
    chosen = []
    picked = jnp.zeros(scores.shape, F32)
    for kq in range(TOP_K):
        _, ie = first_argmax(masked)
        sel = lane == ie
        wk = jnp.sum(jnp.where(sel, scores, 0.0), -1, keepdims=True)
        masked = jnp.where(sel, ninf, masked)
        idx_out = jnp.where(out_lane == kq, ie, idx_out)
        w_out = jnp.where(out_lane == kq, wk, w_out)
        w_sum = w_sum + wk
        chosen.append(sel)
        picked = jnp.where(sel, 1.0, picked)
    idx_ref[0] = idx_out
    wt_ref[0] = w_out / w_sum * ROUTED_SCALE
    r_i = lax.broadcasted_iota(jnp.int32, (tm, tm), 0)
    c_i = lax.broadcasted_iota(jnp.int32, (tm, tm), 1)
    before = _dot(jnp.where(r_i > c_i, 1.0, 0.0).astype(BF16), picked.astype(BF16)) + carry_ref[...]
    rank_out = jnp.zeros((tm, LANE), jnp.int32)
    for kq in range(TOP_K):
        rk = jnp.sum(jnp.where(chosen[kq], before, 0.0), -1, keepdims=True)
        rank_out = jnp.where(out_lane == kq, rk.astype(jnp.int32), rank_out)
    rank_ref[0] = rank_out
    carry_ref[...] += jnp.sum(picked, axis=0, keepdims=True)
    cnt_ref[...] = carry_ref[...].astype(jnp.int32)


def _router(x_all, modl, modc, rw, rb, ctx):
    b, ta, d = x_all.shape
    tm = _tile(ta, 544)
    row = lambda bb, i: (bb, i, 0)
    tok = jax.ShapeDtypeStruct((b, ta, LANE), jnp.int32)
    return pl.pallas_call(
        functools.partial(_router_kernel, tm=tm, ctx=ctx),
        grid=(b, ta // tm),
        in_specs=[pl.BlockSpec((1, tm, d), row),
                  pl.BlockSpec((1, 6, d), lambda bb, i: (bb, 0, 0)),
                  pl.BlockSpec((1, 6, d), lambda bb, i: (0, 0, 0)),
                  pl.BlockSpec((d, N_EXPERTS), lambda bb, i: (0, 0)),
                  pl.BlockSpec((1, N_EXPERTS), lambda bb, i: (0, 0))],
        out_specs=[pl.BlockSpec((1, tm, LANE), row), pl.BlockSpec((1, tm, LANE), row),
                   pl.BlockSpec((1, tm, LANE), row), pl.BlockSpec((1, N_EXPERTS), lambda bb, i: (0, 0))],
        out_shape=[tok, jax.ShapeDtypeStruct((b, ta, LANE), F32), tok,
                   jax.ShapeDtypeStruct((1, N_EXPERTS), jnp.int32)],
        scratch_shapes=[pltpu.VMEM((1, N_EXPERTS), F32)],
        compiler_params=_cparams(("arbitrary", "arbitrary")),
        name="router",
    )(x_all, modl, modc, rw, rb.reshape(1, N_EXPERTS))


def _row_copy(src_ref, src_row, dst_ref, dst_row, sem):
    return pltpu.make_async_copy(src_ref.at[pl.ds(src_row, 1)], dst_ref.at[pl.ds(dst_row, 1)], sem)


def _dispatch_kernel(pos_ref, h2_ref, xg_in_ref, xg_ref, sem, *, tm):
    del xg_in_ref

    def issue(t, carry):
        for kq in range(TOP_K):
            _row_copy(h2_ref, t, xg_ref, pos_ref[0, 0, t * TOP_K + kq], sem).start(priority=kq % 2)
        return carry

    lax.fori_loop(0, tm, issue, 0, unroll=4)
    for _ in range(TOP_K):
        pltpu.make_async_copy(h2_ref, xg_ref.at[pl.ds(0, tm)], sem).wait()


def _dispatch(pos, h2p, n_rows, tm):
    n, half = h2p.shape
    xg0 = jnp.zeros((n_rows, half), jnp.uint32)
    return pl.pallas_call(
        functools.partial(_dispatch_kernel, tm=tm),
        grid=(n // tm,),
        in_specs=[pl.BlockSpec((1, 1, tm * TOP_K), lambda i: (i, 0, 0), memory_space=pltpu.SMEM),
                  pl.BlockSpec((tm, half), lambda i: (i, 0)),
                  pl.BlockSpec(memory_space=pl.ANY)],
        out_specs=pl.BlockSpec(memory_space=pl.ANY),
        out_shape=jax.ShapeDtypeStruct((n_rows, half), jnp.uint32),
        scratch_shapes=[pltpu.SemaphoreType.DMA(())],
        input_output_aliases={2: 0},
        compiler_params=_cparams(("arbitrary",)),
        name="moe_dispatch",
    )(pos, h2p, xg0)


def _swiglu(x_lo, x_hi, wgu, wdn, ff):
    half = x_lo.shape[1]
    h = _dot(x_lo, wgu[:half, :]) + _dot(x_hi, wgu[half:, :])
    g = h[:, :ff]
    return _dot((g * _sigmoid(g) * h[:, ff:]).astype(BF16), wdn)


def _experts_kernel(be_ref, nu_ref, first_ref, slot_ref, nxt_ref, x_ref, wgu_hbm, wdn_hbm, o_ref,
                    gu_buf, dn_buf, wgu_s, wdn_s, sems, *, ff, l):
    i = pl.program_id(0)

    def weight_copies(e, s):
        return (pltpu.make_async_copy(wgu_hbm.at[l, e], gu_buf.at[s], sems.at[0, s]),
                pltpu.make_async_copy(wdn_hbm.at[l, e], dn_buf.at[s], sems.at[1, s]))

    @pl.when(i == 0)
    def _():
        for cp in weight_copies(be_ref[0], slot_ref[0]):
            cp.start()

    @pl.when(first_ref[i] == 1)
    def _():
        s = slot_ref[i]
        for cp in weight_copies(be_ref[i], s):
            cp.wait()
        wgu_s[...] = gu_buf[s].astype(BF16)
        wdn_s[...] = dn_buf[s].astype(BF16)

        @pl.when(nxt_ref[i] >= 0)
        def _():
            for cp in weight_copies(nxt_ref[i], 1 - s):
                cp.start()

    @pl.when(i < nu_ref[0])
    def _():
        lo, hi = _unpack_halves(x_ref[...])
        o_ref[...] = _pack_halves(_swiglu(lo.astype(BF16), hi.astype(BF16), wgu_s[...], wdn_s[...], ff))

    @pl.when(i >= nu_ref[0])
    def _():
        o_ref[...] = jnp.zeros_like(o_ref)


def _experts(xg, blk_e, n_used, w_gu, w_dn, l, blk):
    n_rows, half = xg.shape
    d = 2 * half
    ff = w_dn.shape[2]
    n_blocks = n_rows // blk
    first = jnp.concatenate([jnp.ones((1,), jnp.int32), (blk_e[1:] != blk_e[:-1]).astype(jnp.int32)])
    slot = (jnp.cumsum(first) - 1) % 2
    n_le = jnp.sum(blk_e[None, :] <= blk_e[:, None], axis=1)
    pick = jnp.arange(n_blocks)[None, :] == n_le[:, None]
    nxt = jnp.where(n_le < n_blocks, jnp.sum(jnp.where(pick, blk_e[None, :], 0), axis=1), -1)
    grid_spec = pltpu.PrefetchScalarGridSpec(
        num_scalar_prefetch=5,
        grid=(n_blocks,),
        in_specs=[pl.BlockSpec((blk, half), lambda i, *_: (i, 0)),
                  pl.BlockSpec(memory_space=pl.ANY),
                  pl.BlockSpec(memory_space=pl.ANY)],
        out_specs=pl.BlockSpec((blk, half), lambda i, *_: (i, 0)),
        scratch_shapes=[pltpu.VMEM((2, d, 2 * ff), F32), pltpu.VMEM((2, ff, d), F32),
                        pltpu.VMEM((d, 2 * ff), BF16), pltpu.VMEM((ff, d), BF16),
                        pltpu.SemaphoreType.DMA((2, 2))],
    )
    return pl.pallas_call(
        functools.partial(_experts_kernel, ff=ff, l=l),
        grid_spec=grid_spec,
        out_shape=jax.ShapeDtypeStruct((n_rows, half), jnp.uint32),
        compiler_params=_cparams(("arbitrary",), 56),
        name="moe_experts",
    )(blk_e, n_used, first, slot.astype(jnp.int32), nxt.astype(jnp.int32), xg, w_gu, w_dn)


def _moe_plan(idx, rank, counts, blk):
    n_assign = idx.size
    padded = (counts + blk - 1) // blk * blk
    pad_end = jnp.cumsum(padded)
    start_pad = pad_end - padded
    experts = jnp.arange(N_EXPERTS, dtype=jnp.int32)
    start_of = jnp.sum(jnp.where(idx[..., None] == experts, start_pad, 0), axis=-1)
    pos = (start_of + rank).astype(jnp.int32)
    n_blocks = -(-(n_assign + N_EXPERTS * (blk - 1)) // blk)
    blk_start = jnp.arange(n_blocks, dtype=jnp.int32) * blk
    blk_e = jnp.minimum(jnp.sum(blk_start[:, None] >= pad_end[None, :], axis=-1), N_EXPERTS - 1).astype(jnp.int32)
    n_used = (pad_end[-1] // blk).astype(jnp.int32).reshape(1)
    return pos, blk_e, n_used, n_blocks * blk


def _combine_kernel(pos_ref, y_ref, wt_ref, h2_ref, sgu_ref, sdn_ref, x_ref, ml_ref, mc_ref, lg_ref, lb_ref,
                    xo_ref, buf, sem, *, tm, ctx, alpha, ff, t0):
    i = pl.program_id(1) + t0

    def issue(t, carry):
        for kq in range(TOP_K):
            _row_copy(y_ref, pos_ref[0, 0, t * TOP_K + kq], buf.at[kq], t, sem).start(priority=kq % 2)
        return carry

    lax.fori_loop(0, tm, issue, 0, unroll=4)
    lo, hi = _unpack_halves(h2_ref[0])
    f = _swiglu(lo.astype(BF16), hi.astype(BF16), sgu_ref[...], sdn_ref[...], ff)
    for kq in range(TOP_K):
        pltpu.make_async_copy(y_ref.at[pl.ds(0, tm)], buf.at[kq], sem).wait()
    wt = wt_ref[0]
    half = f.shape[1] // 2
    f_lo = f[:, :half]
    f_hi = f[:, half:]
    for kq in range(TOP_K):
        lo, hi = _unpack_halves(buf[kq])
        f_lo = f_lo + wt[:, kq:kq + 1] * lo
        f_hi = f_hi + wt[:, kq:kq + 1] * hi
    f = jnp.concatenate([f_lo, f_hi], axis=1)
    isc = _row_is_ctx(tm, i * tm, ctx)
    g2 = jnp.where(isc, mc_ref[0][5:6], ml_ref[0][5:6])
    xo_ref[0] = _ln(alpha * x_ref[0] + g2 * f, POST_LN_EPS) * lg_ref[...] + lb_ref[...]


def _combine(pos, y_rows, wts, h2p, sh_gu, sh_dn, x_all, modl, modc, ln_g, ln_b, ctx, alpha, tm, skip_ctx):
    b, ta, d = x_all.shape
    ff = sh_dn.shape[0]
    nt = ta // tm
    t0 = ctx // tm if skip_ctx else 0
    row = lambda bb, i: (bb, i + t0, 0)
    return pl.pallas_call(
        functools.partial(_combine_kernel, tm=tm, ctx=ctx, alpha=alpha, ff=ff, t0=t0),
        grid=(b, nt - t0),
        in_specs=[pl.BlockSpec((1, 1, tm * TOP_K), lambda bb, i: (bb * nt + i + t0, 0, 0),
                               memory_space=pltpu.SMEM),
                  pl.BlockSpec(memory_space=pl.ANY),
                  pl.BlockSpec((1, tm, LANE), row),
                  pl.BlockSpec((1, tm, d // 2), row),
                  pl.BlockSpec((d, 2 * ff), lambda bb, i: (0, 0)),
                  pl.BlockSpec((ff, d), lambda bb, i: (0, 0)),
                  pl.BlockSpec((1, tm, d), row),
                  pl.BlockSpec((1, 6, d), lambda bb, i: (bb, 0, 0)),
                  pl.BlockSpec((1, 6, d), lambda bb, i: (0, 0, 0)),
                  pl.BlockSpec((1, d), lambda bb, i: (0, 0)),
                  pl.BlockSpec((1, d), lambda bb, i: (0, 0))],
        out_specs=pl.BlockSpec((1, tm, d), lambda bb, i: (bb, i, 0)),
        out_shape=jax.ShapeDtypeStruct((b, ta - t0 * tm, d), F32),
        scratch_shapes=[pltpu.VMEM((TOP_K, tm, d // 2), jnp.uint32), pltpu.SemaphoreType.DMA(())],
        compiler_params=_cparams(("arbitrary", "arbitrary")),
        name="moe_combine_ln",
    )(pos, y_rows, wts, h2p, sh_gu, sh_dn, x_all, modl, modc, ln_g.reshape(1, d), ln_b.reshape(1, d))


def _rope_tables(n_lat, ctx):
    rows = n_lat // GRID_W
    row = jnp.repeat(jnp.arange(rows, dtype=F32), GRID_W)
    col = jnp.tile(jnp.arange(GRID_W, dtype=F32), rows)
    n_freq = ROPE_DIM // 4
    inv_freq = ROPE_BASE ** (-jnp.arange(n_freq, dtype=F32) / n_freq)
    ang = jnp.concatenate([row[:, None] * inv_freq, col[:, None] * inv_freq], -1)
    cos, sin = jnp.cos(ang), jnp.sin(ang)
    cos = jnp.concatenate([jnp.ones((ctx, ROPE_DIM // 2), F32), cos], 0)
    sin = jnp.concatenate([jnp.zeros((ctx, ROPE_DIM // 2), F32), sin], 0)
    cosf = jnp.tile(jnp.concatenate([cos, cos], -1), (1, LANE // ROPE_DIM))
    sinf = jnp.tile(jnp.concatenate([-sin, sin], -1), (1, LANE // ROPE_DIM))
    return cosf, sinf


def _pack_small_weights(w_in_l):
    d = w_in_l.shape[0]
    offs = np.cumsum([0, MLA_Q_RANK, MLA_KV_RANK + MLA_ROPE, RWKV_STREAM,
                      (SWA_HEADS + 2 * SWA_KV_HEADS) * SWA_HEAD, 3 * DIFF_HEADS * 2 * DIFF_HEAD])
    placed = sorted(zip([SEG_MQ, SEG_MKV, SEG_RW, SEG_SW, SEG_DF], range(5)))
    parts = []
    cur = 0
    for start, s in placed:
        if start > cur:
            parts.append(jnp.zeros((d, start - cur), BF16))
        parts.append(w_in_l[:, int(offs[s]):int(offs[s + 1])].astype(BF16))
        cur = start + int(offs[s + 1] - offs[s])
    parts.append(jnp.zeros((d, SMALL_COLS - cur), BF16))
    return jnp.concatenate(parts, axis=1), int(offs[5])


def kernel(x, c, ctx, c_ctx, w_mod, b_mod, w_in, mla_q_norm, mla_w_qup, mla_kv_norm, mla_w_kvup, rwkv_mu, rwkv_w0, rwkv_w_lora, rwkv_a0, rwkv_a_lora, rwkv_g_lora, rwkv_k_k, rwkv_k_a, rwkv_r_k, rwkv_ln_g, rwkv_ln_b, swa_sink, diff_lambda, diff_subln, w_branch, w_out, ln1_g, ln1_b, router_w, router_bias, exp_w_gu, exp_w_dn, sh_w_gu, sh_w_dn, ln2_g, ln2_b):
    depth = w_mod.shape[0]
    b, n_lat, d = x.shape
    n_ctx = ctx.shape[1]
    ta = n_ctx + n_lat
    alpha = (2 * depth) ** 0.25
    cosf, sinf = _rope_tables(n_lat, n_ctx)
    x_all = jnp.concatenate([ctx, x], axis=1)

    cond = jnp.concatenate([c, c_ctx[None]], axis=0)
    cond = jnp.pad(jax.nn.silu(cond), ((0, SUBLANE - (b + 1) % SUBLANE), (0, 0)))

    for l in range(depth):
        mod = _matmul_bias(cond, w_mod, b_mod, l).reshape(cond.shape[0], 6, d)
        modl, modc = mod[:b], mod[b:b + 1]

        w_small, gate_off = _pack_small_weights(w_in[l])
        p = _inproj(x_all, modl, modc, w_small, n_ctx)

        wq = mla_w_qup[l].reshape(MLA_Q_RANK, MLA_HEADS, MLA_NOPE + MLA_ROPE)
        wq = jnp.pad(wq, ((0, 0), (0, 0), (0, 256 - MLA_NOPE - MLA_ROPE))).reshape(MLA_Q_RANK, 1024)
        q_a, k_a, v_a = _mla_prep(p, mla_q_norm[l].reshape(1, -1), mla_kv_norm[l].reshape(1, -1),
                                  wq.astype(BF16), mla_w_kvup[l].astype(BF16), cosf, sinf)
        out_a = _mla_attn(q_a, k_a, v_a, n_ctx)

        out_b = _rwkv_mixer(p, n_ctx, rwkv_mu[l], rwkv_w0[l],
                            rwkv_w_lora[l], rwkv_a0[l], rwkv_a_lora[l], rwkv_g_lora[l], rwkv_k_k[l],
                            rwkv_k_a[l], rwkv_r_k[l], rwkv_ln_g[l], rwkv_ln_b[l])

        sinkv = jnp.broadcast_to(swa_sink[l][:, None], (SWA_HEADS, LANE))
        out_s = _swa(p, cosf, sinf, sinkv, n_ctx)

        lf = diff_lambda[l]
        lam_init = 0.8 - 0.6 * math.exp(-0.3 * l)
        lam = jnp.exp(jnp.sum(lf[0] * lf[1])) - jnp.exp(jnp.sum(lf[2] * lf[3])) + lam_init
        out_d = _diff_attn(p, cosf, sinf, jnp.full((1, LANE), lam, F32), diff_subln[l].reshape(1, LANE),
                           n_ctx, lam_init)

        o4 = jnp.stack([out_a, out_b, out_s, out_d], axis=0)
        wg, wb, wo = _gate_window_weights(w_in[l], gate_off, w_branch[l], w_out[l])
        merged = _merge_a(x_all, modl, modc, o4, wg, wb, n_ctx)
        x_all, h2p = _merge_b(merged, wo, x_all, modl, modc, ln1_g[l], ln1_b[l], n_ctx, alpha)

        idx, wts, rank, counts = _router(x_all, modl, modc, router_w[l], router_bias[l], n_ctx)
        pos, blk_e, n_used, n_rows = _moe_plan(idx[..., :TOP_K], rank[..., :TOP_K], counts[0], MOE_BLK)
        tmd = _tile(ta, 128)
        tmc = _tile(math.gcd(ta, n_ctx), 256)
        xg = _dispatch(pos.reshape(b * ta // tmd, 1, tmd * TOP_K), h2p.reshape(b * ta, d // 2), n_rows, tmd)
        y_rows = _experts(xg, blk_e, n_used, exp_w_gu, exp_w_dn, l, MOE_BLK)
        x_all = _combine(pos.reshape(b * ta // tmc, 1, tmc * TOP_K), y_rows, wts, h2p, sh_w_gu[l].astype(BF16),
                         sh_w_dn[l].astype(BF16), x_all, modl, modc, ln2_g[l], ln2_b[l], n_ctx, alpha, tmc,
                         skip_ctx=(l == depth - 1))
    return x_all
```
